```python
import jax, jax.numpy as jnp
from jax import lax
import numpy as np

D_MODEL = 2048
BATCH = 8
SEQ = 8192
DEPTH = 4

CHUNK = 64
EPS = 1e-6
MIN_FORGET = 1e-30
A_VAL = 128
A_KEY = 128
A_VW = D_MODEL // 2
A_HEADS = A_VW // A_VAL
A_KW = A_HEADS * A_KEY
B_WIDTH = D_MODEL // 2
B_WINDOWS = (2, 4, 8, 16)
B_GROUPS = len(B_WINDOWS)
B_GW = B_WIDTH // B_GROUPS
IN_SIZES = (A_KW, A_KW, A_VW, A_VW, B_WIDTH, B_WIDTH, D_MODEL, D_MODEL)
IN_COLS = sum(IN_SIZES)
IN_SPLITS = [int(v) for v in np.cumsum(IN_SIZES)[:-1]]

kernel_name = "hybrid_hgrn2_pool_gated_trunk"


def rmsnorm(x, gain):
    xf = x.astype(jnp.float32)
    y = xf * lax.rsqrt(jnp.mean(xf * xf, axis=-1, keepdims=True) + EPS)
    return y.astype(x.dtype) * gain


def hgrn2_mixer(q_raw, f_raw, v, lb):
    f32 = jnp.float32
    bsz, seq, _ = q_raw.shape
    nc = seq // CHUNK
    q = jax.nn.silu(q_raw.astype(f32))
    a = f_raw.astype(f32)
    lb = lb.astype(f32)
    f = lb + (1.0 - lb) * jax.nn.sigmoid(a)
    log_f = jnp.log(jnp.maximum(f, MIN_FORGET))
    k = (1.0 - lb) * jax.nn.sigmoid(-a)
    v = v.astype(f32)

    def to_chunks(t, d):
        return t.reshape(bsz, nc, CHUNK, A_HEADS, d).transpose(1, 0, 3, 2, 4)

    qc, kc, lfc = to_chunks(q, A_KEY), to_chunks(k, A_KEY), to_chunks(log_f, A_KEY)
    vc = to_chunks(v, A_VAL)
    causal = jnp.tril(jnp.ones((CHUNK, CHUNK), dtype=bool))[:, :, None]
    causal_f = causal.astype(f32)

    def step(state, inp):
        qi, ki, vi, lfi = inp
        b = jnp.cumsum(lfi, axis=2)
        o_inter = jnp.einsum('bhtk,bhkv->bhtv', qi * jnp.exp(b), state)
        diff = b[:, :, :, None, :] - b[:, :, None, :, :]
        decay = jnp.exp(jnp.where(causal, diff, 0.0)) * causal_f
        scores = jnp.einsum('bhtk,bhtsk,bhsk->bhts', qi, decay, ki)
        o = o_inter + jnp.einsum('bhts,bhsv->bhtv', scores, vi)
        b_last = b[:, :, -1:, :]
        state = (jnp.exp(b_last[:, :, 0, :])[..., None] * state
                 + jnp.einsum('bhsk,bhsv->bhkv', ki * jnp.exp(b_last - b), vi))
        return state, o

    state0 = jnp.zeros((bsz, A_HEADS, A_KEY, A_VAL), f32)
    _, o = lax.scan(step, state0, (qc, kc, vc, lfc))
    return o.transpose(1, 0, 3, 2, 4).reshape(bsz, seq, A_HEADS, A_VAL)


def pool_mixer(v, pool_w, pool_scale):
    f32 = jnp.float32
    bsz, seq, _ = v.shape
    vf = v.astype(f32)
    cs = jnp.concatenate([jnp.zeros((bsz, 1, B_WIDTH), f32), lax.cumsum(vf, axis=1)], axis=1)
    pos = jnp.arange(1, seq + 1, dtype=f32)[None, :, None]
    outs = []
    for g, w in enumerate(B_WINDOWS):
        sl = slice(g * B_GW, (g + 1) * B_GW)
        cs_g = cs[:, :, sl]
        shifted = jnp.pad(cs_g, ((0, 0), (w, 0), (0, 0)))[:, :seq + 1]
        mean = (cs_g - shifted)[:, 1:] / jnp.minimum(pos, float(w))
        outs.append(mean - vf[:, :, sl])
    pooled = jnp.stack(outs, axis=2)
    mixed = jnp.einsum('bsgc,gcd->bsgd', pooled, pool_w.astype(f32)).reshape(bsz, seq, B_WIDTH)
    return (mixed * pool_scale.astype(f32)).astype(v.dtype)


def _fwd_setup_inputs(seed: int = 0) -> dict:
    key = jax.random.key(seed)
    ks = jax.random.split(key, 16)
    nrm = jax.random.normal
    f32 = jnp.float32
    return {
        "x": nrm(ks[0], (BATCH, SEQ, D_MODEL), f32),
        "c": nrm(ks[1], (BATCH, D_MODEL), f32),
        "w_ada": nrm(ks[2], (DEPTH, D_MODEL, 3 * D_MODEL), f32) * (0.5 * D_MODEL ** -0.5),
        "b_ada": nrm(ks[3], (DEPTH, 3 * D_MODEL), f32) * 0.02,
        "norm_pre": 1.0 + 0.1 * nrm(ks[4], (DEPTH, D_MODEL), f32),
        "norm_post": 1.0 + 0.1 * nrm(ks[5], (DEPTH, D_MODEL), f32),
        "w_in": nrm(ks[6], (DEPTH, D_MODEL, IN_COLS), f32) * D_MODEL ** -0.5,
        "lower_bounds": nrm(ks[7], (DEPTH, A_KW), f32),
        "hgrn_norm": 1.0 + 0.1 * nrm(ks[8], (DEPTH, A_VW), f32),
        "pool_w": nrm(ks[9], (DEPTH, B_GROUPS, B_GW, B_GW), f32) * B_GW ** -0.5,
        "pool_scale": 1.0 + 0.1 * nrm(ks[10], (DEPTH, B_WIDTH), f32),
        "w_proj_a": nrm(ks[11], (DEPTH, A_VW, D_MODEL), f32) * A_VW ** -0.5,
        "w_proj_b": nrm(ks[12], (DEPTH, B_WIDTH, D_MODEL), f32) * B_WIDTH ** -0.5,
        "w_out": nrm(ks[13], (DEPTH, D_MODEL, D_MODEL), f32) * D_MODEL ** -0.5,
    }


def _fwd_reference(x, c, w_ada, b_ada, norm_pre, norm_post, w_in, lower_bounds, hgrn_norm,
              pool_w, pool_scale, w_proj_a, w_proj_b, w_out):
    bsz, seq, _ = x.shape
    sm = jax.nn.softmax(lower_bounds.astype(jnp.float32), axis=0)
    lb_all = jnp.cumsum(sm, axis=0) - sm[0:1]
    c_act = jax.nn.silu(c)
    for l in range(DEPTH):
        mod = c_act @ w_ada[l] + b_ada[l]
        shift, scale, gate = jnp.split(mod, 3, axis=-1)
        h = rmsnorm(x, norm_pre[l]) * (1.0 + scale[:, None, :]) + shift[:, None, :]
        proj = h @ w_in[l]
        q_a, f_a, v_a, z_a, v_b, z_b, g_a, g_b = jnp.split(proj, IN_SPLITS, axis=-1)
        o_a = hgrn2_mixer(q_a, f_a, v_a, lb_all[l])
        o_a = o_a * lax.rsqrt(jnp.mean(o_a * o_a, axis=-1, keepdims=True) + EPS)
        y_a = o_a.reshape(bsz, seq, A_VW).astype(x.dtype) * hgrn_norm[l] * jax.nn.silu(z_a)
        y_b = pool_mixer(v_b, pool_w[l], pool_scale[l]) * jax.nn.silu(z_b)
        merged = jax.nn.sigmoid(g_a) * (y_a @ w_proj_a[l]) + jax.nn.sigmoid(g_b) * (y_b @ w_proj_b[l])
        out = merged @ w_out[l]
        x = x + gate[:, None, :] * rmsnorm(out, norm_post[l])
    return x


import jax as _jax
import jax.numpy as _jnp

TWIN_FORMAT = 'train_step'
FWD_PARAMS = ['x', 'c', 'w_ada', 'b_ada', 'norm_pre', 'norm_post', 'w_in', 'lower_bounds', 'hgrn_norm', 'pool_w', 'pool_scale', 'w_proj_a', 'w_proj_b', 'w_out']
TWIN_WEIGHTS = ['w_ada', 'b_ada', 'norm_pre', 'norm_post', 'w_in', 'lower_bounds', 'hgrn_norm', 'pool_w', 'pool_scale', 'w_proj_a', 'w_proj_b', 'w_out']
TWIN_DIFF_INPUT = 'x'
TWIN_INPUTS = ['x', 'c', 'w_ada', 'b_ada', 'norm_pre', 'norm_post', 'w_in', 'lower_bounds', 'hgrn_norm', 'pool_w', 'pool_scale', 'w_proj_a', 'w_proj_b', 'w_out', 'loss_target', 'm_w_ada', 'm_b_ada', 'm_norm_pre', 'm_norm_post', 'm_w_in', 'm_lower_bounds', 'm_hgrn_norm', 'm_pool_w', 'm_pool_scale', 'm_w_proj_a', 'm_w_proj_b', 'm_w_out', 'v_w_ada', 'v_b_ada', 'v_norm_pre', 'v_norm_post', 'v_w_in', 'v_lower_bounds', 'v_hgrn_norm', 'v_pool_w', 'v_pool_scale', 'v_w_proj_a', 'v_w_proj_b', 'v_w_out']
TWIN_OUTPUTS = ['loss', 'grad_x', 'grad_w_ada', 'grad_b_ada', 'grad_norm_pre', 'grad_norm_post', 'grad_w_in', 'grad_lower_bounds', 'grad_hgrn_norm', 'grad_pool_w', 'grad_pool_scale', 'grad_w_proj_a', 'grad_w_proj_b', 'grad_w_out', 'delta_w_ada', 'delta_b_ada', 'delta_norm_pre', 'delta_norm_post', 'delta_w_in', 'delta_lower_bounds', 'delta_hgrn_norm', 'delta_pool_w', 'delta_pool_scale', 'delta_w_proj_a', 'delta_w_proj_b', 'delta_w_out', 'new_m_w_ada', 'new_m_b_ada', 'new_m_norm_pre', 'new_m_norm_post', 'new_m_w_in', 'new_m_lower_bounds', 'new_m_hgrn_norm', 'new_m_pool_w', 'new_m_pool_scale', 'new_m_w_proj_a', 'new_m_w_proj_b', 'new_m_w_out', 'new_v_w_ada', 'new_v_b_ada', 'new_v_norm_pre', 'new_v_norm_post', 'new_v_w_in', 'new_v_lower_bounds', 'new_v_hgrn_norm', 'new_v_pool_w', 'new_v_pool_scale', 'new_v_w_proj_a', 'new_v_w_proj_b', 'new_v_w_out']
TWIN_LEAF_KINDS = {'loss': 'loss', 'grad_x': 'grad_x', 'grad_w_ada': 'grad_w', 'grad_b_ada': 'grad_w', 'grad_norm_pre': 'grad_w', 'grad_norm_post': 'grad_w', 'grad_w_in': 'grad_w', 'grad_lower_bounds': 'grad_w', 'grad_hgrn_norm': 'grad_w', 'grad_pool_w': 'grad_w', 'grad_pool_scale': 'grad_w', 'grad_w_proj_a': 'grad_w', 'grad_w_proj_b': 'grad_w', 'grad_w_out': 'grad_w', 'delta_w_ada': 'delta_w', 'delta_b_ada': 'delta_w', 'delta_norm_pre': 'delta_w', 'delta_norm_post': 'delta_w', 'delta_w_in': 'delta_w', 'delta_lower_bounds': 'delta_w', 'delta_hgrn_norm': 'delta_w', 'delta_pool_w': 'delta_w', 'delta_pool_scale': 'delta_w', 'delta_w_proj_a': 'delta_w', 'delta_w_proj_b': 'delta_w', 'delta_w_out': 'delta_w', 'new_m_w_ada': 'new_m', 'new_m_b_ada': 'new_m', 'new_m_norm_pre': 'new_m', 'new_m_norm_post': 'new_m', 'new_m_w_in': 'new_m', 'new_m_lower_bounds': 'new_m', 'new_m_hgrn_norm': 'new_m', 'new_m_pool_w': 'new_m', 'new_m_pool_scale': 'new_m', 'new_m_w_proj_a': 'new_m', 'new_m_w_proj_b': 'new_m', 'new_m_w_out': 'new_m', 'new_v_w_ada': 'new_v', 'new_v_b_ada': 'new_v', 'new_v_norm_pre': 'new_v', 'new_v_norm_post': 'new_v', 'new_v_w_in': 'new_v', 'new_v_lower_bounds': 'new_v', 'new_v_hgrn_norm': 'new_v', 'new_v_pool_w': 'new_v', 'new_v_pool_scale': 'new_v', 'new_v_w_proj_a': 'new_v', 'new_v_w_proj_b': 'new_v', 'new_v_w_out': 'new_v'}


def _forward(args):
    return _fwd_reference(*[args[k] for k in FWD_PARAMS])


def _output_shape():
    def fwd():
        inp = _fwd_setup_inputs(0)
        return _fwd_reference(*[inp[k] for k in FWD_PARAMS])
    out = _jax.eval_shape(fwd)
    return out.shape, out.dtype

N_MICROBATCH = 1
ADAM_LR = 0.001
ADAM_B1 = 0.9
ADAM_B2 = 0.999
ADAM_EPS = 1e-08
ADAM_WD = 0.01
ADAM_STEP = 10
PER_EXAMPLE_BATCH_AXIS = {'x': 0, 'c': 0, 'loss_target': 0}
SHARED_INPUTS = []
_WEIGHT_DTYPES = {'w_ada': _jnp.float32, 'b_ada': _jnp.float32, 'norm_pre': _jnp.float32, 'norm_post': _jnp.float32, 'w_in': _jnp.float32, 'lower_bounds': _jnp.float32, 'hgrn_norm': _jnp.float32, 'pool_w': _jnp.float32, 'pool_scale': _jnp.float32, 'w_proj_a': _jnp.float32, 'w_proj_b': _jnp.float32, 'w_out': _jnp.float32}
MOMENT_SCALE = {'w_ada': 1.191858e+00, 'b_ada': 2.622781e+00, 'norm_pre': 9.323498e-02, 'norm_post': 3.219047e+00, 'w_in': 5.110595e-02, 'lower_bounds': 4.623262e-03, 'hgrn_norm': 1.062766e-01, 'pool_w': 6.606641e-02, 'pool_scale': 6.959197e-02, 'w_proj_a': 7.009250e-02, 'w_proj_b': 4.815446e-02, 'w_out': 8.718872e-02}


def _to_microbatches(a, axis):
    t = _jnp.moveaxis(a, axis, 0)
    t = t.reshape((N_MICROBATCH, t.shape[0] // N_MICROBATCH) + t.shape[1:])
    return _jnp.moveaxis(t, 1, axis + 1)


def setup_inputs(seed: int = 0) -> dict:
    inp = _fwd_setup_inputs(seed)
    key = _jax.random.fold_in(_jax.random.key(seed), 7919)
    shape, _ = _output_shape()
    out = dict(inp)
    out["loss_target"] = _jax.random.normal(_jax.random.fold_in(key, 0), shape, _jnp.float32)
    for i, name in enumerate(TWIN_WEIGHTS):
        w = inp[name].astype(_jnp.float32)
        if MOMENT_SCALE is None:
            s = _jnp.sqrt(_jnp.mean(_jnp.square(w)) + 1e-30)
        else:
            s = MOMENT_SCALE[name]
        km, kv = _jax.random.split(_jax.random.fold_in(key, i + 1))
        out[name] = w
        out["m_" + name] = s * _jax.random.normal(km, w.shape, _jnp.float32)
        out["v_" + name] = (s * s) * _jax.random.uniform(kv, w.shape, _jnp.float32, 0.5, 1.5)
    if N_MICROBATCH > 1:
        for name, axis in PER_EXAMPLE_BATCH_AXIS.items():
            out[name] = _to_microbatches(out[name], axis)
    return {'x': out['x'], 'c': out['c'], 'w_ada': out['w_ada'], 'b_ada': out['b_ada'], 'norm_pre': out['norm_pre'], 'norm_post': out['norm_post'], 'w_in': out['w_in'], 'lower_bounds': out['lower_bounds'], 'hgrn_norm': out['hgrn_norm'], 'pool_w': out['pool_w'], 'pool_scale': out['pool_scale'], 'w_proj_a': out['w_proj_a'], 'w_proj_b': out['w_proj_b'], 'w_out': out['w_out'], 'loss_target': out['loss_target'], 'm_w_ada': out['m_w_ada'], 'm_b_ada': out['m_b_ada'], 'm_norm_pre': out['m_norm_pre'], 'm_norm_post': out['m_norm_post'], 'm_w_in': out['m_w_in'], 'm_lower_bounds': out['m_lower_bounds'], 'm_hgrn_norm': out['m_hgrn_norm'], 'm_pool_w': out['m_pool_w'], 'm_pool_scale': out['m_pool_scale'], 'm_w_proj_a': out['m_w_proj_a'], 'm_w_proj_b': out['m_w_proj_b'], 'm_w_out': out['m_w_out'], 'v_w_ada': out['v_w_ada'], 'v_b_ada': out['v_b_ada'], 'v_norm_pre': out['v_norm_pre'], 'v_norm_post': out['v_norm_post'], 'v_w_in': out['v_w_in'], 'v_lower_bounds': out['v_lower_bounds'], 'v_hgrn_norm': out['v_hgrn_norm'], 'v_pool_w': out['v_pool_w'], 'v_pool_scale': out['v_pool_scale'], 'v_w_proj_a': out['v_w_proj_a'], 'v_w_proj_b': out['v_w_proj_b'], 'v_w_out': out['v_w_out']}


def _loss(weights, diff, rest, loss_target):
    with _jax.named_scope("forward"):
        args = {**rest, TWIN_DIFF_INPUT: diff, **{k: w.astype(_WEIGHT_DTYPES[k]) for k, w in weights.items()}}
        y = _forward(args)
    with _jax.named_scope("loss_head"):
        err = _jnp.square(y.astype(_jnp.float32) - loss_target)
        return 0.5 * _jnp.sum(_jnp.mean(err, axis=-1)) if err.ndim else 0.5 * err


def _adamw(w, g, m, v):
    m = ADAM_B1 * m + (1.0 - ADAM_B1) * g
    v = ADAM_B2 * v + (1.0 - ADAM_B2) * _jnp.square(g)
    m_hat = m / (1.0 - ADAM_B1 ** ADAM_STEP)
    v_hat = v / (1.0 - ADAM_B2 ** ADAM_STEP)
    delta = -ADAM_LR * (m_hat / (_jnp.sqrt(v_hat) + ADAM_EPS) + ADAM_WD * w)
    return delta, m, v


def reference(x, c, w_ada, b_ada, norm_pre, norm_post, w_in, lower_bounds, hgrn_norm, pool_w, pool_scale, w_proj_a, w_proj_b, w_out, loss_target, m_w_ada, m_b_ada, m_norm_pre, m_norm_post, m_w_in, m_lower_bounds, m_hgrn_norm, m_pool_w, m_pool_scale, m_w_proj_a, m_w_proj_b, m_w_out, v_w_ada, v_b_ada, v_norm_pre, v_norm_post, v_w_in, v_lower_bounds, v_hgrn_norm, v_pool_w, v_pool_scale, v_w_proj_a, v_w_proj_b, v_w_out):
    given = dict(x=x, c=c, w_ada=w_ada, b_ada=b_ada, norm_pre=norm_pre, norm_post=norm_post, w_in=w_in, lower_bounds=lower_bounds, hgrn_norm=hgrn_norm, pool_w=pool_w, pool_scale=pool_scale, w_proj_a=w_proj_a, w_proj_b=w_proj_b, w_out=w_out, loss_target=loss_target, m_w_ada=m_w_ada, m_b_ada=m_b_ada, m_norm_pre=m_norm_pre, m_norm_post=m_norm_post, m_w_in=m_w_in, m_lower_bounds=m_lower_bounds, m_hgrn_norm=m_hgrn_norm, m_pool_w=m_pool_w, m_pool_scale=m_pool_scale, m_w_proj_a=m_w_proj_a, m_w_proj_b=m_w_proj_b, m_w_out=m_w_out, v_w_ada=v_w_ada, v_b_ada=v_b_ada, v_norm_pre=v_norm_pre, v_norm_post=v_norm_post, v_w_in=v_w_in, v_lower_bounds=v_lower_bounds, v_hgrn_norm=v_hgrn_norm, v_pool_w=v_pool_w, v_pool_scale=v_pool_scale, v_w_proj_a=v_w_proj_a, v_w_proj_b=v_w_proj_b, v_w_out=v_w_out)
    weights = {n: given[n] for n in TWIN_WEIGHTS}
    shared = {n: given[n] for n in SHARED_INPUTS}
    per_example = {n: given[n] for n in ['x', 'c']}
    grad_fn = _jax.value_and_grad(_loss, argnums=(0, 1))

    def one_microbatch(ex, loss_target):
        ex = dict(ex)
        diff = ex.pop(TWIN_DIFF_INPUT)
        return grad_fn(weights, diff, {**shared, **ex}, loss_target)

    if N_MICROBATCH == 1:
        loss, (grad_w, grad_x) = one_microbatch(per_example, given["loss_target"])
    else:
        def body(carry, xs):
            loss_sum, grad_sum = carry
            l_k, (gw_k, gx_k) = one_microbatch(xs[0], xs[1])
            with _jax.named_scope("update"):
                return (loss_sum + l_k, _jax.tree.map(_jnp.add, grad_sum, gw_k)), gx_k

        init = (_jnp.zeros((), _jnp.float32), _jax.tree.map(_jnp.zeros_like, weights))
        (loss, grad_w), grad_x = _jax.lax.scan(body, init, (per_example, given["loss_target"]))
    with _jax.named_scope("update"):
        delta_w, new_m, new_v = {}, {}, {}
        for n in TWIN_WEIGHTS:
            delta_w[n], new_m[n], new_v[n] = _adamw(weights[n], grad_w[n], given["m_" + n], given["v_" + n])
    return (loss, grad_x, *[grad_w[n] for n in TWIN_WEIGHTS], *[delta_w[n] for n in TWIN_WEIGHTS],
            *[new_m[n] for n in TWIN_WEIGHTS], *[new_v[n] for n in TWIN_WEIGHTS])
```

```python
import functools

import jax
import jax.numpy as jnp
from jax import lax
from jax.experimental import pallas as pl
from jax.experimental.pallas import tpu as pltpu

F32 = jnp.float32
BF16 = jnp.bfloat16
MXU_DT = jnp.bfloat16

CHUNK = 64
SUB = 16
HEAD = 128
EPS = 1e-6
MIN_FORGET = 1e-30
WINDOWS = (2, 4, 8, 16)
HALO = 16
N_CHIPS = 4
N_DEV = 8
VMEM_LIMIT_BYTES = 56 * 1024 * 1024

ADAM_LR = 0.001
ADAM_B1 = 0.9
ADAM_B2 = 0.999
ADAM_EPS = 1e-08
ADAM_WD = 0.01
ADAM_STEP = 10

NN = ((1,), (0,))
NT = ((1,), (1,))
TN = ((0,), (0,))
MESH = pl.DeviceIdType.MESH


def _params(sem):
    return pltpu.CompilerParams(dimension_semantics=sem, vmem_limit_bytes=VMEM_LIMIT_BYTES)


def _tile(n, pref):
    return pref if n % pref == 0 else n


def _dot(a, b, dims):
    return lax.dot_general(a.astype(MXU_DT), b.astype(MXU_DT), (dims, ((), ())), preferred_element_type=F32)


def _sig_pair(a):
    e = jnp.exp(-jnp.abs(a))
    inv = 1.0 / (1.0 + e)
    pos = a >= 0
    return jnp.where(pos, inv, e * inv), jnp.where(pos, e * inv, inv)


def _sig(a):
    return _sig_pair(a)[0]


def _split_dot(tri, x):
    def top(y):
        return lax.bitcast_convert_type(lax.bitcast_convert_type(y, jnp.uint32) & jnp.uint32(0xFFFF0000), F32)

    hi = top(x)
    r1 = x - hi
    mid = top(r1)
    lo = r1 - mid
    d = lambda y: jnp.dot(tri, y.astype(BF16), preferred_element_type=F32)
    return d(hi) + d(mid) + d(lo)


def _rms(xv):
    return lax.rsqrt(jnp.mean(xv * xv, axis=-1, keepdims=True) + EPS)


def _prenorm(x, g, scale, shift):
    S, D = x.shape
    tm = _tile(S, 256)

    def body(x_ref, g_ref, sc_ref, sh_ref, h_ref):
        xv = x_ref[...]
        h = (xv * _rms(xv)) * g_ref[...] * (1.0 + sc_ref[...]) + sh_ref[...]
        h_ref[...] = h.astype(h_ref.dtype)

    row = pl.BlockSpec((tm, D), lambda i: (i, 0))
    vec = pl.BlockSpec((1, D), lambda i: (0, 0))
    return pl.pallas_call(body, name="prenorm", grid=(S // tm,), in_specs=[row, vec, vec, vec], out_specs=row,
                          out_shape=jax.ShapeDtypeStruct((S, D), MXU_DT), compiler_params=_params(("parallel",)))(x, g, scale, shift)


def _prenorm_bwd(dh, x, dx, g, scale):
    S, D = x.shape
    tm = _tile(S, 256)

    def body(dh_ref, x_ref, dx_ref, g_ref, sc_ref, o_ref, dsh_ref, dsc_ref, dg_ref):
        @pl.when(pl.program_id(0) == 0)
        def _():
            dsh_ref[...] = jnp.zeros_like(dsh_ref)
            dsc_ref[...] = jnp.zeros_like(dsc_ref)
            dg_ref[...] = jnp.zeros_like(dg_ref)

        xv = x_ref[...]
        r = _rms(xv)
        xh = xv * r
        dh = dh_ref[...]
        gv = g_ref[...]
        one_sc = 1.0 + sc_ref[...]
        dsh_ref[...] += jnp.sum(dh, axis=0, keepdims=True)
        dsc_ref[...] += jnp.sum(dh * xh * gv, axis=0, keepdims=True)
        dg_ref[...] += jnp.sum(dh * xh * one_sc, axis=0, keepdims=True)
        dxh = dh * gv * one_sc
        o_ref[...] = dx_ref[...] + r * (dxh - xh * jnp.mean(dxh * xh, axis=-1, keepdims=True))

    row = pl.BlockSpec((tm, D), lambda i: (i, 0))
    vec = pl.BlockSpec((1, D), lambda i: (0, 0))
    vs = jax.ShapeDtypeStruct((1, D), F32)
    return pl.pallas_call(body, name="prenorm_bwd", grid=(S // tm,), in_specs=[row, row, row, vec, vec],
                          out_specs=[row, vec, vec, vec], out_shape=[jax.ShapeDtypeStruct((S, D), F32), vs, vs, vs],
                          compiler_params=_params(("arbitrary",)))(dh, x, dx, g, scale)


def _postnorm(x, out, gate, g):
    S, D = x.shape
    tm = _tile(S, 256)

    def body(x_ref, o_ref, gt_ref, g_ref, y_ref):
        ov = o_ref[...]
        y_ref[...] = x_ref[...] + gt_ref[...] * ((ov * _rms(ov)) * g_ref[...])

    row = pl.BlockSpec((tm, D), lambda i: (i, 0))
    vec = pl.BlockSpec((1, D), lambda i: (0, 0))
    return pl.pallas_call(body, name="postnorm", grid=(S // tm,), in_specs=[row, row, vec, vec], out_specs=row,
                          out_shape=jax.ShapeDtypeStruct((S, D), F32), compiler_params=_params(("parallel",)))(x, out, gate, g)


def _postnorm_bwd(dx, out, gate, g):
    S, D = dx.shape
    tm = _tile(S, 256)

    def body(dx_ref, o_ref, gt_ref, g_ref, do_ref, dgt_ref, dg_ref):
        @pl.when(pl.program_id(0) == 0)
        def _():
            dgt_ref[...] = jnp.zeros_like(dgt_ref)
            dg_ref[...] = jnp.zeros_like(dg_ref)

        ov = o_ref[...]
        r = _rms(ov)
        rn = ov * r
        dxv = dx_ref[...]
        gv = g_ref[...]
        dgt_ref[...] += jnp.sum(dxv * rn * gv, axis=0, keepdims=True)
        du = dxv * gt_ref[...]
        dg_ref[...] += jnp.sum(du * rn, axis=0, keepdims=True)
        drn = du * gv
        do_ref[...] = (r * (drn - rn * jnp.mean(drn * rn, axis=-1, keepdims=True))).astype(do_ref.dtype)

    row = pl.BlockSpec((tm, D), lambda i: (i, 0))
    vec = pl.BlockSpec((1, D), lambda i: (0, 0))
    vs = jax.ShapeDtypeStruct((1, D), F32)
    return pl.pallas_call(body, name="postnorm_bwd", grid=(S // tm,), in_specs=[row, row, vec, vec],
                          out_specs=[row, vec, vec], out_shape=[jax.ShapeDtypeStruct((S, D), MXU_DT), vs, vs],
                          compiler_params=_params(("arbitrary",)))(dx, out, gate, g)


def _loss_head(y, target):
    S, D = y.shape
    tm = _tile(S, 256)

    def body(y_ref, t_ref, dy_ref, sq_ref):
        @pl.when(pl.program_id(0) == 0)
        def _():
            sq_ref[...] = jnp.zeros_like(sq_ref)

        e = y_ref[...] - t_ref[...]
        dy_ref[...] = e * (1.0 / D)
        sq_ref[...] += jnp.sum(e * e, axis=0, keepdims=True)

    row = pl.BlockSpec((tm, D), lambda i: (i, 0))
    vec = pl.BlockSpec((1, D), lambda i: (0, 0))
    return pl.pallas_call(body, name="loss_head", grid=(S // tm,), in_specs=[row, row], out_specs=[row, vec],
                          out_shape=[jax.ShapeDtypeStruct((S, D), F32), jax.ShapeDtypeStruct((1, D), F32)],
                          compiler_params=_params(("arbitrary",)))(y, target)


def _matmul(a, b, mode, out_dtype, name, *, b_layer=None, into=None, tm=512, tn=512, tk=2048):
    if mode == "tn":
        K, M = a.shape
    else:
        M, K = a.shape
    bshape = b.shape[-2:]
    N = bshape[0] if mode == "nt" else bshape[1]
    tm, tn, tk = _tile(M, tm), _tile(N, tn), _tile(K, tk)
    nk = K // tk
    dims = {"nn": NN, "nt": NT, "tn": TN}[mode]

    if mode == "tn":
        a_spec = pl.BlockSpec((tk, tm), lambda i, j, k: (k, i))
    else:
        a_spec = pl.BlockSpec((tm, tk), lambda i, j, k: (i, k))
    if mode == "nt":
        bblk, bidx = (tn, tk), (lambda i, j, k: (j, k))
    else:
        bblk, bidx = (tk, tn), (lambda i, j, k: (k, j))
    if b_layer is None:
        b_spec = pl.BlockSpec(bblk, bidx)
    else:
        b_spec = pl.BlockSpec((None,) + bblk, lambda i, j, k: (b_layer,) + bidx(i, j, k))

    def body(a_ref, b_ref, *rest):
        o_ref, acc_ref = rest[-2], rest[-1]
        p = lax.dot_general(a_ref[...], b_ref[...], (dims, ((), ())), preferred_element_type=F32)
        if nk == 1:
            o_ref[...] = p.astype(o_ref.dtype)
        else:
            k = pl.program_id(2)

            @pl.when(k == 0)
            def _():
                acc_ref[...] = p

            @pl.when(k > 0)
            def _():
                acc_ref[...] += p

            @pl.when(k == nk - 1)
            def _():
                o_ref[...] = acc_ref[...].astype(o_ref.dtype)

    in_specs, args, aliases = [a_spec, b_spec], [a, b], {}
    if into is None:
        out_spec = pl.BlockSpec((tm, tn), lambda i, j, k: (i, j))
        out_shape = jax.ShapeDtypeStruct((M, N), out_dtype)
    else:
        buf, layer = into
        in_specs.append(pl.BlockSpec(memory_space=pl.ANY))
        args.append(buf)
        aliases = {2: 0}
        out_spec = pl.BlockSpec((None, tm, tn), lambda i, j, k: (layer, i, j))
        out_shape = jax.ShapeDtypeStruct(buf.shape, buf.dtype)
    return pl.pallas_call(body, name=name, grid=(M // tm, N // tn, nk), in_specs=in_specs, out_specs=out_spec,
                          out_shape=out_shape, scratch_shapes=[pltpu.VMEM((tm, tn), F32)], input_output_aliases=aliases,
                          compiler_params=_params(("parallel", "parallel", "arbitrary")))(*args)


def _merge(ya, yb, wpa, wpb, proj, l):
    S, KW = ya.shape
    D = 2 * KW
    tm, tn = _tile(S, 512), _tile(D, 512)
    ga0, gb0 = 3 * D // tn, 4 * D // tn

    def body(ya_ref, yb_ref, wa_ref, wb_ref, ga_ref, gb_ref, pa_ref, pb_ref, m_ref):
        pa = jnp.dot(ya_ref[...], wa_ref[...], preferred_element_type=F32)
        pb = jnp.dot(yb_ref[...], wb_ref[...], preferred_element_type=F32)
        pa_ref[...] = pa
        pb_ref[...] = pb
        m_ref[...] = (_sig(ga_ref[...]) * pa + _sig(gb_ref[...]) * pb).astype(m_ref.dtype)

    y_spec = pl.BlockSpec((tm, KW), lambda i, j: (i, 0))
    w_spec = pl.BlockSpec((None, KW, tn), lambda i, j: (l, 0, j))
    o_spec = pl.BlockSpec((tm, tn), lambda i, j: (i, j))
    return pl.pallas_call(
        body, name="merge", grid=(S // tm, D // tn),
        in_specs=[y_spec, y_spec, w_spec, w_spec, pl.BlockSpec((tm, tn), lambda i, j: (i, ga0 + j)),
                  pl.BlockSpec((tm, tn), lambda i, j: (i, gb0 + j))],
        out_specs=[o_spec, o_spec, o_spec],
        out_shape=[jax.ShapeDtypeStruct((S, D), F32), jax.ShapeDtypeStruct((S, D), F32), jax.ShapeDtypeStruct((S, D), MXU_DT)],
        compiler_params=_params(("parallel", "parallel")))(ya, yb, wpa, wpb, proj, proj)


def _merge_bwd(dout, w_out, proj, pa, pb, l):
    S, D = dout.shape
    tm, tn = _tile(S, 512), _tile(D, 512)
    ga0, gb0 = 3 * D // tn, 4 * D // tn

    def body(do_ref, w_ref, ga_ref, gb_ref, pa_ref, pb_ref, dpa_ref, dpb_ref, dga_ref, dgb_ref):
        dm = lax.dot_general(do_ref[...], w_ref[...], (NT, ((), ())), preferred_element_type=F32)
        sa, sna = _sig_pair(ga_ref[...])
        sb, snb = _sig_pair(gb_ref[...])
        dpa = dm * sa
        dpb = dm * sb
        dpa_ref[...] = dpa.astype(dpa_ref.dtype)
        dpb_ref[...] = dpb.astype(dpb_ref.dtype)
        dga_ref[...] = (dpa * pa_ref[...] * sna).astype(dga_ref.dtype)
        dgb_ref[...] = (dpb * pb_ref[...] * snb).astype(dgb_ref.dtype)

    blk = pl.BlockSpec((tm, tn), lambda i, j: (i, j))
    os = jax.ShapeDtypeStruct((S, D), MXU_DT)
    return pl.pallas_call(
        body, name="merge_bwd", grid=(S // tm, D // tn),
        in_specs=[pl.BlockSpec((tm, D), lambda i, j: (i, 0)), pl.BlockSpec((None, tn, D), lambda i, j: (l, j, 0)),
                  pl.BlockSpec((tm, tn), lambda i, j: (i, ga0 + j)), pl.BlockSpec((tm, tn), lambda i, j: (i, gb0 + j)), blk, blk],
        out_specs=[blk, blk, blk, blk], out_shape=[os, os, os, os],
        compiler_params=_params(("parallel", "parallel")))(dout, w_out, proj, proj, pa, pb)


def _gates(qr, a, lbv):
    sq = _sig(qr)
    q = qr * sq
    s, sn = _sig_pair(a)
    omlb = 1.0 - lbv
    f = lbv + omlb * s
    logf = jnp.log(jnp.maximum(f, MIN_FORGET))
    kk = omlb * sn
    return sq, q, s, sn, omlb, f, logf, kk


def _tri(lower):
    r = lax.broadcasted_iota(jnp.int32, (CHUNK, CHUNK), 0)
    c = lax.broadcasted_iota(jnp.int32, (CHUNK, CHUNK), 1)
    return jnp.where((r >= c) if lower else (c >= r), 1.0, 0.0).astype(BF16)


def _hgrn_fwd(proj, lb, hn, KW):
    S = proj.shape[0]
    H = KW // HEAD
    T = _tile(S, 512)
    nci, nb = T // CHUNK, S // T

    def body(q_ref, f_ref, v_ref, z_ref, lb_ref, hn_ref, o_ref, y_ref, s0_ref, st_ref):
        @pl.when(pl.program_id(1) == 0)
        def _():
            st_ref[...] = jnp.zeros_like(st_ref)

        lbv = lb_ref[...]
        hnv = hn_ref[...]
        tril = _tri(True)
        ri = lax.broadcasted_iota(jnp.int32, (SUB, 1), 0)

        def chunk(ci, carry):
            rows = pl.ds(pl.multiple_of(ci * CHUNK, CHUNK), CHUNK)
            vv = v_ref[rows, :]
            _, q, _, _, _, _, logf, kk = _gates(q_ref[rows, :], f_ref[rows, :], lbv)
            b = _split_dot(tril, logf)
            eb = jnp.exp(b)
            st = st_ref[...]
            s0_ref[ci] = st
            o_all = _dot(q * eb, st, NT)
            blocks = []
            for i in range(CHUNK // SUB):
                r = SUB * i
                qi, bi = q[r:r + SUB], b[r:r + SUB]
                oi = o_all[r:r + SUB]
                if i > 0:
                    bref = b[r - 1:r]
                    p = _dot(qi * jnp.exp(bi - bref), kk[0:r] * jnp.exp(bref - b[0:r]), NT)
                    oi = oi + _dot(p, vv[0:r], NN)
                for s_ in range(SUB):
                    ks, bs, vs = kk[r + s_:r + s_ + 1], b[r + s_:r + s_ + 1], vv[r + s_:r + s_ + 1]
                    e = jnp.exp(jnp.minimum(bi - bs, 0.0))
                    col = jnp.sum(qi * (ks * e), axis=-1, keepdims=True)
                    oi = oi + jnp.where(ri >= s_, col, 0.0) * vs
                blocks.append(oi)
            o = jnp.concatenate(blocks, axis=0)
            bl = b[CHUNK - 1:CHUNK]
            st_ref[...] = st * eb[CHUNK - 1:CHUNK] + _dot(vv, kk * jnp.exp(bl - b), TN)
            o_ref[rows, :] = o
            z = z_ref[rows, :]
            y_ref[rows, :] = ((o * _rms(o)) * hnv * (z * _sig(z))).astype(y_ref.dtype)
            return carry

        lax.fori_loop(0, nci, chunk, 0)

    def col(off):
        return pl.BlockSpec((T, HEAD), lambda h, t: (t, off + h))

    vec = pl.BlockSpec((1, HEAD), lambda h, t: (0, h))
    return pl.pallas_call(
        body, name="hgrn_fwd", grid=(H, nb),
        in_specs=[col(0), col(H), col(2 * H), col(3 * H), vec, vec],
        out_specs=[pl.BlockSpec((T, HEAD), lambda h, t: (t, h)), pl.BlockSpec((T, HEAD), lambda h, t: (t, h)),
                   pl.BlockSpec((nci, None, HEAD, HEAD), lambda h, t: (t, h, 0, 0))],
        out_shape=[jax.ShapeDtypeStruct((S, KW), F32), jax.ShapeDtypeStruct((S, KW), MXU_DT),
                   jax.ShapeDtypeStruct((S // CHUNK, H, HEAD, HEAD), F32)],
        scratch_shapes=[pltpu.VMEM((HEAD, HEAD), F32)],
        compiler_params=_params(("parallel", "arbitrary")))(proj, proj, proj, proj, lb, hn)


def _gate_a_bwd(dya, o, proj, hn, KW):
    S = o.shape[0]
    H = KW // HEAD
    T = _tile(S, 512)

    def body(dy_ref, o_ref, z_ref, hn_ref, do_ref, dz_ref, dhn_ref):
        @pl.when(pl.program_id(1) == 0)
        def _():
            dhn_ref[...] = jnp.zeros_like(dhn_ref)

        ov = o_ref[...]
        r = _rms(ov)
        rn = ov * r
        z = z_ref[...]
        sz = _sig(z)
        silu = z * sz
        dy = dy_ref[...]
        hnv = hn_ref[...]
        dz_ref[...] = (dy * rn * hnv * (sz * (1.0 + z * (1.0 - sz)))).astype(dz_ref.dtype)
        dhn_ref[...] += jnp.sum(dy * rn * silu, axis=0, keepdims=True)
        drn = dy * hnv * silu
        do_ref[...] = r * (drn - rn * jnp.mean(drn * rn, axis=-1, keepdims=True))

    blk = pl.BlockSpec((T, HEAD), lambda h, t: (t, h))
    vec = pl.BlockSpec((1, HEAD), lambda h, t: (0, h))
    return pl.pallas_call(
        body, name="gate_a_bwd", grid=(H, S // T),
        in_specs=[blk, blk, pl.BlockSpec((T, HEAD), lambda h, t: (t, 3 * H + h)), vec],
        out_specs=[blk, blk, vec],
        out_shape=[jax.ShapeDtypeStruct((S, KW), F32), jax.ShapeDtypeStruct((S, KW), MXU_DT), jax.ShapeDtypeStruct((1, KW), F32)],
        compiler_params=_params(("parallel", "arbitrary")))(dya, o, proj, hn)


def _hgrn_bwd(proj, do, s0, lb, KW):
    S = proj.shape[0]
    H = KW // HEAD
    T = _tile(S, 512)
    nci, nb = T // CHUNK, S // T
    nsub = CHUNK // SUB

    def body(q_ref, f_ref, v_ref, do_ref, s0_ref, lb_ref, dq_ref, df_ref, dv_ref, dlb_ref, dst_ref):
        @pl.when(pl.program_id(1) == 0)
        def _():
            dst_ref[...] = jnp.zeros_like(dst_ref)
            dlb_ref[...] = jnp.zeros_like(dlb_ref)

        lbv = lb_ref[...]
        tril, triu = _tri(True), _tri(False)
        ri = lax.broadcasted_iota(jnp.int32, (SUB, 1), 0)

        def chunk(it, carry):
            ci = nci - 1 - it
            rows = pl.ds(pl.multiple_of(ci * CHUNK, CHUNK), CHUNK)
            qr, a, vv, dov = q_ref[rows, :], f_ref[rows, :], v_ref[rows, :], do_ref[rows, :]
            sq, q, s, sn, omlb, f, logf, kk = _gates(qr, a, lbv)
            b = _split_dot(tril, logf)
            eb = jnp.exp(b)
            st0 = s0_ref[ci]
            dst = dst_ref[...]
            bl, ec = b[CHUNK - 1:CHUNK], eb[CHUNK - 1:CHUNK]
            decl = jnp.exp(bl - b)
            kdec = kk * decl
            dq_all = eb * _dot(dov, st0, NN)
            dst_ref[...] = dst * ec + _dot(dov, q * eb, TN)
            dv_all = _dot(kdec, dst, NT)
            dk_all = decl * _dot(vv, dst, NN)
            stc = st0 * ec + _dot(vv, kdec, TN)
            w = jnp.sum(stc * dst, axis=0, keepdims=True)
            dqb = [dq_all[SUB * i:SUB * (i + 1)] for i in range(nsub)]
            dkb = [dk_all[SUB * i:SUB * (i + 1)] for i in range(nsub)]
            dvb = [dv_all[SUB * i:SUB * (i + 1)] for i in range(nsub)]
            for i in range(nsub):
                r = SUB * i
                qi, bi, doi = q[r:r + SUB], b[r:r + SUB], dov[r:r + SUB]
                if i > 0:
                    bref = b[r - 1:r]
                    ei = jnp.exp(bi - bref)
                    ej = jnp.exp(bref - b[0:r])
                    qd, kd = qi * ei, kk[0:r] * ej
                    p = _dot(qd, kd, NT)
                    dp = _dot(doi, vv[0:r], NT)
                    dqb[i] = dqb[i] + ei * _dot(dp, kd, NN)
                    dkc = ej * _dot(dp, qd, TN)
                    dvc = _dot(p, doi, TN)
                    for j in range(i):
                        dkb[j] = dkb[j] + dkc[SUB * j:SUB * (j + 1)]
                        dvb[j] = dvb[j] + dvc[SUB * j:SUB * (j + 1)]
                dqd = jnp.zeros((SUB, HEAD), F32)
                dkd = jnp.zeros((SUB, HEAD), F32)
                dvd = jnp.zeros((SUB, HEAD), F32)
                for s_ in range(SUB):
                    ks, bs, vs = kk[r + s_:r + s_ + 1], b[r + s_:r + s_ + 1], vv[r + s_:r + s_ + 1]
                    e = jnp.exp(jnp.minimum(bi - bs, 0.0))
                    ke = ks * e
                    m = ri >= s_
                    pcol = jnp.where(m, jnp.sum(qi * ke, axis=-1, keepdims=True), 0.0)
                    dpcol = jnp.where(m, jnp.sum(doi * vs, axis=-1, keepdims=True), 0.0)
                    dqd = dqd + dpcol * ke
                    here = ri == s_
                    dkd = dkd + jnp.where(here, jnp.sum(dpcol * (qi * e), axis=0, keepdims=True), 0.0)
                    dvd = dvd + jnp.where(here, jnp.sum(pcol * doi, axis=0, keepdims=True), 0.0)
                dqb[i] = dqb[i] + dqd
                dkb[i] = dkb[i] + dkd
                dvb[i] = dvb[i] + dvd
            dq_t = jnp.concatenate(dqb, axis=0)
            dk_t = jnp.concatenate(dkb, axis=0)
            dv_t = jnp.concatenate(dvb, axis=0)
            dlogf = _split_dot(triu, q * dq_t - kk * dk_t) + w
            dlf = jnp.where(f > MIN_FORGET, dlogf / jnp.maximum(f, MIN_FORGET), 0.0)
            t1 = dlf - dk_t
            dlb_ref[...] += jnp.sum(sn * t1, axis=0, keepdims=True)
            df_ref[rows, :] = (omlb * (s * sn) * t1).astype(df_ref.dtype)
            dq_ref[rows, :] = (dq_t * (sq * (1.0 + qr * (1.0 - sq)))).astype(dq_ref.dtype)
            dv_ref[rows, :] = dv_t.astype(dv_ref.dtype)
            return carry

        lax.fori_loop(0, nci, chunk, 0)

    def col(off):
        return pl.BlockSpec((T, HEAD), lambda h, t: (nb - 1 - t, off + h))

    blk = pl.BlockSpec((T, HEAD), lambda h, t: (nb - 1 - t, h))
    vec = pl.BlockSpec((1, HEAD), lambda h, t: (0, h))
    os = jax.ShapeDtypeStruct((S, KW), MXU_DT)
    return pl.pallas_call(
        body, name="hgrn_bwd", grid=(H, nb),
        in_specs=[col(0), col(H), col(2 * H), blk, pl.BlockSpec((nci, None, HEAD, HEAD), lambda h, t: (nb - 1 - t, h, 0, 0)), vec],
        out_specs=[blk, blk, blk, vec], out_shape=[os, os, os, jax.ShapeDtypeStruct((1, KW), F32)],
        scratch_shapes=[pltpu.VMEM((HEAD, HEAD), F32)],
        compiler_params=_params(("parallel", "arbitrary")))(proj, proj, proj, do, s0, lb)


def _pool_pos(i, T):
    return (i * T + lax.broadcasted_iota(jnp.int32, (T, 1), 0) + 1).astype(F32)


def _pool_fwd(proj, pool_w, pool_scale, KW, l):
    S = proj.shape[0]
    GW = KW // len(WINDOWS)
    T = _tile(S, 256)

    def body(v_ref, halo_ref, z_ref, pw_ref, ps_ref, p_ref, y_ref):
        i = pl.program_id(0)
        vb = v_ref[...]
        ext = jnp.concatenate([jnp.where(i > 0, halo_ref[...], 0.0), vb], axis=0)
        pos = _pool_pos(i, T)
        z = z_ref[...]
        gate = ps_ref[...] * (z * _sig(z))
        for g, w in enumerate(WINDOWS):
            sl = slice(g * GW, (g + 1) * GW)
            sg = ext[:, sl]
            for jj in range(g + 1):
                sg = sg + pltpu.roll(sg, 1 << jj, axis=0)
            pooled = (sg[HALO:] / jnp.minimum(pos, float(w)) - vb[:, sl]).astype(p_ref.dtype)
            p_ref[:, sl] = pooled
            y_ref[:, sl] = (_dot(pooled, pw_ref[g], NN) * gate[:, sl]).astype(y_ref.dtype)

    row = pl.BlockSpec((T, KW), lambda i: (i, 0))
    hb = T // HALO
    return pl.pallas_call(
        body, name="pool_fwd", grid=(S // T,),
        in_specs=[pl.BlockSpec((T, KW), lambda i: (i, 4)), pl.BlockSpec((HALO, KW), lambda i: (jnp.maximum(i * hb - 1, 0), 4)),
                  pl.BlockSpec((T, KW), lambda i: (i, 5)), pl.BlockSpec((None, len(WINDOWS), GW, GW), lambda i: (l, 0, 0, 0)),
                  pl.BlockSpec((1, KW), lambda i: (0, 0))],
        out_specs=[row, row], out_shape=[jax.ShapeDtypeStruct((S, KW), MXU_DT), jax.ShapeDtypeStruct((S, KW), MXU_DT)],
        compiler_params=_params(("parallel",)))(proj, proj, proj, pool_w, pool_scale)


def _pool_bwd(dyb, proj, pooled, pool_w, pool_scale, KW, l):
    S = proj.shape[0]
    G = len(WINDOWS)
    GW = KW // G
    T = _tile(S, 256)

    def body(dy_ref, z_ref, p_ref, pw_ref, ps_ref, u_ref, dz_ref, dps_ref, dpw_ref):
        i = pl.program_id(0)

        @pl.when(i == 0)
        def _():
            dps_ref[...] = jnp.zeros_like(dps_ref)
            dpw_ref[...] = jnp.zeros_like(dpw_ref)

        pos = _pool_pos(i, T)
        z = z_ref[...]
        sz = _sig(z)
        silu = z * sz
        dsilu = sz * (1.0 + z * (1.0 - sz))
        dy = dy_ref[...]
        ps = ps_ref[...]
        for g, w in enumerate(WINDOWS):
            sl = slice(g * GW, (g + 1) * GW)
            pg = p_ref[:, sl]
            mixed = _dot(pg, pw_ref[g], NN)
            dyg = dy[:, sl]
            dz_ref[:, sl] = (dyg * mixed * ps[:, sl] * dsilu[:, sl]).astype(dz_ref.dtype)
            dps_ref[:, sl] += jnp.sum(dyg * mixed * silu[:, sl], axis=0, keepdims=True)
            dmix = dyg * ps[:, sl] * silu[:, sl]
            dpw_ref[g] += _dot(pg, dmix, TN)
            u_ref[:, sl] = _dot(dmix, pw_ref[g], NT) / jnp.minimum(pos, float(w))

    row = pl.BlockSpec((T, KW), lambda i: (i, 0))
    return pl.pallas_call(
        body, name="pool_bwd", grid=(S // T,),
        in_specs=[row, pl.BlockSpec((T, KW), lambda i: (i, 5)), row,
                  pl.BlockSpec((None, G, GW, GW), lambda i: (l, 0, 0, 0)), pl.BlockSpec((1, KW), lambda i: (0, 0))],
        out_specs=[row, row, pl.BlockSpec((1, KW), lambda i: (0, 0)), pl.BlockSpec((G, GW, GW), lambda i: (0, 0, 0))],
        out_shape=[jax.ShapeDtypeStruct((S, KW), F32), jax.ShapeDtypeStruct((S, KW), MXU_DT),
                   jax.ShapeDtypeStruct((1, KW), F32), jax.ShapeDtypeStruct((G, GW, GW), F32)],
        compiler_params=_params(("arbitrary",)))(dyb, proj, pooled, pool_w, pool_scale)


def _pool_bwd_window(u, KW):
    S = u.shape[0]
    GW = KW // len(WINDOWS)
    T = _tile(S, 256)
    nb = S // T
    n = T + HALO

    def body(u_ref, halo_ref, dv_ref):
        i = pl.program_id(0)
        uv = u_ref[...]
        ext = jnp.concatenate([uv, jnp.where(i < nb - 1, halo_ref[...], 0.0)], axis=0)
        pos = _pool_pos(i, T)
        for g, w in enumerate(WINDOWS):
            sl = slice(g * GW, (g + 1) * GW)
            sg = ext[:, sl]
            for jj in range(g + 1):
                sg = sg + pltpu.roll(sg, n - (1 << jj), axis=0)
            dv_ref[:, sl] = (sg[:T] - uv[:, sl] * jnp.minimum(pos, float(w))).astype(dv_ref.dtype)

    hb = T // HALO
    return pl.pallas_call(
        body, name="pool_bwd_window", grid=(nb,),
        in_specs=[pl.BlockSpec((T, KW), lambda i: (i, 0)),
                  pl.BlockSpec((HALO, KW), lambda i: (jnp.minimum((i + 1) * hb, S // HALO - 1), 0))],
        out_specs=pl.BlockSpec((T, KW), lambda i: (i, 0)), out_shape=jax.ShapeDtypeStruct((S, KW), MXU_DT),
        compiler_params=_params(("parallel",)))(u, u)


def _softmax_rows(x):
    m = jnp.max(x, axis=0, keepdims=True)
    e = jnp.exp(x - m)
    return e / jnp.sum(e, axis=0, keepdims=True)


def _lb_table(lower_bounds):
    L, KW = lower_bounds.shape

    def body(x_ref, o_ref):
        sm = _softmax_rows(x_ref[...])
        acc = jnp.zeros((1, KW), F32)
        o_ref[0:1, :] = acc
        for l in range(1, L):
            acc = acc + sm[l:l + 1]
            o_ref[l:l + 1, :] = acc

    return pl.pallas_call(body, name="lb_table", out_shape=jax.ShapeDtypeStruct((L, KW), F32))(lower_bounds)


def _mod_part(c_all, w_ada):
    L, D, C = w_ada.shape
    B = c_all.shape[0]
    tn = _tile(C, 512)

    def body(c_ref, w_ref, o_ref):
        cv = c_ref[...]
        o_ref[...] = _dot(cv * _sig(cv), w_ref[...], NN)

    return pl.pallas_call(
        body, name="mod_part", grid=(L, C // tn),
        in_specs=[pl.BlockSpec((B, D), lambda l, j: (0, 0)), pl.BlockSpec((None, D, tn), lambda l, j: (l, 0, j))],
        out_specs=pl.BlockSpec((None, B, tn), lambda l, j: (l, 0, j)), out_shape=jax.ShapeDtypeStruct((L, B, C), F32),
        compiler_params=_params(("parallel", "parallel")))(c_all, w_ada)


def _adamw_math(w, g, m, v):
    m = ADAM_B1 * m + (1.0 - ADAM_B1) * g
    v = ADAM_B2 * v + (1.0 - ADAM_B2) * (g * g)
    m_hat = m / (1.0 - ADAM_B1 ** ADAM_STEP)
    v_hat = v / (1.0 - ADAM_B2 ** ADAM_STEP)
    delta = -ADAM_LR * (m_hat / (jnp.sqrt(v_hat) + ADAM_EPS) + ADAM_WD * w)
    return delta, m, v


def _adamw(w, g, m, v, name):
    shape = w.shape
    C = shape[-1]
    R = w.size // C
    tr = _tile(R, 128)
    flat = lambda t: t.reshape(R, C)

    def body(w_ref, g_ref, m_ref, v_ref, d_ref, nm_ref, nv_ref):
        d, nm, nv = _adamw_math(w_ref[...], g_ref[...], m_ref[...], v_ref[...])
        d_ref[...] = d
        nm_ref[...] = nm
        nv_ref[...] = nv

    blk = pl.BlockSpec((tr, C), lambda i: (i, 0))
    os = jax.ShapeDtypeStruct((R, C), F32)
    outs = pl.pallas_call(body, name=name, grid=(R // tr,), in_specs=[blk] * 4, out_specs=[blk] * 3, out_shape=[os] * 3,
                          compiler_params=_params(("parallel",)))(flat(w), flat(g), flat(m), flat(v))
    return [t.reshape(shape) for t in outs]


def _w_ada_update(c_all, dmod, w, m, v):
    L, D, C = w.shape
    B = c_all.shape[0]
    tn = _tile(C, 256)

    def body(c_ref, dm_ref, w_ref, m_ref, v_ref, g_ref, d_ref, nm_ref, nv_ref):
        cv = c_ref[...]
        g = lax.dot_general(cv * _sig(cv), dm_ref[...], (TN, ((), ())), preferred_element_type=F32,
                            precision=lax.Precision.HIGHEST)
        d, nm, nv = _adamw_math(w_ref[...], g, m_ref[...], v_ref[...])
        g_ref[...] = g
        d_ref[...] = d
        nm_ref[...] = nm
        nv_ref[...] = nv

    blk = pl.BlockSpec((None, D, tn), lambda l, j: (l, 0, j))
    os = jax.ShapeDtypeStruct((L, D, C), F32)
    return pl.pallas_call(
        body, name="w_ada_update", grid=(L, C // tn),
        in_specs=[pl.BlockSpec((B, D), lambda l, j: (0, 0)), pl.BlockSpec((None, B, tn), lambda l, j: (l, 0, j)), blk, blk, blk],
        out_specs=[blk] * 4, out_shape=[os] * 4, compiler_params=_params(("parallel", "parallel")))(c_all, dmod, w, m, v)


def _small_update(parts, lower_bounds, wmv, D, KW):
    L = parts.shape[1]
    widths = [3 * D, D, D, KW, KW, KW]
    offs = [sum(widths[:i]) for i in range(len(widths))]

    def body(p_ref, *refs):
        ins, outs = refs[:18], refs[18:]
        tot = p_ref[0]
        for dev in range(1, N_DEV):
            tot = tot + p_ref[dev]
        grads = [tot[:, o:o + wd] for o, wd in zip(offs, widths)]
        sm = _softmax_rows(ins[9][...])
        dlb = grads[3]
        dsm = [jnp.zeros((1, KW), F32)]
        for j in range(1, L):
            acc = dlb[j:j + 1]
            for l in range(j + 1, L):
                acc = acc + dlb[l:l + 1]
            dsm.append(acc)
        dsm = jnp.concatenate(dsm, axis=0)
        grads[3] = sm * (dsm - jnp.sum(sm * dsm, axis=0, keepdims=True))
        for p in range(6):
            w_ref, m_ref, v_ref = ins[3 * p:3 * p + 3]
            d, nm, nv = _adamw_math(w_ref[...], grads[p], m_ref[...], v_ref[...])
            outs[4 * p][...] = grads[p]
            outs[4 * p + 1][...] = d
            outs[4 * p + 2][...] = nm
            outs[4 * p + 3][...] = nv

    flat = [t for trip in wmv for t in trip]
    out_shape = []
    for wd in widths:
        out_shape += [jax.ShapeDtypeStruct((L, wd), F32)] * 4
    res = pl.pallas_call(body, name="small_update", out_shape=out_shape,
                         compiler_params=pltpu.CompilerParams(vmem_limit_bytes=VMEM_LIMIT_BYTES))(parts, *flat)
    return [res[4 * p:4 * p + 4] for p in range(6)]


def _place():
    x, y, c = lax.axis_index("x"), lax.axis_index("y"), lax.axis_index("c")
    chips = [(1 - x, y), (x, 1 - y), (1 - x, 1 - y)]
    return x, y, c, chips


def _all_gather_rows(blk):
    m_per, n = blk.shape

    def body(x_ref, out_ref, send_sems, recv_sems, local_sem):
        x, y, c, chips = _place()
        me, sibling = (x, y, c), (x, y, 1 - c)

        def rows(px, py, pc):
            return out_ref.at[pl.ds((4 * px + 2 * py + pc) * m_per, m_per), :]

        def copy(k, block, to, src=None):
            return pltpu.make_async_remote_copy(src_ref=rows(*block) if src is None else src, dst_ref=rows(*block),
                                                send_sem=send_sems.at[k], recv_sem=recv_sems.at[k], device_id=to, device_id_type=MESH)

        mine = pltpu.make_async_copy(x_ref, rows(*me), local_sem)
        mine.start()
        first = [copy(0, me, sibling, src=x_ref)]
        first += [copy(1 + j, me, (*chip, c), src=x_ref) for j, chip in enumerate(chips)]
        for cp in first:
            cp.start()
        passed = [copy(4 + j, (*chip, c), sibling) for j, chip in enumerate(chips)]
        for j, chip in enumerate(chips):
            copy(1 + j, (*chip, c), me).wait_recv()
            passed[j].start()
        copy(0, sibling, me).wait_recv()
        for j, chip in enumerate(chips):
            copy(4 + j, (*chip, 1 - c), me).wait_recv()
        for cp in first + passed:
            cp.wait_send()
        mine.wait()

    return pl.pallas_call(
        body, name="all_gather_rows", out_shape=jax.ShapeDtypeStruct((N_DEV * m_per, n), blk.dtype),
        in_specs=[pl.BlockSpec(memory_space=pltpu.VMEM)], out_specs=pl.BlockSpec(memory_space=pltpu.VMEM),
        scratch_shapes=[pltpu.SemaphoreType.DMA((7,)), pltpu.SemaphoreType.DMA((7,)), pltpu.SemaphoreType.DMA],
        compiler_params=pltpu.CompilerParams(vmem_limit_bytes=VMEM_LIMIT_BYTES))(blk)


SHARD_AXIS = (2, 2, 2, 2, 1)


def _shard_of(ref, axis, j, n):
    idx = [slice(None)] * len(ref.shape)
    idx[axis] = pl.ds(pl.multiple_of(j * n, n), n)
    return ref.at[tuple(idx)]


def _cast_shards(shards):
    out = []
    for t, w in enumerate(shards):
        C = w.shape[-1]
        R = w.size // C
        tr = _tile(R, 256)

        def body(w_ref, o_ref):
            o_ref[...] = w_ref[...].astype(o_ref.dtype)

        blk = pl.BlockSpec((tr, C), lambda i: (i, 0))
        o = pl.pallas_call(body, name=f"cast_shard{t}", grid=(R // tr,), in_specs=[blk], out_specs=blk,
                           out_shape=jax.ShapeDtypeStruct((R, C), MXU_DT), compiler_params=_params(("parallel",)))(w.reshape(R, C))
        out.append(o.reshape(w.shape))
    return out


def _gather_weights(shards):
    nt = len(shards)
    L = shards[0].shape[0]
    lh = L // 2
    sizes = [s.shape[ax] for s, ax in zip(shards, SHARD_AXIS)]
    full = []
    for s, ax in zip(shards, SHARD_AXIS):
        shp = list(s.shape)
        shp[ax] *= N_CHIPS
        full.append(jax.ShapeDtypeStruct(tuple(shp), s.dtype))

    def body(*refs):
        srcs, outs = refs[:nt], refs[nt:2 * nt]
        send_sems, recv_sems, local_sems = refs[2 * nt:]
        x, y, c, chips = _place()
        j = 2 * x + y
        half = pl.ds(c * lh, lh)
        other = pl.ds((1 - c) * lh, lh)

        def slot(t, jj):
            return _shard_of(outs[t], SHARD_AXIS[t], jj, sizes[t])

        local = [pltpu.make_async_copy(srcs[t], slot(t, j), local_sems.at[t]) for t in range(nt)]
        for cp in local:
            cp.start()

        def ici(k, t, to, jj, src=None):
            dst = slot(t, jj).at[half]
            return pltpu.make_async_remote_copy(src_ref=dst if src is None else src, dst_ref=dst, send_sem=send_sems.at[k * nt + t],
                                                recv_sem=recv_sems.at[k * nt + t], device_id=to, device_id_type=MESH)

        def d2d(k, t, jj, lay):
            ref = slot(t, jj).at[lay]
            return pltpu.make_async_remote_copy(src_ref=ref, dst_ref=ref, send_sem=send_sems.at[(3 + k) * nt + t],
                                                recv_sem=recv_sems.at[(3 + k) * nt + t], device_id=(x, y, 1 - c), device_id_type=MESH)

        first = [ici(k, t, (*chip, c), j, src=srcs[t].at[half]) for k, chip in enumerate(chips) for t in range(nt)]
        for cp in first:
            cp.start()
        passed = []
        for k, (cx, cy) in enumerate(chips):
            for t in range(nt):
                ici(k, t, (x, y, c), 2 * cx + cy).wait_recv()
                cp = d2d(k, t, 2 * cx + cy, half)
                cp.start()
                passed.append(cp)
        for k, (cx, cy) in enumerate(chips):
            for t in range(nt):
                d2d(k, t, 2 * cx + cy, other).wait_recv()
        for cp in first + passed:
            cp.wait_send()
        for cp in local:
            cp.wait()

    any_spec = pl.BlockSpec(memory_space=pl.ANY)
    return pl.pallas_call(
        body, name="gather_weights", out_shape=full, in_specs=[any_spec] * nt, out_specs=[any_spec] * nt,
        scratch_shapes=[pltpu.SemaphoreType.DMA((6 * nt,)), pltpu.SemaphoreType.DMA((6 * nt,)), pltpu.SemaphoreType.DMA((nt,))])(*shards)


def _pair_swap_halves(grads):
    nt = len(grads)
    lh = grads[0].shape[0] // 2

    def body(*refs):
        srcs, outs, send_sems, recv_sems = refs[:nt], refs[nt:2 * nt], refs[2 * nt], refs[2 * nt + 1]
        x, y, c, _ = _place()
        cps = [pltpu.make_async_remote_copy(src_ref=srcs[t].at[pl.ds((1 - c) * lh, lh)], dst_ref=outs[t], send_sem=send_sems.at[t],
                                            recv_sem=recv_sems.at[t], device_id=(x, y, 1 - c), device_id_type=MESH) for t in range(nt)]
        for cp in cps:
            cp.start()
        for cp in cps:
            cp.wait()

    any_spec = pl.BlockSpec(memory_space=pl.ANY)
    return pl.pallas_call(
        body, name="pair_swap_halves", out_shape=[jax.ShapeDtypeStruct((lh,) + g.shape[1:], g.dtype) for g in grads],
        in_specs=[any_spec] * nt, out_specs=[any_spec] * nt,
        scratch_shapes=[pltpu.SemaphoreType.DMA((nt,)), pltpu.SemaphoreType.DMA((nt,))])(*grads)


def _as3d(t):
    return t.reshape(t.shape[0], -1, t.shape[-1])


def _pair_sum(g, recv, c_idx, name):
    lh = recv.shape[0]
    g3, r3 = _as3d(g), _as3d(recv)
    _, R, C = g3.shape
    tr, tc = _tile(R, 256), _tile(C, 2048)

    def body(c_ref, g_ref, r_ref, o_ref):
        o_ref[...] = (g_ref[...] + r_ref[...]).astype(o_ref.dtype)

    grid_spec = pltpu.PrefetchScalarGridSpec(
        num_scalar_prefetch=1, grid=(lh, R // tr, C // tc),
        in_specs=[pl.BlockSpec((None, tr, tc), lambda i, r, cc, c_ref: (c_ref[0] * lh + i, r, cc)),
                  pl.BlockSpec((None, tr, tc), lambda i, r, cc, c_ref: (i, r, cc))],
        out_specs=pl.BlockSpec((None, tr, tc), lambda i, r, cc, c_ref: (i, r, cc)))
    out = pl.pallas_call(body, name=name, grid_spec=grid_spec, out_shape=jax.ShapeDtypeStruct((lh, R, C), BF16),
                         compiler_params=_params(("parallel", "parallel", "parallel")))(c_idx, g3, r3)
    return out.reshape(recv.shape)


def _scatter_partials(parts):
    nt = len(parts)
    sizes = [p.shape[ax] // N_CHIPS for p, ax in zip(parts, SHARD_AXIS)]
    out_shape = []
    for p, ax, n in zip(parts, SHARD_AXIS, sizes):
        shp = list(p.shape)
        shp[ax] = n
        out_shape.append(jax.ShapeDtypeStruct((N_CHIPS,) + tuple(shp), p.dtype))

    def body(*refs):
        srcs, outs = refs[:nt], refs[nt:2 * nt]
        send_sems, recv_sems, local_sems = refs[2 * nt:]
        x, y, c, chips = _place()
        j = 2 * x + y

        def mine_of(t, jj):
            return _shard_of(srcs[t], SHARD_AXIS[t], jj, sizes[t])

        local = [pltpu.make_async_copy(mine_of(t, j), outs[t].at[j], local_sems.at[t]) for t in range(nt)]
        for cp in local:
            cp.start()
        cps = []
        for k, (cx, cy) in enumerate(chips):
            for t in range(nt):
                cps.append((pltpu.make_async_remote_copy(src_ref=mine_of(t, 2 * cx + cy), dst_ref=outs[t].at[j], send_sem=send_sems.at[k * nt + t],
                                                         recv_sem=recv_sems.at[k * nt + t], device_id=(cx, cy, c), device_id_type=MESH),
                            pltpu.make_async_remote_copy(src_ref=mine_of(t, 2 * cx + cy), dst_ref=outs[t].at[2 * cx + cy], send_sem=send_sems.at[k * nt + t],
                                                         recv_sem=recv_sems.at[k * nt + t], device_id=(cx, cy, c), device_id_type=MESH)))
        for snd, _ in cps:
            snd.start()
        for snd, rcv in cps:
            snd.wait_send()
            rcv.wait_recv()
        for cp in local:
            cp.wait()

    any_spec = pl.BlockSpec(memory_space=pl.ANY)
    return pl.pallas_call(
        body, name="scatter_partials", out_shape=out_shape, in_specs=[any_spec] * nt, out_specs=[any_spec] * nt,
        scratch_shapes=[pltpu.SemaphoreType.DMA((3 * nt,)), pltpu.SemaphoreType.DMA((3 * nt,)), pltpu.SemaphoreType.DMA((nt,))])(*parts)


def _sum_chips(slots, name):
    shape = slots.shape[1:]
    s4 = slots.reshape(N_CHIPS, shape[0], -1, shape[-1])
    _, lh, R, C = s4.shape
    tr, tc = _tile(R, 256), _tile(C, 2048)

    def body(s_ref, o_ref):
        acc = s_ref[0].astype(F32)
        for j in range(1, N_CHIPS):
            acc = acc + s_ref[j].astype(F32)
        o_ref[...] = acc

    out = pl.pallas_call(body, name=name, grid=(lh, R // tr, C // tc),
                         in_specs=[pl.BlockSpec((N_CHIPS, None, tr, tc), lambda i, r, cc: (0, i, r, cc))],
                         out_specs=pl.BlockSpec((None, tr, tc), lambda i, r, cc: (i, r, cc)),
                         out_shape=jax.ShapeDtypeStruct((lh, R, C), F32),
                         compiler_params=_params(("parallel", "parallel", "parallel")))(s4)
    return out.reshape(shape)


def _pair_join_halves(halves):
    nt = len(halves)
    lh = halves[0].shape[0]

    def body(*refs):
        srcs, outs = refs[:nt], refs[nt:2 * nt]
        send_sems, recv_sems, local_sems = refs[2 * nt:]
        x, y, c, _ = _place()
        mine = pl.ds(c * lh, lh)
        local = [pltpu.make_async_copy(srcs[t], outs[t].at[mine], local_sems.at[t]) for t in range(nt)]
        for cp in local:
            cp.start()
        snd = [pltpu.make_async_remote_copy(src_ref=srcs[t], dst_ref=outs[t].at[mine], send_sem=send_sems.at[t], recv_sem=recv_sems.at[t],
                                            device_id=(x, y, 1 - c), device_id_type=MESH) for t in range(nt)]
        for cp in snd:
            cp.start()
        for t in range(nt):
            snd[t].wait_send()
            pltpu.make_async_remote_copy(src_ref=srcs[t], dst_ref=outs[t].at[pl.ds((1 - c) * lh, lh)], send_sem=send_sems.at[t],
                                         recv_sem=recv_sems.at[t], device_id=(x, y, 1 - c), device_id_type=MESH).wait_recv()
        for cp in local:
            cp.wait()

    any_spec = pl.BlockSpec(memory_space=pl.ANY)
    return pl.pallas_call(
        body, name="pair_join_halves", out_shape=[jax.ShapeDtypeStruct((2 * lh,) + h.shape[1:], h.dtype) for h in halves],
        in_specs=[any_spec] * nt, out_specs=[any_spec] * nt,
        scratch_shapes=[pltpu.SemaphoreType.DMA((nt,)), pltpu.SemaphoreType.DMA((nt,)), pltpu.SemaphoreType.DMA((nt,))])(*halves)


def _forward_layer(x, l, wg, small, mods, lb_all, KW):
    norm_pre, norm_post, hgrn_norm, pool_scale = small
    shift, scale, gate = mods
    row = lambda t: t[l:l + 1]
    h = _prenorm(x, row(norm_pre), row(scale), row(shift))
    proj = _matmul(h, wg[0], "nn", F32, "proj", b_layer=l)
    o_a, y_a, s0 = _hgrn_fwd(proj, row(lb_all), row(hgrn_norm), KW)
    pooled, y_b = _pool_fwd(proj, wg[1], row(pool_scale), KW, l)
    pa, pb, merged = _merge(y_a, y_b, wg[2], wg[3], proj, l)
    out = _matmul(merged, wg[4], "nn", F32, "out_proj", b_layer=l)
    x_new = _postnorm(x, out, row(gate), row(norm_post))
    saved = (x, h, proj, o_a, y_a, s0, pooled, y_b, pa, pb, merged, out)
    return x_new, saved


def _backward_layer(dx, l, saved, wg, small, mods, lb_all, gbufs, KW):
    norm_pre, norm_post, hgrn_norm, pool_scale = small
    shift, scale, gate = mods
    x, h, proj, o_a, y_a, s0, pooled, y_b, pa, pb, merged, out = saved
    row = lambda t: t[l:l + 1]
    g_in, g_pool, g_pa, g_pb, g_out = gbufs
    dout, d_gate, d_npost = _postnorm_bwd(dx, out, row(gate), row(norm_post))
    g_out = _matmul(merged, dout, "tn", F32, "grad_w_out", into=(g_out, l), tk=1024)
    dpa, dpb, dga, dgb = _merge_bwd(dout, wg[4], proj, pa, pb, l)
    g_pa = _matmul(y_a, dpa, "tn", F32, "grad_w_proj_a", into=(g_pa, l), tk=1024)
    g_pb = _matmul(y_b, dpb, "tn", F32, "grad_w_proj_b", into=(g_pb, l), tk=1024)
    dya = _matmul(dpa, wg[2], "nt", F32, "d_y_a", b_layer=l)
    dyb = _matmul(dpb, wg[3], "nt", F32, "d_y_b", b_layer=l)
    do, dza, d_hn = _gate_a_bwd(dya, o_a, proj, row(hgrn_norm), KW)
    dq, df, dva, d_lb = _hgrn_bwd(proj, do, s0, row(lb_all), KW)
    u, dzb, d_ps, d_pw = _pool_bwd(dyb, proj, pooled, wg[1], row(pool_scale), KW, l)
    dvb = _pool_bwd_window(u, KW)
    g_pool = lax.dynamic_update_slice_in_dim(g_pool, d_pw[None], l, axis=0)
    dproj = jnp.concatenate([dq, df, dva, dza, dvb, dzb, dga, dgb], axis=1)
    g_in = _matmul(h, dproj, "tn", F32, "grad_w_in", into=(g_in, l), tk=1024)
    dh = _matmul(dproj, wg[0], "nt", F32, "d_h", b_layer=l)
    dx_new, d_shift, d_scale, d_npre = _prenorm_bwd(dh, x, dx, row(norm_pre), row(scale))
    small_g = jnp.concatenate([d_shift, d_scale, d_gate, d_npre, d_npost, d_lb, d_hn, d_ps], axis=1)
    return dx_new, small_g, (g_in, g_pool, g_pa, g_pb, g_out)


def kernel(x, c, w_ada, b_ada, norm_pre, norm_post, w_in, lower_bounds, hgrn_norm, pool_w, pool_scale, w_proj_a, w_proj_b, w_out, loss_target, m_w_ada, m_b_ada, m_norm_pre, m_norm_post, m_w_in, m_lower_bounds, m_hgrn_norm, m_pool_w, m_pool_scale, m_w_proj_a, m_w_proj_b, m_w_out, v_w_ada, v_b_ada, v_norm_pre, v_norm_post, v_w_in, v_lower_bounds, v_hgrn_norm, v_pool_w, v_pool_scale, v_w_proj_a, v_w_proj_b, v_w_out):
    _, S, D = x.shape
    L = w_in.shape[0]
    KW = D // 2
    xi, yi, ci = lax.axis_index("x"), lax.axis_index("y"), lax.axis_index("c")
    chip = 2 * xi + yi
    dev = 4 * xi + 2 * yi + ci

    c_all = _all_gather_rows(c.reshape(8, D // 8)).reshape(N_DEV, D)
    modp = _mod_part(c_all, w_ada)
    cols = modp.shape[-1]
    modg = _all_gather_rows(modp.reshape(L * N_DEV, cols)).reshape(N_DEV, L, N_DEV, cols)
    mod_all = jnp.transpose(modg[0::2], (1, 2, 0, 3)).reshape(L, N_DEV, 3 * D)
    mod = lax.dynamic_index_in_dim(mod_all, dev, axis=1, keepdims=False) + b_ada
    mods = (mod[:, :D], mod[:, D:2 * D], mod[:, 2 * D:])
    lb_all = _lb_table(lower_bounds)

    shards = [w_in, pool_w, w_proj_a, w_proj_b, w_out]
    wg = _gather_weights(_cast_shards(shards))
    small = (norm_pre, norm_post, hgrn_norm, pool_scale)

    xs = x[0]
    saved = []
    for l in range(L):
        xs, sv = _forward_layer(xs, l, wg, small, mods, lb_all, KW)
        saved.append(sv)
    dx, sq = _loss_head(xs, loss_target[0])
    loss = lax.psum(0.5 * jnp.sum(sq) / D, ("x", "y", "c"))

    gbufs = tuple(lax.empty(w.shape, F32) for w in wg)
    small_g = [None] * L
    for l in reversed(range(L)):
        dx, small_g[l], gbufs = _backward_layer(dx, l, saved[l], wg, small, mods, lb_all, gbufs, KW)
    grad_x = dx[None]

    sg = jnp.concatenate(small_g, axis=0)
    sg = jnp.concatenate([sg, jnp.zeros((8 - L, sg.shape[1]), F32)], axis=0)
    parts = _all_gather_rows(sg).reshape(N_DEV, 8, sg.shape[1])[:, :L]
    wmv = [(b_ada, m_b_ada, v_b_ada), (norm_pre, m_norm_pre, v_norm_pre), (norm_post, m_norm_post, v_norm_post),
           (lower_bounds, m_lower_bounds, v_lower_bounds), (hgrn_norm, m_hgrn_norm, v_hgrn_norm), (pool_scale, m_pool_scale, v_pool_scale)]
    r_b_ada, r_npre, r_npost, r_lb, r_hn, r_ps = _small_update(parts, lower_bounds, wmv, D, KW)
    dmod = lax.dynamic_slice_in_dim(parts[:, :, :3 * D], chip * cols, cols, axis=2)
    r_w_ada = _w_ada_update(c_all, jnp.transpose(dmod, (1, 0, 2)), w_ada, m_w_ada, v_w_ada)

    c_idx = ci.astype(jnp.int32).reshape(1)
    recv = _pair_swap_halves(list(gbufs))
    pair = [_pair_sum(g, r, c_idx, f"pair_sum{t}") for t, (g, r) in enumerate(zip(gbufs, recv))]
    slots = _scatter_partials(pair)
    halves = [_sum_chips(s, f"sum_chips{t}") for t, s in enumerate(slots)]
    grads = _pair_join_halves(halves)
    ms = [m_w_in, m_pool_w, m_w_proj_a, m_w_proj_b, m_w_out]
    vs = [v_w_in, v_pool_w, v_w_proj_a, v_w_proj_b, v_w_out]
    r_big = [[g] + _adamw(w, g, m, v, f"adamw{t}") for t, (w, g, m, v) in enumerate(zip(shards, grads, ms, vs))]
    r_w_in, r_pool_w, r_pa, r_pb, r_w_out = r_big

    order = [r_w_ada, r_b_ada, r_npre, r_npost, r_w_in, r_lb, r_hn, r_pool_w, r_ps, r_pa, r_pb, r_w_out]
    outs = [loss, grad_x]
    for k in range(4):
        outs += [r[k] for r in order]
    return tuple(outs)
```

```python
import functools

import jax
import jax.numpy as jnp
from jax import lax
from jax.experimental import pallas as pl
from jax.experimental.pallas import tpu as pltpu

F32 = jnp.float32
BF16 = jnp.bfloat16
MXU_DT = jnp.bfloat16

CHUNK = 64
SUB = 16
HEAD = 128
EPS = 1e-6
MIN_FORGET = 1e-30
WINDOWS = (2, 4, 8, 16)
HALO = 16
N_CHIPS = 4
N_DEV = 8
VMEM_LIMIT_BYTES = 56 * 1024 * 1024

ADAM_LR = 0.001
ADAM_B1 = 0.9
ADAM_B2 = 0.999
ADAM_EPS = 1e-08
ADAM_WD = 0.01
ADAM_STEP = 10

NN = ((1,), (0,))
NT = ((1,), (1,))
TN = ((0,), (0,))
MESH = pl.DeviceIdType.MESH


def _params(sem):
    return pltpu.CompilerParams(dimension_semantics=sem, vmem_limit_bytes=VMEM_LIMIT_BYTES)


def _tile(n, pref):
    return pref if n % pref == 0 else n


def _dot(a, b, dims):
    return lax.dot_general(a.astype(MXU_DT), b.astype(MXU_DT), (dims, ((), ())), preferred_element_type=F32)


def _sig_pair(a):
    e = jnp.exp(-jnp.abs(a))
    inv = 1.0 / (1.0 + e)
    pos = a >= 0
    return jnp.where(pos, inv, e * inv), jnp.where(pos, e * inv, inv)


def _sig(a):
    return _sig_pair(a)[0]


def _split_dot(tri, x):
    def top(y):
        return lax.bitcast_convert_type(lax.bitcast_convert_type(y, jnp.uint32) & jnp.uint32(0xFFFF0000), F32)

    hi = top(x)
    r1 = x - hi
    mid = top(r1)
    lo = r1 - mid
    d = lambda y: jnp.dot(tri, y.astype(BF16), preferred_element_type=F32)
    return d(hi) + d(mid) + d(lo)


def _rms(xv):
    return lax.rsqrt(jnp.mean(xv * xv, axis=-1, keepdims=True) + EPS)


def _prenorm(x, g, scale, shift):
    S, D = x.shape
    tm = _tile(S, 256)

    def body(x_ref, g_ref, sc_ref, sh_ref, h_ref):
        xv = x_ref[...]
        h = (xv * _rms(xv)) * g_ref[...] * (1.0 + sc_ref[...]) + sh_ref[...]
        h_ref[...] = h.astype(h_ref.dtype)

    row = pl.BlockSpec((tm, D), lambda i: (i, 0))
    vec = pl.BlockSpec((1, D), lambda i: (0, 0))
    return pl.pallas_call(body, name="prenorm", grid=(S // tm,), in_specs=[row, vec, vec, vec], out_specs=row,
                          out_shape=jax.ShapeDtypeStruct((S, D), MXU_DT), compiler_params=_params(("parallel",)))(x, g, scale, shift)


def _prenorm_bwd(dh, x, dx, g, scale):
    S, D = x.shape
    tm = _tile(S, 256)

    def body(dh_ref, x_ref, dx_ref, g_ref, sc_ref, o_ref, dsh_ref, dsc_ref, dg_ref):
        @pl.when(pl.program_id(0) == 0)
        def _():
            dsh_ref[...] = jnp.zeros_like(dsh_ref)
            dsc_ref[...] = jnp.zeros_like(dsc_ref)
            dg_ref[...] = jnp.zeros_like(dg_ref)

        xv = x_ref[...]
        r = _rms(xv)
        xh = xv * r
        dh = dh_ref[...]
        gv = g_ref[...]
        one_sc = 1.0 + sc_ref[...]
        dsh_ref[...] += jnp.sum(dh, axis=0, keepdims=True)
        dsc_ref[...] += jnp.sum(dh * xh * gv, axis=0, keepdims=True)
        dg_ref[...] += jnp.sum(dh * xh * one_sc, axis=0, keepdims=True)
        dxh = dh * gv * one_sc
        o_ref[...] = dx_ref[...] + r * (dxh - xh * jnp.mean(dxh * xh, axis=-1, keepdims=True))

    row = pl.BlockSpec((tm, D), lambda i: (i, 0))
    vec = pl.BlockSpec((1, D), lambda i: (0, 0))
    vs = jax.ShapeDtypeStruct((1, D), F32)
    return pl.pallas_call(body, name="prenorm_bwd", grid=(S // tm,), in_specs=[row, row, row, vec, vec],
                          out_specs=[row, vec, vec, vec], out_shape=[jax.ShapeDtypeStruct((S, D), F32), vs, vs, vs],
                          compiler_params=_params(("arbitrary",)))(dh, x, dx, g, scale)


def _postnorm(x, out, gate, g):
    S, D = x.shape
    tm = _tile(S, 256)

    def body(x_ref, o_ref, gt_ref, g_ref, y_ref):
        ov = o_ref[...]
        y_ref[...] = x_ref[...] + gt_ref[...] * ((ov * _rms(ov)) * g_ref[...])

    row = pl.BlockSpec((tm, D), lambda i: (i, 0))
    vec = pl.BlockSpec((1, D), lambda i: (0, 0))
    return pl.pallas_call(body, name="postnorm", grid=(S // tm,), in_specs=[row, row, vec, vec], out_specs=row,
                          out_shape=jax.ShapeDtypeStruct((S, D), F32), compiler_params=_params(("parallel",)))(x, out, gate, g)


def _postnorm_bwd(dx, out, gate, g):
    S, D = dx.shape
    tm = _tile(S, 256)

    def body(dx_ref, o_ref, gt_ref, g_ref, do_ref, dgt_ref, dg_ref):
        @pl.when(pl.program_id(0) == 0)
        def _():
            dgt_ref[...] = jnp.zeros_like(dgt_ref)
            dg_ref[...] = jnp.zeros_like(dg_ref)

        ov = o_ref[...]
        r = _rms(ov)
        rn = ov * r
        dxv = dx_ref[...]
        gv = g_ref[...]
        dgt_ref[...] += jnp.sum(dxv * rn * gv, axis=0, keepdims=True)
        du = dxv * gt_ref[...]
        dg_ref[...] += jnp.sum(du * rn, axis=0, keepdims=True)
        drn = du * gv
        do_ref[...] = (r * (drn - rn * jnp.mean(drn * rn, axis=-1, keepdims=True))).astype(do_ref.dtype)

    row = pl.BlockSpec((tm, D), lambda i: (i, 0))
    vec = pl.BlockSpec((1, D), lambda i: (0, 0))
    vs = jax.ShapeDtypeStruct((1, D), F32)
    return pl.pallas_call(body, name="postnorm_bwd", grid=(S // tm,), in_specs=[row, row, vec, vec],
                          out_specs=[row, vec, vec], out_shape=[jax.ShapeDtypeStruct((S, D), MXU_DT), vs, vs],
                          compiler_params=_params(("arbitrary",)))(dx, out, gate, g)


def _loss_head(y, target):
    S, D = y.shape
    tm = _tile(S, 256)

    def body(y_ref, t_ref, dy_ref, sq_ref):
        @pl.when(pl.program_id(0) == 0)
        def _():
            sq_ref[...] = jnp.zeros_like(sq_ref)

        e = y_ref[...] - t_ref[...]
        dy_ref[...] = e * (1.0 / D)
        sq_ref[...] += jnp.sum(e * e, axis=0, keepdims=True)

    row = pl.BlockSpec((tm, D), lambda i: (i, 0))
    vec = pl.BlockSpec((1, D), lambda i: (0, 0))
    return pl.pallas_call(body, name="loss_head", grid=(S // tm,), in_specs=[row, row], out_specs=[row, vec],
                          out_shape=[jax.ShapeDtypeStruct((S, D), F32), jax.ShapeDtypeStruct((1, D), F32)],
                          compiler_params=_params(("arbitrary",)))(y, target)


def _matmul(a, b, mode, out_dtype, name, *, b_layer=None, into=None, tm=1024, tn=1024, tk=2048):
    if mode == "tn":
        K, M = a.shape
    else:
        M, K = a.shape
    bshape = b.shape[-2:]
    N = bshape[0] if mode == "nt" else bshape[1]
    tm, tn, tk = _tile(M, tm), _tile(N, tn), _tile(K, tk)
    nk = K // tk
    dims = {"nn": NN, "nt": NT, "tn": TN}[mode]

    if mode == "tn":
        a_spec = pl.BlockSpec((tk, tm), lambda i, j, k: (k, i))
    else:
        a_spec = pl.BlockSpec((tm, tk), lambda i, j, k: (i, k))
    if mode == "nt":
        bblk, bidx = (tn, tk), (lambda i, j, k: (j, k))
    else:
        bblk, bidx = (tk, tn), (lambda i, j, k: (k, j))
    if b_layer is None:
        b_spec = pl.BlockSpec(bblk, bidx)
    else:
        b_spec = pl.BlockSpec((None,) + bblk, lambda i, j, k: (b_layer,) + bidx(i, j, k))

    def body(a_ref, b_ref, *rest):
        p = lax.dot_general(a_ref[...], b_ref[...], (dims, ((), ())), preferred_element_type=F32)
        if nk == 1:
            o_ref = rest[-1]
            o_ref[...] = p.astype(o_ref.dtype)
        else:
            o_ref, acc_ref = rest[-2], rest[-1]
            k = pl.program_id(2)

            @pl.when(k == 0)
            def _():
                acc_ref[...] = p

            @pl.when(k > 0)
            def _():
                acc_ref[...] += p

            @pl.when(k == nk - 1)
            def _():
                o_ref[...] = acc_ref[...].astype(o_ref.dtype)

    in_specs, args, aliases = [a_spec, b_spec], [a, b], {}
    if into is None:
        out_spec = pl.BlockSpec((tm, tn), lambda i, j, k: (i, j))
        out_shape = jax.ShapeDtypeStruct((M, N), out_dtype)
    else:
        buf, layer = into
        in_specs.append(pl.BlockSpec(memory_space=pl.ANY))
        args.append(buf)
        aliases = {2: 0}
        out_spec = pl.BlockSpec((None, tm, tn), lambda i, j, k: (layer, i, j))
        out_shape = jax.ShapeDtypeStruct(buf.shape, buf.dtype)
    return pl.pallas_call(body, name=name, grid=(M // tm, N // tn, nk), in_specs=in_specs, out_specs=out_spec,
                          out_shape=out_shape, scratch_shapes=[pltpu.VMEM((tm, tn), F32)] if nk > 1 else [], input_output_aliases=aliases,
                          compiler_params=_params(("parallel", "parallel", "arbitrary")))(*args)


def _merge(ya, yb, wpa, wpb, proj, l):
    S, KW = ya.shape
    D = 2 * KW
    tm, tn = _tile(S, 1024), _tile(D, 512)
    ga0, gb0 = 3 * D // tn, 4 * D // tn

    def body(ya_ref, yb_ref, wa_ref, wb_ref, ga_ref, gb_ref, pa_ref, pb_ref, m_ref):
        pa = jnp.dot(ya_ref[...], wa_ref[...], preferred_element_type=F32)
        pb = jnp.dot(yb_ref[...], wb_ref[...], preferred_element_type=F32)
        pa_ref[...] = pa
        pb_ref[...] = pb
        m_ref[...] = (_sig(ga_ref[...]) * pa + _sig(gb_ref[...]) * pb).astype(m_ref.dtype)

    y_spec = pl.BlockSpec((tm, KW), lambda i, j: (i, 0))
    w_spec = pl.BlockSpec((None, KW, tn), lambda i, j: (l, 0, j))
    o_spec = pl.BlockSpec((tm, tn), lambda i, j: (i, j))
    return pl.pallas_call(
        body, name="merge", grid=(S // tm, D // tn),
        in_specs=[y_spec, y_spec, w_spec, w_spec, pl.BlockSpec((tm, tn), lambda i, j: (i, ga0 + j)),
                  pl.BlockSpec((tm, tn), lambda i, j: (i, gb0 + j))],
        out_specs=[o_spec, o_spec, o_spec],
        out_shape=[jax.ShapeDtypeStruct((S, D), F32), jax.ShapeDtypeStruct((S, D), F32), jax.ShapeDtypeStruct((S, D), MXU_DT)],
        compiler_params=_params(("parallel", "parallel")))(ya, yb, wpa, wpb, proj, proj)


def _merge_bwd(dout, w_out, proj, pa, pb, l):
    S, D = dout.shape
    tm, tn = _tile(S, 1024), _tile(D, 512)
    ga0, gb0 = 3 * D // tn, 4 * D // tn

    def body(do_ref, w_ref, ga_ref, gb_ref, pa_ref, pb_ref, dpa_ref, dpb_ref, dga_ref, dgb_ref):
        dm = lax.dot_general(do_ref[...], w_ref[...], (NT, ((), ())), preferred_element_type=F32)
        sa, sna = _sig_pair(ga_ref[...])
        sb, snb = _sig_pair(gb_ref[...])
        dpa = dm * sa
        dpb = dm * sb
        dpa_ref[...] = dpa.astype(dpa_ref.dtype)
        dpb_ref[...] = dpb.astype(dpb_ref.dtype)
        dga_ref[...] = (dpa * pa_ref[...] * sna).astype(dga_ref.dtype)
        dgb_ref[...] = (dpb * pb_ref[...] * snb).astype(dgb_ref.dtype)

    blk = pl.BlockSpec((tm, tn), lambda i, j: (i, j))
    os = jax.ShapeDtypeStruct((S, D), MXU_DT)
    return pl.pallas_call(
        body, name="merge_bwd", grid=(S // tm, D // tn),
        in_specs=[pl.BlockSpec((tm, D), lambda i, j: (i, 0)), pl.BlockSpec((None, tn, D), lambda i, j: (l, j, 0)),
                  pl.BlockSpec((tm, tn), lambda i, j: (i, ga0 + j)), pl.BlockSpec((tm, tn), lambda i, j: (i, gb0 + j)), blk, blk],
        out_specs=[blk, blk, blk, blk], out_shape=[os, os, os, os],
        compiler_params=_params(("parallel", "parallel")))(dout, w_out, proj, proj, pa, pb)


def _gates(qr, a, lbv):
    sq = _sig(qr)
    q = qr * sq
    s, sn = _sig_pair(a)
    omlb = 1.0 - lbv
    f = lbv + omlb * s
    logf = jnp.log(jnp.maximum(f, MIN_FORGET))
    kk = omlb * sn
    return sq, q, s, sn, omlb, f, logf, kk


def _heads_per_step(H, pref):
    return pref if H % pref == 0 else 1


def _tri(lower):
    r = lax.broadcasted_iota(jnp.int32, (CHUNK, CHUNK), 0)
    c = lax.broadcasted_iota(jnp.int32, (CHUNK, CHUNK), 1)
    return jnp.where((r >= c) if lower else (c >= r), 1.0, 0.0).astype(BF16)


def _hgrn_fwd(proj, lb, hn, KW):
    S = proj.shape[0]
    H = KW // HEAD
    T = _tile(S, 512)
    nci, nb = T // CHUNK, S // T
    HP = _heads_per_step(H, 4)

    def body(q_ref, f_ref, v_ref, z_ref, lb_ref, hn_ref, o_ref, y_ref, s0_ref, st_ref):
        @pl.when(pl.program_id(1) == 0)
        def _():
            st_ref[...] = jnp.zeros_like(st_ref)

        tril = _tri(True)
        ri = lax.broadcasted_iota(jnp.int32, (SUB, 1), 0)

        def chunk(ci, carry):
            rows = pl.ds(pl.multiple_of(ci * CHUNK, CHUNK), CHUNK)
            for hp in range(HP):
                one_head(ci, rows, hp, slice(hp * HEAD, (hp + 1) * HEAD))
            return carry

        def one_head(ci, rows, hp, cols):
            lbv = lb_ref[:, cols]
            hnv = hn_ref[:, cols]
            vv = v_ref[rows, cols]
            _, q, _, _, _, _, logf, kk = _gates(q_ref[rows, cols], f_ref[rows, cols], lbv)
            b = _split_dot(tril, logf)
            eb = jnp.exp(b)
            st = st_ref[hp]
            s0_ref[ci, hp] = st
            o_all = _dot(q * eb, st, NT)
            blocks = []
            for i in range(CHUNK // SUB):
                r = SUB * i
                qi, bi = q[r:r + SUB], b[r:r + SUB]
                oi = o_all[r:r + SUB]
                if i > 0:
                    bref = b[r - 1:r]
                    p = _dot(qi * jnp.exp(bi - bref), kk[0:r] * jnp.exp(bref - b[0:r]), NT)
                    oi = oi + _dot(p, vv[0:r], NN)
                for s_ in range(SUB):
                    ks, bs, vs = kk[r + s_:r + s_ + 1], b[r + s_:r + s_ + 1], vv[r + s_:r + s_ + 1]
                    e = jnp.exp(jnp.minimum(bi - bs, 0.0))
                    col = jnp.sum(qi * (ks * e), axis=-1, keepdims=True)
                    oi = oi + jnp.where(ri >= s_, col, 0.0) * vs
                blocks.append(oi)
            o = jnp.concatenate(blocks, axis=0)
            bl = b[CHUNK - 1:CHUNK]
            st_ref[hp] = st * eb[CHUNK - 1:CHUNK] + _dot(vv, kk * jnp.exp(bl - b), TN)
            o_ref[rows, cols] = o
            z = z_ref[rows, cols]
            y_ref[rows, cols] = ((o * _rms(o)) * hnv * (z * _sig(z))).astype(y_ref.dtype)

        lax.fori_loop(0, nci, chunk, 0)

    W = HP * HEAD
    G = H // HP

    def col(off):
        return pl.BlockSpec((T, W), lambda h, t: (t, off + h))

    vec = pl.BlockSpec((1, W), lambda h, t: (0, h))
    return pl.pallas_call(
        body, name="hgrn_fwd", grid=(G, nb),
        in_specs=[col(0), col(G), col(2 * G), col(3 * G), vec, vec],
        out_specs=[pl.BlockSpec((T, W), lambda h, t: (t, h)), pl.BlockSpec((T, W), lambda h, t: (t, h)),
                   pl.BlockSpec((nci, HP, HEAD, HEAD), lambda h, t: (t, h, 0, 0))],
        out_shape=[jax.ShapeDtypeStruct((S, KW), F32), jax.ShapeDtypeStruct((S, KW), MXU_DT),
                   jax.ShapeDtypeStruct((S // CHUNK, H, HEAD, HEAD), F32)],
        scratch_shapes=[pltpu.VMEM((HP, HEAD, HEAD), F32)],
        compiler_params=_params(("parallel", "arbitrary")))(proj, proj, proj, proj, lb, hn)


def _gate_a_bwd(dya, o, proj, hn, KW):
    S = o.shape[0]
    H = KW // HEAD
    T = _tile(S, 512)

    def body(dy_ref, o_ref, z_ref, hn_ref, do_ref, dz_ref, dhn_ref):
        @pl.when(pl.program_id(1) == 0)
        def _():
            dhn_ref[...] = jnp.zeros_like(dhn_ref)

        ov = o_ref[...]
        r = _rms(ov)
        rn = ov * r
        z = z_ref[...]
        sz = _sig(z)
        silu = z * sz
        dy = dy_ref[...]
        hnv = hn_ref[...]
        dz_ref[...] = (dy * rn * hnv * (sz * (1.0 + z * (1.0 - sz)))).astype(dz_ref.dtype)
        dhn_ref[...] += jnp.sum(dy * rn * silu, axis=0, keepdims=True)
        drn = dy * hnv * silu
        do_ref[...] = r * (drn - rn * jnp.mean(drn * rn, axis=-1, keepdims=True))

    blk = pl.BlockSpec((T, HEAD), lambda h, t: (t, h))
    vec = pl.BlockSpec((1, HEAD), lambda h, t: (0, h))
    return pl.pallas_call(
        body, name="gate_a_bwd", grid=(H, S // T),
        in_specs=[blk, blk, pl.BlockSpec((T, HEAD), lambda h, t: (t, 3 * H + h)), vec],
        out_specs=[blk, blk, vec],
        out_shape=[jax.ShapeDtypeStruct((S, KW), F32), jax.ShapeDtypeStruct((S, KW), MXU_DT), jax.ShapeDtypeStruct((1, KW), F32)],
        compiler_params=_params(("parallel", "arbitrary")))(dya, o, proj, hn)


def _hgrn_bwd(proj, do, s0, lb, KW):
    S = proj.shape[0]
    H = KW // HEAD
    T = _tile(S, 512)
    nci, nb = T // CHUNK, S // T
    nsub = CHUNK // SUB
    HP = _heads_per_step(H, 4)

    def body(q_ref, f_ref, v_ref, do_ref, s0_ref, lb_ref, dq_ref, df_ref, dv_ref, dlb_ref, dst_ref):
        @pl.when(pl.program_id(1) == 0)
        def _():
            dst_ref[...] = jnp.zeros_like(dst_ref)
            dlb_ref[...] = jnp.zeros_like(dlb_ref)

        tril, triu = _tri(True), _tri(False)
        ri = lax.broadcasted_iota(jnp.int32, (SUB, 1), 0)

        def chunk(it, carry):
            ci = nci - 1 - it
            rows = pl.ds(pl.multiple_of(ci * CHUNK, CHUNK), CHUNK)
            for hp in range(HP):
                one_head(ci, rows, hp, slice(hp * HEAD, (hp + 1) * HEAD))
            return carry

        def one_head(ci, rows, hp, cols):
            lbv = lb_ref[:, cols]
            qr, a, vv, dov = q_ref[rows, cols], f_ref[rows, cols], v_ref[rows, cols], do_ref[rows, cols]
            sq, q, s, sn, omlb, f, logf, kk = _gates(qr, a, lbv)
            b = _split_dot(tril, logf)
            eb = jnp.exp(b)
            st0 = s0_ref[ci, hp]
            dst = dst_ref[hp]
            bl, ec = b[CHUNK - 1:CHUNK], eb[CHUNK - 1:CHUNK]
            decl = jnp.exp(bl - b)
            kdec = kk * decl
            dq_all = eb * _dot(dov, st0, NN)
            dst_ref[hp] = dst * ec + _dot(dov, q * eb, TN)
            dv_all = _dot(kdec, dst, NT)
            dk_all = decl * _dot(vv, dst, NN)
            stc = st0 * ec + _dot(vv, kdec, TN)
            w = jnp.sum(stc * dst, axis=0, keepdims=True)
            dqb = [dq_all[SUB * i:SUB * (i + 1)] for i in range(nsub)]
            dkb = [dk_all[SUB * i:SUB * (i + 1)] for i in range(nsub)]
            dvb = [dv_all[SUB * i:SUB * (i + 1)] for i in range(nsub)]
            for i in range(nsub):
                r = SUB * i
                qi, bi, doi = q[r:r + SUB], b[r:r + SUB], dov[r:r + SUB]
                if i > 0:
                    bref = b[r - 1:r]
                    ei = jnp.exp(bi - bref)
                    ej = jnp.exp(bref - b[0:r])
                    qd, kd = qi * ei, kk[0:r] * ej
                    p = _dot(qd, kd, NT)
                    dp = _dot(doi, vv[0:r], NT)
                    dqb[i] = dqb[i] + ei * _dot(dp, kd, NN)
                    dkc = ej * _dot(dp, qd, TN)
                    dvc = _dot(p, doi, TN)
                    for j in range(i):
                        dkb[j] = dkb[j] + dkc[SUB * j:SUB * (j + 1)]
                        dvb[j] = dvb[j] + dvc[SUB * j:SUB * (j + 1)]
                dqd = jnp.zeros((SUB, HEAD), F32)
                dkd = jnp.zeros((SUB, HEAD), F32)
                dvd = jnp.zeros((SUB, HEAD), F32)
                for s_ in range(SUB):
                    ks, bs, vs = kk[r + s_:r + s_ + 1], b[r + s_:r + s_ + 1], vv[r + s_:r + s_ + 1]
                    e = jnp.exp(jnp.minimum(bi - bs, 0.0))
                    ke = ks * e
                    m = ri >= s_
                    pcol = jnp.where(m, jnp.sum(qi * ke, axis=-1, keepdims=True), 0.0)
                    dpcol = jnp.where(m, jnp.sum(doi * vs, axis=-1, keepdims=True), 0.0)
                    dqd = dqd + dpcol * ke
                    here = ri == s_
                    dkd = dkd + jnp.where(here, jnp.sum(dpcol * (qi * e), axis=0, keepdims=True), 0.0)
                    dvd = dvd + jnp.where(here, jnp.sum(pcol * doi, axis=0, keepdims=True), 0.0)
                dqb[i] = dqb[i] + dqd
                dkb[i] = dkb[i] + dkd
                dvb[i] = dvb[i] + dvd
            dq_t = jnp.concatenate(dqb, axis=0)
            dk_t = jnp.concatenate(dkb, axis=0)
            dv_t = jnp.concatenate(dvb, axis=0)
            dlogf = _split_dot(triu, q * dq_t - kk * dk_t) + w
            dlf = jnp.where(f > MIN_FORGET, dlogf / jnp.maximum(f, MIN_FORGET), 0.0)
            t1 = dlf - dk_t
            dlb_ref[:, cols] += jnp.sum(sn * t1, axis=0, keepdims=True)
            df_ref[rows, cols] = (omlb * (s * sn) * t1).astype(df_ref.dtype)
            dq_ref[rows, cols] = (dq_t * (sq * (1.0 + qr * (1.0 - sq)))).astype(dq_ref.dtype)
            dv_ref[rows, cols] = dv_t.astype(dv_ref.dtype)

        lax.fori_loop(0, nci, chunk, 0)

    W = HP * HEAD
    G = H // HP

    def col(off):
        return pl.BlockSpec((T, W), lambda h, t: (nb - 1 - t, off + h))

    blk = pl.BlockSpec((T, W), lambda h, t: (nb - 1 - t, h))
    vec = pl.BlockSpec((1, W), lambda h, t: (0, h))
    os = jax.ShapeDtypeStruct((S, KW), MXU_DT)
    return pl.pallas_call(
        body, name="hgrn_bwd", grid=(G, nb),
        in_specs=[col(0), col(G), col(2 * G), blk, pl.BlockSpec((nci, HP, HEAD, HEAD), lambda h, t: (nb - 1 - t, h, 0, 0)), vec],
        out_specs=[blk, blk, blk, vec], out_shape=[os, os, os, jax.ShapeDtypeStruct((1, KW), F32)],
        scratch_shapes=[pltpu.VMEM((HP, HEAD, HEAD), F32)],
        compiler_params=_params(("parallel", "arbitrary")))(proj, proj, proj, do, s0, lb)


def _pool_pos(i, T):
    return (i * T + lax.broadcasted_iota(jnp.int32, (T, 1), 0) + 1).astype(F32)


def _pool_fwd(proj, pool_w, pool_scale, KW, l):
    S = proj.shape[0]
    GW = KW // len(WINDOWS)
    T = _tile(S, 256)

    def body(v_ref, halo_ref, z_ref, pw_ref, ps_ref, p_ref, y_ref):
        i = pl.program_id(0)
        vb = v_ref[...]
        ext = jnp.concatenate([jnp.where(i > 0, halo_ref[...], 0.0), vb], axis=0)
        pos = _pool_pos(i, T)
        z = z_ref[...]
        gate = ps_ref[...] * (z * _sig(z))
        for g, w in enumerate(WINDOWS):
            sl = slice(g * GW, (g + 1) * GW)
            sg = ext[:, sl]
            for jj in range(g + 1):
                sg = sg + pltpu.roll(sg, 1 << jj, axis=0)
            pooled = (sg[HALO:] / jnp.minimum(pos, float(w)) - vb[:, sl]).astype(p_ref.dtype)
            p_ref[:, sl] = pooled
            y_ref[:, sl] = (_dot(pooled, pw_ref[g], NN) * gate[:, sl]).astype(y_ref.dtype)

    row = pl.BlockSpec((T, KW), lambda i: (i, 0))
    hb = T // HALO
    return pl.pallas_call(
        body, name="pool_fwd", grid=(S // T,),
        in_specs=[pl.BlockSpec((T, KW), lambda i: (i, 4)), pl.BlockSpec((HALO, KW), lambda i: (jnp.maximum(i * hb - 1, 0), 4)),
                  pl.BlockSpec((T, KW), lambda i: (i, 5)), pl.BlockSpec((None, len(WINDOWS), GW, GW), lambda i: (l, 0, 0, 0)),
                  pl.BlockSpec((1, KW), lambda i: (0, 0))],
        out_specs=[row, row], out_shape=[jax.ShapeDtypeStruct((S, KW), MXU_DT), jax.ShapeDtypeStruct((S, KW), MXU_DT)],
        compiler_params=_params(("parallel",)))(proj, proj, proj, pool_w, pool_scale)


def _pool_bwd(dyb, proj, pooled, pool_w, pool_scale, KW, l):
    S = proj.shape[0]
    G = len(WINDOWS)
    GW = KW // G
    T = _tile(S, 256)

    def body(dy_ref, z_ref, p_ref, pw_ref, ps_ref, u_ref, dz_ref, dps_ref, dpw_ref):
        i = pl.program_id(0)

        @pl.when(i == 0)
        def _():
            dps_ref[...] = jnp.zeros_like(dps_ref)
            dpw_ref[...] = jnp.zeros_like(dpw_ref)

        pos = _pool_pos(i, T)
        z = z_ref[...]
        sz = _sig(z)
        silu = z * sz
        dsilu = sz * (1.0 + z * (1.0 - sz))
        dy = dy_ref[...]
        ps = ps_ref[...]
        for g, w in enumerate(WINDOWS):
            sl = slice(g * GW, (g + 1) * GW)
            pg = p_ref[:, sl]
            mixed = _dot(pg, pw_ref[g], NN)
            dyg = dy[:, sl]
            dz_ref[:, sl] = (dyg * mixed * ps[:, sl] * dsilu[:, sl]).astype(dz_ref.dtype)
            dps_ref[:, sl] += jnp.sum(dyg * mixed * silu[:, sl], axis=0, keepdims=True)
            dmix = dyg * ps[:, sl] * silu[:, sl]
            dpw_ref[g] += _dot(pg, dmix, TN)
            u_ref[:, sl] = _dot(dmix, pw_ref[g], NT) / jnp.minimum(pos, float(w))

    row = pl.BlockSpec((T, KW), lambda i: (i, 0))
    return pl.pallas_call(
        body, name="pool_bwd", grid=(S // T,),
        in_specs=[row, pl.BlockSpec((T, KW), lambda i: (i, 5)), row,
                  pl.BlockSpec((None, G, GW, GW), lambda i: (l, 0, 0, 0)), pl.BlockSpec((1, KW), lambda i: (0, 0))],
        out_specs=[row, row, pl.BlockSpec((1, KW), lambda i: (0, 0)), pl.BlockSpec((G, GW, GW), lambda i: (0, 0, 0))],
        out_shape=[jax.ShapeDtypeStruct((S, KW), F32), jax.ShapeDtypeStruct((S, KW), MXU_DT),
                   jax.ShapeDtypeStruct((1, KW), F32), jax.ShapeDtypeStruct((G, GW, GW), F32)],
        compiler_params=_params(("arbitrary",)))(dyb, proj, pooled, pool_w, pool_scale)


def _pool_bwd_window(u, KW):
    S = u.shape[0]
    GW = KW // len(WINDOWS)
    T = _tile(S, 256)
    nb = S // T
    n = T + HALO

    def body(u_ref, halo_ref, dv_ref):
        i = pl.program_id(0)
        uv = u_ref[...]
        ext = jnp.concatenate([uv, jnp.where(i < nb - 1, halo_ref[...], 0.0)], axis=0)
        pos = _pool_pos(i, T)
        for g, w in enumerate(WINDOWS):
            sl = slice(g * GW, (g + 1) * GW)
            sg = ext[:, sl]
            for jj in range(g + 1):
                sg = sg + pltpu.roll(sg, n - (1 << jj), axis=0)
            dv_ref[:, sl] = (sg[:T] - uv[:, sl] * jnp.minimum(pos, float(w))).astype(dv_ref.dtype)

    hb = T // HALO
    return pl.pallas_call(
        body, name="pool_bwd_window", grid=(nb,),
        in_specs=[pl.BlockSpec((T, KW), lambda i: (i, 0)),
                  pl.BlockSpec((HALO, KW), lambda i: (jnp.minimum((i + 1) * hb, S // HALO - 1), 0))],
        out_specs=pl.BlockSpec((T, KW), lambda i: (i, 0)), out_shape=jax.ShapeDtypeStruct((S, KW), MXU_DT),
        compiler_params=_params(("parallel",)))(u, u)


def _softmax_rows(x):
    m = jnp.max(x, axis=0, keepdims=True)
    e = jnp.exp(x - m)
    return e / jnp.sum(e, axis=0, keepdims=True)


def _lb_table(lower_bounds):
    L, KW = lower_bounds.shape

    def body(x_ref, o_ref):
        sm = _softmax_rows(x_ref[...])
        acc = jnp.zeros((1, KW), F32)
        o_ref[0:1, :] = acc
        for l in range(1, L):
            acc = acc + sm[l:l + 1]
            o_ref[l:l + 1, :] = acc

    return pl.pallas_call(body, name="lb_table", out_shape=jax.ShapeDtypeStruct((L, KW), F32))(lower_bounds)


def _mod_part(c_all, w_ada):
    L, D, C = w_ada.shape
    B = c_all.shape[0]
    tn = _tile(C, 512)

    def body(c_ref, w_ref, o_ref):
        cv = c_ref[...]
        o_ref[...] = _dot(cv * _sig(cv), w_ref[...], NN)

    return pl.pallas_call(
        body, name="mod_part", grid=(L, C // tn),
        in_specs=[pl.BlockSpec((B, D), lambda l, j: (0, 0)), pl.BlockSpec((None, D, tn), lambda l, j: (l, 0, j))],
        out_specs=pl.BlockSpec((None, B, tn), lambda l, j: (l, 0, j)), out_shape=jax.ShapeDtypeStruct((L, B, C), F32),
        compiler_params=_params(("parallel", "parallel")))(c_all, w_ada)


def _adamw_math(w, g, m, v):
    m = ADAM_B1 * m + (1.0 - ADAM_B1) * g
    v = ADAM_B2 * v + (1.0 - ADAM_B2) * (g * g)
    m_hat = m / (1.0 - ADAM_B1 ** ADAM_STEP)
    v_hat = v / (1.0 - ADAM_B2 ** ADAM_STEP)
    delta = -ADAM_LR * (m_hat / (jnp.sqrt(v_hat) + ADAM_EPS) + ADAM_WD * w)
    return delta, m, v


def _adamw(w, g, m, v, name):
    shape = w.shape
    C = shape[-1]
    R = w.size // C
    tr = _tile(R, 128)
    flat = lambda t: t.reshape(R, C)

    def body(w_ref, g_ref, m_ref, v_ref, d_ref, nm_ref, nv_ref):
        d, nm, nv = _adamw_math(w_ref[...], g_ref[...], m_ref[...], v_ref[...])
        d_ref[...] = d
        nm_ref[...] = nm
        nv_ref[...] = nv

    blk = pl.BlockSpec((tr, C), lambda i: (i, 0))
    os = jax.ShapeDtypeStruct((R, C), F32)
    outs = pl.pallas_call(body, name=name, grid=(R // tr,), in_specs=[blk] * 4, out_specs=[blk] * 3, out_shape=[os] * 3,
                          compiler_params=_params(("parallel",)))(flat(w), flat(g), flat(m), flat(v))
    return [t.reshape(shape) for t in outs]


def _w_ada_update(c_all, dmod, w, m, v):
    L, D, C = w.shape
    B = c_all.shape[0]
    tn = _tile(C, 256)

    def body(c_ref, dm_ref, w_ref, m_ref, v_ref, g_ref, d_ref, nm_ref, nv_ref):
        cv = c_ref[...]
        g = lax.dot_general(cv * _sig(cv), dm_ref[...], (TN, ((), ())), preferred_element_type=F32,
                            precision=lax.Precision.HIGHEST)
        d, nm, nv = _adamw_math(w_ref[...], g, m_ref[...], v_ref[...])
        g_ref[...] = g
        d_ref[...] = d
        nm_ref[...] = nm
        nv_ref[...] = nv

    blk = pl.BlockSpec((None, D, tn), lambda l, j: (l, 0, j))
    os = jax.ShapeDtypeStruct((L, D, C), F32)
    return pl.pallas_call(
        body, name="w_ada_update", grid=(L, C // tn),
        in_specs=[pl.BlockSpec((B, D), lambda l, j: (0, 0)), pl.BlockSpec((None, B, tn), lambda l, j: (l, 0, j)), blk, blk, blk],
        out_specs=[blk] * 4, out_shape=[os] * 4, compiler_params=_params(("parallel", "parallel")))(c_all, dmod, w, m, v)


def _small_update(parts, lower_bounds, wmv, D, KW):
    L = parts.shape[1]
    widths = [3 * D, D, D, KW, KW, KW]
    offs = [sum(widths[:i]) for i in range(len(widths))]

    def body(p_ref, *refs):
        ins, outs = refs[:18], refs[18:]
        tot = p_ref[0]
        for dev in range(1, N_DEV):
            tot = tot + p_ref[dev]
        grads = [tot[:, o:o + wd] for o, wd in zip(offs, widths)]
        sm = _softmax_rows(ins[9][...])
        dlb = grads[3]
        dsm = [jnp.zeros((1, KW), F32)]
        for j in range(1, L):
            acc = dlb[j:j + 1]
            for l in range(j + 1, L):
                acc = acc + dlb[l:l + 1]
            dsm.append(acc)
        dsm = jnp.concatenate(dsm, axis=0)
        grads[3] = sm * (dsm - jnp.sum(sm * dsm, axis=0, keepdims=True))
        for p in range(6):
            w_ref, m_ref, v_ref = ins[3 * p:3 * p + 3]
            d, nm, nv = _adamw_math(w_ref[...], grads[p], m_ref[...], v_ref[...])
            outs[4 * p][...] = grads[p]
            outs[4 * p + 1][...] = d
            outs[4 * p + 2][...] = nm
            outs[4 * p + 3][...] = nv

    flat = [t for trip in wmv for t in trip]
    out_shape = []
    for wd in widths:
        out_shape += [jax.ShapeDtypeStruct((L, wd), F32)] * 4
    res = pl.pallas_call(body, name="small_update", out_shape=out_shape,
                         compiler_params=pltpu.CompilerParams(vmem_limit_bytes=VMEM_LIMIT_BYTES))(parts, *flat)
    return [res[4 * p:4 * p + 4] for p in range(6)]


def _place():
    x, y, c = lax.axis_index("x"), lax.axis_index("y"), lax.axis_index("c")
    chips = [(1 - x, y), (x, 1 - y), (1 - x, 1 - y)]
    return x, y, c, chips


def _all_gather_rows(blk):
    m_per, n = blk.shape

    def body(x_ref, out_ref, send_sems, recv_sems, local_sem):
        x, y, c, chips = _place()
        me, sibling = (x, y, c), (x, y, 1 - c)

        def rows(px, py, pc):
            return out_ref.at[pl.ds((4 * px + 2 * py + pc) * m_per, m_per), :]

        def copy(k, block, to, src=None):
            return pltpu.make_async_remote_copy(src_ref=rows(*block) if src is None else src, dst_ref=rows(*block),
                                                send_sem=send_sems.at[k], recv_sem=recv_sems.at[k], device_id=to, device_id_type=MESH)

        mine = pltpu.make_async_copy(x_ref, rows(*me), local_sem)
        mine.start()
        first = [copy(0, me, sibling, src=x_ref)]
        first += [copy(1 + j, me, (*chip, c), src=x_ref) for j, chip in enumerate(chips)]
        for cp in first:
            cp.start()
        passed = [copy(4 + j, (*chip, c), sibling) for j, chip in enumerate(chips)]
        for j, chip in enumerate(chips):
            copy(1 + j, (*chip, c), me).wait_recv()
            passed[j].start()
        copy(0, sibling, me).wait_recv()
        for j, chip in enumerate(chips):
            copy(4 + j, (*chip, 1 - c), me).wait_recv()
        for cp in first + passed:
            cp.wait_send()
        mine.wait()

    return pl.pallas_call(
        body, name="all_gather_rows", out_shape=jax.ShapeDtypeStruct((N_DEV * m_per, n), blk.dtype),
        in_specs=[pl.BlockSpec(memory_space=pltpu.VMEM)], out_specs=pl.BlockSpec(memory_space=pltpu.VMEM),
        scratch_shapes=[pltpu.SemaphoreType.DMA((7,)), pltpu.SemaphoreType.DMA((7,)), pltpu.SemaphoreType.DMA],
        compiler_params=pltpu.CompilerParams(vmem_limit_bytes=VMEM_LIMIT_BYTES))(blk)


SHARD_AXIS = (2, 2, 2, 2, 1)


def _shard_of(ref, axis, j, n):
    idx = [slice(None)] * len(ref.shape)
    idx[axis] = pl.ds(pl.multiple_of(j * n, n), n)
    return ref.at[tuple(idx)]


SHARD_LAST = (True, False, True, True, False)


def _rows_view(t):
    return t.reshape((-1,) + t.shape[-2:])


def _cast_into_full(shards, chip_idx):
    out = []
    for t, w in enumerate(shards):
        w3 = _rows_view(w)
        B, R, C = w3.shape
        tr = _tile(R, 256)
        nr = R // tr
        last = SHARD_LAST[t]

        def body(j_ref, w_ref, o_ref):
            o_ref[...] = w_ref[...].astype(o_ref.dtype)

        if last:
            full3 = (B, R, N_CHIPS * C)
            o_idx = lambda b, r, j_ref: (b, r, j_ref[0])
        else:
            full3 = (B, N_CHIPS * R, C)
            o_idx = lambda b, r, j_ref, nr=nr: (b, j_ref[0] * nr + r, 0)
        grid_spec = pltpu.PrefetchScalarGridSpec(
            num_scalar_prefetch=1, grid=(B, nr),
            in_specs=[pl.BlockSpec((None, tr, C), lambda b, r, j_ref: (b, r, 0))],
            out_specs=pl.BlockSpec((None, tr, C), o_idx))
        o = pl.pallas_call(body, name=f"cast_into_full{t}", grid_spec=grid_spec, out_shape=jax.ShapeDtypeStruct(full3, MXU_DT),
                           compiler_params=_params(("parallel", "parallel")))(chip_idx, w3)
        shp = list(w.shape)
        shp[SHARD_AXIS[t]] *= N_CHIPS
        out.append(o.reshape(shp))
    return out


def _gather_weights(fulls):
    nt = len(fulls)
    lh = fulls[0].shape[0] // 2
    sizes = [f.shape[ax] // N_CHIPS for f, ax in zip(fulls, SHARD_AXIS)]

    def body(*refs):
        outs = refs[nt:2 * nt]
        send_sems, recv_sems = refs[2 * nt:]
        x, y, c, chips = _place()
        j = 2 * x + y
        half = pl.ds(c * lh, lh)
        other = pl.ds((1 - c) * lh, lh)

        def slot(t, jj):
            return _shard_of(outs[t], SHARD_AXIS[t], jj, sizes[t])

        def ici(k, t, to, jj):
            ref = slot(t, jj).at[half]
            return pltpu.make_async_remote_copy(src_ref=ref, dst_ref=ref, send_sem=send_sems.at[k * nt + t],
                                                recv_sem=recv_sems.at[k * nt + t], device_id=to, device_id_type=MESH)

        def d2d(k, t, jj, lay):
            ref = slot(t, jj).at[lay]
            return pltpu.make_async_remote_copy(src_ref=ref, dst_ref=ref, send_sem=send_sems.at[(3 + k) * nt + t],
                                                recv_sem=recv_sems.at[(3 + k) * nt + t], device_id=(x, y, 1 - c), device_id_type=MESH)

        first = [ici(k, t, (*chip, c), j) for k, chip in enumerate(chips) for t in range(nt)]
        for cp in first:
            cp.start()
        passed = []
        for k, (cx, cy) in enumerate(chips):
            for t in range(nt):
                ici(k, t, (x, y, c), 2 * cx + cy).wait_recv()
                cp = d2d(k, t, 2 * cx + cy, half)
                cp.start()
                passed.append(cp)
        for k, (cx, cy) in enumerate(chips):
            for t in range(nt):
                d2d(k, t, 2 * cx + cy, other).wait_recv()
        for cp in first + passed:
            cp.wait_send()

    any_spec = pl.BlockSpec(memory_space=pl.ANY)
    return pl.pallas_call(
        body, name="gather_weights", out_shape=[jax.ShapeDtypeStruct(f.shape, f.dtype) for f in fulls],
        in_specs=[any_spec] * nt, out_specs=[any_spec] * nt, input_output_aliases={t: t for t in range(nt)},
        scratch_shapes=[pltpu.SemaphoreType.DMA((6 * nt,)), pltpu.SemaphoreType.DMA((6 * nt,))])(*fulls)


def _pair_swap_halves(grads):
    nt = len(grads)
    lh = grads[0].shape[0] // 2

    def body(*refs):
        srcs, outs, send_sems, recv_sems = refs[:nt], refs[nt:2 * nt], refs[2 * nt], refs[2 * nt + 1]
        x, y, c, _ = _place()
        cps = [pltpu.make_async_remote_copy(src_ref=srcs[t].at[pl.ds((1 - c) * lh, lh)], dst_ref=outs[t], send_sem=send_sems.at[t],
                                            recv_sem=recv_sems.at[t], device_id=(x, y, 1 - c), device_id_type=MESH) for t in range(nt)]
        for cp in cps:
            cp.start()
        for cp in cps:
            cp.wait()

    any_spec = pl.BlockSpec(memory_space=pl.ANY)
    return pl.pallas_call(
        body, name="pair_swap_halves", out_shape=[jax.ShapeDtypeStruct((lh,) + g.shape[1:], g.dtype) for g in grads],
        in_specs=[any_spec] * nt, out_specs=[any_spec] * nt,
        scratch_shapes=[pltpu.SemaphoreType.DMA((nt,)), pltpu.SemaphoreType.DMA((nt,))])(*grads)


def _as3d(t):
    return t.reshape(t.shape[0], -1, t.shape[-1])


def _pair_sum(g, recv, c_idx, name):
    lh = recv.shape[0]
    g3, r3 = _as3d(g), _as3d(recv)
    _, R, C = g3.shape
    tr, tc = _tile(R, 256), _tile(C, 2048)

    def body(c_ref, g_ref, r_ref, o_ref):
        o_ref[...] = (g_ref[...] + r_ref[...]).astype(o_ref.dtype)

    grid_spec = pltpu.PrefetchScalarGridSpec(
        num_scalar_prefetch=1, grid=(lh, R // tr, C // tc),
        in_specs=[pl.BlockSpec((None, tr, tc), lambda i, r, cc, c_ref: (c_ref[0] * lh + i, r, cc)),
                  pl.BlockSpec((None, tr, tc), lambda i, r, cc, c_ref: (i, r, cc))],
        out_specs=pl.BlockSpec((None, tr, tc), lambda i, r, cc, c_ref: (i, r, cc)))
    out = pl.pallas_call(body, name=name, grid_spec=grid_spec, out_shape=jax.ShapeDtypeStruct((lh, R, C), BF16),
                         compiler_params=_params(("parallel", "parallel", "parallel")))(c_idx, g3, r3)
    return out.reshape(recv.shape)


def _scatter_partials(parts):
    nt = len(parts)
    sizes = [p.shape[ax] // N_CHIPS for p, ax in zip(parts, SHARD_AXIS)]
    out_shape = []
    for p, ax, n in zip(parts, SHARD_AXIS, sizes):
        shp = list(p.shape)
        shp[ax] = n
        out_shape.append(jax.ShapeDtypeStruct((N_CHIPS - 1,) + tuple(shp), p.dtype))

    def body(*refs):
        srcs, outs = refs[:nt], refs[nt:2 * nt]
        send_sems, recv_sems = refs[2 * nt:]
        x, y, c, chips = _place()
        cps = [pltpu.make_async_remote_copy(src_ref=_shard_of(srcs[t], SHARD_AXIS[t], 2 * cx + cy, sizes[t]), dst_ref=outs[t].at[k],
                                            send_sem=send_sems.at[k * nt + t], recv_sem=recv_sems.at[k * nt + t],
                                            device_id=(cx, cy, c), device_id_type=MESH)
               for k, (cx, cy) in enumerate(chips) for t in range(nt)]
        for cp in cps:
            cp.start()
        for cp in cps:
            cp.wait()

    any_spec = pl.BlockSpec(memory_space=pl.ANY)
    return pl.pallas_call(
        body, name="scatter_partials", out_shape=out_shape, in_specs=[any_spec] * nt, out_specs=[any_spec] * nt,
        scratch_shapes=[pltpu.SemaphoreType.DMA((3 * nt,)), pltpu.SemaphoreType.DMA((3 * nt,))])(*parts)


def _sum_chips(pair, slots, t, L, chip_idx, c_idx):
    own3 = _rows_view(pair)
    s4 = slots.reshape((N_CHIPS - 1, -1) + slots.shape[-2:])
    _, B, R, C = s4.shape
    tr = _tile(R, 256)
    nr = R // tr

    def body(j_ref, c_ref, own_ref, s_ref, o_ref):
        acc = own_ref[...].astype(F32)
        for k in range(N_CHIPS - 1):
            acc = acc + s_ref[k].astype(F32)
        o_ref[...] = acc

    if SHARD_LAST[t]:
        own_idx = lambda b, r, j_ref, c_ref: (b, r, j_ref[0])
    else:
        own_idx = lambda b, r, j_ref, c_ref: (b, j_ref[0] * nr + r, 0)
    grid_spec = pltpu.PrefetchScalarGridSpec(
        num_scalar_prefetch=2, grid=(B, nr),
        in_specs=[pl.BlockSpec((None, tr, C), own_idx),
                  pl.BlockSpec((N_CHIPS - 1, None, tr, C), lambda b, r, j_ref, c_ref: (0, b, r, 0))],
        out_specs=pl.BlockSpec((None, tr, C), lambda b, r, j_ref, c_ref: (c_ref[0] * B + b, r, 0)))
    out = pl.pallas_call(body, name=f"sum_chips{t}", grid_spec=grid_spec, out_shape=jax.ShapeDtypeStruct((2 * B, R, C), F32),
                         compiler_params=_params(("parallel", "parallel")))(chip_idx, c_idx, own3, s4)
    return out.reshape((L,) + slots.shape[2:])


def _pair_join_halves(stacks):
    nt = len(stacks)
    lh = stacks[0].shape[0] // 2

    def body(*refs):
        outs = refs[nt:2 * nt]
        send_sems, recv_sems = refs[2 * nt:]
        x, y, c, _ = _place()

        def cp(t, lay):
            ref = outs[t].at[pl.ds(lay * lh, lh)]
            return pltpu.make_async_remote_copy(src_ref=ref, dst_ref=ref, send_sem=send_sems.at[t], recv_sem=recv_sems.at[t],
                                                device_id=(x, y, 1 - c), device_id_type=MESH)

        for t in range(nt):
            cp(t, c).start()
        for t in range(nt):
            cp(t, c).wait_send()
            cp(t, 1 - c).wait_recv()

    any_spec = pl.BlockSpec(memory_space=pl.ANY)
    return pl.pallas_call(
        body, name="pair_join_halves", out_shape=[jax.ShapeDtypeStruct(s.shape, s.dtype) for s in stacks],
        in_specs=[any_spec] * nt, out_specs=[any_spec] * nt, input_output_aliases={t: t for t in range(nt)},
        scratch_shapes=[pltpu.SemaphoreType.DMA((nt,)), pltpu.SemaphoreType.DMA((nt,))])(*stacks)


def _forward_layer(x, l, wg, small, mods, lb_all, KW):
    norm_pre, norm_post, hgrn_norm, pool_scale = small
    shift, scale, gate = mods
    row = lambda t: t[l:l + 1]
    h = _prenorm(x, row(norm_pre), row(scale), row(shift))
    proj = _matmul(h, wg[0], "nn", F32, "proj", b_layer=l)
    o_a, y_a, s0 = _hgrn_fwd(proj, row(lb_all), row(hgrn_norm), KW)
    pooled, y_b = _pool_fwd(proj, wg[1], row(pool_scale), KW, l)
    pa, pb, merged = _merge(y_a, y_b, wg[2], wg[3], proj, l)
    out = _matmul(merged, wg[4], "nn", F32, "out_proj", b_layer=l)
    x_new = _postnorm(x, out, row(gate), row(norm_post))
    saved = (x, h, proj, o_a, y_a, s0, pooled, y_b, pa, pb, merged, out)
    return x_new, saved


def _backward_layer(dx, l, saved, wg, small, mods, lb_all, gbufs, KW):
    norm_pre, norm_post, hgrn_norm, pool_scale = small
    shift, scale, gate = mods
    x, h, proj, o_a, y_a, s0, pooled, y_b, pa, pb, merged, out = saved
    row = lambda t: t[l:l + 1]
    g_in, g_pool, g_pa, g_pb, g_out = gbufs
    dout, d_gate, d_npost = _postnorm_bwd(dx, out, row(gate), row(norm_post))
    g_out = _matmul(merged, dout, "tn", F32, "grad_w_out", into=(g_out, l), tk=1024)
    dpa, dpb, dga, dgb = _merge_bwd(dout, wg[4], proj, pa, pb, l)
    g_pa = _matmul(y_a, dpa, "tn", F32, "grad_w_proj_a", into=(g_pa, l), tk=1024)
    g_pb = _matmul(y_b, dpb, "tn", F32, "grad_w_proj_b", into=(g_pb, l), tk=1024)
    dya = _matmul(dpa, wg[2], "nt", F32, "d_y_a", b_layer=l)
    dyb = _matmul(dpb, wg[3], "nt", F32, "d_y_b", b_layer=l)
    do, dza, d_hn = _gate_a_bwd(dya, o_a, proj, row(hgrn_norm), KW)
    dq, df, dva, d_lb = _hgrn_bwd(proj, do, s0, row(lb_all), KW)
    u, dzb, d_ps, d_pw = _pool_bwd(dyb, proj, pooled, wg[1], row(pool_scale), KW, l)
    dvb = _pool_bwd_window(u, KW)
    g_pool = lax.dynamic_update_slice_in_dim(g_pool, d_pw[None], l, axis=0)
    dproj = jnp.concatenate([dq, df, dva, dza, dvb, dzb, dga, dgb], axis=1)
    g_in = _matmul(h, dproj, "tn", F32, "grad_w_in", into=(g_in, l), tk=1024)
    dh = _matmul(dproj, wg[0], "nt", F32, "d_h", b_layer=l)
    dx_new, d_shift, d_scale, d_npre = _prenorm_bwd(dh, x, dx, row(norm_pre), row(scale))
    small_g = jnp.concatenate([d_shift, d_scale, d_gate, d_npre, d_npost, d_lb, d_hn, d_ps], axis=1)
    return dx_new, small_g, (g_in, g_pool, g_pa, g_pb, g_out)


def kernel(x, c, w_ada, b_ada, norm_pre, norm_post, w_in, lower_bounds, hgrn_norm, pool_w, pool_scale, w_proj_a, w_proj_b, w_out, loss_target, m_w_ada, m_b_ada, m_norm_pre, m_norm_post, m_w_in, m_lower_bounds, m_hgrn_norm, m_pool_w, m_pool_scale, m_w_proj_a, m_w_proj_b, m_w_out, v_w_ada, v_b_ada, v_norm_pre, v_norm_post, v_w_in, v_lower_bounds, v_hgrn_norm, v_pool_w, v_pool_scale, v_w_proj_a, v_w_proj_b, v_w_out):
    _, S, D = x.shape
    L = w_in.shape[0]
    KW = D // 2
    xi, yi, ci = lax.axis_index("x"), lax.axis_index("y"), lax.axis_index("c")
    chip = 2 * xi + yi
    dev = 4 * xi + 2 * yi + ci

    c_all = _all_gather_rows(c.reshape(8, D // 8)).reshape(N_DEV, D)
    modp = _mod_part(c_all, w_ada)
    cols = modp.shape[-1]
    modg = _all_gather_rows(modp.reshape(L * N_DEV, cols)).reshape(N_DEV, L, N_DEV, cols)
    mod_all = jnp.transpose(modg[0::2], (1, 2, 0, 3)).reshape(L, N_DEV, 3 * D)
    mod = lax.dynamic_index_in_dim(mod_all, dev, axis=1, keepdims=False) + b_ada
    mods = (mod[:, :D], mod[:, D:2 * D], mod[:, 2 * D:])
    lb_all = _lb_table(lower_bounds)

    shards = [w_in, pool_w, w_proj_a, w_proj_b, w_out]
    wg = _gather_weights(_cast_into_full(shards, chip.astype(jnp.int32).reshape(1)))
    small = (norm_pre, norm_post, hgrn_norm, pool_scale)

    xs = x[0]
    saved = []
    for l in range(L):
        xs, sv = _forward_layer(xs, l, wg, small, mods, lb_all, KW)
        saved.append(sv)
    dx, sq = _loss_head(xs, loss_target[0])
    loss = lax.psum(0.5 * jnp.sum(sq) / D, ("x", "y", "c"))

    gbufs = tuple(lax.empty(w.shape, F32) for w in wg)
    small_g = [None] * L
    for l in reversed(range(L)):
        dx, small_g[l], gbufs = _backward_layer(dx, l, saved[l], wg, small, mods, lb_all, gbufs, KW)
    grad_x = dx[None]

    sg = jnp.concatenate(small_g, axis=0)
    sg = jnp.concatenate([sg, jnp.zeros((8 - L, sg.shape[1]), F32)], axis=0)
    parts = _all_gather_rows(sg).reshape(N_DEV, 8, sg.shape[1])[:, :L]
    wmv = [(b_ada, m_b_ada, v_b_ada), (norm_pre, m_norm_pre, v_norm_pre), (norm_post, m_norm_post, v_norm_post),
           (lower_bounds, m_lower_bounds, v_lower_bounds), (hgrn_norm, m_hgrn_norm, v_hgrn_norm), (pool_scale, m_pool_scale, v_pool_scale)]
    r_b_ada, r_npre, r_npost, r_lb, r_hn, r_ps = _small_update(parts, lower_bounds, wmv, D, KW)
    dmod = lax.dynamic_slice_in_dim(parts[:, :, :3 * D], chip * cols, cols, axis=2)
    r_w_ada = _w_ada_update(c_all, jnp.transpose(dmod, (1, 0, 2)), w_ada, m_w_ada, v_w_ada)

    c_idx = ci.astype(jnp.int32).reshape(1)
    recv = _pair_swap_halves(list(gbufs))
    pair = [_pair_sum(g, r, c_idx, f"pair_sum{t}") for t, (g, r) in enumerate(zip(gbufs, recv))]
    slots = _scatter_partials(pair)
    chip_idx = chip.astype(jnp.int32).reshape(1)
    grads = _pair_join_halves([_sum_chips(p, s, t, L, chip_idx, c_idx) for t, (p, s) in enumerate(zip(pair, slots))])
    ms = [m_w_in, m_pool_w, m_w_proj_a, m_w_proj_b, m_w_out]
    vs = [v_w_in, v_pool_w, v_w_proj_a, v_w_proj_b, v_w_out]
    r_big = [[g] + _adamw(w, g, m, v, f"adamw{t}") for t, (w, g, m, v) in enumerate(zip(shards, grads, ms, vs))]
    r_w_in, r_pool_w, r_pa, r_pb, r_w_out = r_big

    order = [r_w_ada, r_b_ada, r_npre, r_npost, r_w_in, r_lb, r_hn, r_pool_w, r_ps, r_pa, r_pb, r_w_out]
    outs = [loss, grad_x]
    for k in range(4):
        outs += [r[k] for r in order]
    return tuple(outs)
```

```python
import functools

import jax
import jax.numpy as jnp
from jax import lax
from jax.experimental import pallas as pl
from jax.experimental.pallas import tpu as pltpu

F32 = jnp.float32
BF16 = jnp.bfloat16
MXU_DT = jnp.bfloat16

CHUNK = 64
SUB = 8
NBLK = CHUNK // SUB
LEVELS = (32, 16, 8)
HEAD = 128
EPS = 1e-6
MIN_FORGET = 1e-30
WINDOWS = (2, 4, 8, 16)
HALO = 16
N_CHIPS = 4
N_DEV = 8
VMEM_LIMIT_BYTES = 56 * 1024 * 1024

ADAM_LR = 0.001
ADAM_B1 = 0.9
ADAM_B2 = 0.999
ADAM_EPS = 1e-08
ADAM_WD = 0.01
ADAM_STEP = 10

NN = ((1,), (0,))
NT = ((1,), (1,))
TN = ((0,), (0,))
MESH = pl.DeviceIdType.MESH


def _params(sem):
    return pltpu.CompilerParams(dimension_semantics=sem, vmem_limit_bytes=VMEM_LIMIT_BYTES)


def _tile(n, pref):
    return pref if n % pref == 0 else n


def _dot(a, b, dims):
    return lax.dot_general(a.astype(MXU_DT), b.astype(MXU_DT), (dims, ((), ())), preferred_element_type=F32)


def _sig_pair(a):
    e = jnp.exp(-jnp.abs(a))
    inv = 1.0 / (1.0 + e)
    pos = a >= 0
    return jnp.where(pos, inv, e * inv), jnp.where(pos, e * inv, inv)


def _sig(a):
    return _sig_pair(a)[0]


def _split_dot(tri, x):
    def top(y):
        return lax.bitcast_convert_type(lax.bitcast_convert_type(y, jnp.uint32) & jnp.uint32(0xFFFF0000), F32)

    hi = top(x)
    r1 = x - hi
    mid = top(r1)
    lo = r1 - mid
    d = lambda y: jnp.dot(tri, y.astype(BF16), preferred_element_type=F32)
    return d(hi) + d(mid) + d(lo)


def _rms(xv):
    return lax.rsqrt(jnp.mean(xv * xv, axis=-1, keepdims=True) + EPS)


def _prenorm(x, g, scale, shift):
    S, D = x.shape
    tm = _tile(S, 256)

    def body(x_ref, g_ref, sc_ref, sh_ref, h_ref):
        xv = x_ref[...]
        h = (xv * _rms(xv)) * g_ref[...] * (1.0 + sc_ref[...]) + sh_ref[...]
        h_ref[...] = h.astype(h_ref.dtype)

    row = pl.BlockSpec((tm, D), lambda i: (i, 0))
    vec = pl.BlockSpec((1, D), lambda i: (0, 0))
    return pl.pallas_call(body, name="prenorm", grid=(S // tm,), in_specs=[row, vec, vec, vec], out_specs=row,
                          out_shape=jax.ShapeDtypeStruct((S, D), MXU_DT), compiler_params=_params(("parallel",)))(x, g, scale, shift)


def _prenorm_bwd(dh, x, dx, g, scale):
    S, D = x.shape
    tm = _tile(S, 256)

    def body(dh_ref, x_ref, dx_ref, g_ref, sc_ref, o_ref, dsh_ref, dsc_ref, dg_ref):
        @pl.when(pl.program_id(0) == 0)
        def _():
            dsh_ref[...] = jnp.zeros_like(dsh_ref)
            dsc_ref[...] = jnp.zeros_like(dsc_ref)
            dg_ref[...] = jnp.zeros_like(dg_ref)

        xv = x_ref[...]
        r = _rms(xv)
        xh = xv * r
        dh = dh_ref[...]
        gv = g_ref[...]
        one_sc = 1.0 + sc_ref[...]
        dsh_ref[...] += jnp.sum(dh, axis=0, keepdims=True)
        dsc_ref[...] += jnp.sum(dh * xh * gv, axis=0, keepdims=True)
        dg_ref[...] += jnp.sum(dh * xh * one_sc, axis=0, keepdims=True)
        dxh = dh * gv * one_sc
        o_ref[...] = dx_ref[...] + r * (dxh - xh * jnp.mean(dxh * xh, axis=-1, keepdims=True))

    row = pl.BlockSpec((tm, D), lambda i: (i, 0))
    vec = pl.BlockSpec((1, D), lambda i: (0, 0))
    vs = jax.ShapeDtypeStruct((1, D), F32)
    return pl.pallas_call(body, name="prenorm_bwd", grid=(S // tm,), in_specs=[row, row, row, vec, vec],
                          out_specs=[row, vec, vec, vec], out_shape=[jax.ShapeDtypeStruct((S, D), F32), vs, vs, vs],
                          compiler_params=_params(("arbitrary",)))(dh, x, dx, g, scale)


def _postnorm(x, out, gate, g):
    S, D = x.shape
    tm = _tile(S, 256)

    def body(x_ref, o_ref, gt_ref, g_ref, y_ref):
        ov = o_ref[...]
        y_ref[...] = x_ref[...] + gt_ref[...] * ((ov * _rms(ov)) * g_ref[...])

    row = pl.BlockSpec((tm, D), lambda i: (i, 0))
    vec = pl.BlockSpec((1, D), lambda i: (0, 0))
    return pl.pallas_call(body, name="postnorm", grid=(S // tm,), in_specs=[row, row, vec, vec], out_specs=row,
                          out_shape=jax.ShapeDtypeStruct((S, D), F32), compiler_params=_params(("parallel",)))(x, out, gate, g)


def _postnorm_bwd(dx, out, gate, g):
    S, D = dx.shape
    tm = _tile(S, 256)

    def body(dx_ref, o_ref, gt_ref, g_ref, do_ref, dgt_ref, dg_ref):
        @pl.when(pl.program_id(0) == 0)
        def _():
            dgt_ref[...] = jnp.zeros_like(dgt_ref)
            dg_ref[...] = jnp.zeros_like(dg_ref)

        ov = o_ref[...]
        r = _rms(ov)
        rn = ov * r
        dxv = dx_ref[...]
        gv = g_ref[...]
        dgt_ref[...] += jnp.sum(dxv * rn * gv, axis=0, keepdims=True)
        du = dxv * gt_ref[...]
        dg_ref[...] += jnp.sum(du * rn, axis=0, keepdims=True)
        drn = du * gv
        do_ref[...] = (r * (drn - rn * jnp.mean(drn * rn, axis=-1, keepdims=True))).astype(do_ref.dtype)

    row = pl.BlockSpec((tm, D), lambda i: (i, 0))
    vec = pl.BlockSpec((1, D), lambda i: (0, 0))
    vs = jax.ShapeDtypeStruct((1, D), F32)
    return pl.pallas_call(body, name="postnorm_bwd", grid=(S // tm,), in_specs=[row, row, vec, vec],
                          out_specs=[row, vec, vec], out_shape=[jax.ShapeDtypeStruct((S, D), MXU_DT), vs, vs],
                          compiler_params=_params(("arbitrary",)))(dx, out, gate, g)


def _loss_head(y, target):
    S, D = y.shape
    tm = _tile(S, 256)

    def body(y_ref, t_ref, dy_ref, sq_ref):
        @pl.when(pl.program_id(0) == 0)
        def _():
            sq_ref[...] = jnp.zeros_like(sq_ref)

        e = y_ref[...] - t_ref[...]
        dy_ref[...] = e * (1.0 / D)
        sq_ref[...] += jnp.sum(e * e, axis=0, keepdims=True)

    row = pl.BlockSpec((tm, D), lambda i: (i, 0))
    vec = pl.BlockSpec((1, D), lambda i: (0, 0))
    return pl.pallas_call(body, name="loss_head", grid=(S // tm,), in_specs=[row, row], out_specs=[row, vec],
                          out_shape=[jax.ShapeDtypeStruct((S, D), F32), jax.ShapeDtypeStruct((1, D), F32)],
                          compiler_params=_params(("arbitrary",)))(y, target)


def _matmul(a, b, mode, out_dtype, name, *, b_layer=None, into=None, tm=1024, tn=1024, tk=2048):
    if mode == "tn":
        K, M = a.shape
    else:
        M, K = a.shape
    bshape = b.shape[-2:]
    N = bshape[0] if mode == "nt" else bshape[1]
    tm, tn, tk = _tile(M, tm), _tile(N, tn), _tile(K, tk)
    nk = K // tk
    dims = {"nn": NN, "nt": NT, "tn": TN}[mode]

    if mode == "tn":
        a_spec = pl.BlockSpec((tk, tm), lambda i, j, k: (k, i))
    else:
        a_spec = pl.BlockSpec((tm, tk), lambda i, j, k: (i, k))
    if mode == "nt":
        bblk, bidx = (tn, tk), (lambda i, j, k: (j, k))
    else:
        bblk, bidx = (tk, tn), (lambda i, j, k: (k, j))
    if b_layer is None:
        b_spec = pl.BlockSpec(bblk, bidx)
    else:
        b_spec = pl.BlockSpec((None,) + bblk, lambda i, j, k: (b_layer,) + bidx(i, j, k))

    def body(a_ref, b_ref, *rest):
        p = lax.dot_general(a_ref[...], b_ref[...], (dims, ((), ())), preferred_element_type=F32)
        if nk == 1:
            o_ref = rest[-1]
            o_ref[...] = p.astype(o_ref.dtype)
        else:
            o_ref, acc_ref = rest[-2], rest[-1]
            k = pl.program_id(2)

            @pl.when(k == 0)
            def _():
                acc_ref[...] = p

            @pl.when(k > 0)
            def _():
                acc_ref[...] += p

            @pl.when(k == nk - 1)
            def _():
                o_ref[...] = acc_ref[...].astype(o_ref.dtype)

    in_specs, args, aliases = [a_spec, b_spec], [a, b], {}
    if into is None:
        out_spec = pl.BlockSpec((tm, tn), lambda i, j, k: (i, j))
        out_shape = jax.ShapeDtypeStruct((M, N), out_dtype)
    else:
        buf, layer = into
        in_specs.append(pl.BlockSpec(memory_space=pl.ANY))
        args.append(buf)
        aliases = {2: 0}
        out_spec = pl.BlockSpec((None, tm, tn), lambda i, j, k: (layer, i, j))
        out_shape = jax.ShapeDtypeStruct(buf.shape, buf.dtype)
    return pl.pallas_call(body, name=name, grid=(M // tm, N // tn, nk), in_specs=in_specs, out_specs=out_spec,
                          out_shape=out_shape, scratch_shapes=[pltpu.VMEM((tm, tn), F32)] if nk > 1 else [], input_output_aliases=aliases,
                          compiler_params=_params(("parallel", "parallel", "arbitrary")))(*args)


def _merge(ya, yb, wpa, wpb, proj, l):
    S, KW = ya.shape
    D = 2 * KW
    tm, tn = _tile(S, 1024), _tile(D, 512)
    ga0, gb0 = 3 * D // tn, 4 * D // tn

    def body(ya_ref, yb_ref, wa_ref, wb_ref, ga_ref, gb_ref, pa_ref, pb_ref, m_ref):
        pa = jnp.dot(ya_ref[...], wa_ref[...], preferred_element_type=F32)
        pb = jnp.dot(yb_ref[...], wb_ref[...], preferred_element_type=F32)
        pa_ref[...] = pa
        pb_ref[...] = pb
        m_ref[...] = (_sig(ga_ref[...]) * pa + _sig(gb_ref[...]) * pb).astype(m_ref.dtype)

    y_spec = pl.BlockSpec((tm, KW), lambda i, j: (i, 0))
    w_spec = pl.BlockSpec((None, KW, tn), lambda i, j: (l, 0, j))
    o_spec = pl.BlockSpec((tm, tn), lambda i, j: (i, j))
    return pl.pallas_call(
        body, name="merge", grid=(S // tm, D // tn),
        in_specs=[y_spec, y_spec, w_spec, w_spec, pl.BlockSpec((tm, tn), lambda i, j: (i, ga0 + j)),
                  pl.BlockSpec((tm, tn), lambda i, j: (i, gb0 + j))],
        out_specs=[o_spec, o_spec, o_spec],
        out_shape=[jax.ShapeDtypeStruct((S, D), F32), jax.ShapeDtypeStruct((S, D), F32), jax.ShapeDtypeStruct((S, D), MXU_DT)],
        compiler_params=_params(("parallel", "parallel")))(ya, yb, wpa, wpb, proj, proj)


def _merge_bwd(dout, w_out, proj, pa, pb, l):
    S, D = dout.shape
    tm, tn = _tile(S, 1024), _tile(D, 512)
    ga0, gb0 = 3 * D // tn, 4 * D // tn

    def body(do_ref, w_ref, ga_ref, gb_ref, pa_ref, pb_ref, dpa_ref, dpb_ref, dga_ref, dgb_ref):
        dm = lax.dot_general(do_ref[...], w_ref[...], (NT, ((), ())), preferred_element_type=F32)
        sa, sna = _sig_pair(ga_ref[...])
        sb, snb = _sig_pair(gb_ref[...])
        dpa = dm * sa
        dpb = dm * sb
        dpa_ref[...] = dpa.astype(dpa_ref.dtype)
        dpb_ref[...] = dpb.astype(dpb_ref.dtype)
        dga_ref[...] = (dpa * pa_ref[...] * sna).astype(dga_ref.dtype)
        dgb_ref[...] = (dpb * pb_ref[...] * snb).astype(dgb_ref.dtype)

    blk = pl.BlockSpec((tm, tn), lambda i, j: (i, j))
    os = jax.ShapeDtypeStruct((S, D), MXU_DT)
    return pl.pallas_call(
        body, name="merge_bwd", grid=(S // tm, D // tn),
        in_specs=[pl.BlockSpec((tm, D), lambda i, j: (i, 0)), pl.BlockSpec((None, tn, D), lambda i, j: (l, j, 0)),
                  pl.BlockSpec((tm, tn), lambda i, j: (i, ga0 + j)), pl.BlockSpec((tm, tn), lambda i, j: (i, gb0 + j)), blk, blk],
        out_specs=[blk, blk, blk, blk], out_shape=[os, os, os, os],
        compiler_params=_params(("parallel", "parallel")))(dout, w_out, proj, proj, pa, pb)


def _gates(qr, a, lbv):
    sq = _sig(qr)
    q = qr * sq
    s, sn = _sig_pair(a)
    omlb = 1.0 - lbv
    f = lbv + omlb * s
    logf = jnp.log(jnp.maximum(f, MIN_FORGET))
    kk = omlb * sn
    return sq, q, s, sn, omlb, f, logf, kk


def _shift_in_block(x, d):
    return pltpu.roll(x.reshape(NBLK, SUB, HEAD), d, axis=1).reshape(CHUNK, HEAD)


def _level_masks(h):
    ti = lax.broadcasted_iota(jnp.int32, (CHUNK, 1), 0)
    si = lax.broadcasted_iota(jnp.int32, (1, CHUNK), 1)
    t2, s2 = (ti & (2 * h - 1)) >= h, (si & (2 * h - 1)) >= h
    same = (ti & -(2 * h)) == (si & -(2 * h))
    return t2, jnp.where(same & t2 & jnp.logical_not(s2), 1.0, 0.0)


def _level_factor(b, h):
    parts = [jnp.broadcast_to(b[g + h - 1:g + h], (2 * h, HEAD)) for g in range(0, CHUNK, 2 * h)]
    ref = parts[0] if len(parts) == 1 else jnp.concatenate(parts, axis=0)
    return jnp.exp(-jnp.abs(b - ref))


def _level_operands(q, kk, b, h, t2):
    fac = _level_factor(b, h)
    return jnp.where(t2, q * fac, 0.0), jnp.where(t2, 0.0, kk * fac), fac


def _heads_per_step(H, pref):
    return pref if H % pref == 0 else 1


def _tri(lower):
    r = lax.broadcasted_iota(jnp.int32, (CHUNK, CHUNK), 0)
    c = lax.broadcasted_iota(jnp.int32, (CHUNK, CHUNK), 1)
    return jnp.where((r >= c) if lower else (c >= r), 1.0, 0.0).astype(BF16)


def _hgrn_fwd(proj, lb, hn, KW):
    S = proj.shape[0]
    H = KW // HEAD
    T = _tile(S, 512)
    nci, nb = T // CHUNK, S // T
    HP = _heads_per_step(H, 8)

    def body(q_ref, f_ref, v_ref, z_ref, lb_ref, hn_ref, o_ref, y_ref, s0_ref, st_ref):
        @pl.when(pl.program_id(1) == 0)
        def _():
            st_ref[...] = jnp.zeros_like(st_ref)

        tril = _tri(True)
        rmod = lax.broadcasted_iota(jnp.int32, (CHUNK, 1), 0) & (SUB - 1)
        masks = [_level_masks(h) for h in LEVELS]
        lb_all = lb_ref[...]

        def chunk(ci, carry):
            rows = pl.ds(pl.multiple_of(ci * CHUNK, CHUNK), CHUNK)
            _, q_all, _, _, _, _, logf_all, kk_all = _gates(q_ref[rows, :], f_ref[rows, :], lb_all)
            b_all = _split_dot(tril, logf_all)
            for hp in range(HP):
                cols = slice(hp * HEAD, (hp + 1) * HEAD)
                one_head(ci, rows, hp, cols, q_all[:, cols], kk_all[:, cols], b_all[:, cols])
            return carry

        def one_head(ci, rows, hp, cols, q, kk, b):
            hnv = hn_ref[:, cols]
            vv = v_ref[rows, cols]
            eb = jnp.exp(b)
            st = st_ref[hp]
            s0_ref[ci, hp] = st
            p = None
            for h, (t2, m) in zip(LEVELS, masks):
                qs, ks, _ = _level_operands(q, kk, b, h, t2)
                pm = m * _dot(qs, ks, NT)
                p = pm if p is None else p + pm
            o = _dot(q * eb, st, NT) + _dot(p, vv, NN) + jnp.sum(q * kk, axis=-1, keepdims=True) * vv
            for d in range(1, SUB):
                kd_, bd_, vd_ = _shift_in_block(kk, d), _shift_in_block(b, d), _shift_in_block(vv, d)
                e = jnp.exp(jnp.minimum(b - bd_, 0.0))
                p = jnp.where(rmod >= d, jnp.sum(q * (kd_ * e), axis=-1, keepdims=True), 0.0)
                o = o + p * vd_
            bl = b[CHUNK - 1:CHUNK]
            st_ref[hp] = st * eb[CHUNK - 1:CHUNK] + _dot(vv, kk * jnp.exp(bl - b), TN)
            o_ref[rows, cols] = o
            z = z_ref[rows, cols]
            y_ref[rows, cols] = ((o * _rms(o)) * hnv * (z * _sig(z))).astype(y_ref.dtype)

        lax.fori_loop(0, nci, chunk, 0)

    W = HP * HEAD
    G = H // HP

    def col(off):
        return pl.BlockSpec((T, W), lambda h, t: (t, off + h))

    vec = pl.BlockSpec((1, W), lambda h, t: (0, h))
    return pl.pallas_call(
        body, name="hgrn_fwd", grid=(G, nb),
        in_specs=[col(0), col(G), col(2 * G), col(3 * G), vec, vec],
        out_specs=[pl.BlockSpec((T, W), lambda h, t: (t, h)), pl.BlockSpec((T, W), lambda h, t: (t, h)),
                   pl.BlockSpec((nci, HP, HEAD, HEAD), lambda h, t: (t, h, 0, 0))],
        out_shape=[jax.ShapeDtypeStruct((S, KW), F32), jax.ShapeDtypeStruct((S, KW), MXU_DT),
                   jax.ShapeDtypeStruct((S // CHUNK, H, HEAD, HEAD), F32)],
        scratch_shapes=[pltpu.VMEM((HP, HEAD, HEAD), F32)],
        compiler_params=_params(("parallel", "arbitrary")))(proj, proj, proj, proj, lb, hn)


def _gate_a_bwd(dya, o, proj, hn, KW):
    S = o.shape[0]
    H = KW // HEAD
    T = _tile(S, 512)

    def body(dy_ref, o_ref, z_ref, hn_ref, do_ref, dz_ref, dhn_ref):
        @pl.when(pl.program_id(1) == 0)
        def _():
            dhn_ref[...] = jnp.zeros_like(dhn_ref)

        ov = o_ref[...]
        r = _rms(ov)
        rn = ov * r
        z = z_ref[...]
        sz = _sig(z)
        silu = z * sz
        dy = dy_ref[...]
        hnv = hn_ref[...]
        dz_ref[...] = (dy * rn * hnv * (sz * (1.0 + z * (1.0 - sz)))).astype(dz_ref.dtype)
        dhn_ref[...] += jnp.sum(dy * rn * silu, axis=0, keepdims=True)
        drn = dy * hnv * silu
        do_ref[...] = r * (drn - rn * jnp.mean(drn * rn, axis=-1, keepdims=True))

    blk = pl.BlockSpec((T, HEAD), lambda h, t: (t, h))
    vec = pl.BlockSpec((1, HEAD), lambda h, t: (0, h))
    return pl.pallas_call(
        body, name="gate_a_bwd", grid=(H, S // T),
        in_specs=[blk, blk, pl.BlockSpec((T, HEAD), lambda h, t: (t, 3 * H + h)), vec],
        out_specs=[blk, blk, vec],
        out_shape=[jax.ShapeDtypeStruct((S, KW), F32), jax.ShapeDtypeStruct((S, KW), MXU_DT), jax.ShapeDtypeStruct((1, KW), F32)],
        compiler_params=_params(("parallel", "arbitrary")))(dya, o, proj, hn)


def _hgrn_bwd(proj, do, s0, lb, KW):
    S = proj.shape[0]
    H = KW // HEAD
    T = _tile(S, 512)
    nci, nb = T // CHUNK, S // T
    nsub = CHUNK // SUB
    HP = _heads_per_step(H, 4)

    def body(q_ref, f_ref, v_ref, do_ref, s0_ref, lb_ref, dq_ref, df_ref, dv_ref, dlb_ref, dst_ref):
        @pl.when(pl.program_id(1) == 0)
        def _():
            dst_ref[...] = jnp.zeros_like(dst_ref)
            dlb_ref[...] = jnp.zeros_like(dlb_ref)

        tril, triu = _tri(True), _tri(False)
        rmod = lax.broadcasted_iota(jnp.int32, (CHUNK, 1), 0) & (SUB - 1)
        masks = [_level_masks(h) for h in LEVELS]
        lb_all = lb_ref[...]

        def chunk(it, carry):
            ci = nci - 1 - it
            rows = pl.ds(pl.multiple_of(ci * CHUNK, CHUNK), CHUNK)
            qr_all = q_ref[rows, :]
            sq_all, q_all, s_all, sn_all, omlb, f_all, logf_all, kk_all = _gates(qr_all, f_ref[rows, :], lb_all)
            b_all = _split_dot(tril, logf_all)
            res = []
            for hp in range(HP):
                cols = slice(hp * HEAD, (hp + 1) * HEAD)
                res.append(one_head(ci, rows, hp, cols, q_all[:, cols], kk_all[:, cols], b_all[:, cols]))
            dq_t, dk_t, dv_t = (jnp.concatenate([r[i] for r in res], axis=1) for i in range(3))
            w = jnp.concatenate([r[3] for r in res], axis=1)
            dlogf = _split_dot(triu, q_all * dq_t - kk_all * dk_t) + w
            dlf = jnp.where(f_all > MIN_FORGET, dlogf / jnp.maximum(f_all, MIN_FORGET), 0.0)
            t1 = dlf - dk_t
            dlb_ref[...] += jnp.sum(sn_all * t1, axis=0, keepdims=True)
            df_ref[rows, :] = (omlb * (s_all * sn_all) * t1).astype(df_ref.dtype)
            dq_ref[rows, :] = (dq_t * (sq_all * (1.0 + qr_all * (1.0 - sq_all)))).astype(dq_ref.dtype)
            dv_ref[rows, :] = dv_t.astype(dv_ref.dtype)
            return carry

        def one_head(ci, rows, hp, cols, q, kk, b):
            vv, dov = v_ref[rows, cols], do_ref[rows, cols]
            eb = jnp.exp(b)
            st0 = s0_ref[ci, hp]
            dst = dst_ref[hp]
            bl, ec = b[CHUNK - 1:CHUNK], eb[CHUNK - 1:CHUNK]
            decl = jnp.exp(bl - b)
            kdec = kk * decl
            dq_t = eb * _dot(dov, st0, NN)
            dst_ref[hp] = dst * ec + _dot(dov, q * eb, TN)
            dv_t = _dot(kdec, dst, NT)
            dk_t = decl * _dot(vv, dst, NN)
            stc = st0 * ec + _dot(vv, kdec, TN)
            w = jnp.sum(stc * dst, axis=0, keepdims=True)
            dp_all = _dot(dov, vv, NT)
            p = None
            for h, (t2, m) in zip(LEVELS, masks):
                qs, ks, fac = _level_operands(q, kk, b, h, t2)
                pm = m * _dot(qs, ks, NT)
                p = pm if p is None else p + pm
                dpm = m * dp_all
                dq_t = dq_t + fac * _dot(dpm, ks, NN)
                dk_t = dk_t + fac * _dot(dpm, qs, TN)
            p0 = jnp.sum(q * kk, axis=-1, keepdims=True)
            dp0 = jnp.sum(dov * vv, axis=-1, keepdims=True)
            dq_t = dq_t + dp0 * kk
            dk_t = dk_t + dp0 * q
            dv_t = dv_t + _dot(p, dov, TN) + p0 * dov
            for d in range(1, SUB):
                kd_, bd_, vd_ = _shift_in_block(kk, d), _shift_in_block(b, d), _shift_in_block(vv, d)
                e = jnp.exp(jnp.minimum(b - bd_, 0.0))
                ke = kd_ * e
                m = rmod >= d
                p = jnp.where(m, jnp.sum(q * ke, axis=-1, keepdims=True), 0.0)
                dp = jnp.where(m, jnp.sum(dov * vd_, axis=-1, keepdims=True), 0.0)
                dq_t = dq_t + dp * ke
                dk_t = dk_t + _shift_in_block(dp * (q * e), SUB - d)
                dv_t = dv_t + _shift_in_block(p * dov, SUB - d)
            return dq_t, dk_t, dv_t, w

        lax.fori_loop(0, nci, chunk, 0)

    W = HP * HEAD
    G = H // HP

    def col(off):
        return pl.BlockSpec((T, W), lambda h, t: (nb - 1 - t, off + h))

    blk = pl.BlockSpec((T, W), lambda h, t: (nb - 1 - t, h))
    vec = pl.BlockSpec((1, W), lambda h, t: (0, h))
    os = jax.ShapeDtypeStruct((S, KW), MXU_DT)
    return pl.pallas_call(
        body, name="hgrn_bwd", grid=(G, nb),
        in_specs=[col(0), col(G), col(2 * G), blk, pl.BlockSpec((nci, HP, HEAD, HEAD), lambda h, t: (nb - 1 - t, h, 0, 0)), vec],
        out_specs=[blk, blk, blk, vec], out_shape=[os, os, os, jax.ShapeDtypeStruct((1, KW), F32)],
        scratch_shapes=[pltpu.VMEM((HP, HEAD, HEAD), F32)],
        compiler_params=_params(("parallel", "arbitrary")))(proj, proj, proj, do, s0, lb)


def _pool_pos(i, T):
    return (i * T + lax.broadcasted_iota(jnp.int32, (T, 1), 0) + 1).astype(F32)


def _pool_fwd(proj, pool_w, pool_scale, KW, l):
    S = proj.shape[0]
    GW = KW // len(WINDOWS)
    T = _tile(S, 256)

    def body(v_ref, halo_ref, z_ref, pw_ref, ps_ref, p_ref, y_ref):
        i = pl.program_id(0)
        vb = v_ref[...]
        ext = jnp.concatenate([jnp.where(i > 0, halo_ref[...], 0.0), vb], axis=0)
        pos = _pool_pos(i, T)
        z = z_ref[...]
        gate = ps_ref[...] * (z * _sig(z))
        for g, w in enumerate(WINDOWS):
            sl = slice(g * GW, (g + 1) * GW)
            sg = ext[:, sl]
            for jj in range(g + 1):
                sg = sg + pltpu.roll(sg, 1 << jj, axis=0)
            pooled = (sg[HALO:] / jnp.minimum(pos, float(w)) - vb[:, sl]).astype(p_ref.dtype)
            p_ref[:, sl] = pooled
            y_ref[:, sl] = (_dot(pooled, pw_ref[g], NN) * gate[:, sl]).astype(y_ref.dtype)

    row = pl.BlockSpec((T, KW), lambda i: (i, 0))
    hb = T // HALO
    return pl.pallas_call(
        body, name="pool_fwd", grid=(S // T,),
        in_specs=[pl.BlockSpec((T, KW), lambda i: (i, 4)), pl.BlockSpec((HALO, KW), lambda i: (jnp.maximum(i * hb - 1, 0), 4)),
                  pl.BlockSpec((T, KW), lambda i: (i, 5)), pl.BlockSpec((None, len(WINDOWS), GW, GW), lambda i: (l, 0, 0, 0)),
                  pl.BlockSpec((1, KW), lambda i: (0, 0))],
        out_specs=[row, row], out_shape=[jax.ShapeDtypeStruct((S, KW), MXU_DT), jax.ShapeDtypeStruct((S, KW), MXU_DT)],
        compiler_params=_params(("parallel",)))(proj, proj, proj, pool_w, pool_scale)


def _pool_bwd(dyb, proj, pooled, pool_w, pool_scale, KW, l):
    S = proj.shape[0]
    G = len(WINDOWS)
    GW = KW // G
    T = _tile(S, 256)

    def body(dy_ref, z_ref, p_ref, pw_ref, ps_ref, u_ref, dz_ref, dps_ref, dpw_ref):
        i = pl.program_id(0)

        @pl.when(i == 0)
        def _():
            dps_ref[...] = jnp.zeros_like(dps_ref)
            dpw_ref[...] = jnp.zeros_like(dpw_ref)

        pos = _pool_pos(i, T)
        z = z_ref[...]
        sz = _sig(z)
        silu = z * sz
        dsilu = sz * (1.0 + z * (1.0 - sz))
        dy = dy_ref[...]
        ps = ps_ref[...]
        for g, w in enumerate(WINDOWS):
            sl = slice(g * GW, (g + 1) * GW)
            pg = p_ref[:, sl]
            mixed = _dot(pg, pw_ref[g], NN)
            dyg = dy[:, sl]
            dz_ref[:, sl] = (dyg * mixed * ps[:, sl] * dsilu[:, sl]).astype(dz_ref.dtype)
            dps_ref[:, sl] += jnp.sum(dyg * mixed * silu[:, sl], axis=0, keepdims=True)
            dmix = dyg * ps[:, sl] * silu[:, sl]
            dpw_ref[g] += _dot(pg, dmix, TN)
            u_ref[:, sl] = _dot(dmix, pw_ref[g], NT) / jnp.minimum(pos, float(w))

    row = pl.BlockSpec((T, KW), lambda i: (i, 0))
    return pl.pallas_call(
        body, name="pool_bwd", grid=(S // T,),
        in_specs=[row, pl.BlockSpec((T, KW), lambda i: (i, 5)), row,
                  pl.BlockSpec((None, G, GW, GW), lambda i: (l, 0, 0, 0)), pl.BlockSpec((1, KW), lambda i: (0, 0))],
        out_specs=[row, row, pl.BlockSpec((1, KW), lambda i: (0, 0)), pl.BlockSpec((G, GW, GW), lambda i: (0, 0, 0))],
        out_shape=[jax.ShapeDtypeStruct((S, KW), F32), jax.ShapeDtypeStruct((S, KW), MXU_DT),
                   jax.ShapeDtypeStruct((1, KW), F32), jax.ShapeDtypeStruct((G, GW, GW), F32)],
        compiler_params=_params(("arbitrary",)))(dyb, proj, pooled, pool_w, pool_scale)


def _pool_bwd_window(u, KW):
    S = u.shape[0]
    GW = KW // len(WINDOWS)
    T = _tile(S, 256)
    nb = S // T
    n = T + HALO

    def body(u_ref, halo_ref, dv_ref):
        i = pl.program_id(0)
        uv = u_ref[...]
        ext = jnp.concatenate([uv, jnp.where(i < nb - 1, halo_ref[...], 0.0)], axis=0)
        pos = _pool_pos(i, T)
        for g, w in enumerate(WINDOWS):
            sl = slice(g * GW, (g + 1) * GW)
            sg = ext[:, sl]
            for jj in range(g + 1):
                sg = sg + pltpu.roll(sg, n - (1 << jj), axis=0)
            dv_ref[:, sl] = (sg[:T] - uv[:, sl] * jnp.minimum(pos, float(w))).astype(dv_ref.dtype)

    hb = T // HALO
    return pl.pallas_call(
        body, name="pool_bwd_window", grid=(nb,),
        in_specs=[pl.BlockSpec((T, KW), lambda i: (i, 0)),
                  pl.BlockSpec((HALO, KW), lambda i: (jnp.minimum((i + 1) * hb, S // HALO - 1), 0))],
        out_specs=pl.BlockSpec((T, KW), lambda i: (i, 0)), out_shape=jax.ShapeDtypeStruct((S, KW), MXU_DT),
        compiler_params=_params(("parallel",)))(u, u)


def _softmax_rows(x):
    m = jnp.max(x, axis=0, keepdims=True)
    e = jnp.exp(x - m)
    return e / jnp.sum(e, axis=0, keepdims=True)


def _lb_table(lower_bounds):
    L, KW = lower_bounds.shape

    def body(x_ref, o_ref):
        sm = _softmax_rows(x_ref[...])
        acc = jnp.zeros((1, KW), F32)
        o_ref[0:1, :] = acc
        for l in range(1, L):
            acc = acc + sm[l:l + 1]
            o_ref[l:l + 1, :] = acc

    return pl.pallas_call(body, name="lb_table", out_shape=jax.ShapeDtypeStruct((L, KW), F32))(lower_bounds)


def _mod_part(c_all, w_ada):
    L, D, C = w_ada.shape
    B = c_all.shape[0]
    tn = _tile(C, 512)

    def body(c_ref, w_ref, o_ref):
        cv = c_ref[...]
        o_ref[...] = _dot(cv * _sig(cv), w_ref[...], NN)

    return pl.pallas_call(
        body, name="mod_part", grid=(L, C // tn),
        in_specs=[pl.BlockSpec((B, D), lambda l, j: (0, 0)), pl.BlockSpec((None, D, tn), lambda l, j: (l, 0, j))],
        out_specs=pl.BlockSpec((None, B, tn), lambda l, j: (l, 0, j)), out_shape=jax.ShapeDtypeStruct((L, B, C), F32),
        compiler_params=_params(("parallel", "parallel")))(c_all, w_ada)


def _adamw_math(w, g, m, v):
    m = ADAM_B1 * m + (1.0 - ADAM_B1) * g
    v = ADAM_B2 * v + (1.0 - ADAM_B2) * (g * g)
    m_hat = m / (1.0 - ADAM_B1 ** ADAM_STEP)
    v_hat = v / (1.0 - ADAM_B2 ** ADAM_STEP)
    delta = -ADAM_LR * (m_hat / (jnp.sqrt(v_hat) + ADAM_EPS) + ADAM_WD * w)
    return delta, m, v


def _adamw(w, g, m, v, name):
    shape = w.shape
    C = shape[-1]
    R = w.size // C
    tr = _tile(R, 128)
    flat = lambda t: t.reshape(R, C)

    def body(w_ref, g_ref, m_ref, v_ref, d_ref, nm_ref, nv_ref):
        d, nm, nv = _adamw_math(w_ref[...], g_ref[...], m_ref[...], v_ref[...])
        d_ref[...] = d
        nm_ref[...] = nm
        nv_ref[...] = nv

    blk = pl.BlockSpec((tr, C), lambda i: (i, 0))
    os = jax.ShapeDtypeStruct((R, C), F32)
    outs = pl.pallas_call(body, name=name, grid=(R // tr,), in_specs=[blk] * 4, out_specs=[blk] * 3, out_shape=[os] * 3,
                          compiler_params=_params(("parallel",)))(flat(w), flat(g), flat(m), flat(v))
    return [t.reshape(shape) for t in outs]


def _w_ada_update(c_all, dmod, w, m, v):
    L, D, C = w.shape
    B = c_all.shape[0]
    tn = _tile(C, 256)

    def body(c_ref, dm_ref, w_ref, m_ref, v_ref, g_ref, d_ref, nm_ref, nv_ref):
        cv = c_ref[...]
        g = lax.dot_general(cv * _sig(cv), dm_ref[...], (TN, ((), ())), preferred_element_type=F32,
                            precision=lax.Precision.HIGHEST)
        d, nm, nv = _adamw_math(w_ref[...], g, m_ref[...], v_ref[...])
        g_ref[...] = g
        d_ref[...] = d
        nm_ref[...] = nm
        nv_ref[...] = nv

    blk = pl.BlockSpec((None, D, tn), lambda l, j: (l, 0, j))
    os = jax.ShapeDtypeStruct((L, D, C), F32)
    return pl.pallas_call(
        body, name="w_ada_update", grid=(L, C // tn),
        in_specs=[pl.BlockSpec((B, D), lambda l, j: (0, 0)), pl.BlockSpec((None, B, tn), lambda l, j: (l, 0, j)), blk, blk, blk],
        out_specs=[blk] * 4, out_shape=[os] * 4, compiler_params=_params(("parallel", "parallel")))(c_all, dmod, w, m, v)


def _small_update(parts, lower_bounds, wmv, D, KW):
    L = parts.shape[1]
    widths = [3 * D, D, D, KW, KW, KW]
    offs = [sum(widths[:i]) for i in range(len(widths))]

    def body(p_ref, *refs):
        ins, outs = refs[:18], refs[18:]
        tot = p_ref[0]
        for dev in range(1, N_DEV):
            tot = tot + p_ref[dev]
        grads = [tot[:, o:o + wd] for o, wd in zip(offs, widths)]
        sm = _softmax_rows(ins[9][...])
        dlb = grads[3]
        dsm = [jnp.zeros((1, KW), F32)]
        for j in range(1, L):
            acc = dlb[j:j + 1]
            for l in range(j + 1, L):
                acc = acc + dlb[l:l + 1]
            dsm.append(acc)
        dsm = jnp.concatenate(dsm, axis=0)
        grads[3] = sm * (dsm - jnp.sum(sm * dsm, axis=0, keepdims=True))
        for p in range(6):
            w_ref, m_ref, v_ref = ins[3 * p:3 * p + 3]
            d, nm, nv = _adamw_math(w_ref[...], grads[p], m_ref[...], v_ref[...])
            outs[4 * p][...] = grads[p]
            outs[4 * p + 1][...] = d
            outs[4 * p + 2][...] = nm
            outs[4 * p + 3][...] = nv

    flat = [t for trip in wmv for t in trip]
    out_shape = []
    for wd in widths:
        out_shape += [jax.ShapeDtypeStruct((L, wd), F32)] * 4
    res = pl.pallas_call(body, name="small_update", out_shape=out_shape,
                         compiler_params=pltpu.CompilerParams(vmem_limit_bytes=VMEM_LIMIT_BYTES))(parts, *flat)
    return [res[4 * p:4 * p + 4] for p in range(6)]


def _place():
    x, y, c = lax.axis_index("x"), lax.axis_index("y"), lax.axis_index("c")
    chips = [(1 - x, y), (x, 1 - y), (1 - x, 1 - y)]
    return x, y, c, chips


def _all_gather_rows(blk):
    m_per, n = blk.shape

    def body(x_ref, out_ref, send_sems, recv_sems, local_sem):
        x, y, c, chips = _place()
        me, sibling = (x, y, c), (x, y, 1 - c)

        def rows(px, py, pc):
            return out_ref.at[pl.ds((4 * px + 2 * py + pc) * m_per, m_per), :]

        def copy(k, block, to, src=None):
            return pltpu.make_async_remote_copy(src_ref=rows(*block) if src is None else src, dst_ref=rows(*block),
                                                send_sem=send_sems.at[k], recv_sem=recv_sems.at[k], device_id=to, device_id_type=MESH)

        mine = pltpu.make_async_copy(x_ref, rows(*me), local_sem)
        mine.start()
        first = [copy(0, me, sibling, src=x_ref)]
        first += [copy(1 + j, me, (*chip, c), src=x_ref) for j, chip in enumerate(chips)]
        for cp in first:
            cp.start()
        passed = [copy(4 + j, (*chip, c), sibling) for j, chip in enumerate(chips)]
        for j, chip in enumerate(chips):
            copy(1 + j, (*chip, c), me).wait_recv()
            passed[j].start()
        copy(0, sibling, me).wait_recv()
        for j, chip in enumerate(chips):
            copy(4 + j, (*chip, 1 - c), me).wait_recv()
        for cp in first + passed:
            cp.wait_send()
        mine.wait()

    return pl.pallas_call(
        body, name="all_gather_rows", out_shape=jax.ShapeDtypeStruct((N_DEV * m_per, n), blk.dtype),
        in_specs=[pl.BlockSpec(memory_space=pltpu.VMEM)], out_specs=pl.BlockSpec(memory_space=pltpu.VMEM),
        scratch_shapes=[pltpu.SemaphoreType.DMA((7,)), pltpu.SemaphoreType.DMA((7,)), pltpu.SemaphoreType.DMA],
        compiler_params=pltpu.CompilerParams(vmem_limit_bytes=VMEM_LIMIT_BYTES))(blk)


SHARD_AXIS = (2, 2, 2, 2, 1)


def _shard_of(ref, axis, j, n):
    idx = [slice(None)] * len(ref.shape)
    idx[axis] = pl.ds(pl.multiple_of(j * n, n), n)
    return ref.at[tuple(idx)]


SHARD_LAST = (True, False, True, True, False)


def _rows_view(t):
    return t.reshape((-1,) + t.shape[-2:])


def _cast_into_full(shards, chip_idx):
    out = []
    for t, w in enumerate(shards):
        w3 = _rows_view(w)
        B, R, C = w3.shape
        tr = _tile(R, 256)
        nr = R // tr
        last = SHARD_LAST[t]

        def body(j_ref, w_ref, o_ref):
            o_ref[...] = w_ref[...].astype(o_ref.dtype)

        if last:
            full3 = (B, R, N_CHIPS * C)
            o_idx = lambda b, r, j_ref: (b, r, j_ref[0])
        else:
            full3 = (B, N_CHIPS * R, C)
            o_idx = lambda b, r, j_ref, nr=nr: (b, j_ref[0] * nr + r, 0)
        grid_spec = pltpu.PrefetchScalarGridSpec(
            num_scalar_prefetch=1, grid=(B, nr),
            in_specs=[pl.BlockSpec((None, tr, C), lambda b, r, j_ref: (b, r, 0))],
            out_specs=pl.BlockSpec((None, tr, C), o_idx))
        o = pl.pallas_call(body, name=f"cast_into_full{t}", grid_spec=grid_spec, out_shape=jax.ShapeDtypeStruct(full3, MXU_DT),
                           compiler_params=_params(("parallel", "parallel")))(chip_idx, w3)
        shp = list(w.shape)
        shp[SHARD_AXIS[t]] *= N_CHIPS
        out.append(o.reshape(shp))
    return out


def _gather_weights(fulls):
    nt = len(fulls)
    lh = fulls[0].shape[0] // 2
    sizes = [f.shape[ax] // N_CHIPS for f, ax in zip(fulls, SHARD_AXIS)]

    def body(*refs):
        outs = refs[nt:2 * nt]
        send_sems, recv_sems = refs[2 * nt:]
        x, y, c, chips = _place()
        j = 2 * x + y
        half = pl.ds(c * lh, lh)
        other = pl.ds((1 - c) * lh, lh)

        def slot(t, jj):
            return _shard_of(outs[t], SHARD_AXIS[t], jj, sizes[t])

        def ici(k, t, to, jj):
            ref = slot(t, jj).at[half]
            return pltpu.make_async_remote_copy(src_ref=ref, dst_ref=ref, send_sem=send_sems.at[k * nt + t],
                                                recv_sem=recv_sems.at[k * nt + t], device_id=to, device_id_type=MESH)

        def d2d(k, t, jj, lay):
            ref = slot(t, jj).at[lay]
            return pltpu.make_async_remote_copy(src_ref=ref, dst_ref=ref, send_sem=send_sems.at[(3 + k) * nt + t],
                                                recv_sem=recv_sems.at[(3 + k) * nt + t], device_id=(x, y, 1 - c), device_id_type=MESH)

        first = [ici(k, t, (*chip, c), j) for k, chip in enumerate(chips) for t in range(nt)]
        for cp in first:
            cp.start()
        passed = []
        for k, (cx, cy) in enumerate(chips):
            for t in range(nt):
                ici(k, t, (x, y, c), 2 * cx + cy).wait_recv()
                cp = d2d(k, t, 2 * cx + cy, half)
                cp.start()
                passed.append(cp)
        for k, (cx, cy) in enumerate(chips):
            for t in range(nt):
                d2d(k, t, 2 * cx + cy, other).wait_recv()
        for cp in first + passed:
            cp.wait_send()

    any_spec = pl.BlockSpec(memory_space=pl.ANY)
    return pl.pallas_call(
        body, name="gather_weights", out_shape=[jax.ShapeDtypeStruct(f.shape, f.dtype) for f in fulls],
        in_specs=[any_spec] * nt, out_specs=[any_spec] * nt, input_output_aliases={t: t for t in range(nt)},
        scratch_shapes=[pltpu.SemaphoreType.DMA((6 * nt,)), pltpu.SemaphoreType.DMA((6 * nt,))])(*fulls)


def _pair_swap_halves(grads):
    nt = len(grads)
    lh = grads[0].shape[0] // 2

    def body(*refs):
        srcs, outs, send_sems, recv_sems = refs[:nt], refs[nt:2 * nt], refs[2 * nt], refs[2 * nt + 1]
        x, y, c, _ = _place()
        cps = [pltpu.make_async_remote_copy(src_ref=srcs[t].at[pl.ds((1 - c) * lh, lh)], dst_ref=outs[t], send_sem=send_sems.at[t],
                                            recv_sem=recv_sems.at[t], device_id=(x, y, 1 - c), device_id_type=MESH) for t in range(nt)]
        for cp in cps:
            cp.start()
        for cp in cps:
            cp.wait()

    any_spec = pl.BlockSpec(memory_space=pl.ANY)
    return pl.pallas_call(
        body, name="pair_swap_halves", out_shape=[jax.ShapeDtypeStruct((lh,) + g.shape[1:], g.dtype) for g in grads],
        in_specs=[any_spec] * nt, out_specs=[any_spec] * nt,
        scratch_shapes=[pltpu.SemaphoreType.DMA((nt,)), pltpu.SemaphoreType.DMA((nt,))])(*grads)


def _as3d(t):
    return t.reshape(t.shape[0], -1, t.shape[-1])


def _pair_sum(g, recv, c_idx, name):
    lh = recv.shape[0]
    g3, r3 = _as3d(g), _as3d(recv)
    _, R, C = g3.shape
    tr, tc = _tile(R, 256), _tile(C, 2048)

    def body(c_ref, g_ref, r_ref, o_ref):
        o_ref[...] = (g_ref[...] + r_ref[...]).astype(o_ref.dtype)

    grid_spec = pltpu.PrefetchScalarGridSpec(
        num_scalar_prefetch=1, grid=(lh, R // tr, C // tc),
        in_specs=[pl.BlockSpec((None, tr, tc), lambda i, r, cc, c_ref: (c_ref[0] * lh + i, r, cc)),
                  pl.BlockSpec((None, tr, tc), lambda i, r, cc, c_ref: (i, r, cc))],
        out_specs=pl.BlockSpec((None, tr, tc), lambda i, r, cc, c_ref: (i, r, cc)))
    out = pl.pallas_call(body, name=name, grid_spec=grid_spec, out_shape=jax.ShapeDtypeStruct((lh, R, C), BF16),
                         compiler_params=_params(("parallel", "parallel", "parallel")))(c_idx, g3, r3)
    return out.reshape(recv.shape)


def _scatter_partials(parts):
    nt = len(parts)
    sizes = [p.shape[ax] // N_CHIPS for p, ax in zip(parts, SHARD_AXIS)]
    out_shape = []
    for p, ax, n in zip(parts, SHARD_AXIS, sizes):
        shp = list(p.shape)
        shp[ax] = n
        out_shape.append(jax.ShapeDtypeStruct((N_CHIPS - 1,) + tuple(shp), p.dtype))

    def body(*refs):
        srcs, outs = refs[:nt], refs[nt:2 * nt]
        send_sems, recv_sems = refs[2 * nt:]
        x, y, c, chips = _place()
        cps = [pltpu.make_async_remote_copy(src_ref=_shard_of(srcs[t], SHARD_AXIS[t], 2 * cx + cy, sizes[t]), dst_ref=outs[t].at[k],
                                            send_sem=send_sems.at[k * nt + t], recv_sem=recv_sems.at[k * nt + t],
                                            device_id=(cx, cy, c), device_id_type=MESH)
               for k, (cx, cy) in enumerate(chips) for t in range(nt)]
        for cp in cps:
            cp.start()
        for cp in cps:
            cp.wait()

    any_spec = pl.BlockSpec(memory_space=pl.ANY)
    return pl.pallas_call(
        body, name="scatter_partials", out_shape=out_shape, in_specs=[any_spec] * nt, out_specs=[any_spec] * nt,
        scratch_shapes=[pltpu.SemaphoreType.DMA((3 * nt,)), pltpu.SemaphoreType.DMA((3 * nt,))])(*parts)


def _sum_chips(pair, slots, t, L, chip_idx, c_idx):
    own3 = _rows_view(pair)
    s4 = slots.reshape((N_CHIPS - 1, -1) + slots.shape[-2:])
    _, B, R, C = s4.shape
    tr = _tile(R, 256)
    nr = R // tr

    def body(j_ref, c_ref, own_ref, s_ref, o_ref):
        acc = own_ref[...].astype(F32)
        for k in range(N_CHIPS - 1):
            acc = acc + s_ref[k].astype(F32)
        o_ref[...] = acc

    if SHARD_LAST[t]:
        own_idx = lambda b, r, j_ref, c_ref: (b, r, j_ref[0])
    else:
        own_idx = lambda b, r, j_ref, c_ref: (b, j_ref[0] * nr + r, 0)
    grid_spec = pltpu.PrefetchScalarGridSpec(
        num_scalar_prefetch=2, grid=(B, nr),
        in_specs=[pl.BlockSpec((None, tr, C), own_idx),
                  pl.BlockSpec((N_CHIPS - 1, None, tr, C), lambda b, r, j_ref, c_ref: (0, b, r, 0))],
        out_specs=pl.BlockSpec((None, tr, C), lambda b, r, j_ref, c_ref: (c_ref[0] * B + b, r, 0)))
    out = pl.pallas_call(body, name=f"sum_chips{t}", grid_spec=grid_spec, out_shape=jax.ShapeDtypeStruct((2 * B, R, C), F32),
                         compiler_params=_params(("parallel", "parallel")))(chip_idx, c_idx, own3, s4)
    return out.reshape((L,) + slots.shape[2:])


def _pair_join_halves(stacks):
    nt = len(stacks)
    lh = stacks[0].shape[0] // 2

    def body(*refs):
        outs = refs[nt:2 * nt]
        send_sems, recv_sems = refs[2 * nt:]
        x, y, c, _ = _place()

        def cp(t, lay):
            ref = outs[t].at[pl.ds(lay * lh, lh)]
            return pltpu.make_async_remote_copy(src_ref=ref, dst_ref=ref, send_sem=send_sems.at[t], recv_sem=recv_sems.at[t],
                                                device_id=(x, y, 1 - c), device_id_type=MESH)

        for t in range(nt):
            cp(t, c).start()
        for t in range(nt):
            cp(t, c).wait_send()
            cp(t, 1 - c).wait_recv()

    any_spec = pl.BlockSpec(memory_space=pl.ANY)
    return pl.pallas_call(
        body, name="pair_join_halves", out_shape=[jax.ShapeDtypeStruct(s.shape, s.dtype) for s in stacks],
        in_specs=[any_spec] * nt, out_specs=[any_spec] * nt, input_output_aliases={t: t for t in range(nt)},
        scratch_shapes=[pltpu.SemaphoreType.DMA((nt,)), pltpu.SemaphoreType.DMA((nt,))])(*stacks)


def _forward_layer(x, l, wg, small, mods, lb_all, KW):
    norm_pre, norm_post, hgrn_norm, pool_scale = small
    shift, scale, gate = mods
    row = lambda t: t[l:l + 1]
    h = _prenorm(x, row(norm_pre), row(scale), row(shift))
    proj = _matmul(h, wg[0], "nn", F32, "proj", b_layer=l)
    o_a, y_a, s0 = _hgrn_fwd(proj, row(lb_all), row(hgrn_norm), KW)
    pooled, y_b = _pool_fwd(proj, wg[1], row(pool_scale), KW, l)
    pa, pb, merged = _merge(y_a, y_b, wg[2], wg[3], proj, l)
    out = _matmul(merged, wg[4], "nn", F32, "out_proj", b_layer=l)
    x_new = _postnorm(x, out, row(gate), row(norm_post))
    saved = (x, h, proj, o_a, y_a, s0, pooled, y_b, pa, pb, merged, out)
    return x_new, saved


def _backward_layer(dx, l, saved, wg, small, mods, lb_all, gbufs, KW):
    norm_pre, norm_post, hgrn_norm, pool_scale = small
    shift, scale, gate = mods
    x, h, proj, o_a, y_a, s0, pooled, y_b, pa, pb, merged, out = saved
    row = lambda t: t[l:l + 1]
    g_in, g_pool, g_pa, g_pb, g_out = gbufs
    dout, d_gate, d_npost = _postnorm_bwd(dx, out, row(gate), row(norm_post))
    g_out = _matmul(merged, dout, "tn", F32, "grad_w_out", into=(g_out, l), tk=1024)
    dpa, dpb, dga, dgb = _merge_bwd(dout, wg[4], proj, pa, pb, l)
    g_pa = _matmul(y_a, dpa, "tn", F32, "grad_w_proj_a", into=(g_pa, l), tk=1024)
    g_pb = _matmul(y_b, dpb, "tn", F32, "grad_w_proj_b", into=(g_pb, l), tk=1024)
    dya = _matmul(dpa, wg[2], "nt", F32, "d_y_a", b_layer=l)
    dyb = _matmul(dpb, wg[3], "nt", F32, "d_y_b", b_layer=l)
    do, dza, d_hn = _gate_a_bwd(dya, o_a, proj, row(hgrn_norm), KW)
    dq, df, dva, d_lb = _hgrn_bwd(proj, do, s0, row(lb_all), KW)
    u, dzb, d_ps, d_pw = _pool_bwd(dyb, proj, pooled, wg[1], row(pool_scale), KW, l)
    dvb = _pool_bwd_window(u, KW)
    g_pool = lax.dynamic_update_slice_in_dim(g_pool, d_pw[None], l, axis=0)
    dproj = jnp.concatenate([dq, df, dva, dza, dvb, dzb, dga, dgb], axis=1)
    g_in = _matmul(h, dproj, "tn", F32, "grad_w_in", into=(g_in, l), tk=1024)
    dh = _matmul(dproj, wg[0], "nt", F32, "d_h", b_layer=l)
    dx_new, d_shift, d_scale, d_npre = _prenorm_bwd(dh, x, dx, row(norm_pre), row(scale))
    small_g = jnp.concatenate([d_shift, d_scale, d_gate, d_npre, d_npost, d_lb, d_hn, d_ps], axis=1)
    return dx_new, small_g, (g_in, g_pool, g_pa, g_pb, g_out)


def kernel(x, c, w_ada, b_ada, norm_pre, norm_post, w_in, lower_bounds, hgrn_norm, pool_w, pool_scale, w_proj_a, w_proj_b, w_out, loss_target, m_w_ada, m_b_ada, m_norm_pre, m_norm_post, m_w_in, m_lower_bounds, m_hgrn_norm, m_pool_w, m_pool_scale, m_w_proj_a, m_w_proj_b, m_w_out, v_w_ada, v_b_ada, v_norm_pre, v_norm_post, v_w_in, v_lower_bounds, v_hgrn_norm, v_pool_w, v_pool_scale, v_w_proj_a, v_w_proj_b, v_w_out):
    _, S, D = x.shape
    L = w_in.shape[0]
    KW = D // 2
    xi, yi, ci = lax.axis_index("x"), lax.axis_index("y"), lax.axis_index("c")
    chip = 2 * xi + yi
    dev = 4 * xi + 2 * yi + ci

    c_all = _all_gather_rows(c.reshape(8, D // 8)).reshape(N_DEV, D)
    modp = _mod_part(c_all, w_ada)
    cols = modp.shape[-1]
    modg = _all_gather_rows(modp.reshape(L * N_DEV, cols)).reshape(N_DEV, L, N_DEV, cols)
    mod_all = jnp.transpose(modg[0::2], (1, 2, 0, 3)).reshape(L, N_DEV, 3 * D)
    mod = lax.dynamic_index_in_dim(mod_all, dev, axis=1, keepdims=False) + b_ada
    mods = (mod[:, :D], mod[:, D:2 * D], mod[:, 2 * D:])
    lb_all = _lb_table(lower_bounds)

    shards = [w_in, pool_w, w_proj_a, w_proj_b, w_out]
    wg = _gather_weights(_cast_into_full(shards, chip.astype(jnp.int32).reshape(1)))
    small = (norm_pre, norm_post, hgrn_norm, pool_scale)

    xs = x[0]
    saved = []
    for l in range(L):
        xs, sv = _forward_layer(xs, l, wg, small, mods, lb_all, KW)
        saved.append(sv)
    dx, sq = _loss_head(xs, loss_target[0])
    loss = lax.psum(0.5 * jnp.sum(sq) / D, ("x", "y", "c"))

    gbufs = tuple(lax.empty(w.shape, F32) for w in wg)
    small_g = [None] * L
    for l in reversed(range(L)):
        dx, small_g[l], gbufs = _backward_layer(dx, l, saved[l], wg, small, mods, lb_all, gbufs, KW)
    grad_x = dx[None]

    sg = jnp.concatenate(small_g, axis=0)
    sg = jnp.concatenate([sg, jnp.zeros((8 - L, sg.shape[1]), F32)], axis=0)
    parts = _all_gather_rows(sg).reshape(N_DEV, 8, sg.shape[1])[:, :L]
    wmv = [(b_ada, m_b_ada, v_b_ada), (norm_pre, m_norm_pre, v_norm_pre), (norm_post, m_norm_post, v_norm_post),
           (lower_bounds, m_lower_bounds, v_lower_bounds), (hgrn_norm, m_hgrn_norm, v_hgrn_norm), (pool_scale, m_pool_scale, v_pool_scale)]
    r_b_ada, r_npre, r_npost, r_lb, r_hn, r_ps = _small_update(parts, lower_bounds, wmv, D, KW)
    dmod = lax.dynamic_slice_in_dim(parts[:, :, :3 * D], chip * cols, cols, axis=2)
    r_w_ada = _w_ada_update(c_all, jnp.transpose(dmod, (1, 0, 2)), w_ada, m_w_ada, v_w_ada)

    c_idx = ci.astype(jnp.int32).reshape(1)
    recv = _pair_swap_halves(list(gbufs))
    pair = [_pair_sum(g, r, c_idx, f"pair_sum{t}") for t, (g, r) in enumerate(zip(gbufs, recv))]
    slots = _scatter_partials(pair)
    chip_idx = chip.astype(jnp.int32).reshape(1)
    grads = _pair_join_halves([_sum_chips(p, s, t, L, chip_idx, c_idx) for t, (p, s) in enumerate(zip(pair, slots))])
    ms = [m_w_in, m_pool_w, m_w_proj_a, m_w_proj_b, m_w_out]
    vs = [v_w_in, v_pool_w, v_w_proj_a, v_w_proj_b, v_w_out]
    r_big = [[g] + _adamw(w, g, m, v, f"adamw{t}") for t, (w, g, m, v) in enumerate(zip(shards, grads, ms, vs))]
    r_w_in, r_pool_w, r_pa, r_pb, r_w_out = r_big

    order = [r_w_ada, r_b_ada, r_npre, r_npost, r_w_in, r_lb, r_hn, r_pool_w, r_ps, r_pa, r_pb, r_w_out]
    outs = [loss, grad_x]
    for k in range(4):
        outs += [r[k] for r in order]
    return tuple(outs)
```

```python
import functools

import jax
import jax.numpy as jnp
from jax import lax
from jax.experimental import pallas as pl
from jax.experimental.pallas import tpu as pltpu

F32 = jnp.float32
BF16 = jnp.bfloat16
MXU_DT = jnp.bfloat16

CHUNK = 64
SUB = 8
NBLK = CHUNK // SUB
LEVELS = (32, 16, 8)
HEAD = 128
EPS = 1e-6
MIN_FORGET = 1e-30
WINDOWS = (2, 4, 8, 16)
HALO = 16
N_CHIPS = 4
N_DEV = 8
VMEM_LIMIT_BYTES = 56 * 1024 * 1024

ADAM_LR = 0.001
ADAM_B1 = 0.9
ADAM_B2 = 0.999
ADAM_EPS = 1e-08
ADAM_WD = 0.01
ADAM_STEP = 10

NN = ((1,), (0,))
NT = ((1,), (1,))
TN = ((0,), (0,))
MESH = pl.DeviceIdType.MESH


def _params(sem):
    return pltpu.CompilerParams(dimension_semantics=sem, vmem_limit_bytes=VMEM_LIMIT_BYTES)


def _tile(n, pref):
    return pref if n % pref == 0 else n


def _dot(a, b, dims):
    return lax.dot_general(a.astype(MXU_DT), b.astype(MXU_DT), (dims, ((), ())), preferred_element_type=F32)


def _sig_pair(a):
    e = jnp.exp(-jnp.abs(a))
    inv = 1.0 / (1.0 + e)
    pos = a >= 0
    return jnp.where(pos, inv, e * inv), jnp.where(pos, e * inv, inv)


def _sig(a):
    return _sig_pair(a)[0]


def _split_dot(tri, x):
    def top(y):
        return lax.bitcast_convert_type(lax.bitcast_convert_type(y, jnp.uint32) & jnp.uint32(0xFFFF0000), F32)

    hi = top(x)
    r1 = x - hi
    mid = top(r1)
    lo = r1 - mid
    d = lambda y: jnp.dot(tri, y.astype(BF16), preferred_element_type=F32)
    return d(hi) + d(mid) + d(lo)


def _rms(xv):
    return lax.rsqrt(jnp.mean(xv * xv, axis=-1, keepdims=True) + EPS)


def _prenorm(x, g, scale, shift):
    S, D = x.shape
    tm = _tile(S, 256)

    def body(x_ref, g_ref, sc_ref, sh_ref, h_ref):
        xv = x_ref[...]
        h = (xv * _rms(xv)) * g_ref[...] * (1.0 + sc_ref[...]) + sh_ref[...]
        h_ref[...] = h.astype(h_ref.dtype)

    row = pl.BlockSpec((tm, D), lambda i: (i, 0))
    vec = pl.BlockSpec((1, D), lambda i: (0, 0))
    return pl.pallas_call(body, name="prenorm", grid=(S // tm,), in_specs=[row, vec, vec, vec], out_specs=row,
                          out_shape=jax.ShapeDtypeStruct((S, D), MXU_DT), compiler_params=_params(("parallel",)))(x, g, scale, shift)


def _prenorm_bwd(dh, x, dx, g, scale):
    S, D = x.shape
    tm = _tile(S, 256)

    def body(dh_ref, x_ref, dx_ref, g_ref, sc_ref, o_ref, dsh_ref, dsc_ref, dg_ref):
        @pl.when(pl.program_id(0) == 0)
        def _():
            dsh_ref[...] = jnp.zeros_like(dsh_ref)
            dsc_ref[...] = jnp.zeros_like(dsc_ref)
            dg_ref[...] = jnp.zeros_like(dg_ref)

        xv = x_ref[...]
        r = _rms(xv)
        xh = xv * r
        dh = dh_ref[...]
        gv = g_ref[...]
        one_sc = 1.0 + sc_ref[...]
        dsh_ref[...] += jnp.sum(dh, axis=0, keepdims=True)
        dsc_ref[...] += jnp.sum(dh * xh * gv, axis=0, keepdims=True)
        dg_ref[...] += jnp.sum(dh * xh * one_sc, axis=0, keepdims=True)
        dxh = dh * gv * one_sc
        o_ref[...] = dx_ref[...] + r * (dxh - xh * jnp.mean(dxh * xh, axis=-1, keepdims=True))

    row = pl.BlockSpec((tm, D), lambda i: (i, 0))
    vec = pl.BlockSpec((1, D), lambda i: (0, 0))
    vs = jax.ShapeDtypeStruct((1, D), F32)
    return pl.pallas_call(body, name="prenorm_bwd", grid=(S // tm,), in_specs=[row, row, row, vec, vec],
                          out_specs=[row, vec, vec, vec], out_shape=[jax.ShapeDtypeStruct((S, D), F32), vs, vs, vs],
                          compiler_params=_params(("arbitrary",)))(dh, x, dx, g, scale)


def _postnorm(x, out, gate, g):
    S, D = x.shape
    tm = _tile(S, 256)

    def body(x_ref, o_ref, gt_ref, g_ref, y_ref):
        ov = o_ref[...]
        y_ref[...] = x_ref[...] + gt_ref[...] * ((ov * _rms(ov)) * g_ref[...])

    row = pl.BlockSpec((tm, D), lambda i: (i, 0))
    vec = pl.BlockSpec((1, D), lambda i: (0, 0))
    return pl.pallas_call(body, name="postnorm", grid=(S // tm,), in_specs=[row, row, vec, vec], out_specs=row,
                          out_shape=jax.ShapeDtypeStruct((S, D), F32), compiler_params=_params(("parallel",)))(x, out, gate, g)


def _postnorm_bwd(dx, out, gate, g):
    S, D = dx.shape
    tm = _tile(S, 256)

    def body(dx_ref, o_ref, gt_ref, g_ref, do_ref, dgt_ref, dg_ref):
        @pl.when(pl.program_id(0) == 0)
        def _():
            dgt_ref[...] = jnp.zeros_like(dgt_ref)
            dg_ref[...] = jnp.zeros_like(dg_ref)

        ov = o_ref[...]
        r = _rms(ov)
        rn = ov * r
        dxv = dx_ref[...]
        gv = g_ref[...]
        dgt_ref[...] += jnp.sum(dxv * rn * gv, axis=0, keepdims=True)
        du = dxv * gt_ref[...]
        dg_ref[...] += jnp.sum(du * rn, axis=0, keepdims=True)
        drn = du * gv
        do_ref[...] = (r * (drn - rn * jnp.mean(drn * rn, axis=-1, keepdims=True))).astype(do_ref.dtype)

    row = pl.BlockSpec((tm, D), lambda i: (i, 0))
    vec = pl.BlockSpec((1, D), lambda i: (0, 0))
    vs = jax.ShapeDtypeStruct((1, D), F32)
    return pl.pallas_call(body, name="postnorm_bwd", grid=(S // tm,), in_specs=[row, row, vec, vec],
                          out_specs=[row, vec, vec], out_shape=[jax.ShapeDtypeStruct((S, D), MXU_DT), vs, vs],
                          compiler_params=_params(("arbitrary",)))(dx, out, gate, g)


def _loss_head(y, target):
    S, D = y.shape
    tm = _tile(S, 256)

    def body(y_ref, t_ref, dy_ref, sq_ref):
        @pl.when(pl.program_id(0) == 0)
        def _():
            sq_ref[...] = jnp.zeros_like(sq_ref)

        e = y_ref[...] - t_ref[...]
        dy_ref[...] = e * (1.0 / D)
        sq_ref[...] += jnp.sum(e * e, axis=0, keepdims=True)

    row = pl.BlockSpec((tm, D), lambda i: (i, 0))
    vec = pl.BlockSpec((1, D), lambda i: (0, 0))
    return pl.pallas_call(body, name="loss_head", grid=(S // tm,), in_specs=[row, row], out_specs=[row, vec],
                          out_shape=[jax.ShapeDtypeStruct((S, D), F32), jax.ShapeDtypeStruct((1, D), F32)],
                          compiler_params=_params(("arbitrary",)))(y, target)


def _matmul(a, b, mode, out_dtype, name, *, side=None, tm=1024, tn=1024, tk=2048):
    if mode == "tn":
        K, M = a.shape
    else:
        M, K = a.shape
    N = b.shape[0] if mode == "nt" else b.shape[1]
    tm, tn, tk = _tile(M, tm), _tile(N, tn), _tile(K, tk)
    ni, nj, nk = M // tm, N // tn, K // tk
    dims = {"nn": NN, "nt": NT, "tn": TN}[mode]

    if mode == "tn":
        a_spec = pl.BlockSpec((tk, tm), lambda i, j, k: (k, i))
    else:
        a_spec = pl.BlockSpec((tm, tk), lambda i, j, k: (i, k))
    if mode == "nt":
        b_spec = pl.BlockSpec((tn, tk), lambda i, j, k: (j, k))
    else:
        b_spec = pl.BlockSpec((tk, tn), lambda i, j, k: (k, j))
    n_in = len(side["inputs"]) if side else 0
    n_out = len(side["out_shapes"]) if side else 0

    def body(a_ref, b_ref, *rest):
        side_in, o_ref, side_out = rest[:n_in], rest[n_in], rest[n_in + 1:n_in + 1 + n_out]
        scratch = rest[n_in + 1 + n_out:]
        i, j, k = pl.program_id(0), pl.program_id(1), pl.program_id(2)
        if side:
            sems = scratch[-2:]

            @pl.when((i == 0) & (j == 0) & (k == 0))
            def _():
                side["start"](side_in, side_out, *sems)

        p = lax.dot_general(a_ref[...], b_ref[...], (dims, ((), ())), preferred_element_type=F32)
        if nk == 1:
            o_ref[...] = p.astype(o_ref.dtype)
        else:
            acc_ref = scratch[0]

            @pl.when(k == 0)
            def _():
                acc_ref[...] = p

            @pl.when(k > 0)
            def _():
                acc_ref[...] += p

            @pl.when(k == nk - 1)
            def _():
                o_ref[...] = acc_ref[...].astype(o_ref.dtype)

        if side:
            @pl.when((i == ni - 1) & (j == nj - 1) & (k == nk - 1))
            def _():
                side["finish"](side_in, side_out, *sems)

    any_spec = pl.BlockSpec(memory_space=pl.ANY)
    scratch_shapes = [pltpu.VMEM((tm, tn), F32)] if nk > 1 else []
    if side:
        scratch_shapes += [pltpu.SemaphoreType.DMA((side["n_sems"],)), pltpu.SemaphoreType.DMA((side["n_sems"],))]
    res = pl.pallas_call(
        body, name=name, grid=(ni, nj, nk), in_specs=[a_spec, b_spec] + [any_spec] * n_in,
        out_specs=[pl.BlockSpec((tm, tn), lambda i, j, k: (i, j))] + [any_spec] * n_out,
        out_shape=[jax.ShapeDtypeStruct((M, N), out_dtype)] + (list(side["out_shapes"]) if side else []),
        scratch_shapes=scratch_shapes,
        input_output_aliases={2 + s: 1 + d for s, d in side["aliases"].items()} if side else {},
        compiler_params=_params(("arbitrary",) * 3 if side else ("parallel", "parallel", "arbitrary")))(*([a, b] + (list(side["inputs"]) if side else [])))
    return (res[0], list(res[1:])) if side else res[0]


def _merge(ya, yb, wpa, wpb, proj):
    S, KW = ya.shape
    D = 2 * KW
    tm, tn = _tile(S, 1024), _tile(D, 512)
    ga0, gb0 = 3 * D // tn, 4 * D // tn

    def body(ya_ref, yb_ref, wa_ref, wb_ref, ga_ref, gb_ref, pa_ref, pb_ref, m_ref):
        pa = jnp.dot(ya_ref[...], wa_ref[...], preferred_element_type=F32)
        pb = jnp.dot(yb_ref[...], wb_ref[...], preferred_element_type=F32)
        pa_ref[...] = pa
        pb_ref[...] = pb
        m_ref[...] = (_sig(ga_ref[...]) * pa + _sig(gb_ref[...]) * pb).astype(m_ref.dtype)

    y_spec = pl.BlockSpec((tm, KW), lambda i, j: (i, 0))
    w_spec = pl.BlockSpec((KW, tn), lambda i, j: (0, j))
    o_spec = pl.BlockSpec((tm, tn), lambda i, j: (i, j))
    return pl.pallas_call(
        body, name="merge", grid=(S // tm, D // tn),
        in_specs=[y_spec, y_spec, w_spec, w_spec, pl.BlockSpec((tm, tn), lambda i, j: (i, ga0 + j)),
                  pl.BlockSpec((tm, tn), lambda i, j: (i, gb0 + j))],
        out_specs=[o_spec, o_spec, o_spec],
        out_shape=[jax.ShapeDtypeStruct((S, D), F32), jax.ShapeDtypeStruct((S, D), F32), jax.ShapeDtypeStruct((S, D), MXU_DT)],
        compiler_params=_params(("parallel", "parallel")))(ya, yb, wpa, wpb, proj, proj)


def _merge_bwd(dout, w_out, proj, pa, pb):
    S, D = dout.shape
    tm, tn = _tile(S, 1024), _tile(D, 512)
    ga0, gb0 = 3 * D // tn, 4 * D // tn

    def body(do_ref, w_ref, ga_ref, gb_ref, pa_ref, pb_ref, dpa_ref, dpb_ref, dga_ref, dgb_ref):
        dm = lax.dot_general(do_ref[...], w_ref[...], (NT, ((), ())), preferred_element_type=F32)
        sa, sna = _sig_pair(ga_ref[...])
        sb, snb = _sig_pair(gb_ref[...])
        dpa = dm * sa
        dpb = dm * sb
        dpa_ref[...] = dpa.astype(dpa_ref.dtype)
        dpb_ref[...] = dpb.astype(dpb_ref.dtype)
        dga_ref[...] = (dpa * pa_ref[...] * sna).astype(dga_ref.dtype)
        dgb_ref[...] = (dpb * pb_ref[...] * snb).astype(dgb_ref.dtype)

    blk = pl.BlockSpec((tm, tn), lambda i, j: (i, j))
    os = jax.ShapeDtypeStruct((S, D), MXU_DT)
    return pl.pallas_call(
        body, name="merge_bwd", grid=(S // tm, D // tn),
        in_specs=[pl.BlockSpec((tm, D), lambda i, j: (i, 0)), pl.BlockSpec((tn, D), lambda i, j: (j, 0)),
                  pl.BlockSpec((tm, tn), lambda i, j: (i, ga0 + j)), pl.BlockSpec((tm, tn), lambda i, j: (i, gb0 + j)), blk, blk],
        out_specs=[blk, blk, blk, blk], out_shape=[os, os, os, os],
        compiler_params=_params(("parallel", "parallel")))(dout, w_out, proj, proj, pa, pb)


def _gates(qr, a, lbv):
    sq = _sig(qr)
    q = qr * sq
    s, sn = _sig_pair(a)
    omlb = 1.0 - lbv
    f = lbv + omlb * s
    logf = jnp.log(jnp.maximum(f, MIN_FORGET))
    kk = omlb * sn
    return sq, q, s, sn, omlb, f, logf, kk


def _shift_in_block(x, d):
    return pltpu.roll(x.reshape(NBLK, SUB, HEAD), d, axis=1).reshape(CHUNK, HEAD)


def _level_masks(h):
    ti = lax.broadcasted_iota(jnp.int32, (CHUNK, 1), 0)
    si = lax.broadcasted_iota(jnp.int32, (1, CHUNK), 1)
    t2, s2 = (ti & (2 * h - 1)) >= h, (si & (2 * h - 1)) >= h
    same = (ti & -(2 * h)) == (si & -(2 * h))
    return t2, jnp.where(same & t2 & jnp.logical_not(s2), 1.0, 0.0)


def _level_factor(b, h):
    parts = [jnp.broadcast_to(b[g + h - 1:g + h], (2 * h, HEAD)) for g in range(0, CHUNK, 2 * h)]
    ref = parts[0] if len(parts) == 1 else jnp.concatenate(parts, axis=0)
    return jnp.exp(-jnp.abs(b - ref))


def _level_operands(q, kk, b, h, t2):
    fac = _level_factor(b, h)
    return jnp.where(t2, q * fac, 0.0), jnp.where(t2, 0.0, kk * fac), fac


def _heads_per_step(H, pref):
    return pref if H % pref == 0 else 1


def _tri(lower):
    r = lax.broadcasted_iota(jnp.int32, (CHUNK, CHUNK), 0)
    c = lax.broadcasted_iota(jnp.int32, (CHUNK, CHUNK), 1)
    return jnp.where((r >= c) if lower else (c >= r), 1.0, 0.0).astype(BF16)


def _hgrn_fwd(proj, lb, hn, KW):
    S = proj.shape[0]
    H = KW // HEAD
    T = _tile(S, 512)
    nci, nb = T // CHUNK, S // T
    HP = _heads_per_step(H, 8)

    def body(q_ref, f_ref, v_ref, z_ref, lb_ref, hn_ref, o_ref, y_ref, s0_ref, st_ref):
        @pl.when(pl.program_id(1) == 0)
        def _():
            st_ref[...] = jnp.zeros_like(st_ref)

        tril = _tri(True)
        rmod = lax.broadcasted_iota(jnp.int32, (CHUNK, 1), 0) & (SUB - 1)
        masks = [_level_masks(h) for h in LEVELS]
        lb_all = lb_ref[...]

        def chunk(ci, carry):
            rows = pl.ds(pl.multiple_of(ci * CHUNK, CHUNK), CHUNK)
            _, q_all, _, _, _, _, logf_all, kk_all = _gates(q_ref[rows, :], f_ref[rows, :], lb_all)
            b_all = _split_dot(tril, logf_all)
            for hp in range(HP):
                cols = slice(hp * HEAD, (hp + 1) * HEAD)
                one_head(ci, rows, hp, cols, q_all[:, cols], kk_all[:, cols], b_all[:, cols])
            return carry

        def one_head(ci, rows, hp, cols, q, kk, b):
            hnv = hn_ref[:, cols]
            vv = v_ref[rows, cols]
            eb = jnp.exp(b)
            st = st_ref[hp]
            s0_ref[ci, hp] = st
            p = None
            for h, (t2, m) in zip(LEVELS, masks):
                qs, ks, _ = _level_operands(q, kk, b, h, t2)
                pm = m * _dot(qs, ks, NT)
                p = pm if p is None else p + pm
            o = _dot(q * eb, st, NT) + _dot(p, vv, NN) + jnp.sum(q * kk, axis=-1, keepdims=True) * vv
            for d in range(1, SUB):
                kd_, bd_, vd_ = _shift_in_block(kk, d), _shift_in_block(b, d), _shift_in_block(vv, d)
                e = jnp.exp(jnp.minimum(b - bd_, 0.0))
                p = jnp.where(rmod >= d, jnp.sum(q * (kd_ * e), axis=-1, keepdims=True), 0.0)
                o = o + p * vd_
            bl = b[CHUNK - 1:CHUNK]
            st_ref[hp] = st * eb[CHUNK - 1:CHUNK] + _dot(vv, kk * jnp.exp(bl - b), TN)
            o_ref[rows, cols] = o
            z = z_ref[rows, cols]
            y_ref[rows, cols] = ((o * _rms(o)) * hnv * (z * _sig(z))).astype(y_ref.dtype)

        lax.fori_loop(0, nci, chunk, 0)

    W = HP * HEAD
    G = H // HP

    def col(off):
        return pl.BlockSpec((T, W), lambda h, t: (t, off + h))

    vec = pl.BlockSpec((1, W), lambda h, t: (0, h))
    return pl.pallas_call(
        body, name="hgrn_fwd", grid=(G, nb),
        in_specs=[col(0), col(G), col(2 * G), col(3 * G), vec, vec],
        out_specs=[pl.BlockSpec((T, W), lambda h, t: (t, h)), pl.BlockSpec((T, W), lambda h, t: (t, h)),
                   pl.BlockSpec((nci, HP, HEAD, HEAD), lambda h, t: (t, h, 0, 0))],
        out_shape=[jax.ShapeDtypeStruct((S, KW), F32), jax.ShapeDtypeStruct((S, KW), MXU_DT),
                   jax.ShapeDtypeStruct((S // CHUNK, H, HEAD, HEAD), F32)],
        scratch_shapes=[pltpu.VMEM((HP, HEAD, HEAD), F32)],
        compiler_params=_params(("parallel", "arbitrary")))(proj, proj, proj, proj, lb, hn)


def _gate_a_bwd(dya, o, proj, hn, KW):
    S = o.shape[0]
    H = KW // HEAD
    T = _tile(S, 512)

    def body(dy_ref, o_ref, z_ref, hn_ref, do_ref, dz_ref, dhn_ref):
        @pl.when(pl.program_id(1) == 0)
        def _():
            dhn_ref[...] = jnp.zeros_like(dhn_ref)

        ov = o_ref[...]
        r = _rms(ov)
        rn = ov * r
        z = z_ref[...]
        sz = _sig(z)
        silu = z * sz
        dy = dy_ref[...]
        hnv = hn_ref[...]
        dz_ref[...] = (dy * rn * hnv * (sz * (1.0 + z * (1.0 - sz)))).astype(dz_ref.dtype)
        dhn_ref[...] += jnp.sum(dy * rn * silu, axis=0, keepdims=True)
        drn = dy * hnv * silu
        do_ref[...] = r * (drn - rn * jnp.mean(drn * rn, axis=-1, keepdims=True))

    blk = pl.BlockSpec((T, HEAD), lambda h, t: (t, h))
    vec = pl.BlockSpec((1, HEAD), lambda h, t: (0, h))
    return pl.pallas_call(
        body, name="gate_a_bwd", grid=(H, S // T),
        in_specs=[blk, blk, pl.BlockSpec((T, HEAD), lambda h, t: (t, 3 * H + h)), vec],
        out_specs=[blk, blk, vec],
        out_shape=[jax.ShapeDtypeStruct((S, KW), F32), jax.ShapeDtypeStruct((S, KW), MXU_DT), jax.ShapeDtypeStruct((1, KW), F32)],
        compiler_params=_params(("parallel", "arbitrary")))(dya, o, proj, hn)


def _hgrn_bwd(proj, do, s0, lb, KW):
    S = proj.shape[0]
    H = KW // HEAD
    T = _tile(S, 512)
    nci, nb = T // CHUNK, S // T
    nsub = CHUNK // SUB
    HP = _heads_per_step(H, 4)

    def body(q_ref, f_ref, v_ref, do_ref, s0_ref, lb_ref, dq_ref, df_ref, dv_ref, dlb_ref, dst_ref):
        @pl.when(pl.program_id(1) == 0)
        def _():
            dst_ref[...] = jnp.zeros_like(dst_ref)
            dlb_ref[...] = jnp.zeros_like(dlb_ref)

        tril, triu = _tri(True), _tri(False)
        rmod = lax.broadcasted_iota(jnp.int32, (CHUNK, 1), 0) & (SUB - 1)
        masks = [_level_masks(h) for h in LEVELS]
        lb_all = lb_ref[...]

        def chunk(it, carry):
            ci = nci - 1 - it
            rows = pl.ds(pl.multiple_of(ci * CHUNK, CHUNK), CHUNK)
            qr_all = q_ref[rows, :]
            sq_all, q_all, s_all, sn_all, omlb, f_all, logf_all, kk_all = _gates(qr_all, f_ref[rows, :], lb_all)
            b_all = _split_dot(tril, logf_all)
            res = []
            for hp in range(HP):
                cols = slice(hp * HEAD, (hp + 1) * HEAD)
                res.append(one_head(ci, rows, hp, cols, q_all[:, cols], kk_all[:, cols], b_all[:, cols]))
            dq_t, dk_t, dv_t = (jnp.concatenate([r[i] for r in res], axis=1) for i in range(3))
            w = jnp.concatenate([r[3] for r in res], axis=1)
            dlogf = _split_dot(triu, q_all * dq_t - kk_all * dk_t) + w
            dlf = jnp.where(f_all > MIN_FORGET, dlogf / jnp.maximum(f_all, MIN_FORGET), 0.0)
            t1 = dlf - dk_t
            dlb_ref[...] += jnp.sum(sn_all * t1, axis=0, keepdims=True)
            df_ref[rows, :] = (omlb * (s_all * sn_all) * t1).astype(df_ref.dtype)
            dq_ref[rows, :] = (dq_t * (sq_all * (1.0 + qr_all * (1.0 - sq_all)))).astype(dq_ref.dtype)
            dv_ref[rows, :] = dv_t.astype(dv_ref.dtype)
            return carry

        def one_head(ci, rows, hp, cols, q, kk, b):
            vv, dov = v_ref[rows, cols], do_ref[rows, cols]
            eb = jnp.exp(b)
            st0 = s0_ref[ci, hp]
            dst = dst_ref[hp]
            bl, ec = b[CHUNK - 1:CHUNK], eb[CHUNK - 1:CHUNK]
            decl = jnp.exp(bl - b)
            kdec = kk * decl
            dq_t = eb * _dot(dov, st0, NN)
            dst_ref[hp] = dst * ec + _dot(dov, q * eb, TN)
            dv_t = _dot(kdec, dst, NT)
            dk_t = decl * _dot(vv, dst, NN)
            stc = st0 * ec + _dot(vv, kdec, TN)
            w = jnp.sum(stc * dst, axis=0, keepdims=True)
            dp_all = _dot(dov, vv, NT)
            p = None
            for h, (t2, m) in zip(LEVELS, masks):
                qs, ks, fac = _level_operands(q, kk, b, h, t2)
                pm = m * _dot(qs, ks, NT)
                p = pm if p is None else p + pm
                dpm = m * dp_all
                dq_t = dq_t + fac * _dot(dpm, ks, NN)
                dk_t = dk_t + fac * _dot(dpm, qs, TN)
            p0 = jnp.sum(q * kk, axis=-1, keepdims=True)
            dp0 = jnp.sum(dov * vv, axis=-1, keepdims=True)
            dq_t = dq_t + dp0 * kk
            dk_t = dk_t + dp0 * q
            dv_t = dv_t + _dot(p, dov, TN) + p0 * dov
            for d in range(1, SUB):
                kd_, bd_, vd_ = _shift_in_block(kk, d), _shift_in_block(b, d), _shift_in_block(vv, d)
                e = jnp.exp(jnp.minimum(b - bd_, 0.0))
                ke = kd_ * e
                m = rmod >= d
                p = jnp.where(m, jnp.sum(q * ke, axis=-1, keepdims=True), 0.0)
                dp = jnp.where(m, jnp.sum(dov * vd_, axis=-1, keepdims=True), 0.0)
                dq_t = dq_t + dp * ke
                dk_t = dk_t + _shift_in_block(dp * (q * e), SUB - d)
                dv_t = dv_t + _shift_in_block(p * dov, SUB - d)
            return dq_t, dk_t, dv_t, w

        lax.fori_loop(0, nci, chunk, 0)

    W = HP * HEAD
    G = H // HP

    def col(off):
        return pl.BlockSpec((T, W), lambda h, t: (nb - 1 - t, off + h))

    blk = pl.BlockSpec((T, W), lambda h, t: (nb - 1 - t, h))
    vec = pl.BlockSpec((1, W), lambda h, t: (0, h))
    os = jax.ShapeDtypeStruct((S, KW), MXU_DT)
    return pl.pallas_call(
        body, name="hgrn_bwd", grid=(G, nb),
        in_specs=[col(0), col(G), col(2 * G), blk, pl.BlockSpec((nci, HP, HEAD, HEAD), lambda h, t: (nb - 1 - t, h, 0, 0)), vec],
        out_specs=[blk, blk, blk, vec], out_shape=[os, os, os, jax.ShapeDtypeStruct((1, KW), F32)],
        scratch_shapes=[pltpu.VMEM((HP, HEAD, HEAD), F32)],
        compiler_params=_params(("parallel", "arbitrary")))(proj, proj, proj, do, s0, lb)


def _pool_pos(i, T):
    return (i * T + lax.broadcasted_iota(jnp.int32, (T, 1), 0) + 1).astype(F32)


def _pool_fwd(proj, pool_w, pool_scale, KW):
    S = proj.shape[0]
    GW = KW // len(WINDOWS)
    T = _tile(S, 256)

    def body(v_ref, halo_ref, z_ref, pw_ref, ps_ref, p_ref, y_ref):
        i = pl.program_id(0)
        vb = v_ref[...]
        ext = jnp.concatenate([jnp.where(i > 0, halo_ref[...], 0.0), vb], axis=0)
        pos = _pool_pos(i, T)
        z = z_ref[...]
        gate = ps_ref[...] * (z * _sig(z))
        for g, w in enumerate(WINDOWS):
            sl = slice(g * GW, (g + 1) * GW)
            sg = ext[:, sl]
            for jj in range(g + 1):
                sg = sg + pltpu.roll(sg, 1 << jj, axis=0)
            pooled = (sg[HALO:] / jnp.minimum(pos, float(w)) - vb[:, sl]).astype(p_ref.dtype)
            p_ref[:, sl] = pooled
            y_ref[:, sl] = (_dot(pooled, pw_ref[g], NN) * gate[:, sl]).astype(y_ref.dtype)

    row = pl.BlockSpec((T, KW), lambda i: (i, 0))
    hb = T // HALO
    return pl.pallas_call(
        body, name="pool_fwd", grid=(S // T,),
        in_specs=[pl.BlockSpec((T, KW), lambda i: (i, 4)), pl.BlockSpec((HALO, KW), lambda i: (jnp.maximum(i * hb - 1, 0), 4)),
                  pl.BlockSpec((T, KW), lambda i: (i, 5)), pl.BlockSpec((len(WINDOWS), GW, GW), lambda i: (0, 0, 0)),
                  pl.BlockSpec((1, KW), lambda i: (0, 0))],
        out_specs=[row, row], out_shape=[jax.ShapeDtypeStruct((S, KW), MXU_DT), jax.ShapeDtypeStruct((S, KW), MXU_DT)],
        compiler_params=_params(("parallel",)))(proj, proj, proj, pool_w, pool_scale)


def _pool_bwd(dyb, proj, pooled, pool_w, pool_scale, KW):
    S = proj.shape[0]
    G = len(WINDOWS)
    GW = KW // G
    T = _tile(S, 256)

    def body(dy_ref, z_ref, p_ref, pw_ref, ps_ref, u_ref, dz_ref, dps_ref, dpw_ref):
        i = pl.program_id(0)

        @pl.when(i == 0)
        def _():
            dps_ref[...] = jnp.zeros_like(dps_ref)
            dpw_ref[...] = jnp.zeros_like(dpw_ref)

        pos = _pool_pos(i, T)
        z = z_ref[...]
        sz = _sig(z)
        silu = z * sz
        dsilu = sz * (1.0 + z * (1.0 - sz))
        dy = dy_ref[...]
        ps = ps_ref[...]
        for g, w in enumerate(WINDOWS):
            sl = slice(g * GW, (g + 1) * GW)
            pg = p_ref[:, sl]
            mixed = _dot(pg, pw_ref[g], NN)
            dyg = dy[:, sl]
            dz_ref[:, sl] = (dyg * mixed * ps[:, sl] * dsilu[:, sl]).astype(dz_ref.dtype)
            dps_ref[:, sl] += jnp.sum(dyg * mixed * silu[:, sl], axis=0, keepdims=True)
            dmix = dyg * ps[:, sl] * silu[:, sl]
            dpw_ref[g] += _dot(pg, dmix, TN)
            u_ref[:, sl] = _dot(dmix, pw_ref[g], NT) / jnp.minimum(pos, float(w))

    row = pl.BlockSpec((T, KW), lambda i: (i, 0))
    return pl.pallas_call(
        body, name="pool_bwd", grid=(S // T,),
        in_specs=[row, pl.BlockSpec((T, KW), lambda i: (i, 5)), row,
                  pl.BlockSpec((G, GW, GW), lambda i: (0, 0, 0)), pl.BlockSpec((1, KW), lambda i: (0, 0))],
        out_specs=[row, row, pl.BlockSpec((1, KW), lambda i: (0, 0)), pl.BlockSpec((G, GW, GW), lambda i: (0, 0, 0))],
        out_shape=[jax.ShapeDtypeStruct((S, KW), F32), jax.ShapeDtypeStruct((S, KW), MXU_DT),
                   jax.ShapeDtypeStruct((1, KW), F32), jax.ShapeDtypeStruct((G, GW, GW), F32)],
        compiler_params=_params(("arbitrary",)))(dyb, proj, pooled, pool_w, pool_scale)


def _pool_bwd_window(u, KW):
    S = u.shape[0]
    GW = KW // len(WINDOWS)
    T = _tile(S, 256)
    nb = S // T
    n = T + HALO

    def body(u_ref, halo_ref, dv_ref):
        i = pl.program_id(0)
        uv = u_ref[...]
        ext = jnp.concatenate([uv, jnp.where(i < nb - 1, halo_ref[...], 0.0)], axis=0)
        pos = _pool_pos(i, T)
        for g, w in enumerate(WINDOWS):
            sl = slice(g * GW, (g + 1) * GW)
            sg = ext[:, sl]
            for jj in range(g + 1):
                sg = sg + pltpu.roll(sg, n - (1 << jj), axis=0)
            dv_ref[:, sl] = (sg[:T] - uv[:, sl] * jnp.minimum(pos, float(w))).astype(dv_ref.dtype)

    hb = T // HALO
    return pl.pallas_call(
        body, name="pool_bwd_window", grid=(nb,),
        in_specs=[pl.BlockSpec((T, KW), lambda i: (i, 0)),
                  pl.BlockSpec((HALO, KW), lambda i: (jnp.minimum((i + 1) * hb, S // HALO - 1), 0))],
        out_specs=pl.BlockSpec((T, KW), lambda i: (i, 0)), out_shape=jax.ShapeDtypeStruct((S, KW), MXU_DT),
        compiler_params=_params(("parallel",)))(u, u)


def _softmax_rows(x):
    m = jnp.max(x, axis=0, keepdims=True)
    e = jnp.exp(x - m)
    return e / jnp.sum(e, axis=0, keepdims=True)


def _lb_table(lower_bounds):
    L, KW = lower_bounds.shape

    def body(x_ref, o_ref):
        sm = _softmax_rows(x_ref[...])
        acc = jnp.zeros((1, KW), F32)
        o_ref[0:1, :] = acc
        for l in range(1, L):
            acc = acc + sm[l:l + 1]
            o_ref[l:l + 1, :] = acc

    return pl.pallas_call(body, name="lb_table", out_shape=jax.ShapeDtypeStruct((L, KW), F32))(lower_bounds)


def _mod_part(c_all, w_ada):
    L, D, C = w_ada.shape
    B = c_all.shape[0]
    tn = _tile(C, 512)

    def body(c_ref, w_ref, o_ref):
        cv = c_ref[...]
        o_ref[...] = _dot(cv * _sig(cv), w_ref[...], NN)

    return pl.pallas_call(
        body, name="mod_part", grid=(L, C // tn),
        in_specs=[pl.BlockSpec((B, D), lambda l, j: (0, 0)), pl.BlockSpec((None, D, tn), lambda l, j: (l, 0, j))],
        out_specs=pl.BlockSpec((None, B, tn), lambda l, j: (l, 0, j)), out_shape=jax.ShapeDtypeStruct((L, B, C), F32),
        compiler_params=_params(("parallel", "parallel")))(c_all, w_ada)


def _adamw_math(w, g, m, v):
    m = ADAM_B1 * m + (1.0 - ADAM_B1) * g
    v = ADAM_B2 * v + (1.0 - ADAM_B2) * (g * g)
    m_hat = m / (1.0 - ADAM_B1 ** ADAM_STEP)
    v_hat = v / (1.0 - ADAM_B2 ** ADAM_STEP)
    delta = -ADAM_LR * (m_hat / (jnp.sqrt(v_hat) + ADAM_EPS) + ADAM_WD * w)
    return delta, m, v


def _adamw(w, g, m, v, name):
    shape = w.shape
    C = shape[-1]
    R = w.size // C
    tr = _tile(R, 128)
    flat = lambda t: t.reshape(R, C)

    def body(w_ref, g_ref, m_ref, v_ref, d_ref, nm_ref, nv_ref):
        d, nm, nv = _adamw_math(w_ref[...], g_ref[...], m_ref[...], v_ref[...])
        d_ref[...] = d
        nm_ref[...] = nm
        nv_ref[...] = nv

    blk = pl.BlockSpec((tr, C), lambda i: (i, 0))
    os = jax.ShapeDtypeStruct((R, C), F32)
    outs = pl.pallas_call(body, name=name, grid=(R // tr,), in_specs=[blk] * 4, out_specs=[blk] * 3, out_shape=[os] * 3,
                          compiler_params=_params(("parallel",)))(flat(w), flat(g), flat(m), flat(v))
    return [t.reshape(shape) for t in outs]


def _w_ada_update(c_all, dmod, w, m, v):
    L, D, C = w.shape
    B = c_all.shape[0]
    tn = _tile(C, 256)

    def body(c_ref, dm_ref, w_ref, m_ref, v_ref, g_ref, d_ref, nm_ref, nv_ref):
        cv = c_ref[...]
        g = lax.dot_general(cv * _sig(cv), dm_ref[...], (TN, ((), ())), preferred_element_type=F32,
                            precision=lax.Precision.HIGHEST)
        d, nm, nv = _adamw_math(w_ref[...], g, m_ref[...], v_ref[...])
        g_ref[...] = g
        d_ref[...] = d
        nm_ref[...] = nm
        nv_ref[...] = nv

    blk = pl.BlockSpec((None, D, tn), lambda l, j: (l, 0, j))
    os = jax.ShapeDtypeStruct((L, D, C), F32)
    return pl.pallas_call(
        body, name="w_ada_update", grid=(L, C // tn),
        in_specs=[pl.BlockSpec((B, D), lambda l, j: (0, 0)), pl.BlockSpec((None, B, tn), lambda l, j: (l, 0, j)), blk, blk, blk],
        out_specs=[blk] * 4, out_shape=[os] * 4, compiler_params=_params(("parallel", "parallel")))(c_all, dmod, w, m, v)


def _small_update(parts, lower_bounds, wmv, D, KW):
    L = parts.shape[1]
    widths = [3 * D, D, D, KW, KW, KW]
    offs = [sum(widths[:i]) for i in range(len(widths))]

    def body(p_ref, *refs):
        ins, outs = refs[:18], refs[18:]
        tot = p_ref[0]
        for dev in range(1, N_DEV):
            tot = tot + p_ref[dev]
        grads = [tot[:, o:o + wd] for o, wd in zip(offs, widths)]
        sm = _softmax_rows(ins[9][...])
        dlb = grads[3]
        dsm = [jnp.zeros((1, KW), F32)]
        for j in range(1, L):
            acc = dlb[j:j + 1]
            for l in range(j + 1, L):
                acc = acc + dlb[l:l + 1]
            dsm.append(acc)
        dsm = jnp.concatenate(dsm, axis=0)
        grads[3] = sm * (dsm - jnp.sum(sm * dsm, axis=0, keepdims=True))
        for p in range(6):
            w_ref, m_ref, v_ref = ins[3 * p:3 * p + 3]
            d, nm, nv = _adamw_math(w_ref[...], grads[p], m_ref[...], v_ref[...])
            outs[4 * p][...] = grads[p]
            outs[4 * p + 1][...] = d
            outs[4 * p + 2][...] = nm
            outs[4 * p + 3][...] = nv

    flat = [t for trip in wmv for t in trip]
    out_shape = []
    for wd in widths:
        out_shape += [jax.ShapeDtypeStruct((L, wd), F32)] * 4
    res = pl.pallas_call(body, name="small_update", out_shape=out_shape,
                         compiler_params=pltpu.CompilerParams(vmem_limit_bytes=VMEM_LIMIT_BYTES))(parts, *flat)
    return [res[4 * p:4 * p + 4] for p in range(6)]


def _place():
    x, y, c = lax.axis_index("x"), lax.axis_index("y"), lax.axis_index("c")
    chips = [(1 - x, y), (x, 1 - y), (1 - x, 1 - y)]
    return x, y, c, chips


def _all_gather_rows(blk):
    m_per, n = blk.shape

    def body(x_ref, out_ref, send_sems, recv_sems, local_sem):
        x, y, c, chips = _place()
        me, sibling = (x, y, c), (x, y, 1 - c)

        def rows(px, py, pc):
            return out_ref.at[pl.ds((4 * px + 2 * py + pc) * m_per, m_per), :]

        def copy(k, block, to, src=None):
            return pltpu.make_async_remote_copy(src_ref=rows(*block) if src is None else src, dst_ref=rows(*block),
                                                send_sem=send_sems.at[k], recv_sem=recv_sems.at[k], device_id=to, device_id_type=MESH)

        mine = pltpu.make_async_copy(x_ref, rows(*me), local_sem)
        mine.start()
        first = [copy(0, me, sibling, src=x_ref)]
        first += [copy(1 + j, me, (*chip, c), src=x_ref) for j, chip in enumerate(chips)]
        for cp in first:
            cp.start()
        passed = [copy(4 + j, (*chip, c), sibling) for j, chip in enumerate(chips)]
        for j, chip in enumerate(chips):
            copy(1 + j, (*chip, c), me).wait_recv()
            passed[j].start()
        copy(0, sibling, me).wait_recv()
        for j, chip in enumerate(chips):
            copy(4 + j, (*chip, 1 - c), me).wait_recv()
        for cp in first + passed:
            cp.wait_send()
        mine.wait()

    return pl.pallas_call(
        body, name="all_gather_rows", out_shape=jax.ShapeDtypeStruct((N_DEV * m_per, n), blk.dtype),
        in_specs=[pl.BlockSpec(memory_space=pltpu.VMEM)], out_specs=pl.BlockSpec(memory_space=pltpu.VMEM),
        scratch_shapes=[pltpu.SemaphoreType.DMA((7,)), pltpu.SemaphoreType.DMA((7,)), pltpu.SemaphoreType.DMA],
        compiler_params=pltpu.CompilerParams(vmem_limit_bytes=VMEM_LIMIT_BYTES))(blk)


SHARD_AXIS = (1, 1, 1, 1, 0)
HALF_AXIS = (0, 0, 0, 0, 1)
N_STACKS = 5


def _cut(ref, axis, part, n):
    idx = [slice(None)] * len(ref.shape)
    idx[axis] = pl.ds(pl.multiple_of(part * n, n), n)
    return ref.at[tuple(idx)]


def _quarter(ref, t, chip):
    return _cut(ref, SHARD_AXIS[t], chip, ref.shape[SHARD_AXIS[t]] // N_CHIPS)


def _half(ref, t, core):
    return _cut(ref, HALF_AXIS[t], core, ref.shape[HALF_AXIS[t]] // 2)


def _with_axis(shape, axis, n):
    return tuple(n if a == axis else s for a, s in enumerate(shape))


def _view3(t):
    return t.reshape((1,) * (3 - t.ndim) + t.shape)


def _axis3(t, axis):
    return axis + 3 - t.ndim


def _blocked(shape3, blk, offsets=None):
    offsets = offsets or {}

    def index(b, r, c, *pre):
        idx = [b, r, c]
        for ax, fn in offsets.items():
            idx[ax] = idx[ax] + fn(*pre)
        return tuple(idx)

    return pl.BlockSpec((None,) + tuple(blk), index)


def _cast_into_full(stack, l, t, chip_idx):
    shard = stack.shape[1:]
    nd = len(shard)
    w4 = stack.reshape((stack.shape[0],) + (1,) * (3 - nd) + shard)
    ax = SHARD_AXIS[t] + 3 - nd
    _, B, R, C = w4.shape
    tr = _tile(R, 256)
    per = (R // tr, 1)[ax - 1]

    def body(j_ref, w_ref, o_ref):
        o_ref[...] = w_ref[...].astype(o_ref.dtype)

    full3 = _with_axis((B, R, C), ax, N_CHIPS * (B, R, C)[ax])
    grid_spec = pltpu.PrefetchScalarGridSpec(
        num_scalar_prefetch=1, grid=(B, R // tr, 1),
        in_specs=[pl.BlockSpec((None, None, tr, C), lambda b, r, c, j_ref: (l, b, r, c))],
        out_specs=_blocked(full3, (tr, C), {ax: lambda j_ref: j_ref[0] * per}))
    o = pl.pallas_call(body, name=f"cast_into_full{t}", grid_spec=grid_spec, out_shape=jax.ShapeDtypeStruct(full3, MXU_DT),
                       compiler_params=_params(("parallel",) * 3))(chip_idx, w4)
    return o.reshape(_with_axis(shard, SHARD_AXIS[t], N_CHIPS * shard[SHARD_AXIS[t]]))


def _remote(src, dst, send_sems, recv_sems, k, to):
    return pltpu.make_async_remote_copy(src_ref=src, dst_ref=dst, send_sem=send_sems.at[k], recv_sem=recv_sems.at[k],
                                        device_id=to, device_id_type=MESH)


def _side(inputs, out_shapes, aliases, n_sems, copies):
    def start(ins, outs, ss, rs):
        for snd, _ in copies(ins, outs, ss, rs):
            snd.start()

    def finish(ins, outs, ss, rs):
        for snd, rcv in copies(ins, outs, ss, rs):
            rcv.wait_recv()
            snd.wait_send()

    return dict(inputs=list(inputs), out_shapes=list(out_shapes), aliases=aliases, n_sems=n_sems, start=start, finish=finish)


def _run_side(side, name):
    n_in, n_out = len(side["inputs"]), len(side["out_shapes"])

    def body(*refs):
        ins, outs, sems = refs[:n_in], refs[n_in:n_in + n_out], refs[n_in + n_out:]
        side["start"](ins, outs, *sems)
        side["finish"](ins, outs, *sems)

    any_spec = pl.BlockSpec(memory_space=pl.ANY)
    return list(pl.pallas_call(
        body, name=name, out_shape=side["out_shapes"], in_specs=[any_spec] * n_in, out_specs=[any_spec] * n_out,
        input_output_aliases=side["aliases"],
        scratch_shapes=[pltpu.SemaphoreType.DMA((side["n_sems"],)), pltpu.SemaphoreType.DMA((side["n_sems"],))])(*side["inputs"]))


def _same(arrays):
    return [jax.ShapeDtypeStruct(a.shape, a.dtype) for a in arrays]


def _gather_ici_side(ws):
    def copies(ins, outs, ss, rs):
        x, y, c, chips = _place()
        out = []
        for k, (cx, cy) in enumerate(chips):
            for t in range(N_STACKS):
                mine = _half(_quarter(outs[t], t, 2 * x + y), t, c)
                theirs = _half(_quarter(outs[t], t, 2 * cx + cy), t, c)
                n = k * N_STACKS + t
                out.append((_remote(mine, mine, ss, rs, n, (cx, cy, c)), _remote(theirs, theirs, ss, rs, n, (cx, cy, c))))
        return out

    return _side(ws, _same(ws), {t: t for t in range(N_STACKS)}, 3 * N_STACKS, copies)


def _gather_d2d_side(ws):
    def copies(ins, outs, ss, rs):
        x, y, c, chips = _place()
        out = []
        for k, (cx, cy) in enumerate(chips):
            for t in range(N_STACKS):
                q = _quarter(outs[t], t, 2 * cx + cy)
                mine, theirs = _half(q, t, c), _half(q, t, 1 - c)
                n = k * N_STACKS + t
                out.append((_remote(mine, mine, ss, rs, n, (x, y, 1 - c)), _remote(theirs, theirs, ss, rs, n, (x, y, 1 - c))))
        return out

    return _side(ws, _same(ws), {t: t for t in range(N_STACKS)}, 3 * N_STACKS, copies)


def _swap_side(gs):
    shapes = [jax.ShapeDtypeStruct(_with_axis(g.shape, HALF_AXIS[t], g.shape[HALF_AXIS[t]] // 2), g.dtype) for t, g in enumerate(gs)]

    def copies(ins, outs, ss, rs):
        x, y, c, _ = _place()
        cps = [_remote(_half(ins[t], t, 1 - c), outs[t], ss, rs, t, (x, y, 1 - c)) for t in range(N_STACKS)]
        return [(cp, cp) for cp in cps]

    return _side(gs, shapes, {}, N_STACKS, copies)


def _pair_sum(g, recv, t, c_idx):
    g3, r3 = _view3(g), _view3(recv)
    hf_ax = _axis3(g, HALF_AXIS[t])
    B, R, C = r3.shape
    tr, tc = _tile(R, 256), _tile(C, 2048)
    grid = (B, R // tr, C // tc)

    def body(c_ref, g_ref, r_ref, o_ref):
        o_ref[...] = (g_ref[...] + r_ref[...]).astype(o_ref.dtype)

    grid_spec = pltpu.PrefetchScalarGridSpec(
        num_scalar_prefetch=1, grid=grid,
        in_specs=[_blocked(g3.shape, (tr, tc), {hf_ax: lambda c_ref: c_ref[0] * grid[hf_ax]}), _blocked(r3.shape, (tr, tc))],
        out_specs=_blocked(r3.shape, (tr, tc)))
    out = pl.pallas_call(body, name=f"pair_sum{t}", grid_spec=grid_spec, out_shape=jax.ShapeDtypeStruct(r3.shape, BF16),
                         compiler_params=_params(("parallel",) * 3))(c_idx, g3, r3)
    return out.reshape(recv.shape)


def _scatter_side(ps):
    shapes = [jax.ShapeDtypeStruct((N_CHIPS - 1,) + _with_axis(p.shape, SHARD_AXIS[t], p.shape[SHARD_AXIS[t]] // N_CHIPS), p.dtype)
              for t, p in enumerate(ps)]

    def copies(ins, outs, ss, rs):
        x, y, c, chips = _place()
        cps = [_remote(_quarter(ins[t], t, 2 * cx + cy), outs[t].at[k], ss, rs, k * N_STACKS + t, (cx, cy, c))
               for k, (cx, cy) in enumerate(chips) for t in range(N_STACKS)]
        return [(cp, cp) for cp in cps]

    return _side(ps, shapes, {}, 3 * N_STACKS, copies)


def _sum_chips(p, slots, t, chip_idx, c_idx):
    p3 = _view3(p)
    sh_ax, hf_ax = _axis3(p, SHARD_AXIS[t]), _axis3(p, HALF_AXIS[t])
    piece3 = _with_axis(p3.shape, sh_ax, p3.shape[sh_ax] // N_CHIPS)
    s4 = slots.reshape((N_CHIPS - 1,) + piece3)
    B, R, C = piece3
    tr = _tile(R, 256)
    grid = (B, R // tr, 1)
    full3 = _with_axis(piece3, hf_ax, 2 * piece3[hf_ax])

    def body(j_ref, c_ref, own_ref, s_ref, o_ref):
        acc = own_ref[...].astype(F32)
        for k in range(N_CHIPS - 1):
            acc = acc + s_ref[k].astype(F32)
        o_ref[...] = acc

    grid_spec = pltpu.PrefetchScalarGridSpec(
        num_scalar_prefetch=2, grid=grid,
        in_specs=[_blocked(p3.shape, (tr, C), {sh_ax: lambda j_ref, c_ref: j_ref[0] * grid[sh_ax]}),
                  pl.BlockSpec((N_CHIPS - 1, None, tr, C), lambda b, r, cc, j_ref, c_ref: (0, b, r, cc))],
        out_specs=_blocked(full3, (tr, C), {hf_ax: lambda j_ref, c_ref: c_ref[0] * grid[hf_ax]}))
    out = pl.pallas_call(body, name=f"sum_chips{t}", grid_spec=grid_spec, out_shape=jax.ShapeDtypeStruct(full3, F32),
                         compiler_params=_params(("parallel",) * 3))(chip_idx, c_idx, p3, s4)
    return out.reshape(_with_axis(slots.shape[1:], HALF_AXIS[t], 2 * slots.shape[1 + HALF_AXIS[t]]))


def _join_side(rs_):
    def copies(ins, outs, ss, rs):
        x, y, c, _ = _place()
        return [(_remote(_half(outs[t], t, c), _half(outs[t], t, c), ss, rs, t, (x, y, 1 - c)),
                 _remote(_half(outs[t], t, 1 - c), _half(outs[t], t, 1 - c), ss, rs, t, (x, y, 1 - c))) for t in range(N_STACKS)]

    return _side(rs_, _same(rs_), {t: t for t in range(N_STACKS)}, N_STACKS, copies)


def _layer_weights(shards, l, chip_idx):
    return [_cast_into_full(s, l, t, chip_idx) for t, s in enumerate(shards)]


def _gather_alone(w):
    return _run_side(_gather_d2d_side(_run_side(_gather_ici_side(w), "gather_ici")), "gather_d2d")


def _forward_layer(x, l, w, small, mods, lb_all, KW, nxt):
    norm_pre, norm_post, hgrn_norm, pool_scale = small
    shift, scale, gate = mods
    row = lambda t: t[l:l + 1]
    h = _prenorm(x, row(norm_pre), row(scale), row(shift))
    if nxt is None:
        proj = _matmul(h, w[0], "nn", F32, "proj")
    else:
        proj, nxt = _matmul(h, w[0], "nn", F32, "proj_gather", side=_gather_ici_side(nxt))
    o_a, y_a, s0 = _hgrn_fwd(proj, row(lb_all), row(hgrn_norm), KW)
    pooled, y_b = _pool_fwd(proj, w[1], row(pool_scale), KW)
    pa, pb, merged = _merge(y_a, y_b, w[2], w[3], proj)
    if nxt is None:
        out = _matmul(merged, w[4], "nn", F32, "out_proj")
    else:
        out, nxt = _matmul(merged, w[4], "nn", F32, "out_proj_gather", side=_gather_d2d_side(nxt))
    x_new = _postnorm(x, out, row(gate), row(norm_post))
    saved = (x, h, proj, o_a, y_a, s0, pooled, y_b, pa, pb, merged, out)
    return x_new, saved, nxt


def _backward_layer(dx, l, saved, w, small, mods, lb_all, KW, above, idx):
    norm_pre, norm_post, hgrn_norm, pool_scale = small
    shift, scale, gate = mods
    chip_idx, c_idx = idx
    x, h, proj, o_a, y_a, s0, pooled, y_b, pa, pb, merged, out = saved
    row = lambda t: t[l:l + 1]
    dout, d_gate, d_npost = _postnorm_bwd(dx, out, row(gate), row(norm_post))
    if above is None:
        g_out = _matmul(merged, dout, "tn", F32, "grad_w_out", tk=1024)
    else:
        g_out, recv = _matmul(merged, dout, "tn", F32, "grad_w_out_swap", tk=1024, side=_swap_side(above))
        pair = [_pair_sum(g, r, t, c_idx) for t, (g, r) in enumerate(zip(above, recv))]
    dpa, dpb, dga, dgb = _merge_bwd(dout, w[4], proj, pa, pb)
    g_pa = _matmul(y_a, dpa, "tn", F32, "grad_w_proj_a", tk=1024)
    g_pb = _matmul(y_b, dpb, "tn", F32, "grad_w_proj_b", tk=1024)
    dya = _matmul(dpa, w[2], "nt", F32, "d_y_a")
    dyb = _matmul(dpb, w[3], "nt", F32, "d_y_b")
    do, dza, d_hn = _gate_a_bwd(dya, o_a, proj, row(hgrn_norm), KW)
    dq, df, dva, d_lb = _hgrn_bwd(proj, do, s0, row(lb_all), KW)
    u, dzb, d_ps, g_pool = _pool_bwd(dyb, proj, pooled, w[1], row(pool_scale), KW)
    dvb = _pool_bwd_window(u, KW)
    dproj = jnp.concatenate([dq, df, dva, dza, dvb, dzb, dga, dgb], axis=1)
    if above is None:
        g_in = _matmul(h, dproj, "tn", F32, "grad_w_in", tk=1024)
        dh = _matmul(dproj, w[0], "nt", F32, "d_h")
        reduced = None
    else:
        g_in, slots = _matmul(h, dproj, "tn", F32, "grad_w_in_scatter", tk=1024, side=_scatter_side(pair))
        part = [_sum_chips(p, s, t, chip_idx, c_idx) for t, (p, s) in enumerate(zip(pair, slots))]
        dh, reduced = _matmul(dproj, w[0], "nt", F32, "d_h_join", side=_join_side(part))
    dx_new, d_shift, d_scale, d_npre = _prenorm_bwd(dh, x, dx, row(norm_pre), row(scale))
    small_g = jnp.concatenate([d_shift, d_scale, d_gate, d_npre, d_npost, d_lb, d_hn, d_ps], axis=1)
    return dx_new, small_g, [g_in, g_pool, g_pa, g_pb, g_out], reduced


def kernel(x, c, w_ada, b_ada, norm_pre, norm_post, w_in, lower_bounds, hgrn_norm, pool_w, pool_scale, w_proj_a, w_proj_b, w_out, loss_target, m_w_ada, m_b_ada, m_norm_pre, m_norm_post, m_w_in, m_lower_bounds, m_hgrn_norm, m_pool_w, m_pool_scale, m_w_proj_a, m_w_proj_b, m_w_out, v_w_ada, v_b_ada, v_norm_pre, v_norm_post, v_w_in, v_lower_bounds, v_hgrn_norm, v_pool_w, v_pool_scale, v_w_proj_a, v_w_proj_b, v_w_out):
    _, S, D = x.shape
    L = w_in.shape[0]
    KW = D // 2
    xi, yi, ci = lax.axis_index("x"), lax.axis_index("y"), lax.axis_index("c")
    chip = 2 * xi + yi
    dev = 4 * xi + 2 * yi + ci

    c_all = _all_gather_rows(c.reshape(8, D // 8)).reshape(N_DEV, D)
    modp = _mod_part(c_all, w_ada)
    cols = modp.shape[-1]
    modg = _all_gather_rows(modp.reshape(L * N_DEV, cols)).reshape(N_DEV, L, N_DEV, cols)
    mod_all = jnp.transpose(modg[0::2], (1, 2, 0, 3)).reshape(L, N_DEV, 3 * D)
    mod = lax.dynamic_index_in_dim(mod_all, dev, axis=1, keepdims=False) + b_ada
    mods = (mod[:, :D], mod[:, D:2 * D], mod[:, 2 * D:])
    lb_all = _lb_table(lower_bounds)

    shards = [w_in, pool_w, w_proj_a, w_proj_b, w_out]
    chip_idx = chip.astype(jnp.int32).reshape(1)
    c_idx = ci.astype(jnp.int32).reshape(1)
    small = (norm_pre, norm_post, hgrn_norm, pool_scale)

    xs = x[0]
    saved, ws = [], []
    w = _gather_alone(_layer_weights(shards, 0, chip_idx))
    for l in range(L):
        nxt = _layer_weights(shards, l + 1, chip_idx) if l + 1 < L else None
        xs, sv, nxt = _forward_layer(xs, l, w, small, mods, lb_all, KW, nxt)
        saved.append(sv)
        ws.append(w)
        w = nxt
    dx, sq = _loss_head(xs, loss_target[0])
    loss = lax.psum(0.5 * jnp.sum(sq) / D, ("x", "y", "c"))

    small_g, reduced = [None] * L, [None] * L
    above = None
    for l in reversed(range(L)):
        dx, small_g[l], above, red = _backward_layer(dx, l, saved[l], ws[l], small, mods, lb_all, KW, above, (chip_idx, c_idx))
        if red is not None:
            reduced[l + 1] = red
    recv = _run_side(_swap_side(above), "swap")
    pair = [_pair_sum(g, r, t, c_idx) for t, (g, r) in enumerate(zip(above, recv))]
    slots = _run_side(_scatter_side(pair), "scatter")
    reduced[0] = _run_side(_join_side([_sum_chips(p, s, t, chip_idx, c_idx) for t, (p, s) in enumerate(zip(pair, slots))]), "join")
    grad_x = dx[None]

    sg = jnp.concatenate(small_g, axis=0)
    sg = jnp.concatenate([sg, jnp.zeros((8 - L, sg.shape[1]), F32)], axis=0)
    parts = _all_gather_rows(sg).reshape(N_DEV, 8, sg.shape[1])[:, :L]
    wmv = [(b_ada, m_b_ada, v_b_ada), (norm_pre, m_norm_pre, v_norm_pre), (norm_post, m_norm_post, v_norm_post),
           (lower_bounds, m_lower_bounds, v_lower_bounds), (hgrn_norm, m_hgrn_norm, v_hgrn_norm), (pool_scale, m_pool_scale, v_pool_scale)]
    r_b_ada, r_npre, r_npost, r_lb, r_hn, r_ps = _small_update(parts, lower_bounds, wmv, D, KW)
    dmod = lax.dynamic_slice_in_dim(parts[:, :, :3 * D], chip * cols, cols, axis=2)
    r_w_ada = _w_ada_update(c_all, jnp.transpose(dmod, (1, 0, 2)), w_ada, m_w_ada, v_w_ada)

    grads = [jnp.stack([reduced[l][t] for l in range(L)], axis=0) for t in range(N_STACKS)]
    ms = [m_w_in, m_pool_w, m_w_proj_a, m_w_proj_b, m_w_out]
    vs = [v_w_in, v_pool_w, v_w_proj_a, v_w_proj_b, v_w_out]
    r_big = [[g] + _adamw(w, g, m, v, f"adamw{t}") for t, (w, g, m, v) in enumerate(zip(shards, grads, ms, vs))]
    r_w_in, r_pool_w, r_pa, r_pb, r_w_out = r_big

    order = [r_w_ada, r_b_ada, r_npre, r_npost, r_w_in, r_lb, r_hn, r_pool_w, r_ps, r_pa, r_pb, r_w_out]
    outs = [loss, grad_x]
    for k in range(4):
        outs += [r[k] for r in order]
    return tuple(outs)
```

```python
import functools

import jax
import jax.numpy as jnp
from jax import lax
from jax.experimental import pallas as pl
from jax.experimental.pallas import tpu as pltpu

F32 = jnp.float32
BF16 = jnp.bfloat16
MXU_DT = jnp.bfloat16

CHUNK = 64
SUBLANES = 8
SUB = 8
LEVELS = (32, 16, 8)
HEAD = 128
EPS = 1e-6
MIN_FORGET = 1e-30
WINDOWS = (2, 4, 8, 16)
HALO = 16
N_CHIPS = 4
N_DEV = 8
VMEM_LIMIT_BYTES = 56 * 1024 * 1024

ADAM_LR = 0.001
ADAM_B1 = 0.9
ADAM_B2 = 0.999
ADAM_EPS = 1e-08
ADAM_WD = 0.01
ADAM_STEP = 10

NN = ((1,), (0,))
NT = ((1,), (1,))
TN = ((0,), (0,))
MESH = pl.DeviceIdType.MESH


def _params(sem):
    return pltpu.CompilerParams(dimension_semantics=sem, vmem_limit_bytes=VMEM_LIMIT_BYTES)


def _tile(n, pref):
    return pref if n % pref == 0 else n


def _dot(a, b, dims):
    return lax.dot_general(a.astype(MXU_DT), b.astype(MXU_DT), (dims, ((), ())), preferred_element_type=F32)


def _sig_pair(a):
    e = jnp.exp(-jnp.abs(a))
    inv = 1.0 / (1.0 + e)
    pos = a >= 0
    return jnp.where(pos, inv, e * inv), jnp.where(pos, e * inv, inv)


def _sig(a):
    return _sig_pair(a)[0]


def _split_dot(tri, x):
    def top(y):
        return lax.bitcast_convert_type(lax.bitcast_convert_type(y, jnp.uint32) & jnp.uint32(0xFFFF0000), F32)

    hi = top(x)
    r1 = x - hi
    mid = top(r1)
    lo = r1 - mid
    d = lambda y: jnp.dot(tri, y.astype(BF16), preferred_element_type=F32)
    return d(hi) + d(mid) + d(lo)


def _rms(xv):
    return lax.rsqrt(jnp.mean(xv * xv, axis=-1, keepdims=True) + EPS)


def _prenorm(x, g, scale, shift):
    S, D = x.shape
    tm = _tile(S, 256)

    def body(x_ref, g_ref, sc_ref, sh_ref, h_ref, ht_ref):
        xv = x_ref[...]
        h = (xv * _rms(xv)) * g_ref[...] * (1.0 + sc_ref[...]) + sh_ref[...]
        h_ref[...] = h.astype(h_ref.dtype)
        ht_ref[...] = h.T.astype(ht_ref.dtype)

    row = pl.BlockSpec((tm, D), lambda i: (i, 0))
    vec = pl.BlockSpec((1, D), lambda i: (0, 0))
    return pl.pallas_call(body, name="prenorm", grid=(S // tm,), in_specs=[row, vec, vec, vec],
                          out_specs=[row, pl.BlockSpec((D, tm), lambda i: (0, i))],
                          out_shape=[jax.ShapeDtypeStruct((S, D), MXU_DT), jax.ShapeDtypeStruct((D, S), MXU_DT)],
                          compiler_params=_params(("parallel",)))(x, g, scale, shift)


def _prenorm_bwd(dh, x, dx, g, scale):
    S, D = x.shape
    tm = _tile(S, 256)

    def body(dh_ref, x_ref, dx_ref, g_ref, sc_ref, o_ref, dsh_ref, dsc_ref, dg_ref):
        @pl.when(pl.program_id(0) == 0)
        def _():
            dsh_ref[...] = jnp.zeros_like(dsh_ref)
            dsc_ref[...] = jnp.zeros_like(dsc_ref)
            dg_ref[...] = jnp.zeros_like(dg_ref)

        xv = x_ref[...]
        r = _rms(xv)
        xh = xv * r
        dh = dh_ref[...]
        gv = g_ref[...]
        one_sc = 1.0 + sc_ref[...]
        dsh_ref[...] += jnp.sum(dh, axis=0, keepdims=True)
        dsc_ref[...] += jnp.sum(dh * xh * gv, axis=0, keepdims=True)
        dg_ref[...] += jnp.sum(dh * xh * one_sc, axis=0, keepdims=True)
        dxh = dh * gv * one_sc
        o_ref[...] = dx_ref[...] + r * (dxh - xh * jnp.mean(dxh * xh, axis=-1, keepdims=True))

    row = pl.BlockSpec((tm, D), lambda i: (i, 0))
    vec = pl.BlockSpec((1, D), lambda i: (0, 0))
    vs = jax.ShapeDtypeStruct((1, D), F32)
    return pl.pallas_call(body, name="prenorm_bwd", grid=(S // tm,), in_specs=[row, row, row, vec, vec],
                          out_specs=[row, vec, vec, vec], out_shape=[jax.ShapeDtypeStruct((S, D), F32), vs, vs, vs],
                          compiler_params=_params(("arbitrary",)))(dh, x, dx, g, scale)


def _postnorm(x, out, gate, g):
    S, D = x.shape
    tm = _tile(S, 256)

    def body(x_ref, o_ref, gt_ref, g_ref, y_ref):
        ov = o_ref[...]
        y_ref[...] = x_ref[...] + gt_ref[...] * ((ov * _rms(ov)) * g_ref[...])

    row = pl.BlockSpec((tm, D), lambda i: (i, 0))
    vec = pl.BlockSpec((1, D), lambda i: (0, 0))
    return pl.pallas_call(body, name="postnorm", grid=(S // tm,), in_specs=[row, row, vec, vec], out_specs=row,
                          out_shape=jax.ShapeDtypeStruct((S, D), F32), compiler_params=_params(("parallel",)))(x, out, gate, g)


def _postnorm_bwd(dx, out, gate, g):
    S, D = dx.shape
    tm = _tile(S, 256)

    def body(dx_ref, o_ref, gt_ref, g_ref, do_ref, dgt_ref, dg_ref):
        @pl.when(pl.program_id(0) == 0)
        def _():
            dgt_ref[...] = jnp.zeros_like(dgt_ref)
            dg_ref[...] = jnp.zeros_like(dg_ref)

        ov = o_ref[...]
        r = _rms(ov)
        rn = ov * r
        dxv = dx_ref[...]
        gv = g_ref[...]
        dgt_ref[...] += jnp.sum(dxv * rn * gv, axis=0, keepdims=True)
        du = dxv * gt_ref[...]
        dg_ref[...] += jnp.sum(du * rn, axis=0, keepdims=True)
        drn = du * gv
        do_ref[...] = (r * (drn - rn * jnp.mean(drn * rn, axis=-1, keepdims=True))).astype(do_ref.dtype)

    row = pl.BlockSpec((tm, D), lambda i: (i, 0))
    vec = pl.BlockSpec((1, D), lambda i: (0, 0))
    vs = jax.ShapeDtypeStruct((1, D), F32)
    return pl.pallas_call(body, name="postnorm_bwd", grid=(S // tm,), in_specs=[row, row, vec, vec],
                          out_specs=[row, vec, vec], out_shape=[jax.ShapeDtypeStruct((S, D), MXU_DT), vs, vs],
                          compiler_params=_params(("arbitrary",)))(dx, out, gate, g)


def _loss_head(y, target):
    S, D = y.shape
    tm = _tile(S, 256)

    def body(y_ref, t_ref, dy_ref, sq_ref):
        @pl.when(pl.program_id(0) == 0)
        def _():
            sq_ref[...] = jnp.zeros_like(sq_ref)

        e = y_ref[...] - t_ref[...]
        dy_ref[...] = e * (1.0 / D)
        sq_ref[...] += jnp.sum(e * e, axis=0, keepdims=True)

    row = pl.BlockSpec((tm, D), lambda i: (i, 0))
    vec = pl.BlockSpec((1, D), lambda i: (0, 0))
    return pl.pallas_call(body, name="loss_head", grid=(S // tm,), in_specs=[row, row], out_specs=[row, vec],
                          out_shape=[jax.ShapeDtypeStruct((S, D), F32), jax.ShapeDtypeStruct((1, D), F32)],
                          compiler_params=_params(("arbitrary",)))(y, target)


def _matmul(a, b, mode, out_dtype, name, *, side=None, tm=1024, tn=1024, tk=2048):
    if mode == "tn":
        K, M = a.shape
    else:
        M, K = a.shape
    N = b.shape[0] if mode == "nt" else b.shape[1]
    tm, tn, tk = _tile(M, tm), _tile(N, tn), _tile(K, tk)
    ni, nj, nk = M // tm, N // tn, K // tk
    dims = {"nn": NN, "nt": NT, "tn": TN}[mode]

    if mode == "tn":
        a_spec = pl.BlockSpec((tk, tm), lambda i, j, k: (k, i))
    else:
        a_spec = pl.BlockSpec((tm, tk), lambda i, j, k: (i, k))
    if mode == "nt":
        b_spec = pl.BlockSpec((tn, tk), lambda i, j, k: (j, k))
    else:
        b_spec = pl.BlockSpec((tk, tn), lambda i, j, k: (k, j))
    n_in = len(side["inputs"]) if side else 0
    n_out = len(side["out_shapes"]) if side else 0

    def body(a_ref, b_ref, *rest):
        side_in, o_ref, side_out = rest[:n_in], rest[n_in], rest[n_in + 1:n_in + 1 + n_out]
        scratch = rest[n_in + 1 + n_out:]
        i, j, k = pl.program_id(0), pl.program_id(1), pl.program_id(2)
        if side:
            sems = scratch[-2:]

            @pl.when((i == 0) & (j == 0) & (k == 0))
            def _():
                side["start"](side_in, side_out, *sems)

        p = lax.dot_general(a_ref[...], b_ref[...], (dims, ((), ())), preferred_element_type=F32)
        if nk == 1:
            o_ref[...] = p.astype(o_ref.dtype)
        else:
            acc_ref = scratch[0]

            @pl.when(k == 0)
            def _():
                acc_ref[...] = p

            @pl.when(k > 0)
            def _():
                acc_ref[...] += p

            @pl.when(k == nk - 1)
            def _():
                o_ref[...] = acc_ref[...].astype(o_ref.dtype)

        if side:
            @pl.when((i == ni - 1) & (j == nj - 1) & (k == nk - 1))
            def _():
                side["finish"](side_in, side_out, *sems)

    any_spec = pl.BlockSpec(memory_space=pl.ANY)
    scratch_shapes = [pltpu.VMEM((tm, tn), F32)] if nk > 1 else []
    if side:
        scratch_shapes += [pltpu.SemaphoreType.DMA((side["n_sems"],)), pltpu.SemaphoreType.DMA((side["n_sems"],))]
    res = pl.pallas_call(
        body, name=name, grid=(ni, nj, nk), in_specs=[a_spec, b_spec] + [any_spec] * n_in,
        out_specs=[pl.BlockSpec((tm, tn), lambda i, j, k: (i, j))] + [any_spec] * n_out,
        out_shape=[jax.ShapeDtypeStruct((M, N), out_dtype)] + (list(side["out_shapes"]) if side else []),
        scratch_shapes=scratch_shapes,
        input_output_aliases={2 + s: 1 + d for s, d in side["aliases"].items()} if side else {},
        compiler_params=_params(("arbitrary",) * 3 if side else ("parallel", "parallel", "arbitrary")))(*([a, b] + (list(side["inputs"]) if side else [])))
    return (res[0], list(res[1:])) if side else res[0]


def _merge(ya, yb, wpa, wpb, proj):
    S, KW = ya.shape
    D = 2 * KW
    tm, tn = _tile(S, 1024), _tile(D, 512)
    ga0, gb0 = 3 * D // tn, 4 * D // tn

    def body(ya_ref, yb_ref, wa_ref, wb_ref, ga_ref, gb_ref, pa_ref, pb_ref, m_ref, mt_ref):
        pa = jnp.dot(ya_ref[...], wa_ref[...], preferred_element_type=F32)
        pb = jnp.dot(yb_ref[...], wb_ref[...], preferred_element_type=F32)
        pa_ref[...] = pa.astype(pa_ref.dtype)
        pb_ref[...] = pb.astype(pb_ref.dtype)
        m = _sig(ga_ref[...]) * pa + _sig(gb_ref[...]) * pb
        m_ref[...] = m.astype(m_ref.dtype)
        mt_ref[...] = m.T.astype(mt_ref.dtype)

    y_spec = pl.BlockSpec((tm, KW), lambda i, j: (i, 0))
    w_spec = pl.BlockSpec((KW, tn), lambda i, j: (0, j))
    o_spec = pl.BlockSpec((tm, tn), lambda i, j: (i, j))
    return pl.pallas_call(
        body, name="merge", grid=(S // tm, D // tn),
        in_specs=[y_spec, y_spec, w_spec, w_spec, pl.BlockSpec((tm, tn), lambda i, j: (i, ga0 + j)),
                  pl.BlockSpec((tm, tn), lambda i, j: (i, gb0 + j))],
        out_specs=[o_spec, o_spec, o_spec, pl.BlockSpec((tn, tm), lambda i, j: (j, i))],
        out_shape=[jax.ShapeDtypeStruct((S, D), MXU_DT), jax.ShapeDtypeStruct((S, D), MXU_DT), jax.ShapeDtypeStruct((S, D), MXU_DT),
                   jax.ShapeDtypeStruct((D, S), MXU_DT)],
        compiler_params=_params(("parallel", "parallel")))(ya, yb, wpa, wpb, proj, proj)


def _merge_bwd(dout, w_out, proj, pa, pb):
    S, D = dout.shape
    tm, tn = _tile(S, 1024), _tile(D, 512)
    ga0, gb0 = 3 * D // tn, 4 * D // tn

    def body(do_ref, w_ref, ga_ref, gb_ref, pa_ref, pb_ref, dpa_ref, dpb_ref, dga_ref, dgb_ref):
        dm = lax.dot_general(do_ref[...], w_ref[...], (NT, ((), ())), preferred_element_type=F32)
        sa, sna = _sig_pair(ga_ref[...])
        sb, snb = _sig_pair(gb_ref[...])
        dpa = dm * sa
        dpb = dm * sb
        dpa_ref[...] = dpa.astype(dpa_ref.dtype)
        dpb_ref[...] = dpb.astype(dpb_ref.dtype)
        dga_ref[...] = (dpa * pa_ref[...].astype(F32) * sna).astype(dga_ref.dtype)
        dgb_ref[...] = (dpb * pb_ref[...].astype(F32) * snb).astype(dgb_ref.dtype)

    blk = pl.BlockSpec((tm, tn), lambda i, j: (i, j))
    os = jax.ShapeDtypeStruct((S, D), MXU_DT)
    return pl.pallas_call(
        body, name="merge_bwd", grid=(S // tm, D // tn),
        in_specs=[pl.BlockSpec((tm, D), lambda i, j: (i, 0)), pl.BlockSpec((tn, D), lambda i, j: (j, 0)),
                  pl.BlockSpec((tm, tn), lambda i, j: (i, ga0 + j)), pl.BlockSpec((tm, tn), lambda i, j: (i, gb0 + j)), blk, blk],
        out_specs=[blk, blk, blk, blk], out_shape=[os, os, os, os],
        compiler_params=_params(("parallel", "parallel")))(dout, w_out, proj, proj, pa, pb)


def _gates(qr, a, lbv):
    sq = _sig(qr)
    q = qr * sq
    s, sn = _sig_pair(a)
    omlb = 1.0 - lbv
    f = lbv + omlb * s
    logf = jnp.log(jnp.maximum(f, MIN_FORGET))
    kk = omlb * sn
    return sq, q, s, sn, omlb, f, logf, kk


def _shift_in_block(x, d):
    return pltpu.roll(x.reshape(CHUNK // SUBLANES, SUBLANES, HEAD), d, axis=1).reshape(CHUNK, HEAD)


def _level_masks(h):
    ti = lax.broadcasted_iota(jnp.int32, (CHUNK, 1), 0)
    si = lax.broadcasted_iota(jnp.int32, (1, CHUNK), 1)
    t2, s2 = (ti & (2 * h - 1)) >= h, (si & (2 * h - 1)) >= h
    same = (ti & -(2 * h)) == (si & -(2 * h))
    return t2, jnp.where(same & t2 & jnp.logical_not(s2), 1.0, 0.0)


def _level_factor(b, h):
    parts = [jnp.broadcast_to(b[g + h - 1:g + h], (2 * h, HEAD)) for g in range(0, CHUNK, 2 * h)]
    ref = parts[0] if len(parts) == 1 else jnp.concatenate(parts, axis=0)
    return jnp.exp(-jnp.abs(b - ref))


def _level_operands(q, kk, b, h, t2):
    fac = _level_factor(b, h)
    return jnp.where(t2, q * fac, 0.0), jnp.where(t2, 0.0, kk * fac), fac


def _heads_per_step(H, pref):
    return pref if H % pref == 0 else 1


def _tri(lower):
    r = lax.broadcasted_iota(jnp.int32, (CHUNK, CHUNK), 0)
    c = lax.broadcasted_iota(jnp.int32, (CHUNK, CHUNK), 1)
    return jnp.where((r >= c) if lower else (c >= r), 1.0, 0.0).astype(BF16)


def _hgrn_fwd(proj, lb, hn, KW):
    S = proj.shape[0]
    H = KW // HEAD
    T = _tile(S, 512)
    nci, nb = T // CHUNK, S // T
    HP = _heads_per_step(H, 8)

    def body(q_ref, f_ref, v_ref, z_ref, lb_ref, hn_ref, o_ref, y_ref, yt_ref, s0_ref, st_ref):
        @pl.when(pl.program_id(1) == 0)
        def _():
            st_ref[...] = jnp.zeros_like(st_ref)

        tril = _tri(True)
        rmod = lax.broadcasted_iota(jnp.int32, (CHUNK, 1), 0) & (SUB - 1)
        masks = [_level_masks(h) for h in LEVELS]
        lb_all = lb_ref[...]

        def chunk(ci, carry):
            rows = pl.ds(pl.multiple_of(ci * CHUNK, CHUNK), CHUNK)
            _, q_all, _, _, _, _, logf_all, kk_all = _gates(q_ref[rows, :], f_ref[rows, :], lb_all)
            b_all = _split_dot(tril, logf_all)
            for hp in range(HP):
                cols = slice(hp * HEAD, (hp + 1) * HEAD)
                one_head(ci, rows, hp, cols, q_all[:, cols], kk_all[:, cols], b_all[:, cols])
            return carry

        def one_head(ci, rows, hp, cols, q, kk, b):
            hnv = hn_ref[:, cols]
            vv = v_ref[rows, cols]
            eb = jnp.exp(b)
            st = st_ref[hp]
            s0_ref[ci, hp] = st
            p = None
            for h, (t2, m) in zip(LEVELS, masks):
                qs, ks, _ = _level_operands(q, kk, b, h, t2)
                pm = m * _dot(qs, ks, NT)
                p = pm if p is None else p + pm
            o = _dot(q * eb, st, NT) + _dot(p, vv, NN) + jnp.sum(q * kk, axis=-1, keepdims=True) * vv
            for d in range(1, SUB):
                kd_, bd_, vd_ = _shift_in_block(kk, d), _shift_in_block(b, d), _shift_in_block(vv, d)
                e = jnp.exp(jnp.minimum(b - bd_, 0.0))
                p = jnp.where(rmod >= d, jnp.sum(q * (kd_ * e), axis=-1, keepdims=True), 0.0)
                o = o + p * vd_
            bl = b[CHUNK - 1:CHUNK]
            st_ref[hp] = st * eb[CHUNK - 1:CHUNK] + _dot(vv, kk * jnp.exp(bl - b), TN)
            o_ref[rows, cols] = o
            z = z_ref[rows, cols]
            y_ref[rows, cols] = ((o * _rms(o)) * hnv * (z * _sig(z))).astype(y_ref.dtype)

        lax.fori_loop(0, nci, chunk, 0)
        yt_ref[...] = y_ref[...].astype(F32).T.astype(yt_ref.dtype)

    W = HP * HEAD
    G = H // HP

    def col(off):
        return pl.BlockSpec((T, W), lambda h, t: (t, off + h))

    vec = pl.BlockSpec((1, W), lambda h, t: (0, h))
    return pl.pallas_call(
        body, name="hgrn_fwd", grid=(G, nb),
        in_specs=[col(0), col(G), col(2 * G), col(3 * G), vec, vec],
        out_specs=[pl.BlockSpec((T, W), lambda h, t: (t, h)), pl.BlockSpec((T, W), lambda h, t: (t, h)),
                   pl.BlockSpec((W, T), lambda h, t: (h, t)), pl.BlockSpec((nci, HP, HEAD, HEAD), lambda h, t: (t, h, 0, 0))],
        out_shape=[jax.ShapeDtypeStruct((S, KW), F32), jax.ShapeDtypeStruct((S, KW), MXU_DT),
                   jax.ShapeDtypeStruct((KW, S), MXU_DT), jax.ShapeDtypeStruct((S // CHUNK, H, HEAD, HEAD), F32)],
        scratch_shapes=[pltpu.VMEM((HP, HEAD, HEAD), F32)],
        compiler_params=_params(("parallel", "arbitrary")))(proj, proj, proj, proj, lb, hn)


def _gate_a_bwd(dya, o, proj, hn, KW):
    S = o.shape[0]
    H = KW // HEAD
    T = _tile(S, 512)

    def body(dy_ref, o_ref, z_ref, hn_ref, do_ref, dz_ref, dhn_ref):
        @pl.when(pl.program_id(1) == 0)
        def _():
            dhn_ref[...] = jnp.zeros_like(dhn_ref)

        ov = o_ref[...]
        r = _rms(ov)
        rn = ov * r
        z = z_ref[...]
        sz = _sig(z)
        silu = z * sz
        dy = dy_ref[...]
        hnv = hn_ref[...]
        dz_ref[...] = (dy * rn * hnv * (sz * (1.0 + z * (1.0 - sz)))).astype(dz_ref.dtype)
        dhn_ref[...] += jnp.sum(dy * rn * silu, axis=0, keepdims=True)
        drn = dy * hnv * silu
        do_ref[...] = r * (drn - rn * jnp.mean(drn * rn, axis=-1, keepdims=True))

    blk = pl.BlockSpec((T, HEAD), lambda h, t: (t, h))
    vec = pl.BlockSpec((1, HEAD), lambda h, t: (0, h))
    return pl.pallas_call(
        body, name="gate_a_bwd", grid=(H, S // T),
        in_specs=[blk, blk, pl.BlockSpec((T, HEAD), lambda h, t: (t, 3 * H + h)), vec],
        out_specs=[blk, blk, vec],
        out_shape=[jax.ShapeDtypeStruct((S, KW), F32), jax.ShapeDtypeStruct((S, KW), MXU_DT), jax.ShapeDtypeStruct((1, KW), F32)],
        compiler_params=_params(("parallel", "arbitrary")))(dya, o, proj, hn)


def _hgrn_bwd(proj, do, s0, lb, KW):
    S = proj.shape[0]
    H = KW // HEAD
    T = _tile(S, 512)
    nci, nb = T // CHUNK, S // T
    nsub = CHUNK // SUB
    HP = _heads_per_step(H, 4)

    def body(q_ref, f_ref, v_ref, do_ref, s0_ref, lb_ref, dq_ref, df_ref, dv_ref, dlb_ref, dst_ref):
        @pl.when(pl.program_id(1) == 0)
        def _():
            dst_ref[...] = jnp.zeros_like(dst_ref)
            dlb_ref[...] = jnp.zeros_like(dlb_ref)

        tril, triu = _tri(True), _tri(False)
        rmod = lax.broadcasted_iota(jnp.int32, (CHUNK, 1), 0) & (SUB - 1)
        masks = [_level_masks(h) for h in LEVELS]
        lb_all = lb_ref[...]

        def chunk(it, carry):
            ci = nci - 1 - it
            rows = pl.ds(pl.multiple_of(ci * CHUNK, CHUNK), CHUNK)
            qr_all = q_ref[rows, :]
            sq_all, q_all, s_all, sn_all, omlb, f_all, logf_all, kk_all = _gates(qr_all, f_ref[rows, :], lb_all)
            b_all = _split_dot(tril, logf_all)
            res = []
            for hp in range(HP):
                cols = slice(hp * HEAD, (hp + 1) * HEAD)
                res.append(one_head(ci, rows, hp, cols, q_all[:, cols], kk_all[:, cols], b_all[:, cols]))
            dq_t, dk_t, dv_t = (jnp.concatenate([r[i] for r in res], axis=1) for i in range(3))
            w = jnp.concatenate([r[3] for r in res], axis=1)
            dlogf = _split_dot(triu, q_all * dq_t - kk_all * dk_t) + w
            dlf = jnp.where(f_all > MIN_FORGET, dlogf / jnp.maximum(f_all, MIN_FORGET), 0.0)
            t1 = dlf - dk_t
            dlb_ref[...] += jnp.sum(sn_all * t1, axis=0, keepdims=True)
            df_ref[rows, :] = (omlb * (s_all * sn_all) * t1).astype(df_ref.dtype)
            dq_ref[rows, :] = (dq_t * (sq_all * (1.0 + qr_all * (1.0 - sq_all)))).astype(dq_ref.dtype)
            dv_ref[rows, :] = dv_t.astype(dv_ref.dtype)
            return carry

        def one_head(ci, rows, hp, cols, q, kk, b):
            vv, dov = v_ref[rows, cols], do_ref[rows, cols]
            eb = jnp.exp(b)
            st0 = s0_ref[ci, hp]
            dst = dst_ref[hp]
            bl, ec = b[CHUNK - 1:CHUNK], eb[CHUNK - 1:CHUNK]
            decl = jnp.exp(bl - b)
            kdec = kk * decl
            dq_t = eb * _dot(dov, st0, NN)
            dst_ref[hp] = dst * ec + _dot(dov, q * eb, TN)
            dv_t = _dot(kdec, dst, NT)
            dk_t = decl * _dot(vv, dst, NN)
            stc = st0 * ec + _dot(vv, kdec, TN)
            w = jnp.sum(stc * dst, axis=0, keepdims=True)
            dp_all = _dot(dov, vv, NT)
            p = None
            for h, (t2, m) in zip(LEVELS, masks):
                qs, ks, fac = _level_operands(q, kk, b, h, t2)
                pm = m * _dot(qs, ks, NT)
                p = pm if p is None else p + pm
                dpm = m * dp_all
                dq_t = dq_t + fac * _dot(dpm, ks, NN)
                dk_t = dk_t + fac * _dot(dpm, qs, TN)
            p0 = jnp.sum(q * kk, axis=-1, keepdims=True)
            dp0 = jnp.sum(dov * vv, axis=-1, keepdims=True)
            dq_t = dq_t + dp0 * kk
            dk_t = dk_t + dp0 * q
            dv_t = dv_t + _dot(p, dov, TN) + p0 * dov
            for d in range(1, SUB):
                kd_, bd_, vd_ = _shift_in_block(kk, d), _shift_in_block(b, d), _shift_in_block(vv, d)
                e = jnp.exp(jnp.minimum(b - bd_, 0.0))
                ke = kd_ * e
                m = rmod >= d
                p = jnp.where(m, jnp.sum(q * ke, axis=-1, keepdims=True), 0.0)
                dp = jnp.where(m, jnp.sum(dov * vd_, axis=-1, keepdims=True), 0.0)
                dq_t = dq_t + dp * ke
                dk_t = dk_t + _shift_in_block(dp * (q * e), SUBLANES - d)
                dv_t = dv_t + _shift_in_block(p * dov, SUBLANES - d)
            return dq_t, dk_t, dv_t, w

        lax.fori_loop(0, nci, chunk, 0)

    W = HP * HEAD
    G = H // HP

    def col(off):
        return pl.BlockSpec((T, W), lambda h, t: (nb - 1 - t, off + h))

    blk = pl.BlockSpec((T, W), lambda h, t: (nb - 1 - t, h))
    vec = pl.BlockSpec((1, W), lambda h, t: (0, h))
    os = jax.ShapeDtypeStruct((S, KW), MXU_DT)
    return pl.pallas_call(
        body, name="hgrn_bwd", grid=(G, nb),
        in_specs=[col(0), col(G), col(2 * G), blk, pl.BlockSpec((nci, HP, HEAD, HEAD), lambda h, t: (nb - 1 - t, h, 0, 0)), vec],
        out_specs=[blk, blk, blk, vec], out_shape=[os, os, os, jax.ShapeDtypeStruct((1, KW), F32)],
        scratch_shapes=[pltpu.VMEM((HP, HEAD, HEAD), F32)],
        compiler_params=_params(("parallel", "arbitrary")))(proj, proj, proj, do, s0, lb)


def _pool_pos(i, T):
    return (i * T + lax.broadcasted_iota(jnp.int32, (T, 1), 0) + 1).astype(F32)


def _pool_fwd(proj, pool_w, pool_scale, KW):
    S = proj.shape[0]
    GW = KW // len(WINDOWS)
    T = _tile(S, 256)

    def body(v_ref, halo_ref, z_ref, pw_ref, ps_ref, p_ref, y_ref, yt_ref):
        i = pl.program_id(0)
        vb = v_ref[...]
        ext = jnp.concatenate([jnp.where(i > 0, halo_ref[...], 0.0), vb], axis=0)
        pos = _pool_pos(i, T)
        z = z_ref[...]
        gate = ps_ref[...] * (z * _sig(z))
        for g, w in enumerate(WINDOWS):
            sl = slice(g * GW, (g + 1) * GW)
            sg = ext[:, sl]
            for jj in range(g + 1):
                sg = sg + pltpu.roll(sg, 1 << jj, axis=0)
            pooled = (sg[HALO:] / jnp.minimum(pos, float(w)) - vb[:, sl]).astype(p_ref.dtype)
            p_ref[:, sl] = pooled
            yg = _dot(pooled, pw_ref[g], NN) * gate[:, sl]
            y_ref[:, sl] = yg.astype(y_ref.dtype)
            yt_ref[sl, :] = yg.T.astype(yt_ref.dtype)

    row = pl.BlockSpec((T, KW), lambda i: (i, 0))
    hb = T // HALO
    return pl.pallas_call(
        body, name="pool_fwd", grid=(S // T,),
        in_specs=[pl.BlockSpec((T, KW), lambda i: (i, 4)), pl.BlockSpec((HALO, KW), lambda i: (jnp.maximum(i * hb - 1, 0), 4)),
                  pl.BlockSpec((T, KW), lambda i: (i, 5)), pl.BlockSpec((len(WINDOWS), GW, GW), lambda i: (0, 0, 0)),
                  pl.BlockSpec((1, KW), lambda i: (0, 0))],
        out_specs=[row, row, pl.BlockSpec((KW, T), lambda i: (0, i))],
        out_shape=[jax.ShapeDtypeStruct((S, KW), MXU_DT), jax.ShapeDtypeStruct((S, KW), MXU_DT), jax.ShapeDtypeStruct((KW, S), MXU_DT)],
        compiler_params=_params(("parallel",)))(proj, proj, proj, pool_w, pool_scale)


def _pool_bwd(dyb, proj, pooled, pool_w, pool_scale, KW):
    S = proj.shape[0]
    G = len(WINDOWS)
    GW = KW // G
    T = _tile(S, 256)

    def body(dy_ref, z_ref, p_ref, pw_ref, ps_ref, u_ref, dz_ref, dps_ref, dpw_ref):
        i = pl.program_id(0)

        @pl.when(i == 0)
        def _():
            dps_ref[...] = jnp.zeros_like(dps_ref)
            dpw_ref[...] = jnp.zeros_like(dpw_ref)

        pos = _pool_pos(i, T)
        z = z_ref[...]
        sz = _sig(z)
        silu = z * sz
        dsilu = sz * (1.0 + z * (1.0 - sz))
        dy = dy_ref[...]
        ps = ps_ref[...]
        for g, w in enumerate(WINDOWS):
            sl = slice(g * GW, (g + 1) * GW)
            pg = p_ref[:, sl]
            mixed = _dot(pg, pw_ref[g], NN)
            dyg = dy[:, sl]
            dz_ref[:, sl] = (dyg * mixed * ps[:, sl] * dsilu[:, sl]).astype(dz_ref.dtype)
            dps_ref[:, sl] += jnp.sum(dyg * mixed * silu[:, sl], axis=0, keepdims=True)
            dmix = dyg * ps[:, sl] * silu[:, sl]
            dpw_ref[g] += _dot(pg, dmix, TN)
            u_ref[:, sl] = _dot(dmix, pw_ref[g], NT) / jnp.minimum(pos, float(w))

    row = pl.BlockSpec((T, KW), lambda i: (i, 0))
    return pl.pallas_call(
        body, name="pool_bwd", grid=(S // T,),
        in_specs=[row, pl.BlockSpec((T, KW), lambda i: (i, 5)), row,
                  pl.BlockSpec((G, GW, GW), lambda i: (0, 0, 0)), pl.BlockSpec((1, KW), lambda i: (0, 0))],
        out_specs=[row, row, pl.BlockSpec((1, KW), lambda i: (0, 0)), pl.BlockSpec((G, GW, GW), lambda i: (0, 0, 0))],
        out_shape=[jax.ShapeDtypeStruct((S, KW), F32), jax.ShapeDtypeStruct((S, KW), MXU_DT),
                   jax.ShapeDtypeStruct((1, KW), F32), jax.ShapeDtypeStruct((G, GW, GW), F32)],
        compiler_params=_params(("arbitrary",)))(dyb, proj, pooled, pool_w, pool_scale)


def _pool_bwd_window(u, KW):
    S = u.shape[0]
    GW = KW // len(WINDOWS)
    T = _tile(S, 256)
    nb = S // T
    n = T + HALO

    def body(u_ref, halo_ref, dv_ref):
        i = pl.program_id(0)
        uv = u_ref[...]
        ext = jnp.concatenate([uv, jnp.where(i < nb - 1, halo_ref[...], 0.0)], axis=0)
        pos = _pool_pos(i, T)
        for g, w in enumerate(WINDOWS):
            sl = slice(g * GW, (g + 1) * GW)
            sg = ext[:, sl]
            for jj in range(g + 1):
                sg = sg + pltpu.roll(sg, n - (1 << jj), axis=0)
            dv_ref[:, sl] = (sg[:T] - uv[:, sl] * jnp.minimum(pos, float(w))).astype(dv_ref.dtype)

    hb = T // HALO
    return pl.pallas_call(
        body, name="pool_bwd_window", grid=(nb,),
        in_specs=[pl.BlockSpec((T, KW), lambda i: (i, 0)),
                  pl.BlockSpec((HALO, KW), lambda i: (jnp.minimum((i + 1) * hb, S // HALO - 1), 0))],
        out_specs=pl.BlockSpec((T, KW), lambda i: (i, 0)), out_shape=jax.ShapeDtypeStruct((S, KW), MXU_DT),
        compiler_params=_params(("parallel",)))(u, u)


def _softmax_rows(x):
    m = jnp.max(x, axis=0, keepdims=True)
    e = jnp.exp(x - m)
    return e / jnp.sum(e, axis=0, keepdims=True)


def _lb_table(lower_bounds):
    L, KW = lower_bounds.shape

    def body(x_ref, o_ref):
        sm = _softmax_rows(x_ref[...])
        acc = jnp.zeros((1, KW), F32)
        o_ref[0:1, :] = acc
        for l in range(1, L):
            acc = acc + sm[l:l + 1]
            o_ref[l:l + 1, :] = acc

    return pl.pallas_call(body, name="lb_table", out_shape=jax.ShapeDtypeStruct((L, KW), F32))(lower_bounds)


def _mod_part(c_all, w_ada):
    L, D, C = w_ada.shape
    B = c_all.shape[0]
    tn = _tile(C, 512)

    def body(c_ref, w_ref, o_ref):
        cv = c_ref[...]
        o_ref[...] = _dot(cv * _sig(cv), w_ref[...], NN)

    return pl.pallas_call(
        body, name="mod_part", grid=(L, C // tn),
        in_specs=[pl.BlockSpec((B, D), lambda l, j: (0, 0)), pl.BlockSpec((None, D, tn), lambda l, j: (l, 0, j))],
        out_specs=pl.BlockSpec((None, B, tn), lambda l, j: (l, 0, j)), out_shape=jax.ShapeDtypeStruct((L, B, C), F32),
        compiler_params=_params(("parallel", "parallel")))(c_all, w_ada)


def _adamw_math(w, g, m, v):
    m = ADAM_B1 * m + (1.0 - ADAM_B1) * g
    v = ADAM_B2 * v + (1.0 - ADAM_B2) * (g * g)
    m_hat = m / (1.0 - ADAM_B1 ** ADAM_STEP)
    v_hat = v / (1.0 - ADAM_B2 ** ADAM_STEP)
    delta = -ADAM_LR * (m_hat / (jnp.sqrt(v_hat) + ADAM_EPS) + ADAM_WD * w)
    return delta, m, v


def _adamw(w, g, m, v, name):
    shape = w.shape
    C = shape[-1]
    R = w.size // C
    tr = _tile(R, 128)
    flat = lambda t: t.reshape(R, C)

    def body(w_ref, g_ref, m_ref, v_ref, d_ref, nm_ref, nv_ref):
        d, nm, nv = _adamw_math(w_ref[...], g_ref[...], m_ref[...], v_ref[...])
        d_ref[...] = d
        nm_ref[...] = nm
        nv_ref[...] = nv

    blk = pl.BlockSpec((tr, C), lambda i: (i, 0))
    os = jax.ShapeDtypeStruct((R, C), F32)
    outs = pl.pallas_call(body, name=name, grid=(R // tr,), in_specs=[blk] * 4, out_specs=[blk] * 3, out_shape=[os] * 3,
                          compiler_params=_params(("parallel",)))(flat(w), flat(g), flat(m), flat(v))
    return [t.reshape(shape) for t in outs]


def _w_ada_update(c_all, dmod, w, m, v):
    L, D, C = w.shape
    B = c_all.shape[0]
    tn = _tile(C, 256)

    def body(c_ref, dm_ref, w_ref, m_ref, v_ref, g_ref, d_ref, nm_ref, nv_ref):
        cv = c_ref[...]
        g = lax.dot_general(cv * _sig(cv), dm_ref[...], (TN, ((), ())), preferred_element_type=F32,
                            precision=lax.Precision.HIGHEST)
        d, nm, nv = _adamw_math(w_ref[...], g, m_ref[...], v_ref[...])
        g_ref[...] = g
        d_ref[...] = d
        nm_ref[...] = nm
        nv_ref[...] = nv

    blk = pl.BlockSpec((None, D, tn), lambda l, j: (l, 0, j))
    os = jax.ShapeDtypeStruct((L, D, C), F32)
    return pl.pallas_call(
        body, name="w_ada_update", grid=(L, C // tn),
        in_specs=[pl.BlockSpec((B, D), lambda l, j: (0, 0)), pl.BlockSpec((None, B, tn), lambda l, j: (l, 0, j)), blk, blk, blk],
        out_specs=[blk] * 4, out_shape=[os] * 4, compiler_params=_params(("parallel", "parallel")))(c_all, dmod, w, m, v)


def _small_update(parts, lower_bounds, wmv, D, KW):
    L = parts.shape[1]
    widths = [3 * D, D, D, KW, KW, KW]
    offs = [sum(widths[:i]) for i in range(len(widths))]

    def body(p_ref, *refs):
        ins, outs = refs[:18], refs[18:]
        tot = p_ref[0]
        for dev in range(1, N_DEV):
            tot = tot + p_ref[dev]
        grads = [tot[:, o:o + wd] for o, wd in zip(offs, widths)]
        sm = _softmax_rows(ins[9][...])
        dlb = grads[3]
        dsm = [jnp.zeros((1, KW), F32)]
        for j in range(1, L):
            acc = dlb[j:j + 1]
            for l in range(j + 1, L):
                acc = acc + dlb[l:l + 1]
            dsm.append(acc)
        dsm = jnp.concatenate(dsm, axis=0)
        grads[3] = sm * (dsm - jnp.sum(sm * dsm, axis=0, keepdims=True))
        for p in range(6):
            w_ref, m_ref, v_ref = ins[3 * p:3 * p + 3]
            d, nm, nv = _adamw_math(w_ref[...], grads[p], m_ref[...], v_ref[...])
            outs[4 * p][...] = grads[p]
            outs[4 * p + 1][...] = d
            outs[4 * p + 2][...] = nm
            outs[4 * p + 3][...] = nv

    flat = [t for trip in wmv for t in trip]
    out_shape = []
    for wd in widths:
        out_shape += [jax.ShapeDtypeStruct((L, wd), F32)] * 4
    res = pl.pallas_call(body, name="small_update", out_shape=out_shape,
                         compiler_params=pltpu.CompilerParams(vmem_limit_bytes=VMEM_LIMIT_BYTES))(parts, *flat)
    return [res[4 * p:4 * p + 4] for p in range(6)]


def _place():
    x, y, c = lax.axis_index("x"), lax.axis_index("y"), lax.axis_index("c")
    chips = [(1 - x, y), (x, 1 - y), (1 - x, 1 - y)]
    return x, y, c, chips


def _all_gather_rows(blk):
    m_per, n = blk.shape

    def body(x_ref, out_ref, send_sems, recv_sems, local_sem):
        x, y, c, chips = _place()
        me, sibling = (x, y, c), (x, y, 1 - c)

        def rows(px, py, pc):
            return out_ref.at[pl.ds((4 * px + 2 * py + pc) * m_per, m_per), :]

        def copy(k, block, to, src=None):
            return pltpu.make_async_remote_copy(src_ref=rows(*block) if src is None else src, dst_ref=rows(*block),
                                                send_sem=send_sems.at[k], recv_sem=recv_sems.at[k], device_id=to, device_id_type=MESH)

        mine = pltpu.make_async_copy(x_ref, rows(*me), local_sem)
        mine.start()
        first = [copy(0, me, sibling, src=x_ref)]
        first += [copy(1 + j, me, (*chip, c), src=x_ref) for j, chip in enumerate(chips)]
        for cp in first:
            cp.start()
        passed = [copy(4 + j, (*chip, c), sibling) for j, chip in enumerate(chips)]
        for j, chip in enumerate(chips):
            copy(1 + j, (*chip, c), me).wait_recv()
            passed[j].start()
        copy(0, sibling, me).wait_recv()
        for j, chip in enumerate(chips):
            copy(4 + j, (*chip, 1 - c), me).wait_recv()
        for cp in first + passed:
            cp.wait_send()
        mine.wait()

    return pl.pallas_call(
        body, name="all_gather_rows", out_shape=jax.ShapeDtypeStruct((N_DEV * m_per, n), blk.dtype),
        in_specs=[pl.BlockSpec(memory_space=pltpu.VMEM)], out_specs=pl.BlockSpec(memory_space=pltpu.VMEM),
        scratch_shapes=[pltpu.SemaphoreType.DMA((7,)), pltpu.SemaphoreType.DMA((7,)), pltpu.SemaphoreType.DMA],
        compiler_params=pltpu.CompilerParams(vmem_limit_bytes=VMEM_LIMIT_BYTES))(blk)


SHARD_AXIS = (1, 1, 1, 1, 0)
HALF_AXIS = (0, 0, 0, 0, 1)
N_STACKS = 5


def _cut(ref, axis, part, n):
    idx = [slice(None)] * len(ref.shape)
    idx[axis] = pl.ds(pl.multiple_of(part * n, n), n)
    return ref.at[tuple(idx)]


def _quarter(ref, t, chip):
    return _cut(ref, SHARD_AXIS[t], chip, ref.shape[SHARD_AXIS[t]] // N_CHIPS)


def _half(ref, t, core):
    return _cut(ref, HALF_AXIS[t], core, ref.shape[HALF_AXIS[t]] // 2)


def _with_axis(shape, axis, n):
    return tuple(n if a == axis else s for a, s in enumerate(shape))


def _view3(t):
    return t.reshape((1,) * (3 - t.ndim) + t.shape)


def _axis3(t, axis):
    return axis + 3 - t.ndim


def _blocked(shape3, blk, offsets=None):
    offsets = offsets or {}

    def index(b, r, c, *pre):
        idx = [b, r, c]
        for ax, fn in offsets.items():
            idx[ax] = idx[ax] + fn(*pre)
        return tuple(idx)

    return pl.BlockSpec((None,) + tuple(blk), index)


def _cast_into_full(stack, l, t, chip_idx):
    shard = stack.shape[1:]
    nd = len(shard)
    w4 = stack.reshape((stack.shape[0],) + (1,) * (3 - nd) + shard)
    ax = SHARD_AXIS[t] + 3 - nd
    _, B, R, C = w4.shape
    tr = _tile(R, 256)
    per = (R // tr, 1)[ax - 1]

    def body(j_ref, w_ref, o_ref):
        o_ref[...] = w_ref[...].astype(o_ref.dtype)

    full3 = _with_axis((B, R, C), ax, N_CHIPS * (B, R, C)[ax])
    grid_spec = pltpu.PrefetchScalarGridSpec(
        num_scalar_prefetch=1, grid=(B, R // tr, 1),
        in_specs=[pl.BlockSpec((None, None, tr, C), lambda b, r, c, j_ref: (l, b, r, c))],
        out_specs=_blocked(full3, (tr, C), {ax: lambda j_ref: j_ref[0] * per}))
    o = pl.pallas_call(body, name=f"cast_into_full{t}", grid_spec=grid_spec, out_shape=jax.ShapeDtypeStruct(full3, MXU_DT),
                       compiler_params=_params(("parallel",) * 3))(chip_idx, w4)
    return o.reshape(_with_axis(shard, SHARD_AXIS[t], N_CHIPS * shard[SHARD_AXIS[t]]))


def _remote(src, dst, send_sems, recv_sems, k, to):
    return pltpu.make_async_remote_copy(src_ref=src, dst_ref=dst, send_sem=send_sems.at[k], recv_sem=recv_sems.at[k],
                                        device_id=to, device_id_type=MESH)


def _side(inputs, out_shapes, aliases, n_sems, copies):
    def start(ins, outs, ss, rs):
        for snd, _ in copies(ins, outs, ss, rs):
            snd.start()

    def finish(ins, outs, ss, rs):
        for snd, rcv in copies(ins, outs, ss, rs):
            rcv.wait_recv()
            snd.wait_send()

    return dict(inputs=list(inputs), out_shapes=list(out_shapes), aliases=aliases, n_sems=n_sems, start=start, finish=finish)


def _run_side(side, name):
    n_in, n_out = len(side["inputs"]), len(side["out_shapes"])

    def body(*refs):
        ins, outs, sems = refs[:n_in], refs[n_in:n_in + n_out], refs[n_in + n_out:]
        side["start"](ins, outs, *sems)
        side["finish"](ins, outs, *sems)

    any_spec = pl.BlockSpec(memory_space=pl.ANY)
    return list(pl.pallas_call(
        body, name=name, out_shape=side["out_shapes"], in_specs=[any_spec] * n_in, out_specs=[any_spec] * n_out,
        input_output_aliases=side["aliases"],
        scratch_shapes=[pltpu.SemaphoreType.DMA((side["n_sems"],)), pltpu.SemaphoreType.DMA((side["n_sems"],))])(*side["inputs"]))


def _same(arrays):
    return [jax.ShapeDtypeStruct(a.shape, a.dtype) for a in arrays]


def _gather_ici_side(ws):
    def copies(ins, outs, ss, rs):
        x, y, c, chips = _place()
        out = []
        for k, (cx, cy) in enumerate(chips):
            for t in range(N_STACKS):
                mine = _half(_quarter(outs[t], t, 2 * x + y), t, c)
                theirs = _half(_quarter(outs[t], t, 2 * cx + cy), t, c)
                n = k * N_STACKS + t
                out.append((_remote(mine, mine, ss, rs, n, (cx, cy, c)), _remote(theirs, theirs, ss, rs, n, (cx, cy, c))))
        return out

    return _side(ws, _same(ws), {t: t for t in range(N_STACKS)}, 3 * N_STACKS, copies)


def _gather_d2d_side(ws):
    def copies(ins, outs, ss, rs):
        x, y, c, chips = _place()
        out = []
        for k, (cx, cy) in enumerate(chips):
            for t in range(N_STACKS):
                q = _quarter(outs[t], t, 2 * cx + cy)
                mine, theirs = _half(q, t, c), _half(q, t, 1 - c)
                n = k * N_STACKS + t
                out.append((_remote(mine, mine, ss, rs, n, (x, y, 1 - c)), _remote(theirs, theirs, ss, rs, n, (x, y, 1 - c))))
        return out

    return _side(ws, _same(ws), {t: t for t in range(N_STACKS)}, 3 * N_STACKS, copies)


def _swap_side(gs):
    shapes = [jax.ShapeDtypeStruct(_with_axis(g.shape, HALF_AXIS[t], g.shape[HALF_AXIS[t]] // 2), g.dtype) for t, g in enumerate(gs)]

    def copies(ins, outs, ss, rs):
        x, y, c, _ = _place()
        cps = [_remote(_half(ins[t], t, 1 - c), outs[t], ss, rs, t, (x, y, 1 - c)) for t in range(N_STACKS)]
        return [(cp, cp) for cp in cps]

    return _side(gs, shapes, {}, N_STACKS, copies)


def _pair_sum(g, recv, t, c_idx):
    g3, r3 = _view3(g), _view3(recv)
    hf_ax = _axis3(g, HALF_AXIS[t])
    B, R, C = r3.shape
    tr, tc = _tile(R, 256), _tile(C, 2048)
    grid = (B, R // tr, C // tc)

    def body(c_ref, g_ref, r_ref, o_ref):
        o_ref[...] = (g_ref[...] + r_ref[...]).astype(o_ref.dtype)

    grid_spec = pltpu.PrefetchScalarGridSpec(
        num_scalar_prefetch=1, grid=grid,
        in_specs=[_blocked(g3.shape, (tr, tc), {hf_ax: lambda c_ref: c_ref[0] * grid[hf_ax]}), _blocked(r3.shape, (tr, tc))],
        out_specs=_blocked(r3.shape, (tr, tc)))
    out = pl.pallas_call(body, name=f"pair_sum{t}", grid_spec=grid_spec, out_shape=jax.ShapeDtypeStruct(r3.shape, BF16),
                         compiler_params=_params(("parallel",) * 3))(c_idx, g3, r3)
    return out.reshape(recv.shape)


def _scatter_side(ps):
    shapes = [jax.ShapeDtypeStruct((N_CHIPS - 1,) + _with_axis(p.shape, SHARD_AXIS[t], p.shape[SHARD_AXIS[t]] // N_CHIPS), p.dtype)
              for t, p in enumerate(ps)]

    def copies(ins, outs, ss, rs):
        x, y, c, chips = _place()
        cps = [_remote(_quarter(ins[t], t, 2 * cx + cy), outs[t].at[k], ss, rs, k * N_STACKS + t, (cx, cy, c))
               for k, (cx, cy) in enumerate(chips) for t in range(N_STACKS)]
        return [(cp, cp) for cp in cps]

    return _side(ps, shapes, {}, 3 * N_STACKS, copies)


def _sum_chips(p, slots, t, chip_idx, c_idx):
    p3 = _view3(p)
    sh_ax, hf_ax = _axis3(p, SHARD_AXIS[t]), _axis3(p, HALF_AXIS[t])
    piece3 = _with_axis(p3.shape, sh_ax, p3.shape[sh_ax] // N_CHIPS)
    s4 = slots.reshape((N_CHIPS - 1,) + piece3)
    B, R, C = piece3
    tr = _tile(R, 256)
    grid = (B, R // tr, 1)
    full3 = _with_axis(piece3, hf_ax, 2 * piece3[hf_ax])

    def body(j_ref, c_ref, own_ref, s_ref, o_ref):
        acc = own_ref[...].astype(F32)
        for k in range(N_CHIPS - 1):
            acc = acc + s_ref[k].astype(F32)
        o_ref[...] = acc

    grid_spec = pltpu.PrefetchScalarGridSpec(
        num_scalar_prefetch=2, grid=grid,
        in_specs=[_blocked(p3.shape, (tr, C), {sh_ax: lambda j_ref, c_ref: j_ref[0] * grid[sh_ax]}),
                  pl.BlockSpec((N_CHIPS - 1, None, tr, C), lambda b, r, cc, j_ref, c_ref: (0, b, r, cc))],
        out_specs=_blocked(full3, (tr, C), {hf_ax: lambda j_ref, c_ref: c_ref[0] * grid[hf_ax]}))
    out = pl.pallas_call(body, name=f"sum_chips{t}", grid_spec=grid_spec, out_shape=jax.ShapeDtypeStruct(full3, F32),
                         compiler_params=_params(("parallel",) * 3))(chip_idx, c_idx, p3, s4)
    return out.reshape(_with_axis(slots.shape[1:], HALF_AXIS[t], 2 * slots.shape[1 + HALF_AXIS[t]]))


def _join_side(rs_):
    def copies(ins, outs, ss, rs):
        x, y, c, _ = _place()
        return [(_remote(_half(outs[t], t, c), _half(outs[t], t, c), ss, rs, t, (x, y, 1 - c)),
                 _remote(_half(outs[t], t, 1 - c), _half(outs[t], t, 1 - c), ss, rs, t, (x, y, 1 - c))) for t in range(N_STACKS)]

    return _side(rs_, _same(rs_), {t: t for t in range(N_STACKS)}, N_STACKS, copies)


def _layer_weights(shards, l, chip_idx):
    return [_cast_into_full(s, l, t, chip_idx) for t, s in enumerate(shards)]


def _gather_alone(w):
    return _run_side(_gather_d2d_side(_run_side(_gather_ici_side(w), "gather_ici")), "gather_d2d")


def _forward_layer(x, l, w, small, mods, lb_all, KW, nxt):
    norm_pre, norm_post, hgrn_norm, pool_scale = small
    shift, scale, gate = mods
    row = lambda t: t[l:l + 1]
    h, h_t = _prenorm(x, row(norm_pre), row(scale), row(shift))
    if nxt is None:
        proj = _matmul(h, w[0], "nn", F32, "proj")
    else:
        proj, nxt = _matmul(h, w[0], "nn", F32, "proj_gather", side=_gather_ici_side(nxt))
    o_a, y_a, ya_t, s0 = _hgrn_fwd(proj, row(lb_all), row(hgrn_norm), KW)
    pooled, y_b, yb_t = _pool_fwd(proj, w[1], row(pool_scale), KW)
    pa, pb, merged, merged_t = _merge(y_a, y_b, w[2], w[3], proj)
    if nxt is None:
        out = _matmul(merged, w[4], "nn", F32, "out_proj")
    else:
        out, nxt = _matmul(merged, w[4], "nn", F32, "out_proj_gather", side=_gather_d2d_side(nxt))
    x_new = _postnorm(x, out, row(gate), row(norm_post))
    saved = (x, h_t, proj, o_a, ya_t, s0, pooled, yb_t, pa, pb, merged_t, out)
    return x_new, saved, nxt


def _backward_layer(dx, l, saved, w, small, mods, lb_all, KW, above, idx):
    norm_pre, norm_post, hgrn_norm, pool_scale = small
    shift, scale, gate = mods
    chip_idx, c_idx = idx
    x, h_t, proj, o_a, ya_t, s0, pooled, yb_t, pa, pb, merged_t, out = saved
    row = lambda t: t[l:l + 1]
    dout, d_gate, d_npost = _postnorm_bwd(dx, out, row(gate), row(norm_post))
    if above is None:
        g_out = _matmul(merged_t, dout, "nn", F32, "grad_w_out", tk=1024)
    else:
        g_out, recv = _matmul(merged_t, dout, "nn", F32, "grad_w_out_swap", tk=1024, side=_swap_side(above))
        pair = [_pair_sum(g, r, t, c_idx) for t, (g, r) in enumerate(zip(above, recv))]
    dpa, dpb, dga, dgb = _merge_bwd(dout, w[4], proj, pa, pb)
    g_pa = _matmul(ya_t, dpa, "nn", F32, "grad_w_proj_a", tk=1024)
    g_pb = _matmul(yb_t, dpb, "nn", F32, "grad_w_proj_b", tk=1024)
    dya = _matmul(dpa, w[2], "nt", F32, "d_y_a")
    dyb = _matmul(dpb, w[3], "nt", F32, "d_y_b")
    do, dza, d_hn = _gate_a_bwd(dya, o_a, proj, row(hgrn_norm), KW)
    dq, df, dva, d_lb = _hgrn_bwd(proj, do, s0, row(lb_all), KW)
    u, dzb, d_ps, g_pool = _pool_bwd(dyb, proj, pooled, w[1], row(pool_scale), KW)
    dvb = _pool_bwd_window(u, KW)
    dproj = jnp.concatenate([dq, df, dva, dza, dvb, dzb, dga, dgb], axis=1)
    if above is None:
        g_in = _matmul(h_t, dproj, "nn", F32, "grad_w_in", tk=1024)
        dh = _matmul(dproj, w[0], "nt", F32, "d_h")
        reduced = None
    else:
        g_in, slots = _matmul(h_t, dproj, "nn", F32, "grad_w_in_scatter", tk=1024, side=_scatter_side(pair))
        part = [_sum_chips(p, s, t, chip_idx, c_idx) for t, (p, s) in enumerate(zip(pair, slots))]
        dh, reduced = _matmul(dproj, w[0], "nt", F32, "d_h_join", side=_join_side(part))
    dx_new, d_shift, d_scale, d_npre = _prenorm_bwd(dh, x, dx, row(norm_pre), row(scale))
    small_g = jnp.concatenate([d_shift, d_scale, d_gate, d_npre, d_npost, d_lb, d_hn, d_ps], axis=1)
    return dx_new, small_g, [g_in, g_pool, g_pa, g_pb, g_out], reduced


def kernel(x, c, w_ada, b_ada, norm_pre, norm_post, w_in, lower_bounds, hgrn_norm, pool_w, pool_scale, w_proj_a, w_proj_b, w_out, loss_target, m_w_ada, m_b_ada, m_norm_pre, m_norm_post, m_w_in, m_lower_bounds, m_hgrn_norm, m_pool_w, m_pool_scale, m_w_proj_a, m_w_proj_b, m_w_out, v_w_ada, v_b_ada, v_norm_pre, v_norm_post, v_w_in, v_lower_bounds, v_hgrn_norm, v_pool_w, v_pool_scale, v_w_proj_a, v_w_proj_b, v_w_out):
    _, S, D = x.shape
    L = w_in.shape[0]
    KW = D // 2
    xi, yi, ci = lax.axis_index("x"), lax.axis_index("y"), lax.axis_index("c")
    chip = 2 * xi + yi
    dev = 4 * xi + 2 * yi + ci

    c_all = _all_gather_rows(c.reshape(8, D // 8)).reshape(N_DEV, D)
    modp = _mod_part(c_all, w_ada)
    cols = modp.shape[-1]
    modg = _all_gather_rows(modp.reshape(L * N_DEV, cols)).reshape(N_DEV, L, N_DEV, cols)
    mod_all = jnp.transpose(modg[0::2], (1, 2, 0, 3)).reshape(L, N_DEV, 3 * D)
    mod = lax.dynamic_index_in_dim(mod_all, dev, axis=1, keepdims=False) + b_ada
    mods = (mod[:, :D], mod[:, D:2 * D], mod[:, 2 * D:])
    lb_all = _lb_table(lower_bounds)

    shards = [w_in, pool_w, w_proj_a, w_proj_b, w_out]
    chip_idx = chip.astype(jnp.int32).reshape(1)
    c_idx = ci.astype(jnp.int32).reshape(1)
    small = (norm_pre, norm_post, hgrn_norm, pool_scale)

    xs = x[0]
    saved, ws = [], []
    w = _gather_alone(_layer_weights(shards, 0, chip_idx))
    for l in range(L):
        nxt = _layer_weights(shards, l + 1, chip_idx) if l + 1 < L else None
        xs, sv, nxt = _forward_layer(xs, l, w, small, mods, lb_all, KW, nxt)
        saved.append(sv)
        ws.append(w)
        w = nxt
    dx, sq = _loss_head(xs, loss_target[0])
    loss = lax.psum(0.5 * jnp.sum(sq) / D, ("x", "y", "c"))

    small_g, reduced = [None] * L, [None] * L
    above = None
    for l in reversed(range(L)):
        dx, small_g[l], above, red = _backward_layer(dx, l, saved[l], ws[l], small, mods, lb_all, KW, above, (chip_idx, c_idx))
        if red is not None:
            reduced[l + 1] = red
    recv = _run_side(_swap_side(above), "swap")
    pair = [_pair_sum(g, r, t, c_idx) for t, (g, r) in enumerate(zip(above, recv))]
    slots = _run_side(_scatter_side(pair), "scatter")
    reduced[0] = _run_side(_join_side([_sum_chips(p, s, t, chip_idx, c_idx) for t, (p, s) in enumerate(zip(pair, slots))]), "join")
    grad_x = dx[None]

    sg = jnp.concatenate(small_g, axis=0)
    sg = jnp.concatenate([sg, jnp.zeros((8 - L, sg.shape[1]), F32)], axis=0)
    parts = _all_gather_rows(sg).reshape(N_DEV, 8, sg.shape[1])[:, :L]
    wmv = [(b_ada, m_b_ada, v_b_ada), (norm_pre, m_norm_pre, v_norm_pre), (norm_post, m_norm_post, v_norm_post),
           (lower_bounds, m_lower_bounds, v_lower_bounds), (hgrn_norm, m_hgrn_norm, v_hgrn_norm), (pool_scale, m_pool_scale, v_pool_scale)]
    r_b_ada, r_npre, r_npost, r_lb, r_hn, r_ps = _small_update(parts, lower_bounds, wmv, D, KW)
    dmod = lax.dynamic_slice_in_dim(parts[:, :, :3 * D], chip * cols, cols, axis=2)
    r_w_ada = _w_ada_update(c_all, jnp.transpose(dmod, (1, 0, 2)), w_ada, m_w_ada, v_w_ada)

    grads = [jnp.stack([reduced[l][t] for l in range(L)], axis=0) for t in range(N_STACKS)]
    ms = [m_w_in, m_pool_w, m_w_proj_a, m_w_proj_b, m_w_out]
    vs = [v_w_in, v_pool_w, v_w_proj_a, v_w_proj_b, v_w_out]
    r_big = [[g] + _adamw(w, g, m, v, f"adamw{t}") for t, (w, g, m, v) in enumerate(zip(shards, grads, ms, vs))]
    r_w_in, r_pool_w, r_pa, r_pb, r_w_out = r_big

    order = [r_w_ada, r_b_ada, r_npre, r_npost, r_w_in, r_lb, r_hn, r_pool_w, r_ps, r_pa, r_pb, r_w_out]
    outs = [loss, grad_x]
    for k in range(4):
        outs += [r[k] for r in order]
    return tuple(outs)
```

```python
import functools

import jax
import jax.numpy as jnp
from jax import lax
from jax.experimental import pallas as pl
from jax.experimental.pallas import tpu as pltpu

F32 = jnp.float32
BF16 = jnp.bfloat16
MXU_DT = jnp.bfloat16

CHUNK = 64
SUBLANES = 8
SUB = 8
LEVELS = (32, 16, 8)
HEAD = 128
EPS = 1e-6
MIN_FORGET = 1e-30
WINDOWS = (2, 4, 8, 16)
HALO = 16
N_CHIPS = 4
N_DEV = 8
VMEM_LIMIT_BYTES = 56 * 1024 * 1024

ADAM_LR = 0.001
ADAM_B1 = 0.9
ADAM_B2 = 0.999
ADAM_EPS = 1e-08
ADAM_WD = 0.01
ADAM_STEP = 10

NN = ((1,), (0,))
NT = ((1,), (1,))
TN = ((0,), (0,))
MESH = pl.DeviceIdType.MESH


def _params(sem):
    return pltpu.CompilerParams(dimension_semantics=sem, vmem_limit_bytes=VMEM_LIMIT_BYTES)


def _tile(n, pref):
    return pref if n % pref == 0 else n


def _dot(a, b, dims):
    return lax.dot_general(a.astype(MXU_DT), b.astype(MXU_DT), (dims, ((), ())), preferred_element_type=F32)


def _sig_pair(a):
    e = jnp.exp(-jnp.abs(a))
    inv = 1.0 / (1.0 + e)
    pos = a >= 0
    return jnp.where(pos, inv, e * inv), jnp.where(pos, e * inv, inv)


def _sig(a):
    return 0.5 * jnp.tanh(0.5 * a) + 0.5


def _split_dot(tri, x):
    def top(y):
        return lax.bitcast_convert_type(lax.bitcast_convert_type(y, jnp.uint32) & jnp.uint32(0xFFFF0000), F32)

    hi = top(x)
    r1 = x - hi
    mid = top(r1)
    lo = r1 - mid
    d = lambda y: jnp.dot(tri, y.astype(BF16), preferred_element_type=F32)
    return d(hi) + d(mid) + d(lo)


def _rms(xv):
    return lax.rsqrt(jnp.mean(xv * xv, axis=-1, keepdims=True) + EPS)


def _prenorm(x, g, scale, shift):
    S, D = x.shape
    tm = _tile(S, 256)

    def body(x_ref, g_ref, sc_ref, sh_ref, h_ref, ht_ref):
        xv = x_ref[...]
        h = (xv * _rms(xv)) * g_ref[...] * (1.0 + sc_ref[...]) + sh_ref[...]
        h_ref[...] = h.astype(h_ref.dtype)
        ht_ref[...] = h.T.astype(ht_ref.dtype)

    row = pl.BlockSpec((tm, D), lambda i: (i, 0))
    vec = pl.BlockSpec((1, D), lambda i: (0, 0))
    return pl.pallas_call(body, name="prenorm", grid=(S // tm,), in_specs=[row, vec, vec, vec],
                          out_specs=[row, pl.BlockSpec((D, tm), lambda i: (0, i))],
                          out_shape=[jax.ShapeDtypeStruct((S, D), MXU_DT), jax.ShapeDtypeStruct((D, S), MXU_DT)],
                          compiler_params=_params(("parallel",)))(x, g, scale, shift)


def _prenorm_bwd(dh, x, dx, g, scale):
    S, D = x.shape
    tm = _tile(S, 256)

    def body(dh_ref, x_ref, dx_ref, g_ref, sc_ref, o_ref, dsh_ref, dsc_ref, dg_ref):
        @pl.when(pl.program_id(0) == 0)
        def _():
            dsh_ref[...] = jnp.zeros_like(dsh_ref)
            dsc_ref[...] = jnp.zeros_like(dsc_ref)
            dg_ref[...] = jnp.zeros_like(dg_ref)

        xv = x_ref[...]
        r = _rms(xv)
        xh = xv * r
        dh = dh_ref[...]
        gv = g_ref[...]
        one_sc = 1.0 + sc_ref[...]
        dsh_ref[...] += jnp.sum(dh, axis=0, keepdims=True)
        dsc_ref[...] += jnp.sum(dh * xh * gv, axis=0, keepdims=True)
        dg_ref[...] += jnp.sum(dh * xh * one_sc, axis=0, keepdims=True)
        dxh = dh * gv * one_sc
        o_ref[...] = dx_ref[...] + r * (dxh - xh * jnp.mean(dxh * xh, axis=-1, keepdims=True))

    row = pl.BlockSpec((tm, D), lambda i: (i, 0))
    vec = pl.BlockSpec((1, D), lambda i: (0, 0))
    vs = jax.ShapeDtypeStruct((1, D), F32)
    return pl.pallas_call(body, name="prenorm_bwd", grid=(S // tm,), in_specs=[row, row, row, vec, vec],
                          out_specs=[row, vec, vec, vec], out_shape=[jax.ShapeDtypeStruct((S, D), F32), vs, vs, vs],
                          compiler_params=_params(("arbitrary",)))(dh, x, dx, g, scale)


def _postnorm(x, out, gate, g):
    S, D = x.shape
    tm = _tile(S, 256)

    def body(x_ref, o_ref, gt_ref, g_ref, y_ref):
        ov = o_ref[...]
        y_ref[...] = x_ref[...] + gt_ref[...] * ((ov * _rms(ov)) * g_ref[...])

    row = pl.BlockSpec((tm, D), lambda i: (i, 0))
    vec = pl.BlockSpec((1, D), lambda i: (0, 0))
    return pl.pallas_call(body, name="postnorm", grid=(S // tm,), in_specs=[row, row, vec, vec], out_specs=row,
                          out_shape=jax.ShapeDtypeStruct((S, D), F32), compiler_params=_params(("parallel",)))(x, out, gate, g)


def _postnorm_bwd(dx, out, gate, g):
    S, D = dx.shape
    tm = _tile(S, 256)

    def body(dx_ref, o_ref, gt_ref, g_ref, do_ref, dgt_ref, dg_ref):
        @pl.when(pl.program_id(0) == 0)
        def _():
            dgt_ref[...] = jnp.zeros_like(dgt_ref)
            dg_ref[...] = jnp.zeros_like(dg_ref)

        ov = o_ref[...]
        r = _rms(ov)
        rn = ov * r
        dxv = dx_ref[...]
        gv = g_ref[...]
        dgt_ref[...] += jnp.sum(dxv * rn * gv, axis=0, keepdims=True)
        du = dxv * gt_ref[...]
        dg_ref[...] += jnp.sum(du * rn, axis=0, keepdims=True)
        drn = du * gv
        do_ref[...] = (r * (drn - rn * jnp.mean(drn * rn, axis=-1, keepdims=True))).astype(do_ref.dtype)

    row = pl.BlockSpec((tm, D), lambda i: (i, 0))
    vec = pl.BlockSpec((1, D), lambda i: (0, 0))
    vs = jax.ShapeDtypeStruct((1, D), F32)
    return pl.pallas_call(body, name="postnorm_bwd", grid=(S // tm,), in_specs=[row, row, vec, vec],
                          out_specs=[row, vec, vec], out_shape=[jax.ShapeDtypeStruct((S, D), MXU_DT), vs, vs],
                          compiler_params=_params(("arbitrary",)))(dx, out, gate, g)


def _loss_head(y, target):
    S, D = y.shape
    tm = _tile(S, 256)

    def body(y_ref, t_ref, dy_ref, sq_ref):
        @pl.when(pl.program_id(0) == 0)
        def _():
            sq_ref[...] = jnp.zeros_like(sq_ref)

        e = y_ref[...] - t_ref[...]
        dy_ref[...] = e * (1.0 / D)
        sq_ref[...] += jnp.sum(e * e, axis=0, keepdims=True)

    row = pl.BlockSpec((tm, D), lambda i: (i, 0))
    vec = pl.BlockSpec((1, D), lambda i: (0, 0))
    return pl.pallas_call(body, name="loss_head", grid=(S // tm,), in_specs=[row, row], out_specs=[row, vec],
                          out_shape=[jax.ShapeDtypeStruct((S, D), F32), jax.ShapeDtypeStruct((1, D), F32)],
                          compiler_params=_params(("arbitrary",)))(y, target)


def _matmul(a, b, mode, out_dtype, name, *, side=None, into=None, tm=1024, tn=1024, tk=2048):
    pieces = list(a) if isinstance(a, (list, tuple)) else [a]
    na = len(pieces)
    if mode == "tn":
        K, M = pieces[0].shape
        widths = [K]
    else:
        M = pieces[0].shape[0]
        widths = [p.shape[1] for p in pieces]
        K = sum(widths)
    N = b.shape[0] if mode == "nt" else b.shape[1]
    tm, tn, tk = _tile(M, tm), _tile(N, tn), _tile(min(widths), tk)
    ni, nj, nk = M // tm, N // tn, K // tk
    dims = {"nn": NN, "nt": NT, "tn": TN}[mode]

    a_specs, ranges, k0 = [], [], 0
    for wd in widths:
        n = wd // tk
        if mode == "tn":
            a_specs.append(pl.BlockSpec((tk, tm), lambda i, j, k: (k, i)))
        else:
            a_specs.append(pl.BlockSpec((tm, tk), lambda i, j, k, k0=k0, n=n: (i, jnp.clip(k - k0, 0, n - 1))))
        ranges.append((k0, k0 + n))
        k0 += n
    if mode == "nt":
        b_spec = pl.BlockSpec((tn, tk), lambda i, j, k: (j, k))
    else:
        b_spec = pl.BlockSpec((tk, tn), lambda i, j, k: (k, j))
    n_in = len(side["inputs"]) if side else 0
    n_out = len(side["out_shapes"]) if side else 0
    n_buf = 1 if into else 0

    def body(*refs):
        a_refs, b_ref, rest = refs[:na], refs[na], refs[na + 1:]
        side_in, o_ref = rest[:n_in], rest[n_in + n_buf]
        side_out = rest[n_in + n_buf + 1:n_in + n_buf + 1 + n_out]
        scratch = rest[n_in + n_buf + 1 + n_out:]
        i, j, k = pl.program_id(0), pl.program_id(1), pl.program_id(2)
        if side:
            sems = scratch[-2:]

            @pl.when((i == 0) & (j == 0) & (k == 0))
            def _():
                side["start"](side_in, side_out, *sems)

        def accumulate(a_ref):
            p = lax.dot_general(a_ref[...], b_ref[...], (dims, ((), ())), preferred_element_type=F32)
            if nk == 1:
                o_ref[...] = p.astype(o_ref.dtype)
            else:
                acc_ref = scratch[0]

                @pl.when(k == 0)
                def _():
                    acc_ref[...] = p

                @pl.when(k > 0)
                def _():
                    acc_ref[...] += p

                @pl.when(k == nk - 1)
                def _():
                    o_ref[...] = acc_ref[...].astype(o_ref.dtype)

        if na == 1:
            accumulate(a_refs[0])
        else:
            for (lo, hi), a_ref in zip(ranges, a_refs):
                pl.when((k >= lo) & (k < hi))(functools.partial(accumulate, a_ref))

        if side:
            @pl.when((i == ni - 1) & (j == nj - 1) & (k == nk - 1))
            def _():
                side["finish"](side_in, side_out, *sems)

    any_spec = pl.BlockSpec(memory_space=pl.ANY)
    scratch_shapes = [pltpu.VMEM((tm, tn), F32)] if nk > 1 else []
    if side:
        scratch_shapes += [pltpu.SemaphoreType.DMA((side["n_sems"],)), pltpu.SemaphoreType.DMA((side["n_sems"],))]
    aliases = {na + 1 + s: 1 + d for s, d in side["aliases"].items()} if side else {}
    args = pieces + [b] + (list(side["inputs"]) if side else [])
    if into:
        buf, col = into
        aliases[len(args)] = 0
        args.append(buf)
        j0 = col // tn
        out_spec, out_shape = pl.BlockSpec((tm, tn), lambda i, j, k: (i, j0 + j)), jax.ShapeDtypeStruct(buf.shape, buf.dtype)
    else:
        out_spec, out_shape = pl.BlockSpec((tm, tn), lambda i, j, k: (i, j)), jax.ShapeDtypeStruct((M, N), out_dtype)
    res = pl.pallas_call(
        body, name=name, grid=(ni, nj, nk), in_specs=a_specs + [b_spec] + [any_spec] * (n_in + n_buf),
        out_specs=[out_spec] + [any_spec] * n_out,
        out_shape=[out_shape] + (list(side["out_shapes"]) if side else []),
        scratch_shapes=scratch_shapes, input_output_aliases=aliases,
        compiler_params=_params(("arbitrary",) * 3 if side else ("parallel", "parallel", "arbitrary")))(*args)
    return (res[0], list(res[1:])) if side else res[0]


def _merge(ya, yb, wpa, wpb, proj):
    S, KW = ya.shape
    D = 2 * KW
    tm, tn = _tile(S, 1024), _tile(D, 512)
    ga0, gb0 = 3 * D // tn, 4 * D // tn

    def body(ya_ref, yb_ref, wa_ref, wb_ref, ga_ref, gb_ref, pa_ref, pb_ref, m_ref, mt_ref):
        pa = jnp.dot(ya_ref[...], wa_ref[...], preferred_element_type=F32)
        pb = jnp.dot(yb_ref[...], wb_ref[...], preferred_element_type=F32)
        pa_ref[...] = pa.astype(pa_ref.dtype)
        pb_ref[...] = pb.astype(pb_ref.dtype)
        m = _sig(ga_ref[...]) * pa + _sig(gb_ref[...]) * pb
        m_ref[...] = m.astype(m_ref.dtype)
        mt_ref[...] = m.T.astype(mt_ref.dtype)

    y_spec = pl.BlockSpec((tm, KW), lambda i, j: (i, 0))
    w_spec = pl.BlockSpec((KW, tn), lambda i, j: (0, j))
    o_spec = pl.BlockSpec((tm, tn), lambda i, j: (i, j))
    return pl.pallas_call(
        body, name="merge", grid=(S // tm, D // tn),
        in_specs=[y_spec, y_spec, w_spec, w_spec, pl.BlockSpec((tm, tn), lambda i, j: (i, ga0 + j)),
                  pl.BlockSpec((tm, tn), lambda i, j: (i, gb0 + j))],
        out_specs=[o_spec, o_spec, o_spec, pl.BlockSpec((tn, tm), lambda i, j: (j, i))],
        out_shape=[jax.ShapeDtypeStruct((S, D), MXU_DT), jax.ShapeDtypeStruct((S, D), MXU_DT), jax.ShapeDtypeStruct((S, D), MXU_DT),
                   jax.ShapeDtypeStruct((D, S), MXU_DT)],
        compiler_params=_params(("parallel", "parallel")))(ya, yb, wpa, wpb, proj, proj)


def _merge_bwd(dout, w_out, proj, pa, pb):
    S, D = dout.shape
    tm, tn = _tile(S, 1024), _tile(D, 512)
    ga0, gb0 = 3 * D // tn, 4 * D // tn

    def body(do_ref, w_ref, ga_ref, gb_ref, pa_ref, pb_ref, dpa_ref, dpb_ref, dga_ref, dgb_ref):
        dm = lax.dot_general(do_ref[...], w_ref[...], (NT, ((), ())), preferred_element_type=F32)
        sa, sb = _sig(ga_ref[...]), _sig(gb_ref[...])
        sna, snb = 1.0 - sa, 1.0 - sb
        dpa = dm * sa
        dpb = dm * sb
        dpa_ref[...] = dpa.astype(dpa_ref.dtype)
        dpb_ref[...] = dpb.astype(dpb_ref.dtype)
        dga_ref[...] = (dpa * pa_ref[...].astype(F32) * sna).astype(dga_ref.dtype)
        dgb_ref[...] = (dpb * pb_ref[...].astype(F32) * snb).astype(dgb_ref.dtype)

    blk = pl.BlockSpec((tm, tn), lambda i, j: (i, j))
    os = jax.ShapeDtypeStruct((S, D), MXU_DT)
    return pl.pallas_call(
        body, name="merge_bwd", grid=(S // tm, D // tn),
        in_specs=[pl.BlockSpec((tm, D), lambda i, j: (i, 0)), pl.BlockSpec((tn, D), lambda i, j: (j, 0)),
                  pl.BlockSpec((tm, tn), lambda i, j: (i, ga0 + j)), pl.BlockSpec((tm, tn), lambda i, j: (i, gb0 + j)), blk, blk],
        out_specs=[blk, blk, blk, blk], out_shape=[os, os, os, os],
        compiler_params=_params(("parallel", "parallel")))(dout, w_out, proj, proj, pa, pb)


def _gates(qr, a, lbv):
    sq = _sig(qr)
    q = qr * sq
    s, sn = _sig_pair(a)
    omlb = 1.0 - lbv
    f = lbv + omlb * s
    logf = jnp.log(jnp.maximum(f, MIN_FORGET))
    kk = omlb * sn
    return sq, q, s, sn, omlb, f, logf, kk


def _shift_in_block(x, d):
    return pltpu.roll(x.reshape(CHUNK // SUBLANES, SUBLANES, HEAD), d, axis=1).reshape(CHUNK, HEAD)


def _level_masks(h):
    ti = lax.broadcasted_iota(jnp.int32, (CHUNK, 1), 0)
    si = lax.broadcasted_iota(jnp.int32, (1, CHUNK), 1)
    t2, s2 = (ti & (2 * h - 1)) >= h, (si & (2 * h - 1)) >= h
    same = (ti & -(2 * h)) == (si & -(2 * h))
    return t2, jnp.where(same & t2 & jnp.logical_not(s2), 1.0, 0.0)


def _level_factor(b, h):
    parts = [jnp.broadcast_to(b[g + h - 1:g + h], (2 * h, HEAD)) for g in range(0, CHUNK, 2 * h)]
    ref = parts[0] if len(parts) == 1 else jnp.concatenate(parts, axis=0)
    return jnp.exp(-jnp.abs(b - ref))


def _level_operands(q, kk, b, h, t2):
    fac = _level_factor(b, h)
    return jnp.where(t2, q * fac, 0.0), jnp.where(t2, 0.0, kk * fac), fac


def _heads_per_step(H, pref):
    return pref if H % pref == 0 else 1


def _tri(lower):
    r = lax.broadcasted_iota(jnp.int32, (CHUNK, CHUNK), 0)
    c = lax.broadcasted_iota(jnp.int32, (CHUNK, CHUNK), 1)
    return jnp.where((r >= c) if lower else (c >= r), 1.0, 0.0).astype(BF16)


def _hgrn_fwd(proj, lb, hn, KW):
    S = proj.shape[0]
    H = KW // HEAD
    T = _tile(S, 512)
    nci, nb = T // CHUNK, S // T
    HP = _heads_per_step(H, 8)

    def body(q_ref, f_ref, v_ref, z_ref, lb_ref, hn_ref, o_ref, y_ref, yt_ref, s0_ref, st_ref):
        @pl.when(pl.program_id(1) == 0)
        def _():
            st_ref[...] = jnp.zeros_like(st_ref)

        tril = _tri(True)
        rmod = lax.broadcasted_iota(jnp.int32, (CHUNK, 1), 0) & (SUB - 1)
        masks = [_level_masks(h) for h in LEVELS]
        lb_all = lb_ref[...]

        def chunk(ci, carry):
            rows = pl.ds(pl.multiple_of(ci * CHUNK, CHUNK), CHUNK)
            _, q_all, _, _, _, _, logf_all, kk_all = _gates(q_ref[rows, :], f_ref[rows, :], lb_all)
            b_all = _split_dot(tril, logf_all)
            for hp in range(HP):
                cols = slice(hp * HEAD, (hp + 1) * HEAD)
                one_head(ci, rows, hp, cols, q_all[:, cols], kk_all[:, cols], b_all[:, cols])
            return carry

        def one_head(ci, rows, hp, cols, q, kk, b):
            hnv = hn_ref[:, cols]
            vv = v_ref[rows, cols]
            eb = jnp.exp(b)
            st = st_ref[hp]
            s0_ref[ci, hp] = st
            p = None
            for h, (t2, m) in zip(LEVELS, masks):
                qs, ks, _ = _level_operands(q, kk, b, h, t2)
                pm = m * _dot(qs, ks, NT)
                p = pm if p is None else p + pm
            o = _dot(q * eb, st, NT) + _dot(p, vv, NN) + jnp.sum(q * kk, axis=-1, keepdims=True) * vv
            for d in range(1, SUB):
                kd_, bd_, vd_ = _shift_in_block(kk, d), _shift_in_block(b, d), _shift_in_block(vv, d)
                e = jnp.exp(jnp.minimum(b - bd_, 0.0))
                p = jnp.where(rmod >= d, jnp.sum(q * (kd_ * e), axis=-1, keepdims=True), 0.0)
                o = o + p * vd_
            bl = b[CHUNK - 1:CHUNK]
            st_ref[hp] = st * eb[CHUNK - 1:CHUNK] + _dot(vv, kk * jnp.exp(bl - b), TN)
            o_ref[rows, cols] = o
            z = z_ref[rows, cols]
            y_ref[rows, cols] = ((o * _rms(o)) * hnv * (z * _sig(z))).astype(y_ref.dtype)

        lax.fori_loop(0, nci, chunk, 0)
        yt_ref[...] = y_ref[...].astype(F32).T.astype(yt_ref.dtype)

    W = HP * HEAD
    G = H // HP

    def col(off):
        return pl.BlockSpec((T, W), lambda h, t: (t, off + h))

    vec = pl.BlockSpec((1, W), lambda h, t: (0, h))
    return pl.pallas_call(
        body, name="hgrn_fwd", grid=(G, nb),
        in_specs=[col(0), col(G), col(2 * G), col(3 * G), vec, vec],
        out_specs=[pl.BlockSpec((T, W), lambda h, t: (t, h)), pl.BlockSpec((T, W), lambda h, t: (t, h)),
                   pl.BlockSpec((W, T), lambda h, t: (h, t)), pl.BlockSpec((nci, HP, HEAD, HEAD), lambda h, t: (t, h, 0, 0))],
        out_shape=[jax.ShapeDtypeStruct((S, KW), F32), jax.ShapeDtypeStruct((S, KW), MXU_DT),
                   jax.ShapeDtypeStruct((KW, S), MXU_DT), jax.ShapeDtypeStruct((S // CHUNK, H, HEAD, HEAD), F32)],
        scratch_shapes=[pltpu.VMEM((HP, HEAD, HEAD), F32)],
        compiler_params=_params(("parallel", "arbitrary")))(proj, proj, proj, proj, lb, hn)


def _gate_a_bwd(dya, o, proj, hn, KW):
    S = o.shape[0]
    H = KW // HEAD
    T = _tile(S, 512)

    def body(dy_ref, o_ref, z_ref, hn_ref, do_ref, dz_ref, dhn_ref):
        @pl.when(pl.program_id(1) == 0)
        def _():
            dhn_ref[...] = jnp.zeros_like(dhn_ref)

        ov = o_ref[...]
        r = _rms(ov)
        rn = ov * r
        z = z_ref[...]
        sz = _sig(z)
        silu = z * sz
        dy = dy_ref[...]
        hnv = hn_ref[...]
        dz_ref[...] = (dy * rn * hnv * (sz * (1.0 + z * (1.0 - sz)))).astype(dz_ref.dtype)
        dhn_ref[...] += jnp.sum(dy * rn * silu, axis=0, keepdims=True)
        drn = dy * hnv * silu
        do_ref[...] = r * (drn - rn * jnp.mean(drn * rn, axis=-1, keepdims=True))

    blk = pl.BlockSpec((T, HEAD), lambda h, t: (t, h))
    vec = pl.BlockSpec((1, HEAD), lambda h, t: (0, h))
    return pl.pallas_call(
        body, name="gate_a_bwd", grid=(H, S // T),
        in_specs=[blk, blk, pl.BlockSpec((T, HEAD), lambda h, t: (t, 3 * H + h)), vec],
        out_specs=[blk, blk, vec],
        out_shape=[jax.ShapeDtypeStruct((S, KW), F32), jax.ShapeDtypeStruct((S, KW), MXU_DT), jax.ShapeDtypeStruct((1, KW), F32)],
        compiler_params=_params(("parallel", "arbitrary")))(dya, o, proj, hn)


def _hgrn_bwd(proj, do, s0, lb, KW):
    S = proj.shape[0]
    H = KW // HEAD
    T = _tile(S, 512)
    nci, nb = T // CHUNK, S // T
    nsub = CHUNK // SUB
    HP = _heads_per_step(H, 4)

    def body(q_ref, f_ref, v_ref, do_ref, s0_ref, lb_ref, dq_ref, df_ref, dv_ref, dlb_ref, dst_ref):
        @pl.when(pl.program_id(1) == 0)
        def _():
            dst_ref[...] = jnp.zeros_like(dst_ref)
            dlb_ref[...] = jnp.zeros_like(dlb_ref)

        tril, triu = _tri(True), _tri(False)
        rmod = lax.broadcasted_iota(jnp.int32, (CHUNK, 1), 0) & (SUB - 1)
        masks = [_level_masks(h) for h in LEVELS]
        lb_all = lb_ref[...]

        def chunk(it, carry):
            ci = nci - 1 - it
            rows = pl.ds(pl.multiple_of(ci * CHUNK, CHUNK), CHUNK)
            qr_all = q_ref[rows, :]
            sq_all, q_all, s_all, sn_all, omlb, f_all, logf_all, kk_all = _gates(qr_all, f_ref[rows, :], lb_all)
            b_all = _split_dot(tril, logf_all)
            res = []
            for hp in range(HP):
                cols = slice(hp * HEAD, (hp + 1) * HEAD)
                res.append(one_head(ci, rows, hp, cols, q_all[:, cols], kk_all[:, cols], b_all[:, cols]))
            dq_t, dk_t, dv_t = (jnp.concatenate([r[i] for r in res], axis=1) for i in range(3))
            w = jnp.concatenate([r[3] for r in res], axis=1)
            dlogf = _split_dot(triu, q_all * dq_t - kk_all * dk_t) + w
            dlf = jnp.where(f_all > MIN_FORGET, dlogf / jnp.maximum(f_all, MIN_FORGET), 0.0)
            t1 = dlf - dk_t
            dlb_ref[...] += jnp.sum(sn_all * t1, axis=0, keepdims=True)
            df_ref[rows, :] = (omlb * (s_all * sn_all) * t1).astype(df_ref.dtype)
            dq_ref[rows, :] = (dq_t * (sq_all * (1.0 + qr_all * (1.0 - sq_all)))).astype(dq_ref.dtype)
            dv_ref[rows, :] = dv_t.astype(dv_ref.dtype)
            return carry

        def one_head(ci, rows, hp, cols, q, kk, b):
            vv, dov = v_ref[rows, cols], do_ref[rows, cols]
            eb = jnp.exp(b)
            st0 = s0_ref[ci, hp]
            dst = dst_ref[hp]
            bl, ec = b[CHUNK - 1:CHUNK], eb[CHUNK - 1:CHUNK]
            decl = jnp.exp(bl - b)
            kdec = kk * decl
            dq_t = eb * _dot(dov, st0, NN)
            dst_ref[hp] = dst * ec + _dot(dov, q * eb, TN)
            dv_t = _dot(kdec, dst, NT)
            dk_t = decl * _dot(vv, dst, NN)
            stc = st0 * ec + _dot(vv, kdec, TN)
            w = jnp.sum(stc * dst, axis=0, keepdims=True)
            dp_all = _dot(dov, vv, NT)
            p = None
            for h, (t2, m) in zip(LEVELS, masks):
                qs, ks, fac = _level_operands(q, kk, b, h, t2)
                pm = m * _dot(qs, ks, NT)
                p = pm if p is None else p + pm
                dpm = m * dp_all
                dq_t = dq_t + fac * _dot(dpm, ks, NN)
                dk_t = dk_t + fac * _dot(dpm, qs, TN)
            p0 = jnp.sum(q * kk, axis=-1, keepdims=True)
            dp0 = jnp.sum(dov * vv, axis=-1, keepdims=True)
            dq_t = dq_t + dp0 * kk
            dk_t = dk_t + dp0 * q
            dv_t = dv_t + _dot(p, dov, TN) + p0 * dov
            for d in range(1, SUB):
                kd_, bd_, vd_ = _shift_in_block(kk, d), _shift_in_block(b, d), _shift_in_block(vv, d)
                e = jnp.exp(jnp.minimum(b - bd_, 0.0))
                ke = kd_ * e
                m = rmod >= d
                p = jnp.where(m, jnp.sum(q * ke, axis=-1, keepdims=True), 0.0)
                dp = jnp.where(m, jnp.sum(dov * vd_, axis=-1, keepdims=True), 0.0)
                dq_t = dq_t + dp * ke
                dk_t = dk_t + _shift_in_block(dp * (q * e), SUBLANES - d)
                dv_t = dv_t + _shift_in_block(p * dov, SUBLANES - d)
            return dq_t, dk_t, dv_t, w

        lax.fori_loop(0, nci, chunk, 0)

    W = HP * HEAD
    G = H // HP

    def col(off):
        return pl.BlockSpec((T, W), lambda h, t: (nb - 1 - t, off + h))

    blk = pl.BlockSpec((T, W), lambda h, t: (nb - 1 - t, h))
    vec = pl.BlockSpec((1, W), lambda h, t: (0, h))
    os = jax.ShapeDtypeStruct((S, KW), MXU_DT)
    return pl.pallas_call(
        body, name="hgrn_bwd", grid=(G, nb),
        in_specs=[col(0), col(G), col(2 * G), blk, pl.BlockSpec((nci, HP, HEAD, HEAD), lambda h, t: (nb - 1 - t, h, 0, 0)), vec],
        out_specs=[blk, blk, blk, vec], out_shape=[os, os, os, jax.ShapeDtypeStruct((1, KW), F32)],
        scratch_shapes=[pltpu.VMEM((HP, HEAD, HEAD), F32)],
        compiler_params=_params(("parallel", "arbitrary")))(proj, proj, proj, do, s0, lb)


def _pool_pos(i, T):
    return (i * T + lax.broadcasted_iota(jnp.int32, (T, 1), 0) + 1).astype(F32)


def _pool_fwd(proj, pool_w, pool_scale, KW):
    S = proj.shape[0]
    GW = KW // len(WINDOWS)
    T = _tile(S, 256)

    def body(v_ref, halo_ref, z_ref, pw_ref, ps_ref, p_ref, y_ref, yt_ref):
        i = pl.program_id(0)
        vb = v_ref[...]
        ext = jnp.concatenate([jnp.where(i > 0, halo_ref[...], 0.0), vb], axis=0)
        pos = _pool_pos(i, T)
        z = z_ref[...]
        gate = ps_ref[...] * (z * _sig(z))
        for g, w in enumerate(WINDOWS):
            sl = slice(g * GW, (g + 1) * GW)
            sg = ext[:, sl]
            for jj in range(g + 1):
                sg = sg + pltpu.roll(sg, 1 << jj, axis=0)
            pooled = (sg[HALO:] / jnp.minimum(pos, float(w)) - vb[:, sl]).astype(p_ref.dtype)
            p_ref[:, sl] = pooled
            yg = _dot(pooled, pw_ref[g], NN) * gate[:, sl]
            y_ref[:, sl] = yg.astype(y_ref.dtype)
            yt_ref[sl, :] = yg.T.astype(yt_ref.dtype)

    row = pl.BlockSpec((T, KW), lambda i: (i, 0))
    hb = T // HALO
    return pl.pallas_call(
        body, name="pool_fwd", grid=(S // T,),
        in_specs=[pl.BlockSpec((T, KW), lambda i: (i, 4)), pl.BlockSpec((HALO, KW), lambda i: (jnp.maximum(i * hb - 1, 0), 4)),
                  pl.BlockSpec((T, KW), lambda i: (i, 5)), pl.BlockSpec((len(WINDOWS), GW, GW), lambda i: (0, 0, 0)),
                  pl.BlockSpec((1, KW), lambda i: (0, 0))],
        out_specs=[row, row, pl.BlockSpec((KW, T), lambda i: (0, i))],
        out_shape=[jax.ShapeDtypeStruct((S, KW), MXU_DT), jax.ShapeDtypeStruct((S, KW), MXU_DT), jax.ShapeDtypeStruct((KW, S), MXU_DT)],
        compiler_params=_params(("parallel",)))(proj, proj, proj, pool_w, pool_scale)


def _pool_bwd(dyb, proj, pooled, pool_w, pool_scale, KW):
    S = proj.shape[0]
    G = len(WINDOWS)
    GW = KW // G
    T = _tile(S, 256)

    def body(dy_ref, z_ref, p_ref, pw_ref, ps_ref, u_ref, dz_ref, dps_ref, dpw_ref):
        i = pl.program_id(0)

        @pl.when(i == 0)
        def _():
            dps_ref[...] = jnp.zeros_like(dps_ref)
            dpw_ref[...] = jnp.zeros_like(dpw_ref)

        pos = _pool_pos(i, T)
        z = z_ref[...]
        sz = _sig(z)
        silu = z * sz
        dsilu = sz * (1.0 + z * (1.0 - sz))
        dy = dy_ref[...]
        ps = ps_ref[...]
        for g, w in enumerate(WINDOWS):
            sl = slice(g * GW, (g + 1) * GW)
            pg = p_ref[:, sl]
            mixed = _dot(pg, pw_ref[g], NN)
            dyg = dy[:, sl]
            dz_ref[:, sl] = (dyg * mixed * ps[:, sl] * dsilu[:, sl]).astype(dz_ref.dtype)
            dps_ref[:, sl] += jnp.sum(dyg * mixed * silu[:, sl], axis=0, keepdims=True)
            dmix = dyg * ps[:, sl] * silu[:, sl]
            dpw_ref[g] += _dot(pg, dmix, TN)
            u_ref[:, sl] = _dot(dmix, pw_ref[g], NT) / jnp.minimum(pos, float(w))

    row = pl.BlockSpec((T, KW), lambda i: (i, 0))
    return pl.pallas_call(
        body, name="pool_bwd", grid=(S // T,),
        in_specs=[row, pl.BlockSpec((T, KW), lambda i: (i, 5)), row,
                  pl.BlockSpec((G, GW, GW), lambda i: (0, 0, 0)), pl.BlockSpec((1, KW), lambda i: (0, 0))],
        out_specs=[row, row, pl.BlockSpec((1, KW), lambda i: (0, 0)), pl.BlockSpec((G, GW, GW), lambda i: (0, 0, 0))],
        out_shape=[jax.ShapeDtypeStruct((S, KW), F32), jax.ShapeDtypeStruct((S, KW), MXU_DT),
                   jax.ShapeDtypeStruct((1, KW), F32), jax.ShapeDtypeStruct((G, GW, GW), F32)],
        compiler_params=_params(("arbitrary",)))(dyb, proj, pooled, pool_w, pool_scale)


def _pool_bwd_window(u, KW):
    S = u.shape[0]
    GW = KW // len(WINDOWS)
    T = _tile(S, 256)
    nb = S // T
    n = T + HALO

    def body(u_ref, halo_ref, dv_ref):
        i = pl.program_id(0)
        uv = u_ref[...]
        ext = jnp.concatenate([uv, jnp.where(i < nb - 1, halo_ref[...], 0.0)], axis=0)
        pos = _pool_pos(i, T)
        for g, w in enumerate(WINDOWS):
            sl = slice(g * GW, (g + 1) * GW)
            sg = ext[:, sl]
            for jj in range(g + 1):
                sg = sg + pltpu.roll(sg, n - (1 << jj), axis=0)
            dv_ref[:, sl] = (sg[:T] - uv[:, sl] * jnp.minimum(pos, float(w))).astype(dv_ref.dtype)

    hb = T // HALO
    return pl.pallas_call(
        body, name="pool_bwd_window", grid=(nb,),
        in_specs=[pl.BlockSpec((T, KW), lambda i: (i, 0)),
                  pl.BlockSpec((HALO, KW), lambda i: (jnp.minimum((i + 1) * hb, S // HALO - 1), 0))],
        out_specs=pl.BlockSpec((T, KW), lambda i: (i, 0)), out_shape=jax.ShapeDtypeStruct((S, KW), MXU_DT),
        compiler_params=_params(("parallel",)))(u, u)


def _softmax_rows(x):
    m = jnp.max(x, axis=0, keepdims=True)
    e = jnp.exp(x - m)
    return e / jnp.sum(e, axis=0, keepdims=True)


def _lb_table(lower_bounds):
    L, KW = lower_bounds.shape

    def body(x_ref, o_ref):
        sm = _softmax_rows(x_ref[...])
        acc = jnp.zeros((1, KW), F32)
        o_ref[0:1, :] = acc
        for l in range(1, L):
            acc = acc + sm[l:l + 1]
            o_ref[l:l + 1, :] = acc

    return pl.pallas_call(body, name="lb_table", out_shape=jax.ShapeDtypeStruct((L, KW), F32))(lower_bounds)


def _mod_part(c_all, w_ada):
    L, D, C = w_ada.shape
    B = c_all.shape[0]
    tn = _tile(C, 512)

    def body(c_ref, w_ref, o_ref):
        cv = c_ref[...]
        o_ref[...] = _dot(cv * _sig(cv), w_ref[...], NN)

    return pl.pallas_call(
        body, name="mod_part", grid=(L, C // tn),
        in_specs=[pl.BlockSpec((B, D), lambda l, j: (0, 0)), pl.BlockSpec((None, D, tn), lambda l, j: (l, 0, j))],
        out_specs=pl.BlockSpec((None, B, tn), lambda l, j: (l, 0, j)), out_shape=jax.ShapeDtypeStruct((L, B, C), F32),
        compiler_params=_params(("parallel", "parallel")))(c_all, w_ada)


def _adamw_math(w, g, m, v):
    m = ADAM_B1 * m + (1.0 - ADAM_B1) * g
    v = ADAM_B2 * v + (1.0 - ADAM_B2) * (g * g)
    m_hat = m / (1.0 - ADAM_B1 ** ADAM_STEP)
    v_hat = v / (1.0 - ADAM_B2 ** ADAM_STEP)
    delta = -ADAM_LR * (m_hat / (jnp.sqrt(v_hat) + ADAM_EPS) + ADAM_WD * w)
    return delta, m, v


def _adamw(w, g, m, v, name):
    shape = w.shape
    C = shape[-1]
    R = w.size // C
    tr = _tile(R, 128)
    flat = lambda t: t.reshape(R, C)

    def body(w_ref, g_ref, m_ref, v_ref, d_ref, nm_ref, nv_ref):
        d, nm, nv = _adamw_math(w_ref[...], g_ref[...], m_ref[...], v_ref[...])
        d_ref[...] = d
        nm_ref[...] = nm
        nv_ref[...] = nv

    blk = pl.BlockSpec((tr, C), lambda i: (i, 0))
    os = jax.ShapeDtypeStruct((R, C), F32)
    outs = pl.pallas_call(body, name=name, grid=(R // tr,), in_specs=[blk] * 4, out_specs=[blk] * 3, out_shape=[os] * 3,
                          compiler_params=_params(("parallel",)))(flat(w), flat(g), flat(m), flat(v))
    return [t.reshape(shape) for t in outs]


def _w_ada_update(c_all, dmod, w, m, v):
    L, D, C = w.shape
    B = c_all.shape[0]
    tn = _tile(C, 256)

    def body(c_ref, dm_ref, w_ref, m_ref, v_ref, g_ref, d_ref, nm_ref, nv_ref):
        cv = c_ref[...]
        g = lax.dot_general(cv * _sig(cv), dm_ref[...], (TN, ((), ())), preferred_element_type=F32,
                            precision=lax.Precision.HIGHEST)
        d, nm, nv = _adamw_math(w_ref[...], g, m_ref[...], v_ref[...])
        g_ref[...] = g
        d_ref[...] = d
        nm_ref[...] = nm
        nv_ref[...] = nv

    blk = pl.BlockSpec((None, D, tn), lambda l, j: (l, 0, j))
    os = jax.ShapeDtypeStruct((L, D, C), F32)
    return pl.pallas_call(
        body, name="w_ada_update", grid=(L, C // tn),
        in_specs=[pl.BlockSpec((B, D), lambda l, j: (0, 0)), pl.BlockSpec((None, B, tn), lambda l, j: (l, 0, j)), blk, blk, blk],
        out_specs=[blk] * 4, out_shape=[os] * 4, compiler_params=_params(("parallel", "parallel")))(c_all, dmod, w, m, v)


def _small_update(parts, lower_bounds, wmv, D, KW):
    L = parts.shape[1]
    widths = [3 * D, D, D, KW, KW, KW]
    offs = [sum(widths[:i]) for i in range(len(widths))]

    def body(p_ref, *refs):
        ins, outs = refs[:18], refs[18:]
        tot = p_ref[0]
        for dev in range(1, N_DEV):
            tot = tot + p_ref[dev]
        grads = [tot[:, o:o + wd] for o, wd in zip(offs, widths)]
        sm = _softmax_rows(ins[9][...])
        dlb = grads[3]
        dsm = [jnp.zeros((1, KW), F32)]
        for j in range(1, L):
            acc = dlb[j:j + 1]
            for l in range(j + 1, L):
                acc = acc + dlb[l:l + 1]
            dsm.append(acc)
        dsm = jnp.concatenate(dsm, axis=0)
        grads[3] = sm * (dsm - jnp.sum(sm * dsm, axis=0, keepdims=True))
        for p in range(6):
            w_ref, m_ref, v_ref = ins[3 * p:3 * p + 3]
            d, nm, nv = _adamw_math(w_ref[...], grads[p], m_ref[...], v_ref[...])
            outs[4 * p][...] = grads[p]
            outs[4 * p + 1][...] = d
            outs[4 * p + 2][...] = nm
            outs[4 * p + 3][...] = nv

    flat = [t for trip in wmv for t in trip]
    out_shape = []
    for wd in widths:
        out_shape += [jax.ShapeDtypeStruct((L, wd), F32)] * 4
    res = pl.pallas_call(body, name="small_update", out_shape=out_shape,
                         compiler_params=pltpu.CompilerParams(vmem_limit_bytes=VMEM_LIMIT_BYTES))(parts, *flat)
    return [res[4 * p:4 * p + 4] for p in range(6)]


def _place():
    x, y, c = lax.axis_index("x"), lax.axis_index("y"), lax.axis_index("c")
    chips = [(1 - x, y), (x, 1 - y), (1 - x, 1 - y)]
    return x, y, c, chips


def _all_gather_rows(blk):
    m_per, n = blk.shape

    def body(x_ref, out_ref, send_sems, recv_sems, local_sem):
        x, y, c, chips = _place()
        me, sibling = (x, y, c), (x, y, 1 - c)

        def rows(px, py, pc):
            return out_ref.at[pl.ds((4 * px + 2 * py + pc) * m_per, m_per), :]

        def copy(k, block, to, src=None):
            return pltpu.make_async_remote_copy(src_ref=rows(*block) if src is None else src, dst_ref=rows(*block),
                                                send_sem=send_sems.at[k], recv_sem=recv_sems.at[k], device_id=to, device_id_type=MESH)

        mine = pltpu.make_async_copy(x_ref, rows(*me), local_sem)
        mine.start()
        first = [copy(0, me, sibling, src=x_ref)]
        first += [copy(1 + j, me, (*chip, c), src=x_ref) for j, chip in enumerate(chips)]
        for cp in first:
            cp.start()
        passed = [copy(4 + j, (*chip, c), sibling) for j, chip in enumerate(chips)]
        for j, chip in enumerate(chips):
            copy(1 + j, (*chip, c), me).wait_recv()
            passed[j].start()
        copy(0, sibling, me).wait_recv()
        for j, chip in enumerate(chips):
            copy(4 + j, (*chip, 1 - c), me).wait_recv()
        for cp in first + passed:
            cp.wait_send()
        mine.wait()

    return pl.pallas_call(
        body, name="all_gather_rows", out_shape=jax.ShapeDtypeStruct((N_DEV * m_per, n), blk.dtype),
        in_specs=[pl.BlockSpec(memory_space=pltpu.VMEM)], out_specs=pl.BlockSpec(memory_space=pltpu.VMEM),
        scratch_shapes=[pltpu.SemaphoreType.DMA((7,)), pltpu.SemaphoreType.DMA((7,)), pltpu.SemaphoreType.DMA],
        compiler_params=pltpu.CompilerParams(vmem_limit_bytes=VMEM_LIMIT_BYTES))(blk)


SHARD_AXIS = (1, 1, 1, 1, 0)
HALF_AXIS = (0, 0, 0, 0, 1)
N_STACKS = 5


def _cut(ref, axis, part, n):
    idx = [slice(None)] * len(ref.shape)
    idx[axis] = pl.ds(pl.multiple_of(part * n, n), n)
    return ref.at[tuple(idx)]


def _quarter(ref, t, chip):
    return _cut(ref, SHARD_AXIS[t], chip, ref.shape[SHARD_AXIS[t]] // N_CHIPS)


def _half(ref, t, core):
    return _cut(ref, HALF_AXIS[t], core, ref.shape[HALF_AXIS[t]] // 2)


def _with_axis(shape, axis, n):
    return tuple(n if a == axis else s for a, s in enumerate(shape))


def _view3(t):
    return t.reshape((1,) * (3 - t.ndim) + t.shape)


def _axis3(t, axis):
    return axis + 3 - t.ndim


def _blocked(shape3, blk, offsets=None):
    offsets = offsets or {}

    def index(b, r, c, *pre):
        idx = [b, r, c]
        for ax, fn in offsets.items():
            idx[ax] = idx[ax] + fn(*pre)
        return tuple(idx)

    return pl.BlockSpec((None,) + tuple(blk), index)


def _cast_into_full(stack, l, t, chip_idx):
    shard = stack.shape[1:]
    nd = len(shard)
    w4 = stack.reshape((stack.shape[0],) + (1,) * (3 - nd) + shard)
    ax = SHARD_AXIS[t] + 3 - nd
    _, B, R, C = w4.shape
    tr = _tile(R, 256)
    per = (R // tr, 1)[ax - 1]

    def body(j_ref, w_ref, o_ref):
        o_ref[...] = w_ref[...].astype(o_ref.dtype)

    full3 = _with_axis((B, R, C), ax, N_CHIPS * (B, R, C)[ax])
    grid_spec = pltpu.PrefetchScalarGridSpec(
        num_scalar_prefetch=1, grid=(B, R // tr, 1),
        in_specs=[pl.BlockSpec((None, None, tr, C), lambda b, r, c, j_ref: (l, b, r, c))],
        out_specs=_blocked(full3, (tr, C), {ax: lambda j_ref: j_ref[0] * per}))
    o = pl.pallas_call(body, name=f"cast_into_full{t}", grid_spec=grid_spec, out_shape=jax.ShapeDtypeStruct(full3, MXU_DT),
                       compiler_params=_params(("parallel",) * 3))(chip_idx, w4)
    return o.reshape(_with_axis(shard, SHARD_AXIS[t], N_CHIPS * shard[SHARD_AXIS[t]]))


def _remote(src, dst, send_sems, recv_sems, k, to):
    return pltpu.make_async_remote_copy(src_ref=src, dst_ref=dst, send_sem=send_sems.at[k], recv_sem=recv_sems.at[k],
                                        device_id=to, device_id_type=MESH)


def _side(inputs, out_shapes, aliases, n_sems, copies):
    def start(ins, outs, ss, rs):
        for snd, _ in copies(ins, outs, ss, rs):
            snd.start()

    def finish(ins, outs, ss, rs):
        for snd, rcv in copies(ins, outs, ss, rs):
            rcv.wait_recv()
            snd.wait_send()

    return dict(inputs=list(inputs), out_shapes=list(out_shapes), aliases=aliases, n_sems=n_sems, start=start, finish=finish)


def _run_side(side, name):
    n_in, n_out = len(side["inputs"]), len(side["out_shapes"])

    def body(*refs):
        ins, outs, sems = refs[:n_in], refs[n_in:n_in + n_out], refs[n_in + n_out:]
        side["start"](ins, outs, *sems)
        side["finish"](ins, outs, *sems)

    any_spec = pl.BlockSpec(memory_space=pl.ANY)
    return list(pl.pallas_call(
        body, name=name, out_shape=side["out_shapes"], in_specs=[any_spec] * n_in, out_specs=[any_spec] * n_out,
        input_output_aliases=side["aliases"],
        scratch_shapes=[pltpu.SemaphoreType.DMA((side["n_sems"],)), pltpu.SemaphoreType.DMA((side["n_sems"],))])(*side["inputs"]))


def _same(arrays):
    return [jax.ShapeDtypeStruct(a.shape, a.dtype) for a in arrays]


def _gather_ici_side(ws):
    def copies(ins, outs, ss, rs):
        x, y, c, chips = _place()
        out = []
        for k, (cx, cy) in enumerate(chips):
            for t in range(N_STACKS):
                mine = _half(_quarter(outs[t], t, 2 * x + y), t, c)
                theirs = _half(_quarter(outs[t], t, 2 * cx + cy), t, c)
                n = k * N_STACKS + t
                out.append((_remote(mine, mine, ss, rs, n, (cx, cy, c)), _remote(theirs, theirs, ss, rs, n, (cx, cy, c))))
        return out

    return _side(ws, _same(ws), {t: t for t in range(N_STACKS)}, 3 * N_STACKS, copies)


def _gather_d2d_side(ws):
    def copies(ins, outs, ss, rs):
        x, y, c, chips = _place()
        out = []
        for k, (cx, cy) in enumerate(chips):
            for t in range(N_STACKS):
                q = _quarter(outs[t], t, 2 * cx + cy)
                mine, theirs = _half(q, t, c), _half(q, t, 1 - c)
                n = k * N_STACKS + t
                out.append((_remote(mine, mine, ss, rs, n, (x, y, 1 - c)), _remote(theirs, theirs, ss, rs, n, (x, y, 1 - c))))
        return out

    return _side(ws, _same(ws), {t: t for t in range(N_STACKS)}, 3 * N_STACKS, copies)


def _swap_side(gs):
    shapes = [jax.ShapeDtypeStruct(_with_axis(g.shape, HALF_AXIS[t], g.shape[HALF_AXIS[t]] // 2), g.dtype) for t, g in enumerate(gs)]

    def copies(ins, outs, ss, rs):
        x, y, c, _ = _place()
        cps = [_remote(_half(ins[t], t, 1 - c), outs[t], ss, rs, t, (x, y, 1 - c)) for t in range(N_STACKS)]
        return [(cp, cp) for cp in cps]

    return _side(gs, shapes, {}, N_STACKS, copies)


def _pair_sum(g, recv, t, c_idx):
    g3, r3 = _view3(g), _view3(recv)
    hf_ax = _axis3(g, HALF_AXIS[t])
    B, R, C = r3.shape
    tr, tc = _tile(R, 256), _tile(C, 2048)
    grid = (B, R // tr, C // tc)

    def body(c_ref, g_ref, r_ref, o_ref):
        o_ref[...] = (g_ref[...] + r_ref[...]).astype(o_ref.dtype)

    grid_spec = pltpu.PrefetchScalarGridSpec(
        num_scalar_prefetch=1, grid=grid,
        in_specs=[_blocked(g3.shape, (tr, tc), {hf_ax: lambda c_ref: c_ref[0] * grid[hf_ax]}), _blocked(r3.shape, (tr, tc))],
        out_specs=_blocked(r3.shape, (tr, tc)))
    out = pl.pallas_call(body, name=f"pair_sum{t}", grid_spec=grid_spec, out_shape=jax.ShapeDtypeStruct(r3.shape, BF16),
                         compiler_params=_params(("parallel",) * 3))(c_idx, g3, r3)
    return out.reshape(recv.shape)


def _scatter_side(ps):
    shapes = [jax.ShapeDtypeStruct((N_CHIPS - 1,) + _with_axis(p.shape, SHARD_AXIS[t], p.shape[SHARD_AXIS[t]] // N_CHIPS), p.dtype)
              for t, p in enumerate(ps)]

    def copies(ins, outs, ss, rs):
        x, y, c, chips = _place()
        cps = [_remote(_quarter(ins[t], t, 2 * cx + cy), outs[t].at[k], ss, rs, k * N_STACKS + t, (cx, cy, c))
               for k, (cx, cy) in enumerate(chips) for t in range(N_STACKS)]
        return [(cp, cp) for cp in cps]

    return _side(ps, shapes, {}, 3 * N_STACKS, copies)


def _sum_chips(p, slots, t, chip_idx, c_idx):
    p3 = _view3(p)
    sh_ax, hf_ax = _axis3(p, SHARD_AXIS[t]), _axis3(p, HALF_AXIS[t])
    piece3 = _with_axis(p3.shape, sh_ax, p3.shape[sh_ax] // N_CHIPS)
    s4 = slots.reshape((N_CHIPS - 1,) + piece3)
    B, R, C = piece3
    tr = _tile(R, 256)
    grid = (B, R // tr, 1)
    full3 = _with_axis(piece3, hf_ax, 2 * piece3[hf_ax])

    def body(j_ref, c_ref, own_ref, s_ref, o_ref):
        acc = own_ref[...].astype(F32)
        for k in range(N_CHIPS - 1):
            acc = acc + s_ref[k].astype(F32)
        o_ref[...] = acc

    grid_spec = pltpu.PrefetchScalarGridSpec(
        num_scalar_prefetch=2, grid=grid,
        in_specs=[_blocked(p3.shape, (tr, C), {sh_ax: lambda j_ref, c_ref: j_ref[0] * grid[sh_ax]}),
                  pl.BlockSpec((N_CHIPS - 1, None, tr, C), lambda b, r, cc, j_ref, c_ref: (0, b, r, cc))],
        out_specs=_blocked(full3, (tr, C), {hf_ax: lambda j_ref, c_ref: c_ref[0] * grid[hf_ax]}))
    out = pl.pallas_call(body, name=f"sum_chips{t}", grid_spec=grid_spec, out_shape=jax.ShapeDtypeStruct(full3, F32),
                         compiler_params=_params(("parallel",) * 3))(chip_idx, c_idx, p3, s4)
    return out.reshape(_with_axis(slots.shape[1:], HALF_AXIS[t], 2 * slots.shape[1 + HALF_AXIS[t]]))


def _join_side(rs_):
    def copies(ins, outs, ss, rs):
        x, y, c, _ = _place()
        return [(_remote(_half(outs[t], t, c), _half(outs[t], t, c), ss, rs, t, (x, y, 1 - c)),
                 _remote(_half(outs[t], t, 1 - c), _half(outs[t], t, 1 - c), ss, rs, t, (x, y, 1 - c))) for t in range(N_STACKS)]

    return _side(rs_, _same(rs_), {t: t for t in range(N_STACKS)}, N_STACKS, copies)


def _layer_weights(shards, l, chip_idx):
    return [_cast_into_full(s, l, t, chip_idx) for t, s in enumerate(shards)]


def _gather_alone(w):
    return _run_side(_gather_d2d_side(_run_side(_gather_ici_side(w), "gather_ici")), "gather_d2d")


def _forward_layer(x, l, w, small, mods, lb_all, KW, nxt):
    norm_pre, norm_post, hgrn_norm, pool_scale = small
    shift, scale, gate = mods
    row = lambda t: t[l:l + 1]
    h, h_t = _prenorm(x, row(norm_pre), row(scale), row(shift))
    if nxt is None:
        proj = _matmul(h, w[0], "nn", F32, "proj")
    else:
        proj, nxt = _matmul(h, w[0], "nn", F32, "proj_gather", side=_gather_ici_side(nxt))
    o_a, y_a, ya_t, s0 = _hgrn_fwd(proj, row(lb_all), row(hgrn_norm), KW)
    pooled, y_b, yb_t = _pool_fwd(proj, w[1], row(pool_scale), KW)
    pa, pb, merged, merged_t = _merge(y_a, y_b, w[2], w[3], proj)
    if nxt is None:
        out = _matmul(merged, w[4], "nn", F32, "out_proj")
    else:
        out, nxt = _matmul(merged, w[4], "nn", F32, "out_proj_gather", side=_gather_d2d_side(nxt))
    x_new = _postnorm(x, out, row(gate), row(norm_post))
    saved = (x, h_t, proj, o_a, ya_t, s0, pooled, yb_t, pa, pb, merged_t, out)
    return x_new, saved, nxt


def _backward_layer(dx, l, saved, w, small, mods, lb_all, KW, above, idx):
    norm_pre, norm_post, hgrn_norm, pool_scale = small
    shift, scale, gate = mods
    chip_idx, c_idx = idx
    x, h_t, proj, o_a, ya_t, s0, pooled, yb_t, pa, pb, merged_t, out = saved
    row = lambda t: t[l:l + 1]
    dout, d_gate, d_npost = _postnorm_bwd(dx, out, row(gate), row(norm_post))
    if above is None:
        g_out = _matmul(merged_t, dout, "nn", F32, "grad_w_out")
    else:
        g_out, recv = _matmul(merged_t, dout, "nn", F32, "grad_w_out_swap", side=_swap_side(above))
        pair = [_pair_sum(g, r, t, c_idx) for t, (g, r) in enumerate(zip(above, recv))]
    dpa, dpb, dga, dgb = _merge_bwd(dout, w[4], proj, pa, pb)
    g_pa = _matmul(ya_t, dpa, "nn", F32, "grad_w_proj_a")
    g_pb = _matmul(yb_t, dpb, "nn", F32, "grad_w_proj_b")
    dya = _matmul(dpa, w[2], "nt", F32, "d_y_a")
    dyb = _matmul(dpb, w[3], "nt", F32, "d_y_b")
    do, dza, d_hn = _gate_a_bwd(dya, o_a, proj, row(hgrn_norm), KW)
    dq, df, dva, d_lb = _hgrn_bwd(proj, do, s0, row(lb_all), KW)
    u, dzb, d_ps, g_pool = _pool_bwd(dyb, proj, pooled, w[1], row(pool_scale), KW)
    dvb = _pool_bwd_window(u, KW)
    dproj = [dq, df, dva, dza, dvb, dzb, dga, dgb]
    g_in = lax.empty(w[0].shape, F32)
    col = 0
    if above is None:
        dh = _matmul(dproj, w[0], "nt", F32, "d_h", tm=512)
        reduced = None
        for piece in dproj:
            g_in = _matmul(h_t, piece, "nn", F32, "grad_w_in_piece", into=(g_in, col))
            col += piece.shape[1]
    else:
        dh, slots = _matmul(dproj, w[0], "nt", F32, "d_h_scatter", tm=512, side=_scatter_side(pair))
        part = [_sum_chips(p, s, t, chip_idx, c_idx) for t, (p, s) in enumerate(zip(pair, slots))]
        for n, piece in enumerate(dproj):
            if n == 0:
                g_in, reduced = _matmul(h_t, piece, "nn", F32, "grad_w_in_piece_join", into=(g_in, col), side=_join_side(part))
            else:
                g_in = _matmul(h_t, piece, "nn", F32, "grad_w_in_piece", into=(g_in, col))
            col += piece.shape[1]
    dx_new, d_shift, d_scale, d_npre = _prenorm_bwd(dh, x, dx, row(norm_pre), row(scale))
    small_g = jnp.concatenate([d_shift, d_scale, d_gate, d_npre, d_npost, d_lb, d_hn, d_ps], axis=1)
    return dx_new, small_g, [g_in, g_pool, g_pa, g_pb, g_out], reduced


def kernel(x, c, w_ada, b_ada, norm_pre, norm_post, w_in, lower_bounds, hgrn_norm, pool_w, pool_scale, w_proj_a, w_proj_b, w_out, loss_target, m_w_ada, m_b_ada, m_norm_pre, m_norm_post, m_w_in, m_lower_bounds, m_hgrn_norm, m_pool_w, m_pool_scale, m_w_proj_a, m_w_proj_b, m_w_out, v_w_ada, v_b_ada, v_norm_pre, v_norm_post, v_w_in, v_lower_bounds, v_hgrn_norm, v_pool_w, v_pool_scale, v_w_proj_a, v_w_proj_b, v_w_out):
    _, S, D = x.shape
    L = w_in.shape[0]
    KW = D // 2
    xi, yi, ci = lax.axis_index("x"), lax.axis_index("y"), lax.axis_index("c")
    chip = 2 * xi + yi
    dev = 4 * xi + 2 * yi + ci

    c_all = _all_gather_rows(c.reshape(8, D // 8)).reshape(N_DEV, D)
    modp = _mod_part(c_all, w_ada)
    cols = modp.shape[-1]
    modg = _all_gather_rows(modp.reshape(L * N_DEV, cols)).reshape(N_DEV, L, N_DEV, cols)
    mod_all = jnp.transpose(modg[0::2], (1, 2, 0, 3)).reshape(L, N_DEV, 3 * D)
    mod = lax.dynamic_index_in_dim(mod_all, dev, axis=1, keepdims=False) + b_ada
    mods = (mod[:, :D], mod[:, D:2 * D], mod[:, 2 * D:])
    lb_all = _lb_table(lower_bounds)

    shards = [w_in, pool_w, w_proj_a, w_proj_b, w_out]
    chip_idx = chip.astype(jnp.int32).reshape(1)
    c_idx = ci.astype(jnp.int32).reshape(1)
    small = (norm_pre, norm_post, hgrn_norm, pool_scale)

    xs = x[0]
    saved, ws = [], []
    w = _gather_alone(_layer_weights(shards, 0, chip_idx))
    for l in range(L):
        nxt = _layer_weights(shards, l + 1, chip_idx) if l + 1 < L else None
        xs, sv, nxt = _forward_layer(xs, l, w, small, mods, lb_all, KW, nxt)
        saved.append(sv)
        ws.append(w)
        w = nxt
    dx, sq = _loss_head(xs, loss_target[0])
    loss = lax.psum(0.5 * jnp.sum(sq) / D, ("x", "y", "c"))

    small_g, reduced = [None] * L, [None] * L
    above = None
    for l in reversed(range(L)):
        dx, small_g[l], above, red = _backward_layer(dx, l, saved[l], ws[l], small, mods, lb_all, KW, above, (chip_idx, c_idx))
        if red is not None:
            reduced[l + 1] = red
    recv = _run_side(_swap_side(above), "swap")
    pair = [_pair_sum(g, r, t, c_idx) for t, (g, r) in enumerate(zip(above, recv))]
    slots = _run_side(_scatter_side(pair), "scatter")
    reduced[0] = _run_side(_join_side([_sum_chips(p, s, t, chip_idx, c_idx) for t, (p, s) in enumerate(zip(pair, slots))]), "join")
    grad_x = dx[None]

    sg = jnp.concatenate(small_g, axis=0)
    sg = jnp.concatenate([sg, jnp.zeros((8 - L, sg.shape[1]), F32)], axis=0)
    parts = _all_gather_rows(sg).reshape(N_DEV, 8, sg.shape[1])[:, :L]
    wmv = [(b_ada, m_b_ada, v_b_ada), (norm_pre, m_norm_pre, v_norm_pre), (norm_post, m_norm_post, v_norm_post),
           (lower_bounds, m_lower_bounds, v_lower_bounds), (hgrn_norm, m_hgrn_norm, v_hgrn_norm), (pool_scale, m_pool_scale, v_pool_scale)]
    r_b_ada, r_npre, r_npost, r_lb, r_hn, r_ps = _small_update(parts, lower_bounds, wmv, D, KW)
    dmod = lax.dynamic_slice_in_dim(parts[:, :, :3 * D], chip * cols, cols, axis=2)
    r_w_ada = _w_ada_update(c_all, jnp.transpose(dmod, (1, 0, 2)), w_ada, m_w_ada, v_w_ada)

    grads = [jnp.stack([reduced[l][t] for l in range(L)], axis=0) for t in range(N_STACKS)]
    ms = [m_w_in, m_pool_w, m_w_proj_a, m_w_proj_b, m_w_out]
    vs = [v_w_in, v_pool_w, v_w_proj_a, v_w_proj_b, v_w_out]
    r_big = [[g] + _adamw(w, g, m, v, f"adamw{t}") for t, (w, g, m, v) in enumerate(zip(shards, grads, ms, vs))]
    r_w_in, r_pool_w, r_pa, r_pb, r_w_out = r_big

    order = [r_w_ada, r_b_ada, r_npre, r_npost, r_w_in, r_lb, r_hn, r_pool_w, r_ps, r_pa, r_pb, r_w_out]
    outs = [loss, grad_x]
    for k in range(4):
        outs += [r[k] for r in order]
    return tuple(outs)
```

```python
import functools

import jax
import jax.numpy as jnp
from jax import lax
from jax.experimental import pallas as pl
from jax.experimental.pallas import tpu as pltpu

F32 = jnp.float32
BF16 = jnp.bfloat16
MXU_DT = jnp.bfloat16

CHUNK = 64
SUBLANES = 8
SUB = 8
LEVELS = (32, 16, 8)
HEAD = 128
EPS = 1e-6
MIN_FORGET = 1e-30
WINDOWS = (2, 4, 8, 16)
HALO = 16
N_CHIPS = 4
N_DEV = 8
VMEM_LIMIT_BYTES = 56 * 1024 * 1024

ADAM_LR = 0.001
ADAM_B1 = 0.9
ADAM_B2 = 0.999
ADAM_EPS = 1e-08
ADAM_WD = 0.01
ADAM_STEP = 10

NN = ((1,), (0,))
NT = ((1,), (1,))
TN = ((0,), (0,))
MESH = pl.DeviceIdType.MESH


def _params(sem):
    return pltpu.CompilerParams(dimension_semantics=sem, vmem_limit_bytes=VMEM_LIMIT_BYTES)


def _tile(n, pref):
    return pref if n % pref == 0 else n


def _dot(a, b, dims):
    return lax.dot_general(a.astype(MXU_DT), b.astype(MXU_DT), (dims, ((), ())), preferred_element_type=F32)


def _sig_pair(a):
    e = jnp.exp(-jnp.abs(a))
    inv = 1.0 / (1.0 + e)
    pos = a >= 0
    return jnp.where(pos, inv, e * inv), jnp.where(pos, e * inv, inv)


def _sig(a):
    return 0.5 * jnp.tanh(0.5 * a) + 0.5


def _split_dot(tri, x):
    def top(y):
        return lax.bitcast_convert_type(lax.bitcast_convert_type(y, jnp.uint32) & jnp.uint32(0xFFFF0000), F32)

    hi = top(x)
    r1 = x - hi
    mid = top(r1)
    lo = r1 - mid
    d = lambda y: jnp.dot(tri, y.astype(BF16), preferred_element_type=F32)
    return d(hi) + d(mid) + d(lo)


def _rms(xv):
    return lax.rsqrt(jnp.mean(xv * xv, axis=-1, keepdims=True) + EPS)


def _prenorm(x, g, scale, shift):
    S, D = x.shape
    tm = _tile(S, 256)

    def body(x_ref, g_ref, sc_ref, sh_ref, h_ref, ht_ref):
        xv = x_ref[...]
        h = (xv * _rms(xv)) * g_ref[...] * (1.0 + sc_ref[...]) + sh_ref[...]
        h_ref[...] = h.astype(h_ref.dtype)
        ht_ref[...] = h.T.astype(ht_ref.dtype)

    row = pl.BlockSpec((tm, D), lambda i: (i, 0))
    vec = pl.BlockSpec((1, D), lambda i: (0, 0))
    return pl.pallas_call(body, name="prenorm", grid=(S // tm,), in_specs=[row, vec, vec, vec],
                          out_specs=[row, pl.BlockSpec((D, tm), lambda i: (0, i))],
                          out_shape=[jax.ShapeDtypeStruct((S, D), MXU_DT), jax.ShapeDtypeStruct((D, S), MXU_DT)],
                          compiler_params=_params(("parallel",)))(x, g, scale, shift)


def _prenorm_bwd(dh, x, dx, g, scale):
    S, D = x.shape
    tm = _tile(S, 256)

    def body(dh_ref, x_ref, dx_ref, g_ref, sc_ref, o_ref, dsh_ref, dsc_ref, dg_ref):
        @pl.when(pl.program_id(0) == 0)
        def _():
            dsh_ref[...] = jnp.zeros_like(dsh_ref)
            dsc_ref[...] = jnp.zeros_like(dsc_ref)
            dg_ref[...] = jnp.zeros_like(dg_ref)

        xv = x_ref[...]
        r = _rms(xv)
        xh = xv * r
        dh = dh_ref[...]
        gv = g_ref[...]
        one_sc = 1.0 + sc_ref[...]
        dsh_ref[...] += jnp.sum(dh, axis=0, keepdims=True)
        dsc_ref[...] += jnp.sum(dh * xh * gv, axis=0, keepdims=True)
        dg_ref[...] += jnp.sum(dh * xh * one_sc, axis=0, keepdims=True)
        dxh = dh * gv * one_sc
        o_ref[...] = dx_ref[...] + r * (dxh - xh * jnp.mean(dxh * xh, axis=-1, keepdims=True))

    row = pl.BlockSpec((tm, D), lambda i: (i, 0))
    vec = pl.BlockSpec((1, D), lambda i: (0, 0))
    vs = jax.ShapeDtypeStruct((1, D), F32)
    return pl.pallas_call(body, name="prenorm_bwd", grid=(S // tm,), in_specs=[row, row, row, vec, vec],
                          out_specs=[row, vec, vec, vec], out_shape=[jax.ShapeDtypeStruct((S, D), F32), vs, vs, vs],
                          compiler_params=_params(("arbitrary",)))(dh, x, dx, g, scale)


def _postnorm(x, out, gate, g):
    S, D = x.shape
    tm = _tile(S, 256)

    def body(x_ref, o_ref, gt_ref, g_ref, y_ref):
        ov = o_ref[...]
        y_ref[...] = x_ref[...] + gt_ref[...] * ((ov * _rms(ov)) * g_ref[...])

    row = pl.BlockSpec((tm, D), lambda i: (i, 0))
    vec = pl.BlockSpec((1, D), lambda i: (0, 0))
    return pl.pallas_call(body, name="postnorm", grid=(S // tm,), in_specs=[row, row, vec, vec], out_specs=row,
                          out_shape=jax.ShapeDtypeStruct((S, D), F32), compiler_params=_params(("parallel",)))(x, out, gate, g)


def _postnorm_bwd(dx, out, gate, g):
    S, D = dx.shape
    tm = _tile(S, 256)

    def body(dx_ref, o_ref, gt_ref, g_ref, do_ref, dgt_ref, dg_ref):
        @pl.when(pl.program_id(0) == 0)
        def _():
            dgt_ref[...] = jnp.zeros_like(dgt_ref)
            dg_ref[...] = jnp.zeros_like(dg_ref)

        ov = o_ref[...]
        r = _rms(ov)
        rn = ov * r
        dxv = dx_ref[...]
        gv = g_ref[...]
        dgt_ref[...] += jnp.sum(dxv * rn * gv, axis=0, keepdims=True)
        du = dxv * gt_ref[...]
        dg_ref[...] += jnp.sum(du * rn, axis=0, keepdims=True)
        drn = du * gv
        do_ref[...] = (r * (drn - rn * jnp.mean(drn * rn, axis=-1, keepdims=True))).astype(do_ref.dtype)

    row = pl.BlockSpec((tm, D), lambda i: (i, 0))
    vec = pl.BlockSpec((1, D), lambda i: (0, 0))
    vs = jax.ShapeDtypeStruct((1, D), F32)
    return pl.pallas_call(body, name="postnorm_bwd", grid=(S // tm,), in_specs=[row, row, vec, vec],
                          out_specs=[row, vec, vec], out_shape=[jax.ShapeDtypeStruct((S, D), MXU_DT), vs, vs],
                          compiler_params=_params(("arbitrary",)))(dx, out, gate, g)


def _loss_head(y, target):
    S, D = y.shape
    tm = _tile(S, 256)

    def body(y_ref, t_ref, dy_ref, sq_ref):
        @pl.when(pl.program_id(0) == 0)
        def _():
            sq_ref[...] = jnp.zeros_like(sq_ref)

        e = y_ref[...] - t_ref[...]
        dy_ref[...] = e * (1.0 / D)
        sq_ref[...] += jnp.sum(e * e, axis=0, keepdims=True)

    row = pl.BlockSpec((tm, D), lambda i: (i, 0))
    vec = pl.BlockSpec((1, D), lambda i: (0, 0))
    return pl.pallas_call(body, name="loss_head", grid=(S // tm,), in_specs=[row, row], out_specs=[row, vec],
                          out_shape=[jax.ShapeDtypeStruct((S, D), F32), jax.ShapeDtypeStruct((1, D), F32)],
                          compiler_params=_params(("arbitrary",)))(y, target)


def _matmul(a, b, mode, out_dtype, name, *, side=None, into=None, tm=1024, tn=1024, tk=2048):
    groups = [list(g) if isinstance(g, (list, tuple)) else [g] for g in (a if isinstance(a, (list, tuple)) else [a])]
    pieces = [p for g in groups for p in g]
    na = len(pieces)
    if mode == "tn":
        K, M = pieces[0].shape
        gwidths = [K]
    else:
        M = pieces[0].shape[0]
        gwidths = [sum(p.shape[1] for p in g) for g in groups]
        K = sum(gwidths)
    N = b.shape[0] if mode == "nt" else b.shape[1]
    tm, tn, tk = _tile(M, tm), _tile(N, tn), _tile(min(gwidths), tk)
    ni, nj, nk = M // tm, N // tn, K // tk
    dims = {"nn": NN, "nt": NT, "tn": TN}[mode]

    a_specs, ranges, k0 = [], [], 0
    for g, wd in zip(groups, gwidths):
        n = wd // tk
        if mode == "tn":
            a_specs.append(pl.BlockSpec((tk, tm), lambda i, j, k: (k, i)))
        elif len(g) == 1:
            a_specs.append(pl.BlockSpec((tm, tk), lambda i, j, k, k0=k0, n=n: (i, jnp.clip(k - k0, 0, n - 1))))
        else:
            a_specs += [pl.BlockSpec((tm, p.shape[1]), lambda i, j, k: (i, 0)) for p in g]
        ranges.append((k0, k0 + n))
        k0 += n
    if mode == "nt":
        b_spec = pl.BlockSpec((tn, tk), lambda i, j, k: (j, k))
    else:
        b_spec = pl.BlockSpec((tk, tn), lambda i, j, k: (k, j))
    n_in = len(side["inputs"]) if side else 0
    n_out = len(side["out_shapes"]) if side else 0
    n_buf = 1 if into else 0

    def body(*refs):
        a_refs, b_ref, rest = refs[:na], refs[na], refs[na + 1:]
        side_in, o_ref = rest[:n_in], rest[n_in + n_buf]
        side_out = rest[n_in + n_buf + 1:n_in + n_buf + 1 + n_out]
        scratch = rest[n_in + n_buf + 1 + n_out:]
        i, j, k = pl.program_id(0), pl.program_id(1), pl.program_id(2)
        if side:
            sems = scratch[-2:]

            @pl.when((i == 0) & (j == 0) & (k == 0))
            def _():
                side["start"](side_in, side_out, *sems)

        def accumulate(g_refs):
            p, off = None, 0
            for a_ref in g_refs:
                wd = a_ref.shape[0] if mode == "tn" else a_ref.shape[1]
                bv = b_ref[...] if len(g_refs) == 1 else (b_ref[:, off:off + wd] if mode == "nt" else b_ref[off:off + wd, :])
                q = lax.dot_general(a_ref[...], bv, (dims, ((), ())), preferred_element_type=F32)
                p = q if p is None else p + q
                off += wd
            if nk == 1:
                o_ref[...] = p.astype(o_ref.dtype)
            else:
                acc_ref = scratch[0]

                @pl.when(k == 0)
                def _():
                    acc_ref[...] = p

                @pl.when(k > 0)
                def _():
                    acc_ref[...] += p

                @pl.when(k == nk - 1)
                def _():
                    o_ref[...] = acc_ref[...].astype(o_ref.dtype)

        if len(groups) == 1:
            accumulate(a_refs)
        else:
            first = 0
            for (lo, hi), g in zip(ranges, groups):
                pl.when((k >= lo) & (k < hi))(functools.partial(accumulate, a_refs[first:first + len(g)]))
                first += len(g)

        if side:
            @pl.when((i == ni - 1) & (j == nj - 1) & (k == nk - 1))
            def _():
                side["finish"](side_in, side_out, *sems)

    any_spec = pl.BlockSpec(memory_space=pl.ANY)
    scratch_shapes = [pltpu.VMEM((tm, tn), F32)] if nk > 1 else []
    if side:
        scratch_shapes += [pltpu.SemaphoreType.DMA((side["n_sems"],)), pltpu.SemaphoreType.DMA((side["n_sems"],))]
    aliases = {na + 1 + s: 1 + d for s, d in side["aliases"].items()} if side else {}
    args = pieces + [b] + (list(side["inputs"]) if side else [])
    if into:
        buf, col = into
        aliases[len(args)] = 0
        args.append(buf)
        j0 = col // tn
        out_spec, out_shape = pl.BlockSpec((tm, tn), lambda i, j, k: (i, j0 + j)), jax.ShapeDtypeStruct(buf.shape, buf.dtype)
    else:
        out_spec, out_shape = pl.BlockSpec((tm, tn), lambda i, j, k: (i, j)), jax.ShapeDtypeStruct((M, N), out_dtype)
    res = pl.pallas_call(
        body, name=name, grid=(ni, nj, nk), in_specs=a_specs + [b_spec] + [any_spec] * (n_in + n_buf),
        out_specs=[out_spec] + [any_spec] * n_out,
        out_shape=[out_shape] + (list(side["out_shapes"]) if side else []),
        scratch_shapes=scratch_shapes, input_output_aliases=aliases,
        compiler_params=_params(("arbitrary",) * 3 if side else ("parallel", "parallel", "arbitrary")))(*args)
    return (res[0], list(res[1:])) if side else res[0]


def _merge(ya, yb, wpa, wpb, proj):
    S, KW = ya.shape
    D = 2 * KW
    tm, tn = _tile(S, 1024), _tile(D, 512)
    ga0, gb0 = 3 * D // tn, 4 * D // tn

    def body(ya_ref, yb_ref, wa_ref, wb_ref, ga_ref, gb_ref, pa_ref, pb_ref, m_ref, mt_ref):
        pa = jnp.dot(ya_ref[...], wa_ref[...], preferred_element_type=F32)
        pb = jnp.dot(yb_ref[...], wb_ref[...], preferred_element_type=F32)
        pa_ref[...] = pa.astype(pa_ref.dtype)
        pb_ref[...] = pb.astype(pb_ref.dtype)
        m = _sig(ga_ref[...]) * pa + _sig(gb_ref[...]) * pb
        m_ref[...] = m.astype(m_ref.dtype)
        mt_ref[...] = m.T.astype(mt_ref.dtype)

    y_spec = pl.BlockSpec((tm, KW), lambda i, j: (i, 0))
    w_spec = pl.BlockSpec((KW, tn), lambda i, j: (0, j))
    o_spec = pl.BlockSpec((tm, tn), lambda i, j: (i, j))
    return pl.pallas_call(
        body, name="merge", grid=(S // tm, D // tn),
        in_specs=[y_spec, y_spec, w_spec, w_spec, pl.BlockSpec((tm, tn), lambda i, j: (i, ga0 + j)),
                  pl.BlockSpec((tm, tn), lambda i, j: (i, gb0 + j))],
        out_specs=[o_spec, o_spec, o_spec, pl.BlockSpec((tn, tm), lambda i, j: (j, i))],
        out_shape=[jax.ShapeDtypeStruct((S, D), MXU_DT), jax.ShapeDtypeStruct((S, D), MXU_DT), jax.ShapeDtypeStruct((S, D), MXU_DT),
                   jax.ShapeDtypeStruct((D, S), MXU_DT)],
        compiler_params=_params(("parallel", "parallel")))(ya, yb, wpa, wpb, proj, proj)


def _merge_bwd(dout, w_out, proj, pa, pb):
    S, D = dout.shape
    tm, tn = _tile(S, 1024), _tile(D, 512)
    ga0, gb0 = 3 * D // tn, 4 * D // tn

    def body(do_ref, w_ref, ga_ref, gb_ref, pa_ref, pb_ref, dpa_ref, dpb_ref, dga_ref, dgb_ref):
        dm = lax.dot_general(do_ref[...], w_ref[...], (NT, ((), ())), preferred_element_type=F32)
        sa, sb = _sig(ga_ref[...]), _sig(gb_ref[...])
        sna, snb = 1.0 - sa, 1.0 - sb
        dpa = dm * sa
        dpb = dm * sb
        dpa_ref[...] = dpa.astype(dpa_ref.dtype)
        dpb_ref[...] = dpb.astype(dpb_ref.dtype)
        dga_ref[...] = (dpa * pa_ref[...].astype(F32) * sna).astype(dga_ref.dtype)
        dgb_ref[...] = (dpb * pb_ref[...].astype(F32) * snb).astype(dgb_ref.dtype)

    blk = pl.BlockSpec((tm, tn), lambda i, j: (i, j))
    os = jax.ShapeDtypeStruct((S, D), MXU_DT)
    return pl.pallas_call(
        body, name="merge_bwd", grid=(S // tm, D // tn),
        in_specs=[pl.BlockSpec((tm, D), lambda i, j: (i, 0)), pl.BlockSpec((tn, D), lambda i, j: (j, 0)),
                  pl.BlockSpec((tm, tn), lambda i, j: (i, ga0 + j)), pl.BlockSpec((tm, tn), lambda i, j: (i, gb0 + j)), blk, blk],
        out_specs=[blk, blk, blk, blk], out_shape=[os, os, os, os],
        compiler_params=_params(("parallel", "parallel")))(dout, w_out, proj, proj, pa, pb)


def _gates(qr, a, lbv):
    sq = _sig(qr)
    q = qr * sq
    s, sn = _sig_pair(a)
    omlb = 1.0 - lbv
    f = lbv + omlb * s
    logf = jnp.log(jnp.maximum(f, MIN_FORGET))
    kk = omlb * sn
    return sq, q, s, sn, omlb, f, logf, kk


def _shift_in_block(x, d):
    return pltpu.roll(x.reshape(CHUNK // SUBLANES, SUBLANES, HEAD), d, axis=1).reshape(CHUNK, HEAD)


def _level_masks(h):
    ti = lax.broadcasted_iota(jnp.int32, (CHUNK, 1), 0)
    si = lax.broadcasted_iota(jnp.int32, (1, CHUNK), 1)
    t2, s2 = (ti & (2 * h - 1)) >= h, (si & (2 * h - 1)) >= h
    same = (ti & -(2 * h)) == (si & -(2 * h))
    return t2, jnp.where(same & t2 & jnp.logical_not(s2), 1.0, 0.0)


def _level_factor(b, h):
    parts = [jnp.broadcast_to(b[g + h - 1:g + h], (2 * h, HEAD)) for g in range(0, CHUNK, 2 * h)]
    ref = parts[0] if len(parts) == 1 else jnp.concatenate(parts, axis=0)
    return jnp.exp(-jnp.abs(b - ref))


def _level_operands(q, kk, b, h, t2):
    fac = _level_factor(b, h)
    return jnp.where(t2, q * fac, 0.0), jnp.where(t2, 0.0, kk * fac), fac


def _heads_per_step(H, pref):
    return pref if H % pref == 0 else 1


def _tri(lower):
    r = lax.broadcasted_iota(jnp.int32, (CHUNK, CHUNK), 0)
    c = lax.broadcasted_iota(jnp.int32, (CHUNK, CHUNK), 1)
    return jnp.where((r >= c) if lower else (c >= r), 1.0, 0.0).astype(BF16)


def _hgrn_fwd(proj, lb, hn, KW):
    S = proj.shape[0]
    H = KW // HEAD
    T = _tile(S, 512)
    nci, nb = T // CHUNK, S // T
    HP = _heads_per_step(H, 8)

    def body(q_ref, f_ref, v_ref, z_ref, lb_ref, hn_ref, o_ref, y_ref, yt_ref, s0_ref, st_ref):
        @pl.when(pl.program_id(1) == 0)
        def _():
            st_ref[...] = jnp.zeros_like(st_ref)

        tril = _tri(True)
        rmod = lax.broadcasted_iota(jnp.int32, (CHUNK, 1), 0) & (SUB - 1)
        masks = [_level_masks(h) for h in LEVELS]
        lb_all = lb_ref[...]

        def chunk(ci, carry):
            rows = pl.ds(pl.multiple_of(ci * CHUNK, CHUNK), CHUNK)
            _, q_all, _, _, _, _, logf_all, kk_all = _gates(q_ref[rows, :], f_ref[rows, :], lb_all)
            b_all = _split_dot(tril, logf_all)
            for hp in range(HP):
                cols = slice(hp * HEAD, (hp + 1) * HEAD)
                one_head(ci, rows, hp, cols, q_all[:, cols], kk_all[:, cols], b_all[:, cols])
            return carry

        def one_head(ci, rows, hp, cols, q, kk, b):
            hnv = hn_ref[:, cols]
            vv = v_ref[rows, cols]
            eb = jnp.exp(b)
            st = st_ref[hp]
            s0_ref[ci, hp] = st
            p = None
            for h, (t2, m) in zip(LEVELS, masks):
                qs, ks, _ = _level_operands(q, kk, b, h, t2)
                pm = m * _dot(qs, ks, NT)
                p = pm if p is None else p + pm
            o = _dot(q * eb, st, NT) + _dot(p, vv, NN) + jnp.sum(q * kk, axis=-1, keepdims=True) * vv
            for d in range(1, SUB):
                kd_, bd_, vd_ = _shift_in_block(kk, d), _shift_in_block(b, d), _shift_in_block(vv, d)
                e = jnp.exp(jnp.minimum(b - bd_, 0.0))
                p = jnp.where(rmod >= d, jnp.sum(q * (kd_ * e), axis=-1, keepdims=True), 0.0)
                o = o + p * vd_
            bl = b[CHUNK - 1:CHUNK]
            st_ref[hp] = st * eb[CHUNK - 1:CHUNK] + _dot(vv, kk * jnp.exp(bl - b), TN)
            o_ref[rows, cols] = o
            z = z_ref[rows, cols]
            y_ref[rows, cols] = ((o * _rms(o)) * hnv * (z * _sig(z))).astype(y_ref.dtype)

        lax.fori_loop(0, nci, chunk, 0)
        yt_ref[...] = y_ref[...].astype(F32).T.astype(yt_ref.dtype)

    W = HP * HEAD
    G = H // HP

    def col(off):
        return pl.BlockSpec((T, W), lambda h, t: (t, off + h))

    vec = pl.BlockSpec((1, W), lambda h, t: (0, h))
    return pl.pallas_call(
        body, name="hgrn_fwd", grid=(G, nb),
        in_specs=[col(0), col(G), col(2 * G), col(3 * G), vec, vec],
        out_specs=[pl.BlockSpec((T, W), lambda h, t: (t, h)), pl.BlockSpec((T, W), lambda h, t: (t, h)),
                   pl.BlockSpec((W, T), lambda h, t: (h, t)), pl.BlockSpec((nci, HP, HEAD, HEAD), lambda h, t: (t, h, 0, 0))],
        out_shape=[jax.ShapeDtypeStruct((S, KW), F32), jax.ShapeDtypeStruct((S, KW), MXU_DT),
                   jax.ShapeDtypeStruct((KW, S), MXU_DT), jax.ShapeDtypeStruct((S // CHUNK, H, HEAD, HEAD), F32)],
        scratch_shapes=[pltpu.VMEM((HP, HEAD, HEAD), F32)],
        compiler_params=_params(("parallel", "arbitrary")))(proj, proj, proj, proj, lb, hn)


def _gate_a_bwd(dya, o, proj, hn, KW):
    S = o.shape[0]
    H = KW // HEAD
    T = _tile(S, 512)

    def body(dy_ref, o_ref, z_ref, hn_ref, do_ref, dz_ref, dhn_ref):
        @pl.when(pl.program_id(1) == 0)
        def _():
            dhn_ref[...] = jnp.zeros_like(dhn_ref)

        ov = o_ref[...]
        r = _rms(ov)
        rn = ov * r
        z = z_ref[...]
        sz = _sig(z)
        silu = z * sz
        dy = dy_ref[...]
        hnv = hn_ref[...]
        dz_ref[...] = (dy * rn * hnv * (sz * (1.0 + z * (1.0 - sz)))).astype(dz_ref.dtype)
        dhn_ref[...] += jnp.sum(dy * rn * silu, axis=0, keepdims=True)
        drn = dy * hnv * silu
        do_ref[...] = r * (drn - rn * jnp.mean(drn * rn, axis=-1, keepdims=True))

    blk = pl.BlockSpec((T, HEAD), lambda h, t: (t, h))
    vec = pl.BlockSpec((1, HEAD), lambda h, t: (0, h))
    return pl.pallas_call(
        body, name="gate_a_bwd", grid=(H, S // T),
        in_specs=[blk, blk, pl.BlockSpec((T, HEAD), lambda h, t: (t, 3 * H + h)), vec],
        out_specs=[blk, blk, vec],
        out_shape=[jax.ShapeDtypeStruct((S, KW), F32), jax.ShapeDtypeStruct((S, KW), MXU_DT), jax.ShapeDtypeStruct((1, KW), F32)],
        compiler_params=_params(("parallel", "arbitrary")))(dya, o, proj, hn)


def _hgrn_bwd(proj, do, s0, lb, KW):
    S = proj.shape[0]
    H = KW // HEAD
    T = _tile(S, 512)
    nci, nb = T // CHUNK, S // T
    nsub = CHUNK // SUB
    HP = _heads_per_step(H, 4)

    def body(q_ref, f_ref, v_ref, do_ref, s0_ref, lb_ref, dq_ref, df_ref, dv_ref, dlb_ref, dst_ref):
        @pl.when(pl.program_id(1) == 0)
        def _():
            dst_ref[...] = jnp.zeros_like(dst_ref)
            dlb_ref[...] = jnp.zeros_like(dlb_ref)

        tril, triu = _tri(True), _tri(False)
        rmod = lax.broadcasted_iota(jnp.int32, (CHUNK, 1), 0) & (SUB - 1)
        masks = [_level_masks(h) for h in LEVELS]
        lb_all = lb_ref[...]

        def chunk(it, carry):
            ci = nci - 1 - it
            rows = pl.ds(pl.multiple_of(ci * CHUNK, CHUNK), CHUNK)
            qr_all = q_ref[rows, :]
            sq_all, q_all, s_all, sn_all, omlb, f_all, logf_all, kk_all = _gates(qr_all, f_ref[rows, :], lb_all)
            b_all = _split_dot(tril, logf_all)
            res = []
            for hp in range(HP):
                cols = slice(hp * HEAD, (hp + 1) * HEAD)
                res.append(one_head(ci, rows, hp, cols, q_all[:, cols], kk_all[:, cols], b_all[:, cols]))
            dq_t, dk_t, dv_t = (jnp.concatenate([r[i] for r in res], axis=1) for i in range(3))
            w = jnp.concatenate([r[3] for r in res], axis=1)
            dlogf = _split_dot(triu, q_all * dq_t - kk_all * dk_t) + w
            dlf = jnp.where(f_all > MIN_FORGET, dlogf / jnp.maximum(f_all, MIN_FORGET), 0.0)
            t1 = dlf - dk_t
            dlb_ref[...] += jnp.sum(sn_all * t1, axis=0, keepdims=True)
            df_ref[rows, :] = (omlb * (s_all * sn_all) * t1).astype(df_ref.dtype)
            dq_ref[rows, :] = (dq_t * (sq_all * (1.0 + qr_all * (1.0 - sq_all)))).astype(dq_ref.dtype)
            dv_ref[rows, :] = dv_t.astype(dv_ref.dtype)
            return carry

        def one_head(ci, rows, hp, cols, q, kk, b):
            vv, dov = v_ref[rows, cols], do_ref[rows, cols]
            eb = jnp.exp(b)
            st0 = s0_ref[ci, hp]
            dst = dst_ref[hp]
            bl, ec = b[CHUNK - 1:CHUNK], eb[CHUNK - 1:CHUNK]
            decl = jnp.exp(bl - b)
            kdec = kk * decl
            dq_t = eb * _dot(dov, st0, NN)
            dst_ref[hp] = dst * ec + _dot(dov, q * eb, TN)
            dv_t = _dot(kdec, dst, NT)
            dk_t = decl * _dot(vv, dst, NN)
            stc = st0 * ec + _dot(vv, kdec, TN)
            w = jnp.sum(stc * dst, axis=0, keepdims=True)
            dp_all = _dot(dov, vv, NT)
            p = None
            for h, (t2, m) in zip(LEVELS, masks):
                qs, ks, fac = _level_operands(q, kk, b, h, t2)
                pm = m * _dot(qs, ks, NT)
                p = pm if p is None else p + pm
                dpm = m * dp_all
                dq_t = dq_t + fac * _dot(dpm, ks, NN)
                dk_t = dk_t + fac * _dot(dpm, qs, TN)
            p0 = jnp.sum(q * kk, axis=-1, keepdims=True)
            dp0 = jnp.sum(dov * vv, axis=-1, keepdims=True)
            dq_t = dq_t + dp0 * kk
            dk_t = dk_t + dp0 * q
            dv_t = dv_t + _dot(p, dov, TN) + p0 * dov
            for d in range(1, SUB):
                kd_, bd_, vd_ = _shift_in_block(kk, d), _shift_in_block(b, d), _shift_in_block(vv, d)
                e = jnp.exp(jnp.minimum(b - bd_, 0.0))
                ke = kd_ * e
                m = rmod >= d
                p = jnp.where(m, jnp.sum(q * ke, axis=-1, keepdims=True), 0.0)
                dp = jnp.where(m, jnp.sum(dov * vd_, axis=-1, keepdims=True), 0.0)
                dq_t = dq_t + dp * ke
                dk_t = dk_t + _shift_in_block(dp * (q * e), SUBLANES - d)
                dv_t = dv_t + _shift_in_block(p * dov, SUBLANES - d)
            return dq_t, dk_t, dv_t, w

        lax.fori_loop(0, nci, chunk, 0)

    W = HP * HEAD
    G = H // HP

    def col(off):
        return pl.BlockSpec((T, W), lambda h, t: (nb - 1 - t, off + h))

    blk = pl.BlockSpec((T, W), lambda h, t: (nb - 1 - t, h))
    vec = pl.BlockSpec((1, W), lambda h, t: (0, h))
    os = jax.ShapeDtypeStruct((S, KW), MXU_DT)
    return pl.pallas_call(
        body, name="hgrn_bwd", grid=(G, nb),
        in_specs=[col(0), col(G), col(2 * G), blk, pl.BlockSpec((nci, HP, HEAD, HEAD), lambda h, t: (nb - 1 - t, h, 0, 0)), vec],
        out_specs=[blk, blk, blk, vec], out_shape=[os, os, os, jax.ShapeDtypeStruct((1, KW), F32)],
        scratch_shapes=[pltpu.VMEM((HP, HEAD, HEAD), F32)],
        compiler_params=_params(("parallel", "arbitrary")))(proj, proj, proj, do, s0, lb)


def _pool_pos(i, T):
    return (i * T + lax.broadcasted_iota(jnp.int32, (T, 1), 0) + 1).astype(F32)


def _pool_fwd(proj, pool_w, pool_scale, KW):
    S = proj.shape[0]
    GW = KW // len(WINDOWS)
    T = _tile(S, 256)

    def body(v_ref, halo_ref, z_ref, pw_ref, ps_ref, p_ref, y_ref, yt_ref):
        i = pl.program_id(0)
        vb = v_ref[...]
        ext = jnp.concatenate([jnp.where(i > 0, halo_ref[...], 0.0), vb], axis=0)
        pos = _pool_pos(i, T)
        z = z_ref[...]
        gate = ps_ref[...] * (z * _sig(z))
        for g, w in enumerate(WINDOWS):
            sl = slice(g * GW, (g + 1) * GW)
            sg = ext[:, sl]
            for jj in range(g + 1):
                sg = sg + pltpu.roll(sg, 1 << jj, axis=0)
            pooled = (sg[HALO:] / jnp.minimum(pos, float(w)) - vb[:, sl]).astype(p_ref.dtype)
            p_ref[:, sl] = pooled
            yg = _dot(pooled, pw_ref[g], NN) * gate[:, sl]
            y_ref[:, sl] = yg.astype(y_ref.dtype)
            yt_ref[sl, :] = yg.T.astype(yt_ref.dtype)

    row = pl.BlockSpec((T, KW), lambda i: (i, 0))
    hb = T // HALO
    return pl.pallas_call(
        body, name="pool_fwd", grid=(S // T,),
        in_specs=[pl.BlockSpec((T, KW), lambda i: (i, 4)), pl.BlockSpec((HALO, KW), lambda i: (jnp.maximum(i * hb - 1, 0), 4)),
                  pl.BlockSpec((T, KW), lambda i: (i, 5)), pl.BlockSpec((len(WINDOWS), GW, GW), lambda i: (0, 0, 0)),
                  pl.BlockSpec((1, KW), lambda i: (0, 0))],
        out_specs=[row, row, pl.BlockSpec((KW, T), lambda i: (0, i))],
        out_shape=[jax.ShapeDtypeStruct((S, KW), MXU_DT), jax.ShapeDtypeStruct((S, KW), MXU_DT), jax.ShapeDtypeStruct((KW, S), MXU_DT)],
        compiler_params=_params(("parallel",)))(proj, proj, proj, pool_w, pool_scale)


def _pool_bwd(dyb, proj, pooled, pool_w, pool_scale, KW):
    S = proj.shape[0]
    G = len(WINDOWS)
    GW = KW // G
    T = _tile(S, 256)

    def body(dy_ref, z_ref, p_ref, pw_ref, ps_ref, u_ref, dz_ref, dps_ref, dpw_ref):
        i = pl.program_id(0)

        @pl.when(i == 0)
        def _():
            dps_ref[...] = jnp.zeros_like(dps_ref)
            dpw_ref[...] = jnp.zeros_like(dpw_ref)

        pos = _pool_pos(i, T)
        z = z_ref[...]
        sz = _sig(z)
        silu = z * sz
        dsilu = sz * (1.0 + z * (1.0 - sz))
        dy = dy_ref[...]
        ps = ps_ref[...]
        for g, w in enumerate(WINDOWS):
            sl = slice(g * GW, (g + 1) * GW)
            pg = p_ref[:, sl]
            mixed = _dot(pg, pw_ref[g], NN)
            dyg = dy[:, sl]
            dz_ref[:, sl] = (dyg * mixed * ps[:, sl] * dsilu[:, sl]).astype(dz_ref.dtype)
            dps_ref[:, sl] += jnp.sum(dyg * mixed * silu[:, sl], axis=0, keepdims=True)
            dmix = dyg * ps[:, sl] * silu[:, sl]
            dpw_ref[g] += _dot(pg, dmix, TN)
            u_ref[:, sl] = _dot(dmix, pw_ref[g], NT) / jnp.minimum(pos, float(w))

    row = pl.BlockSpec((T, KW), lambda i: (i, 0))
    return pl.pallas_call(
        body, name="pool_bwd", grid=(S // T,),
        in_specs=[row, pl.BlockSpec((T, KW), lambda i: (i, 5)), row,
                  pl.BlockSpec((G, GW, GW), lambda i: (0, 0, 0)), pl.BlockSpec((1, KW), lambda i: (0, 0))],
        out_specs=[row, row, pl.BlockSpec((1, KW), lambda i: (0, 0)), pl.BlockSpec((G, GW, GW), lambda i: (0, 0, 0))],
        out_shape=[jax.ShapeDtypeStruct((S, KW), F32), jax.ShapeDtypeStruct((S, KW), MXU_DT),
                   jax.ShapeDtypeStruct((1, KW), F32), jax.ShapeDtypeStruct((G, GW, GW), F32)],
        compiler_params=_params(("arbitrary",)))(dyb, proj, pooled, pool_w, pool_scale)


def _pool_bwd_window(u, KW):
    S = u.shape[0]
    GW = KW // len(WINDOWS)
    T = _tile(S, 256)
    nb = S // T
    n = T + HALO

    def body(u_ref, halo_ref, dv_ref):
        i = pl.program_id(0)
        uv = u_ref[...]
        ext = jnp.concatenate([uv, jnp.where(i < nb - 1, halo_ref[...], 0.0)], axis=0)
        pos = _pool_pos(i, T)
        for g, w in enumerate(WINDOWS):
            sl = slice(g * GW, (g + 1) * GW)
            sg = ext[:, sl]
            for jj in range(g + 1):
                sg = sg + pltpu.roll(sg, n - (1 << jj), axis=0)
            dv_ref[:, sl] = (sg[:T] - uv[:, sl] * jnp.minimum(pos, float(w))).astype(dv_ref.dtype)

    hb = T // HALO
    return pl.pallas_call(
        body, name="pool_bwd_window", grid=(nb,),
        in_specs=[pl.BlockSpec((T, KW), lambda i: (i, 0)),
                  pl.BlockSpec((HALO, KW), lambda i: (jnp.minimum((i + 1) * hb, S // HALO - 1), 0))],
        out_specs=pl.BlockSpec((T, KW), lambda i: (i, 0)), out_shape=jax.ShapeDtypeStruct((S, KW), MXU_DT),
        compiler_params=_params(("parallel",)))(u, u)


def _softmax_rows(x):
    m = jnp.max(x, axis=0, keepdims=True)
    e = jnp.exp(x - m)
    return e / jnp.sum(e, axis=0, keepdims=True)


def _lb_table(lower_bounds):
    L, KW = lower_bounds.shape

    def body(x_ref, o_ref):
        sm = _softmax_rows(x_ref[...])
        acc = jnp.zeros((1, KW), F32)
        o_ref[0:1, :] = acc
        for l in range(1, L):
            acc = acc + sm[l:l + 1]
            o_ref[l:l + 1, :] = acc

    return pl.pallas_call(body, name="lb_table", out_shape=jax.ShapeDtypeStruct((L, KW), F32))(lower_bounds)


def _mod_part(c_all, w_ada):
    L, D, C = w_ada.shape
    B = c_all.shape[0]
    tn = _tile(C, 512)

    def body(c_ref, w_ref, o_ref):
        cv = c_ref[...]
        o_ref[...] = _dot(cv * _sig(cv), w_ref[...], NN)

    return pl.pallas_call(
        body, name="mod_part", grid=(L, C // tn),
        in_specs=[pl.BlockSpec((B, D), lambda l, j: (0, 0)), pl.BlockSpec((None, D, tn), lambda l, j: (l, 0, j))],
        out_specs=pl.BlockSpec((None, B, tn), lambda l, j: (l, 0, j)), out_shape=jax.ShapeDtypeStruct((L, B, C), F32),
        compiler_params=_params(("parallel", "parallel")))(c_all, w_ada)


def _adamw_math(w, g, m, v):
    m = ADAM_B1 * m + (1.0 - ADAM_B1) * g
    v = ADAM_B2 * v + (1.0 - ADAM_B2) * (g * g)
    m_hat = m / (1.0 - ADAM_B1 ** ADAM_STEP)
    v_hat = v / (1.0 - ADAM_B2 ** ADAM_STEP)
    delta = -ADAM_LR * (m_hat / (jnp.sqrt(v_hat) + ADAM_EPS) + ADAM_WD * w)
    return delta, m, v


def _adamw(w, g, m, v, name):
    shape = w.shape
    C = shape[-1]
    R = w.size // C
    tr = _tile(R, 128)
    flat = lambda t: t.reshape(R, C)

    def body(w_ref, g_ref, m_ref, v_ref, d_ref, nm_ref, nv_ref):
        d, nm, nv = _adamw_math(w_ref[...], g_ref[...], m_ref[...], v_ref[...])
        d_ref[...] = d
        nm_ref[...] = nm
        nv_ref[...] = nv

    blk = pl.BlockSpec((tr, C), lambda i: (i, 0))
    os = jax.ShapeDtypeStruct((R, C), F32)
    outs = pl.pallas_call(body, name=name, grid=(R // tr,), in_specs=[blk] * 4, out_specs=[blk] * 3, out_shape=[os] * 3,
                          compiler_params=_params(("parallel",)))(flat(w), flat(g), flat(m), flat(v))
    return [t.reshape(shape) for t in outs]


def _w_ada_update(c_all, dmod, w, m, v):
    L, D, C = w.shape
    B = c_all.shape[0]
    tn = _tile(C, 256)

    def body(c_ref, dm_ref, w_ref, m_ref, v_ref, g_ref, d_ref, nm_ref, nv_ref):
        cv = c_ref[...]
        g = lax.dot_general(cv * _sig(cv), dm_ref[...], (TN, ((), ())), preferred_element_type=F32,
                            precision=lax.Precision.HIGHEST)
        d, nm, nv = _adamw_math(w_ref[...], g, m_ref[...], v_ref[...])
        g_ref[...] = g
        d_ref[...] = d
        nm_ref[...] = nm
        nv_ref[...] = nv

    blk = pl.BlockSpec((None, D, tn), lambda l, j: (l, 0, j))
    os = jax.ShapeDtypeStruct((L, D, C), F32)
    return pl.pallas_call(
        body, name="w_ada_update", grid=(L, C // tn),
        in_specs=[pl.BlockSpec((B, D), lambda l, j: (0, 0)), pl.BlockSpec((None, B, tn), lambda l, j: (l, 0, j)), blk, blk, blk],
        out_specs=[blk] * 4, out_shape=[os] * 4, compiler_params=_params(("parallel", "parallel")))(c_all, dmod, w, m, v)


def _small_update(parts, lower_bounds, wmv, D, KW):
    L = parts.shape[1]
    widths = [3 * D, D, D, KW, KW, KW]
    offs = [sum(widths[:i]) for i in range(len(widths))]

    def body(p_ref, *refs):
        ins, outs = refs[:18], refs[18:]
        tot = p_ref[0]
        for dev in range(1, N_DEV):
            tot = tot + p_ref[dev]
        grads = [tot[:, o:o + wd] for o, wd in zip(offs, widths)]
        sm = _softmax_rows(ins[9][...])
        dlb = grads[3]
        dsm = [jnp.zeros((1, KW), F32)]
        for j in range(1, L):
            acc = dlb[j:j + 1]
            for l in range(j + 1, L):
                acc = acc + dlb[l:l + 1]
            dsm.append(acc)
        dsm = jnp.concatenate(dsm, axis=0)
        grads[3] = sm * (dsm - jnp.sum(sm * dsm, axis=0, keepdims=True))
        for p in range(6):
            w_ref, m_ref, v_ref = ins[3 * p:3 * p + 3]
            d, nm, nv = _adamw_math(w_ref[...], grads[p], m_ref[...], v_ref[...])
            outs[4 * p][...] = grads[p]
            outs[4 * p + 1][...] = d
            outs[4 * p + 2][...] = nm
            outs[4 * p + 3][...] = nv

    flat = [t for trip in wmv for t in trip]
    out_shape = []
    for wd in widths:
        out_shape += [jax.ShapeDtypeStruct((L, wd), F32)] * 4
    res = pl.pallas_call(body, name="small_update", out_shape=out_shape,
                         compiler_params=pltpu.CompilerParams(vmem_limit_bytes=VMEM_LIMIT_BYTES))(parts, *flat)
    return [res[4 * p:4 * p + 4] for p in range(6)]


def _place():
    x, y, c = lax.axis_index("x"), lax.axis_index("y"), lax.axis_index("c")
    chips = [(1 - x, y), (x, 1 - y), (1 - x, 1 - y)]
    return x, y, c, chips


def _all_gather_rows(blk):
    m_per, n = blk.shape

    def body(x_ref, out_ref, send_sems, recv_sems, local_sem):
        x, y, c, chips = _place()
        me, sibling = (x, y, c), (x, y, 1 - c)

        def rows(px, py, pc):
            return out_ref.at[pl.ds((4 * px + 2 * py + pc) * m_per, m_per), :]

        def copy(k, block, to, src=None):
            return pltpu.make_async_remote_copy(src_ref=rows(*block) if src is None else src, dst_ref=rows(*block),
                                                send_sem=send_sems.at[k], recv_sem=recv_sems.at[k], device_id=to, device_id_type=MESH)

        mine = pltpu.make_async_copy(x_ref, rows(*me), local_sem)
        mine.start()
        first = [copy(0, me, sibling, src=x_ref)]
        first += [copy(1 + j, me, (*chip, c), src=x_ref) for j, chip in enumerate(chips)]
        for cp in first:
            cp.start()
        passed = [copy(4 + j, (*chip, c), sibling) for j, chip in enumerate(chips)]
        for j, chip in enumerate(chips):
            copy(1 + j, (*chip, c), me).wait_recv()
            passed[j].start()
        copy(0, sibling, me).wait_recv()
        for j, chip in enumerate(chips):
            copy(4 + j, (*chip, 1 - c), me).wait_recv()
        for cp in first + passed:
            cp.wait_send()
        mine.wait()

    return pl.pallas_call(
        body, name="all_gather_rows", out_shape=jax.ShapeDtypeStruct((N_DEV * m_per, n), blk.dtype),
        in_specs=[pl.BlockSpec(memory_space=pltpu.VMEM)], out_specs=pl.BlockSpec(memory_space=pltpu.VMEM),
        scratch_shapes=[pltpu.SemaphoreType.DMA((7,)), pltpu.SemaphoreType.DMA((7,)), pltpu.SemaphoreType.DMA],
        compiler_params=pltpu.CompilerParams(vmem_limit_bytes=VMEM_LIMIT_BYTES))(blk)


SHARD_AXIS = (1, 1, 1, 1, 0)
HALF_AXIS = (0, 0, 0, 0, 1)
N_STACKS = 5


def _cut(ref, axis, part, n):
    idx = [slice(None)] * len(ref.shape)
    idx[axis] = pl.ds(pl.multiple_of(part * n, n), n)
    return ref.at[tuple(idx)]


def _quarter(ref, t, chip):
    return _cut(ref, SHARD_AXIS[t], chip, ref.shape[SHARD_AXIS[t]] // N_CHIPS)


def _half(ref, t, core):
    return _cut(ref, HALF_AXIS[t], core, ref.shape[HALF_AXIS[t]] // 2)


def _with_axis(shape, axis, n):
    return tuple(n if a == axis else s for a, s in enumerate(shape))


def _view3(t):
    return t.reshape((1,) * (3 - t.ndim) + t.shape)


def _axis3(t, axis):
    return axis + 3 - t.ndim


def _blocked(shape3, blk, offsets=None):
    offsets = offsets or {}

    def index(b, r, c, *pre):
        idx = [b, r, c]
        for ax, fn in offsets.items():
            idx[ax] = idx[ax] + fn(*pre)
        return tuple(idx)

    return pl.BlockSpec((None,) + tuple(blk), index)


def _cast_into_full(stack, l, t, chip_idx):
    shard = stack.shape[1:]
    nd = len(shard)
    w4 = stack.reshape((stack.shape[0],) + (1,) * (3 - nd) + shard)
    ax = SHARD_AXIS[t] + 3 - nd
    _, B, R, C = w4.shape
    tr = _tile(R, 256)
    per = (R // tr, 1)[ax - 1]

    def body(j_ref, w_ref, o_ref):
        o_ref[...] = w_ref[...].astype(o_ref.dtype)

    full3 = _with_axis((B, R, C), ax, N_CHIPS * (B, R, C)[ax])
    grid_spec = pltpu.PrefetchScalarGridSpec(
        num_scalar_prefetch=1, grid=(B, R // tr, 1),
        in_specs=[pl.BlockSpec((None, None, tr, C), lambda b, r, c, j_ref: (l, b, r, c))],
        out_specs=_blocked(full3, (tr, C), {ax: lambda j_ref: j_ref[0] * per}))
    o = pl.pallas_call(body, name=f"cast_into_full{t}", grid_spec=grid_spec, out_shape=jax.ShapeDtypeStruct(full3, MXU_DT),
                       compiler_params=_params(("parallel",) * 3))(chip_idx, w4)
    return o.reshape(_with_axis(shard, SHARD_AXIS[t], N_CHIPS * shard[SHARD_AXIS[t]]))


def _remote(src, dst, send_sems, recv_sems, k, to):
    return pltpu.make_async_remote_copy(src_ref=src, dst_ref=dst, send_sem=send_sems.at[k], recv_sem=recv_sems.at[k],
                                        device_id=to, device_id_type=MESH)


def _side(inputs, out_shapes, aliases, n_sems, copies):
    def start(ins, outs, ss, rs):
        for snd, _ in copies(ins, outs, ss, rs):
            snd.start()

    def finish(ins, outs, ss, rs):
        for snd, rcv in copies(ins, outs, ss, rs):
            rcv.wait_recv()
            snd.wait_send()

    return dict(inputs=list(inputs), out_shapes=list(out_shapes), aliases=aliases, n_sems=n_sems, start=start, finish=finish)


def _run_side(side, name):
    n_in, n_out = len(side["inputs"]), len(side["out_shapes"])

    def body(*refs):
        ins, outs, sems = refs[:n_in], refs[n_in:n_in + n_out], refs[n_in + n_out:]
        side["start"](ins, outs, *sems)
        side["finish"](ins, outs, *sems)

    any_spec = pl.BlockSpec(memory_space=pl.ANY)
    return list(pl.pallas_call(
        body, name=name, out_shape=side["out_shapes"], in_specs=[any_spec] * n_in, out_specs=[any_spec] * n_out,
        input_output_aliases=side["aliases"],
        scratch_shapes=[pltpu.SemaphoreType.DMA((side["n_sems"],)), pltpu.SemaphoreType.DMA((side["n_sems"],))])(*side["inputs"]))


def _same(arrays):
    return [jax.ShapeDtypeStruct(a.shape, a.dtype) for a in arrays]


def _gather_ici_side(ws):
    def copies(ins, outs, ss, rs):
        x, y, c, chips = _place()
        out = []
        for k, (cx, cy) in enumerate(chips):
            for t in range(N_STACKS):
                mine = _half(_quarter(outs[t], t, 2 * x + y), t, c)
                theirs = _half(_quarter(outs[t], t, 2 * cx + cy), t, c)
                n = k * N_STACKS + t
                out.append((_remote(mine, mine, ss, rs, n, (cx, cy, c)), _remote(theirs, theirs, ss, rs, n, (cx, cy, c))))
        return out

    return _side(ws, _same(ws), {t: t for t in range(N_STACKS)}, 3 * N_STACKS, copies)


def _gather_d2d_side(ws):
    def copies(ins, outs, ss, rs):
        x, y, c, chips = _place()
        out = []
        for k, (cx, cy) in enumerate(chips):
            for t in range(N_STACKS):
                q = _quarter(outs[t], t, 2 * cx + cy)
                mine, theirs = _half(q, t, c), _half(q, t, 1 - c)
                n = k * N_STACKS + t
                out.append((_remote(mine, mine, ss, rs, n, (x, y, 1 - c)), _remote(theirs, theirs, ss, rs, n, (x, y, 1 - c))))
        return out

    return _side(ws, _same(ws), {t: t for t in range(N_STACKS)}, 3 * N_STACKS, copies)


def _swap_side(gs):
    shapes = [jax.ShapeDtypeStruct(_with_axis(g.shape, HALF_AXIS[t], g.shape[HALF_AXIS[t]] // 2), g.dtype) for t, g in enumerate(gs)]

    def copies(ins, outs, ss, rs):
        x, y, c, _ = _place()
        cps = [_remote(_half(ins[t], t, 1 - c), outs[t], ss, rs, t, (x, y, 1 - c)) for t in range(N_STACKS)]
        return [(cp, cp) for cp in cps]

    return _side(gs, shapes, {}, N_STACKS, copies)


def _pair_sum(g, recv, t, c_idx):
    g3, r3 = _view3(g), _view3(recv)
    hf_ax = _axis3(g, HALF_AXIS[t])
    B, R, C = r3.shape
    tr, tc = _tile(R, 256), _tile(C, 2048)
    grid = (B, R // tr, C // tc)

    def body(c_ref, g_ref, r_ref, o_ref):
        o_ref[...] = (g_ref[...] + r_ref[...]).astype(o_ref.dtype)

    grid_spec = pltpu.PrefetchScalarGridSpec(
        num_scalar_prefetch=1, grid=grid,
        in_specs=[_blocked(g3.shape, (tr, tc), {hf_ax: lambda c_ref: c_ref[0] * grid[hf_ax]}), _blocked(r3.shape, (tr, tc))],
        out_specs=_blocked(r3.shape, (tr, tc)))
    out = pl.pallas_call(body, name=f"pair_sum{t}", grid_spec=grid_spec, out_shape=jax.ShapeDtypeStruct(r3.shape, BF16),
                         compiler_params=_params(("parallel",) * 3))(c_idx, g3, r3)
    return out.reshape(recv.shape)


def _scatter_side(ps):
    shapes = [jax.ShapeDtypeStruct((N_CHIPS - 1,) + _with_axis(p.shape, SHARD_AXIS[t], p.shape[SHARD_AXIS[t]] // N_CHIPS), p.dtype)
              for t, p in enumerate(ps)]

    def copies(ins, outs, ss, rs):
        x, y, c, chips = _place()
        cps = [_remote(_quarter(ins[t], t, 2 * cx + cy), outs[t].at[k], ss, rs, k * N_STACKS + t, (cx, cy, c))
               for k, (cx, cy) in enumerate(chips) for t in range(N_STACKS)]
        return [(cp, cp) for cp in cps]

    return _side(ps, shapes, {}, 3 * N_STACKS, copies)


def _sum_chips(p, slots, t, chip_idx, c_idx):
    p3 = _view3(p)
    sh_ax, hf_ax = _axis3(p, SHARD_AXIS[t]), _axis3(p, HALF_AXIS[t])
    piece3 = _with_axis(p3.shape, sh_ax, p3.shape[sh_ax] // N_CHIPS)
    s4 = slots.reshape((N_CHIPS - 1,) + piece3)
    B, R, C = piece3
    tr = _tile(R, 256)
    grid = (B, R // tr, 1)
    full3 = _with_axis(piece3, hf_ax, 2 * piece3[hf_ax])

    def body(j_ref, c_ref, own_ref, s_ref, o_ref):
        acc = own_ref[...].astype(F32)
        for k in range(N_CHIPS - 1):
            acc = acc + s_ref[k].astype(F32)
        o_ref[...] = acc

    grid_spec = pltpu.PrefetchScalarGridSpec(
        num_scalar_prefetch=2, grid=grid,
        in_specs=[_blocked(p3.shape, (tr, C), {sh_ax: lambda j_ref, c_ref: j_ref[0] * grid[sh_ax]}),
                  pl.BlockSpec((N_CHIPS - 1, None, tr, C), lambda b, r, cc, j_ref, c_ref: (0, b, r, cc))],
        out_specs=_blocked(full3, (tr, C), {hf_ax: lambda j_ref, c_ref: c_ref[0] * grid[hf_ax]}))
    out = pl.pallas_call(body, name=f"sum_chips{t}", grid_spec=grid_spec, out_shape=jax.ShapeDtypeStruct(full3, F32),
                         compiler_params=_params(("parallel",) * 3))(chip_idx, c_idx, p3, s4)
    return out.reshape(_with_axis(slots.shape[1:], HALF_AXIS[t], 2 * slots.shape[1 + HALF_AXIS[t]]))


def _join_side(rs_):
    def copies(ins, outs, ss, rs):
        x, y, c, _ = _place()
        return [(_remote(_half(outs[t], t, c), _half(outs[t], t, c), ss, rs, t, (x, y, 1 - c)),
                 _remote(_half(outs[t], t, 1 - c), _half(outs[t], t, 1 - c), ss, rs, t, (x, y, 1 - c))) for t in range(N_STACKS)]

    return _side(rs_, _same(rs_), {t: t for t in range(N_STACKS)}, N_STACKS, copies)


def _layer_weights(shards, l, chip_idx):
    return [_cast_into_full(s, l, t, chip_idx) for t, s in enumerate(shards)]


def _gather_alone(w):
    return _run_side(_gather_d2d_side(_run_side(_gather_ici_side(w), "gather_ici")), "gather_d2d")


def _forward_layer(x, l, w, small, mods, lb_all, KW, nxt):
    norm_pre, norm_post, hgrn_norm, pool_scale = small
    shift, scale, gate = mods
    row = lambda t: t[l:l + 1]
    h, h_t = _prenorm(x, row(norm_pre), row(scale), row(shift))
    if nxt is None:
        proj = _matmul(h, w[0], "nn", F32, "proj")
    else:
        proj, nxt = _matmul(h, w[0], "nn", F32, "proj_gather", side=_gather_ici_side(nxt))
    o_a, y_a, ya_t, s0 = _hgrn_fwd(proj, row(lb_all), row(hgrn_norm), KW)
    pooled, y_b, yb_t = _pool_fwd(proj, w[1], row(pool_scale), KW)
    pa, pb, merged, merged_t = _merge(y_a, y_b, w[2], w[3], proj)
    if nxt is None:
        out = _matmul(merged, w[4], "nn", F32, "out_proj")
    else:
        out, nxt = _matmul(merged, w[4], "nn", F32, "out_proj_gather", side=_gather_d2d_side(nxt))
    x_new = _postnorm(x, out, row(gate), row(norm_post))
    saved = (x, h_t, proj, o_a, ya_t, s0, pooled, yb_t, pa, pb, merged_t, out)
    return x_new, saved, nxt


def _backward_layer(dx, l, saved, w, small, mods, lb_all, KW, above, idx):
    norm_pre, norm_post, hgrn_norm, pool_scale = small
    shift, scale, gate = mods
    chip_idx, c_idx = idx
    x, h_t, proj, o_a, ya_t, s0, pooled, yb_t, pa, pb, merged_t, out = saved
    row = lambda t: t[l:l + 1]
    dout, d_gate, d_npost = _postnorm_bwd(dx, out, row(gate), row(norm_post))
    if above is None:
        g_out = _matmul(merged_t, dout, "nn", F32, "grad_w_out")
    else:
        g_out, recv = _matmul(merged_t, dout, "nn", F32, "grad_w_out_swap", side=_swap_side(above))
        pair = [_pair_sum(g, r, t, c_idx) for t, (g, r) in enumerate(zip(above, recv))]
    dpa, dpb, dga, dgb = _merge_bwd(dout, w[4], proj, pa, pb)
    g_pa = _matmul(ya_t, dpa, "nn", F32, "grad_w_proj_a")
    g_pb = _matmul(yb_t, dpb, "nn", F32, "grad_w_proj_b")
    dya = _matmul(dpa, w[2], "nt", F32, "d_y_a")
    dyb = _matmul(dpb, w[3], "nt", F32, "d_y_b")
    do, dza, d_hn = _gate_a_bwd(dya, o_a, proj, row(hgrn_norm), KW)
    dq, df, dva, d_lb = _hgrn_bwd(proj, do, s0, row(lb_all), KW)
    u, dzb, d_ps, g_pool = _pool_bwd(dyb, proj, pooled, w[1], row(pool_scale), KW)
    dvb = _pool_bwd_window(u, KW)
    dproj = [dq, df, dva, dza, dvb, dzb, dga, dgb]
    g_in = lax.empty(w[0].shape, F32)
    col = 0
    dproj_k = [[dq, df], [dva, dza], [dvb, dzb], dga, dgb]
    if above is None:
        dh = _matmul(dproj_k, w[0], "nt", F32, "d_h", tm=512)
        reduced = None
        for piece in dproj:
            g_in = _matmul(h_t, piece, "nn", F32, "grad_w_in_piece", into=(g_in, col))
            col += piece.shape[1]
    else:
        dh, slots = _matmul(dproj_k, w[0], "nt", F32, "d_h_scatter", tm=512, side=_scatter_side(pair))
        part = [_sum_chips(p, s, t, chip_idx, c_idx) for t, (p, s) in enumerate(zip(pair, slots))]
        for n, piece in enumerate(dproj):
            if n == 0:
                g_in, reduced = _matmul(h_t, piece, "nn", F32, "grad_w_in_piece_join", into=(g_in, col), side=_join_side(part))
            else:
                g_in = _matmul(h_t, piece, "nn", F32, "grad_w_in_piece", into=(g_in, col))
            col += piece.shape[1]
    dx_new, d_shift, d_scale, d_npre = _prenorm_bwd(dh, x, dx, row(norm_pre), row(scale))
    small_g = jnp.concatenate([d_shift, d_scale, d_gate, d_npre, d_npost, d_lb, d_hn, d_ps], axis=1)
    return dx_new, small_g, [g_in, g_pool, g_pa, g_pb, g_out], reduced


def kernel(x, c, w_ada, b_ada, norm_pre, norm_post, w_in, lower_bounds, hgrn_norm, pool_w, pool_scale, w_proj_a, w_proj_b, w_out, loss_target, m_w_ada, m_b_ada, m_norm_pre, m_norm_post, m_w_in, m_lower_bounds, m_hgrn_norm, m_pool_w, m_pool_scale, m_w_proj_a, m_w_proj_b, m_w_out, v_w_ada, v_b_ada, v_norm_pre, v_norm_post, v_w_in, v_lower_bounds, v_hgrn_norm, v_pool_w, v_pool_scale, v_w_proj_a, v_w_proj_b, v_w_out):
    _, S, D = x.shape
    L = w_in.shape[0]
    KW = D // 2
    xi, yi, ci = lax.axis_index("x"), lax.axis_index("y"), lax.axis_index("c")
    chip = 2 * xi + yi
    dev = 4 * xi + 2 * yi + ci

    c_all = _all_gather_rows(c.reshape(8, D // 8)).reshape(N_DEV, D)
    modp = _mod_part(c_all, w_ada)
    cols = modp.shape[-1]
    modg = _all_gather_rows(modp.reshape(L * N_DEV, cols)).reshape(N_DEV, L, N_DEV, cols)
    mod_all = jnp.transpose(modg[0::2], (1, 2, 0, 3)).reshape(L, N_DEV, 3 * D)
    mod = lax.dynamic_index_in_dim(mod_all, dev, axis=1, keepdims=False) + b_ada
    mods = (mod[:, :D], mod[:, D:2 * D], mod[:, 2 * D:])
    lb_all = _lb_table(lower_bounds)

    shards = [w_in, pool_w, w_proj_a, w_proj_b, w_out]
    chip_idx = chip.astype(jnp.int32).reshape(1)
    c_idx = ci.astype(jnp.int32).reshape(1)
    small = (norm_pre, norm_post, hgrn_norm, pool_scale)

    xs = x[0]
    saved, ws = [], []
    w = _gather_alone(_layer_weights(shards, 0, chip_idx))
    for l in range(L):
        nxt = _layer_weights(shards, l + 1, chip_idx) if l + 1 < L else None
        xs, sv, nxt = _forward_layer(xs, l, w, small, mods, lb_all, KW, nxt)
        saved.append(sv)
        ws.append(w)
        w = nxt
    dx, sq = _loss_head(xs, loss_target[0])
    loss = lax.psum(0.5 * jnp.sum(sq) / D, ("x", "y", "c"))

    small_g, reduced = [None] * L, [None] * L
    above = None
    for l in reversed(range(L)):
        dx, small_g[l], above, red = _backward_layer(dx, l, saved[l], ws[l], small, mods, lb_all, KW, above, (chip_idx, c_idx))
        if red is not None:
            reduced[l + 1] = red
    recv = _run_side(_swap_side(above), "swap")
    pair = [_pair_sum(g, r, t, c_idx) for t, (g, r) in enumerate(zip(above, recv))]
    slots = _run_side(_scatter_side(pair), "scatter")
    reduced[0] = _run_side(_join_side([_sum_chips(p, s, t, chip_idx, c_idx) for t, (p, s) in enumerate(zip(pair, slots))]), "join")
    grad_x = dx[None]

    sg = jnp.concatenate(small_g, axis=0)
    sg = jnp.concatenate([sg, jnp.zeros((8 - L, sg.shape[1]), F32)], axis=0)
    parts = _all_gather_rows(sg).reshape(N_DEV, 8, sg.shape[1])[:, :L]
    wmv = [(b_ada, m_b_ada, v_b_ada), (norm_pre, m_norm_pre, v_norm_pre), (norm_post, m_norm_post, v_norm_post),
           (lower_bounds, m_lower_bounds, v_lower_bounds), (hgrn_norm, m_hgrn_norm, v_hgrn_norm), (pool_scale, m_pool_scale, v_pool_scale)]
    r_b_ada, r_npre, r_npost, r_lb, r_hn, r_ps = _small_update(parts, lower_bounds, wmv, D, KW)
    dmod = lax.dynamic_slice_in_dim(parts[:, :, :3 * D], chip * cols, cols, axis=2)
    r_w_ada = _w_ada_update(c_all, jnp.transpose(dmod, (1, 0, 2)), w_ada, m_w_ada, v_w_ada)

    grads = [jnp.stack([reduced[l][t] for l in range(L)], axis=0) for t in range(N_STACKS)]
    ms = [m_w_in, m_pool_w, m_w_proj_a, m_w_proj_b, m_w_out]
    vs = [v_w_in, v_pool_w, v_w_proj_a, v_w_proj_b, v_w_out]
    r_big = [[g] + _adamw(w, g, m, v, f"adamw{t}") for t, (w, g, m, v) in enumerate(zip(shards, grads, ms, vs))]
    r_w_in, r_pool_w, r_pa, r_pb, r_w_out = r_big

    order = [r_w_ada, r_b_ada, r_npre, r_npost, r_w_in, r_lb, r_hn, r_pool_w, r_ps, r_pa, r_pb, r_w_out]
    outs = [loss, grad_x]
    for k in range(4):
        outs += [r[k] for r in order]
    return tuple(outs)
```

```python
import functools

import jax
import jax.numpy as jnp
from jax import lax
from jax.experimental import pallas as pl
from jax.experimental.pallas import tpu as pltpu

F32 = jnp.float32
BF16 = jnp.bfloat16
MXU_DT = jnp.bfloat16

CHUNK = 64
SUBLANES = 8
SUB = 8
LEVELS = (32, 16, 8)
HEAD = 128
EPS = 1e-6
MIN_FORGET = 1e-30
WINDOWS = (2, 4, 8, 16)
HALO = 16
N_CHIPS = 4
N_DEV = 8
VMEM_LIMIT_BYTES = 56 * 1024 * 1024

ADAM_LR = 0.001
ADAM_B1 = 0.9
ADAM_B2 = 0.999
ADAM_EPS = 1e-08
ADAM_WD = 0.01
ADAM_STEP = 10

NN = ((1,), (0,))
NT = ((1,), (1,))
TN = ((0,), (0,))
MESH = pl.DeviceIdType.MESH


def _params(sem):
    return pltpu.CompilerParams(dimension_semantics=sem, vmem_limit_bytes=VMEM_LIMIT_BYTES)


def _tile(n, pref):
    return pref if n % pref == 0 else n


def _dot(a, b, dims):
    return lax.dot_general(a.astype(MXU_DT), b.astype(MXU_DT), (dims, ((), ())), preferred_element_type=F32)


def _sig_pair(a):
    e = jnp.exp(-jnp.abs(a))
    inv = 1.0 / (1.0 + e)
    pos = a >= 0
    return jnp.where(pos, inv, e * inv), jnp.where(pos, e * inv, inv)


def _sig(a):
    return 0.5 * jnp.tanh(0.5 * a) + 0.5


def _split_dot(tri, x):
    def top(y):
        return lax.bitcast_convert_type(lax.bitcast_convert_type(y, jnp.uint32) & jnp.uint32(0xFFFF0000), F32)

    hi = top(x)
    r1 = x - hi
    mid = top(r1)
    lo = r1 - mid
    d = lambda y: jnp.dot(tri, y.astype(BF16), preferred_element_type=F32)
    return d(hi) + d(mid) + d(lo)


def _rms(xv):
    return lax.rsqrt(jnp.mean(xv * xv, axis=-1, keepdims=True) + EPS)


def _prenorm(x, g, scale, shift):
    S, D = x.shape
    tm = _tile(S, 256)

    def body(x_ref, g_ref, sc_ref, sh_ref, h_ref):
        xv = x_ref[...]
        h = (xv * _rms(xv)) * g_ref[...] * (1.0 + sc_ref[...]) + sh_ref[...]
        h_ref[...] = h.astype(h_ref.dtype)

    row = pl.BlockSpec((tm, D), lambda i: (i, 0))
    vec = pl.BlockSpec((1, D), lambda i: (0, 0))
    return pl.pallas_call(body, name="prenorm", grid=(S // tm,), in_specs=[row, vec, vec, vec], out_specs=row,
                          out_shape=jax.ShapeDtypeStruct((S, D), MXU_DT), compiler_params=_params(("parallel",)))(x, g, scale, shift)


def _prenorm_bwd(dh, x, dx, g, scale):
    S, D = x.shape
    tm = _tile(S, 256)

    def body(dh_ref, x_ref, dx_ref, g_ref, sc_ref, o_ref, dsh_ref, dsc_ref, dg_ref):
        @pl.when(pl.program_id(0) == 0)
        def _():
            dsh_ref[...] = jnp.zeros_like(dsh_ref)
            dsc_ref[...] = jnp.zeros_like(dsc_ref)
            dg_ref[...] = jnp.zeros_like(dg_ref)

        xv = x_ref[...]
        r = _rms(xv)
        xh = xv * r
        dh = dh_ref[...]
        gv = g_ref[...]
        one_sc = 1.0 + sc_ref[...]
        dsh_ref[...] += jnp.sum(dh, axis=0, keepdims=True)
        dsc_ref[...] += jnp.sum(dh * xh * gv, axis=0, keepdims=True)
        dg_ref[...] += jnp.sum(dh * xh * one_sc, axis=0, keepdims=True)
        dxh = dh * gv * one_sc
        o_ref[...] = dx_ref[...] + r * (dxh - xh * jnp.mean(dxh * xh, axis=-1, keepdims=True))

    row = pl.BlockSpec((tm, D), lambda i: (i, 0))
    vec = pl.BlockSpec((1, D), lambda i: (0, 0))
    vs = jax.ShapeDtypeStruct((1, D), F32)
    return pl.pallas_call(body, name="prenorm_bwd", grid=(S // tm,), in_specs=[row, row, row, vec, vec],
                          out_specs=[row, vec, vec, vec], out_shape=[jax.ShapeDtypeStruct((S, D), F32), vs, vs, vs],
                          compiler_params=_params(("arbitrary",)))(dh, x, dx, g, scale)


def _postnorm(x, out, gate, g):
    S, D = x.shape
    tm = _tile(S, 256)

    def body(x_ref, o_ref, gt_ref, g_ref, y_ref):
        ov = o_ref[...]
        y_ref[...] = x_ref[...] + gt_ref[...] * ((ov * _rms(ov)) * g_ref[...])

    row = pl.BlockSpec((tm, D), lambda i: (i, 0))
    vec = pl.BlockSpec((1, D), lambda i: (0, 0))
    return pl.pallas_call(body, name="postnorm", grid=(S // tm,), in_specs=[row, row, vec, vec], out_specs=row,
                          out_shape=jax.ShapeDtypeStruct((S, D), F32), compiler_params=_params(("parallel",)))(x, out, gate, g)


def _postnorm_bwd(dx, out, gate, g):
    S, D = dx.shape
    tm = _tile(S, 256)

    def body(dx_ref, o_ref, gt_ref, g_ref, do_ref, dgt_ref, dg_ref):
        @pl.when(pl.program_id(0) == 0)
        def _():
            dgt_ref[...] = jnp.zeros_like(dgt_ref)
            dg_ref[...] = jnp.zeros_like(dg_ref)

        ov = o_ref[...]
        r = _rms(ov)
        rn = ov * r
        dxv = dx_ref[...]
        gv = g_ref[...]
        dgt_ref[...] += jnp.sum(dxv * rn * gv, axis=0, keepdims=True)
        du = dxv * gt_ref[...]
        dg_ref[...] += jnp.sum(du * rn, axis=0, keepdims=True)
        drn = du * gv
        do_ref[...] = (r * (drn - rn * jnp.mean(drn * rn, axis=-1, keepdims=True))).astype(do_ref.dtype)

    row = pl.BlockSpec((tm, D), lambda i: (i, 0))
    vec = pl.BlockSpec((1, D), lambda i: (0, 0))
    vs = jax.ShapeDtypeStruct((1, D), F32)
    return pl.pallas_call(body, name="postnorm_bwd", grid=(S // tm,), in_specs=[row, row, vec, vec],
                          out_specs=[row, vec, vec], out_shape=[jax.ShapeDtypeStruct((S, D), MXU_DT), vs, vs],
                          compiler_params=_params(("arbitrary",)))(dx, out, gate, g)


def _loss_head(y, target):
    S, D = y.shape
    tm = _tile(S, 256)

    def body(y_ref, t_ref, dy_ref, sq_ref):
        @pl.when(pl.program_id(0) == 0)
        def _():
            sq_ref[...] = jnp.zeros_like(sq_ref)

        e = y_ref[...] - t_ref[...]
        dy_ref[...] = e * (1.0 / D)
        sq_ref[...] += jnp.sum(e * e, axis=0, keepdims=True)

    row = pl.BlockSpec((tm, D), lambda i: (i, 0))
    vec = pl.BlockSpec((1, D), lambda i: (0, 0))
    return pl.pallas_call(body, name="loss_head", grid=(S // tm,), in_specs=[row, row], out_specs=[row, vec],
                          out_shape=[jax.ShapeDtypeStruct((S, D), F32), jax.ShapeDtypeStruct((1, D), F32)],
                          compiler_params=_params(("arbitrary",)))(y, target)


def _matmul(a, b, mode, out_dtype, name, *, side=None, into=None, tm=1024, tn=1024, tk=2048):
    groups = [list(g) if isinstance(g, (list, tuple)) else [g] for g in (a if isinstance(a, (list, tuple)) else [a])]
    pieces = [p for g in groups for p in g]
    na = len(pieces)
    if mode == "tn":
        K, M = pieces[0].shape
        gwidths = [K]
    else:
        M = pieces[0].shape[0]
        gwidths = [sum(p.shape[1] for p in g) for g in groups]
        K = sum(gwidths)
    N = b.shape[0] if mode == "nt" else b.shape[1]
    tm, tn, tk = _tile(M, tm), _tile(N, tn), _tile(min(gwidths), tk)
    ni, nj, nk = M // tm, N // tn, K // tk
    dims = {"nn": NN, "nt": NT, "tn": TN}[mode]

    a_specs, ranges, k0 = [], [], 0
    for g, wd in zip(groups, gwidths):
        n = wd // tk
        if mode == "tn":
            a_specs.append(pl.BlockSpec((tk, tm), lambda i, j, k: (k, i)))
        elif len(g) == 1:
            a_specs.append(pl.BlockSpec((tm, tk), lambda i, j, k, k0=k0, n=n: (i, jnp.clip(k - k0, 0, n - 1))))
        else:
            a_specs += [pl.BlockSpec((tm, p.shape[1]), lambda i, j, k: (i, 0)) for p in g]
        ranges.append((k0, k0 + n))
        k0 += n
    if mode == "nt":
        b_spec = pl.BlockSpec((tn, tk), lambda i, j, k: (j, k))
    else:
        b_spec = pl.BlockSpec((tk, tn), lambda i, j, k: (k, j))
    n_in = len(side["inputs"]) if side else 0
    n_out = len(side["out_shapes"]) if side else 0
    n_buf = 1 if into else 0

    def body(*refs):
        a_refs, b_ref, rest = refs[:na], refs[na], refs[na + 1:]
        side_in, o_ref = rest[:n_in], rest[n_in + n_buf]
        side_out = rest[n_in + n_buf + 1:n_in + n_buf + 1 + n_out]
        scratch = rest[n_in + n_buf + 1 + n_out:]
        i, j, k = pl.program_id(0), pl.program_id(1), pl.program_id(2)
        if side:
            sems = scratch[-2:]

            @pl.when((i == 0) & (j == 0) & (k == 0))
            def _():
                side["start"](side_in, side_out, *sems)

        def accumulate(g_refs):
            p, off = None, 0
            for a_ref in g_refs:
                wd = a_ref.shape[0] if mode == "tn" else a_ref.shape[1]
                bv = b_ref[...] if len(g_refs) == 1 else (b_ref[:, off:off + wd] if mode == "nt" else b_ref[off:off + wd, :])
                q = lax.dot_general(a_ref[...], bv, (dims, ((), ())), preferred_element_type=F32)
                p = q if p is None else p + q
                off += wd
            if nk == 1:
                o_ref[...] = p.astype(o_ref.dtype)
            else:
                acc_ref = scratch[0]

                @pl.when(k == 0)
                def _():
                    acc_ref[...] = p

                @pl.when(k > 0)
                def _():
                    acc_ref[...] += p

                @pl.when(k == nk - 1)
                def _():
                    o_ref[...] = acc_ref[...].astype(o_ref.dtype)

        if len(groups) == 1:
            accumulate(a_refs)
        else:
            first = 0
            for (lo, hi), g in zip(ranges, groups):
                pl.when((k >= lo) & (k < hi))(functools.partial(accumulate, a_refs[first:first + len(g)]))
                first += len(g)

        if side:
            @pl.when((i == ni - 1) & (j == nj - 1) & (k == nk - 1))
            def _():
                side["finish"](side_in, side_out, *sems)

    any_spec = pl.BlockSpec(memory_space=pl.ANY)
    scratch_shapes = [pltpu.VMEM((tm, tn), F32)] if nk > 1 else []
    if side:
        scratch_shapes += [pltpu.SemaphoreType.DMA((side["n_sems"],)), pltpu.SemaphoreType.DMA((side["n_sems"],))]
    aliases = {na + 1 + s: 1 + d for s, d in side["aliases"].items()} if side else {}
    args = pieces + [b] + (list(side["inputs"]) if side else [])
    if into:
        buf, col = into
        aliases[len(args)] = 0
        args.append(buf)
        j0 = col // tn
        out_spec, out_shape = pl.BlockSpec((tm, tn), lambda i, j, k: (i, j0 + j)), jax.ShapeDtypeStruct(buf.shape, buf.dtype)
    else:
        out_spec, out_shape = pl.BlockSpec((tm, tn), lambda i, j, k: (i, j)), jax.ShapeDtypeStruct((M, N), out_dtype)
    res = pl.pallas_call(
        body, name=name, grid=(ni, nj, nk), in_specs=a_specs + [b_spec] + [any_spec] * (n_in + n_buf),
        out_specs=[out_spec] + [any_spec] * n_out,
        out_shape=[out_shape] + (list(side["out_shapes"]) if side else []),
        scratch_shapes=scratch_shapes, input_output_aliases=aliases,
        compiler_params=_params(("arbitrary",) * 3 if side else ("parallel", "parallel", "arbitrary")))(*args)
    return (res[0], list(res[1:])) if side else res[0]


def _merge(ya, yb, wpa, wpb, proj):
    S, KW = ya.shape
    D = 2 * KW
    tm, tn = _tile(S, 1024), _tile(D, 512)
    ga0, gb0 = 3 * D // tn, 4 * D // tn

    def body(ya_ref, yb_ref, wa_ref, wb_ref, ga_ref, gb_ref, pa_ref, pb_ref, m_ref):
        pa = jnp.dot(ya_ref[...], wa_ref[...], preferred_element_type=F32)
        pb = jnp.dot(yb_ref[...], wb_ref[...], preferred_element_type=F32)
        pa_ref[...] = pa.astype(pa_ref.dtype)
        pb_ref[...] = pb.astype(pb_ref.dtype)
        m_ref[...] = (_sig(ga_ref[...]) * pa + _sig(gb_ref[...]) * pb).astype(m_ref.dtype)

    y_spec = pl.BlockSpec((tm, KW), lambda i, j: (i, 0))
    w_spec = pl.BlockSpec((KW, tn), lambda i, j: (0, j))
    o_spec = pl.BlockSpec((tm, tn), lambda i, j: (i, j))
    return pl.pallas_call(
        body, name="merge", grid=(S // tm, D // tn),
        in_specs=[y_spec, y_spec, w_spec, w_spec, pl.BlockSpec((tm, tn), lambda i, j: (i, ga0 + j)),
                  pl.BlockSpec((tm, tn), lambda i, j: (i, gb0 + j))],
        out_specs=[o_spec, o_spec, o_spec],
        out_shape=[jax.ShapeDtypeStruct((S, D), MXU_DT), jax.ShapeDtypeStruct((S, D), MXU_DT), jax.ShapeDtypeStruct((S, D), MXU_DT)],
        compiler_params=_params(("parallel", "parallel")))(ya, yb, wpa, wpb, proj, proj)


def _merge_bwd(dout, w_out, proj, pa, pb):
    S, D = dout.shape
    tm, tn = _tile(S, 1024), _tile(D, 512)
    ga0, gb0 = 3 * D // tn, 4 * D // tn

    def body(do_ref, w_ref, ga_ref, gb_ref, pa_ref, pb_ref, dpa_ref, dpb_ref, dga_ref, dgb_ref):
        dm = lax.dot_general(do_ref[...], w_ref[...], (NT, ((), ())), preferred_element_type=F32)
        sa, sb = _sig(ga_ref[...]), _sig(gb_ref[...])
        sna, snb = 1.0 - sa, 1.0 - sb
        dpa = dm * sa
        dpb = dm * sb
        dpa_ref[...] = dpa.astype(dpa_ref.dtype)
        dpb_ref[...] = dpb.astype(dpb_ref.dtype)
        dga_ref[...] = (dpa * pa_ref[...].astype(F32) * sna).astype(dga_ref.dtype)
        dgb_ref[...] = (dpb * pb_ref[...].astype(F32) * snb).astype(dgb_ref.dtype)

    blk = pl.BlockSpec((tm, tn), lambda i, j: (i, j))
    os = jax.ShapeDtypeStruct((S, D), MXU_DT)
    return pl.pallas_call(
        body, name="merge_bwd", grid=(S // tm, D // tn),
        in_specs=[pl.BlockSpec((tm, D), lambda i, j: (i, 0)), pl.BlockSpec((tn, D), lambda i, j: (j, 0)),
                  pl.BlockSpec((tm, tn), lambda i, j: (i, ga0 + j)), pl.BlockSpec((tm, tn), lambda i, j: (i, gb0 + j)), blk, blk],
        out_specs=[blk, blk, blk, blk], out_shape=[os, os, os, os],
        compiler_params=_params(("parallel", "parallel")))(dout, w_out, proj, proj, pa, pb)


def _gates(qr, a, lbv):
    sq = _sig(qr)
    q = qr * sq
    s, sn = _sig_pair(a)
    omlb = 1.0 - lbv
    f = lbv + omlb * s
    logf = jnp.log(jnp.maximum(f, MIN_FORGET))
    kk = omlb * sn
    return sq, q, s, sn, omlb, f, logf, kk


def _shift_in_block(x, d):
    return pltpu.roll(x.reshape(CHUNK // SUBLANES, SUBLANES, HEAD), d, axis=1).reshape(CHUNK, HEAD)


def _level_masks(h):
    ti = lax.broadcasted_iota(jnp.int32, (CHUNK, 1), 0)
    si = lax.broadcasted_iota(jnp.int32, (1, CHUNK), 1)
    t2, s2 = (ti & (2 * h - 1)) >= h, (si & (2 * h - 1)) >= h
    same = (ti & -(2 * h)) == (si & -(2 * h))
    return t2, jnp.where(same & t2 & jnp.logical_not(s2), 1.0, 0.0)


def _level_factor(b, h):
    parts = [jnp.broadcast_to(b[g + h - 1:g + h], (2 * h, HEAD)) for g in range(0, CHUNK, 2 * h)]
    ref = parts[0] if len(parts) == 1 else jnp.concatenate(parts, axis=0)
    return jnp.exp(-jnp.abs(b - ref))


def _level_operands(q, kk, b, h, t2):
    fac = _level_factor(b, h)
    return jnp.where(t2, q * fac, 0.0), jnp.where(t2, 0.0, kk * fac), fac


def _heads_per_step(H, pref):
    return pref if H % pref == 0 else 1


def _tri(lower):
    r = lax.broadcasted_iota(jnp.int32, (CHUNK, CHUNK), 0)
    c = lax.broadcasted_iota(jnp.int32, (CHUNK, CHUNK), 1)
    return jnp.where((r >= c) if lower else (c >= r), 1.0, 0.0).astype(BF16)


def _hgrn_fwd(proj, lb, hn, KW):
    S = proj.shape[0]
    H = KW // HEAD
    T = _tile(S, 512)
    nci, nb = T // CHUNK, S // T
    HP = _heads_per_step(H, 8)

    def body(q_ref, f_ref, v_ref, z_ref, lb_ref, hn_ref, o_ref, y_ref, s0_ref, st_ref):
        @pl.when(pl.program_id(1) == 0)
        def _():
            st_ref[...] = jnp.zeros_like(st_ref)

        tril = _tri(True)
        rmod = lax.broadcasted_iota(jnp.int32, (CHUNK, 1), 0) & (SUB - 1)
        masks = [_level_masks(h) for h in LEVELS]
        lb_all = lb_ref[...]

        def chunk(ci, carry):
            rows = pl.ds(pl.multiple_of(ci * CHUNK, CHUNK), CHUNK)
            _, q_all, _, _, _, _, logf_all, kk_all = _gates(q_ref[rows, :], f_ref[rows, :], lb_all)
            b_all = _split_dot(tril, logf_all)
            for hp in range(HP):
                cols = slice(hp * HEAD, (hp + 1) * HEAD)
                one_head(ci, rows, hp, cols, q_all[:, cols], kk_all[:, cols], b_all[:, cols])
            return carry

        def one_head(ci, rows, hp, cols, q, kk, b):
            hnv = hn_ref[:, cols]
            vv = v_ref[rows, cols]
            eb = jnp.exp(b)
            st = st_ref[hp]
            s0_ref[ci, hp] = st
            p = None
            for h, (t2, m) in zip(LEVELS, masks):
                qs, ks, _ = _level_operands(q, kk, b, h, t2)
                pm = m * _dot(qs, ks, NT)
                p = pm if p is None else p + pm
            o = _dot(q * eb, st, NT) + _dot(p, vv, NN) + jnp.sum(q * kk, axis=-1, keepdims=True) * vv
            for d in range(1, SUB):
                kd_, bd_, vd_ = _shift_in_block(kk, d), _shift_in_block(b, d), _shift_in_block(vv, d)
                e = jnp.exp(jnp.minimum(b - bd_, 0.0))
                p = jnp.where(rmod >= d, jnp.sum(q * (kd_ * e), axis=-1, keepdims=True), 0.0)
                o = o + p * vd_
            bl = b[CHUNK - 1:CHUNK]
            st_ref[hp] = st * eb[CHUNK - 1:CHUNK] + _dot(vv, kk * jnp.exp(bl - b), TN)
            o_ref[rows, cols] = o
            z = z_ref[rows, cols]
            y_ref[rows, cols] = ((o * _rms(o)) * hnv * (z * _sig(z))).astype(y_ref.dtype)

        lax.fori_loop(0, nci, chunk, 0)

    W = HP * HEAD
    G = H // HP

    def col(off):
        return pl.BlockSpec((T, W), lambda h, t: (t, off + h))

    vec = pl.BlockSpec((1, W), lambda h, t: (0, h))
    return pl.pallas_call(
        body, name="hgrn_fwd", grid=(G, nb),
        in_specs=[col(0), col(G), col(2 * G), col(3 * G), vec, vec],
        out_specs=[pl.BlockSpec((T, W), lambda h, t: (t, h)), pl.BlockSpec((T, W), lambda h, t: (t, h)),
                   pl.BlockSpec((nci, HP, HEAD, HEAD), lambda h, t: (t, h, 0, 0))],
        out_shape=[jax.ShapeDtypeStruct((S, KW), F32), jax.ShapeDtypeStruct((S, KW), MXU_DT),
                   jax.ShapeDtypeStruct((S // CHUNK, H, HEAD, HEAD), F32)],
        scratch_shapes=[pltpu.VMEM((HP, HEAD, HEAD), F32)],
        compiler_params=_params(("parallel", "arbitrary")))(proj, proj, proj, proj, lb, hn)


def _gate_a_bwd(dya, o, proj, hn, KW):
    S = o.shape[0]
    H = KW // HEAD
    T = _tile(S, 512)

    def body(dy_ref, o_ref, z_ref, hn_ref, do_ref, dz_ref, dhn_ref):
        @pl.when(pl.program_id(1) == 0)
        def _():
            dhn_ref[...] = jnp.zeros_like(dhn_ref)

        ov = o_ref[...]
        r = _rms(ov)
        rn = ov * r
        z = z_ref[...]
        sz = _sig(z)
        silu = z * sz
        dy = dy_ref[...]
        hnv = hn_ref[...]
        dz_ref[...] = (dy * rn * hnv * (sz * (1.0 + z * (1.0 - sz)))).astype(dz_ref.dtype)
        dhn_ref[...] += jnp.sum(dy * rn * silu, axis=0, keepdims=True)
        drn = dy * hnv * silu
        do_ref[...] = r * (drn - rn * jnp.mean(drn * rn, axis=-1, keepdims=True))

    blk = pl.BlockSpec((T, HEAD), lambda h, t: (t, h))
    vec = pl.BlockSpec((1, HEAD), lambda h, t: (0, h))
    return pl.pallas_call(
        body, name="gate_a_bwd", grid=(H, S // T),
        in_specs=[blk, blk, pl.BlockSpec((T, HEAD), lambda h, t: (t, 3 * H + h)), vec],
        out_specs=[blk, blk, vec],
        out_shape=[jax.ShapeDtypeStruct((S, KW), F32), jax.ShapeDtypeStruct((S, KW), MXU_DT), jax.ShapeDtypeStruct((1, KW), F32)],
        compiler_params=_params(("parallel", "arbitrary")))(dya, o, proj, hn)


def _hgrn_bwd(proj, do, s0, lb, KW):
    S = proj.shape[0]
    H = KW // HEAD
    T = _tile(S, 512)
    nci, nb = T // CHUNK, S // T
    HP = _heads_per_step(H, 8)

    def body(q_ref, f_ref, v_ref, do_ref, s0_ref, lb_ref, dq_ref, df_ref, dv_ref, dlb_ref, dst_ref):
        @pl.when(pl.program_id(1) == 0)
        def _():
            dst_ref[...] = jnp.zeros_like(dst_ref)
            dlb_ref[...] = jnp.zeros_like(dlb_ref)

        tril, triu = _tri(True), _tri(False)
        rmod = lax.broadcasted_iota(jnp.int32, (CHUNK, 1), 0) & (SUB - 1)
        masks = [_level_masks(h) for h in LEVELS]
        lb_all = lb_ref[...]

        def chunk(it, carry):
            ci = nci - 1 - it
            rows = pl.ds(pl.multiple_of(ci * CHUNK, CHUNK), CHUNK)
            qr_all = q_ref[rows, :]
            sq_all, q_all, s_all, sn_all, omlb, f_all, logf_all, kk_all = _gates(qr_all, f_ref[rows, :], lb_all)
            b_all = _split_dot(tril, logf_all)
            res = []
            for hp in range(HP):
                cols = slice(hp * HEAD, (hp + 1) * HEAD)
                res.append(one_head(ci, rows, hp, cols, q_all[:, cols], kk_all[:, cols], b_all[:, cols]))
            dq_t, dk_t, dv_t = (jnp.concatenate([r[i] for r in res], axis=1) for i in range(3))
            w = jnp.concatenate([r[3] for r in res], axis=1)
            dlogf = _split_dot(triu, q_all * dq_t - kk_all * dk_t) + w
            dlf = jnp.where(f_all > MIN_FORGET, dlogf / jnp.maximum(f_all, MIN_FORGET), 0.0)
            t1 = dlf - dk_t
            dlb_ref[...] += jnp.sum(sn_all * t1, axis=0, keepdims=True)
            df_ref[rows, :] = (omlb * (s_all * sn_all) * t1).astype(df_ref.dtype)
            dq_ref[rows, :] = (dq_t * (sq_all * (1.0 + qr_all * (1.0 - sq_all)))).astype(dq_ref.dtype)
            dv_ref[rows, :] = dv_t.astype(dv_ref.dtype)
            return carry

        def one_head(ci, rows, hp, cols, q, kk, b):
            vv, dov = v_ref[rows, cols], do_ref[rows, cols]
            eb = jnp.exp(b)
            st0 = s0_ref[ci, hp]
            dst = dst_ref[hp]
            bl, ec = b[CHUNK - 1:CHUNK], eb[CHUNK - 1:CHUNK]
            decl = jnp.exp(bl - b)
            kdec = kk * decl
            dq_t = eb * _dot(dov, st0, NN)
            dst_ref[hp] = dst * ec + _dot(dov, q * eb, TN)
            dv_t = _dot(kdec, dst, NT)
            dk_t = decl * _dot(vv, dst, NN)
            stc = st0 * ec + _dot(vv, kdec, TN)
            w = jnp.sum(stc * dst, axis=0, keepdims=True)
            dp_all = _dot(dov, vv, NT)
            p = None
            for h, (t2, m) in zip(LEVELS, masks):
                qs, ks, fac = _level_operands(q, kk, b, h, t2)
                pm = m * _dot(qs, ks, NT)
                p = pm if p is None else p + pm
                dpm = m * dp_all
                dq_t = dq_t + fac * _dot(dpm, ks, NN)
                dk_t = dk_t + fac * _dot(dpm, qs, TN)
            p0 = jnp.sum(q * kk, axis=-1, keepdims=True)
            dp0 = jnp.sum(dov * vv, axis=-1, keepdims=True)
            dq_t = dq_t + dp0 * kk
            dk_t = dk_t + dp0 * q
            dv_t = dv_t + _dot(p, dov, TN) + p0 * dov
            for d in range(1, SUB):
                kd_, bd_, vd_ = _shift_in_block(kk, d), _shift_in_block(b, d), _shift_in_block(vv, d)
                e = jnp.exp(jnp.minimum(b - bd_, 0.0))
                ke = kd_ * e
                m = rmod >= d
                p = jnp.where(m, jnp.sum(q * ke, axis=-1, keepdims=True), 0.0)
                dp = jnp.where(m, jnp.sum(dov * vd_, axis=-1, keepdims=True), 0.0)
                dq_t = dq_t + dp * ke
                dk_t = dk_t + _shift_in_block(dp * (q * e), SUBLANES - d)
                dv_t = dv_t + _shift_in_block(p * dov, SUBLANES - d)
            return dq_t, dk_t, dv_t, w

        lax.fori_loop(0, nci, chunk, 0)

    W = HP * HEAD
    G = H // HP

    def col(off):
        return pl.BlockSpec((T, W), lambda h, t: (nb - 1 - t, off + h))

    blk = pl.BlockSpec((T, W), lambda h, t: (nb - 1 - t, h))
    vec = pl.BlockSpec((1, W), lambda h, t: (0, h))
    os = jax.ShapeDtypeStruct((S, KW), MXU_DT)
    return pl.pallas_call(
        body, name="hgrn_bwd", grid=(G, nb),
        in_specs=[col(0), col(G), col(2 * G), blk, pl.BlockSpec((nci, HP, HEAD, HEAD), lambda h, t: (nb - 1 - t, h, 0, 0)), vec],
        out_specs=[blk, blk, blk, vec], out_shape=[os, os, os, jax.ShapeDtypeStruct((1, KW), F32)],
        scratch_shapes=[pltpu.VMEM((HP, HEAD, HEAD), F32)],
        compiler_params=_params(("parallel", "arbitrary")))(proj, proj, proj, do, s0, lb)


def _pool_pos(i, T):
    return (i * T + lax.broadcasted_iota(jnp.int32, (T, 1), 0) + 1).astype(F32)


def _pool_fwd(proj, pool_w, pool_scale, KW):
    S = proj.shape[0]
    GW = KW // len(WINDOWS)
    T = _tile(S, 256)

    def body(v_ref, halo_ref, z_ref, pw_ref, ps_ref, p_ref, y_ref):
        i = pl.program_id(0)
        vb = v_ref[...]
        ext = jnp.concatenate([jnp.where(i > 0, halo_ref[...], 0.0), vb], axis=0)
        pos = _pool_pos(i, T)
        z = z_ref[...]
        gate = ps_ref[...] * (z * _sig(z))
        for g, w in enumerate(WINDOWS):
            sl = slice(g * GW, (g + 1) * GW)
            sg = ext[:, sl]
            for jj in range(g + 1):
                sg = sg + pltpu.roll(sg, 1 << jj, axis=0)
            pooled = (sg[HALO:] / jnp.minimum(pos, float(w)) - vb[:, sl]).astype(p_ref.dtype)
            p_ref[:, sl] = pooled
            y_ref[:, sl] = (_dot(pooled, pw_ref[g], NN) * gate[:, sl]).astype(y_ref.dtype)

    row = pl.BlockSpec((T, KW), lambda i: (i, 0))
    hb = T // HALO
    return pl.pallas_call(
        body, name="pool_fwd", grid=(S // T,),
        in_specs=[pl.BlockSpec((T, KW), lambda i: (i, 4)), pl.BlockSpec((HALO, KW), lambda i: (jnp.maximum(i * hb - 1, 0), 4)),
                  pl.BlockSpec((T, KW), lambda i: (i, 5)), pl.BlockSpec((len(WINDOWS), GW, GW), lambda i: (0, 0, 0)),
                  pl.BlockSpec((1, KW), lambda i: (0, 0))],
        out_specs=[row, row], out_shape=[jax.ShapeDtypeStruct((S, KW), MXU_DT), jax.ShapeDtypeStruct((S, KW), MXU_DT)],
        compiler_params=_params(("parallel",)))(proj, proj, proj, pool_w, pool_scale)


def _pool_bwd(dyb, proj, pooled, pool_w, pool_scale, KW):
    S = proj.shape[0]
    G = len(WINDOWS)
    GW = KW // G
    T = _tile(S, 256)

    def body(dy_ref, z_ref, p_ref, pw_ref, ps_ref, u_ref, dz_ref, dps_ref, dpw_ref):
        i = pl.program_id(0)

        @pl.when(i == 0)
        def _():
            dps_ref[...] = jnp.zeros_like(dps_ref)
            dpw_ref[...] = jnp.zeros_like(dpw_ref)

        pos = _pool_pos(i, T)
        z = z_ref[...]
        sz = _sig(z)
        silu = z * sz
        dsilu = sz * (1.0 + z * (1.0 - sz))
        dy = dy_ref[...]
        ps = ps_ref[...]
        for g, w in enumerate(WINDOWS):
            sl = slice(g * GW, (g + 1) * GW)
            pg = p_ref[:, sl]
            mixed = _dot(pg, pw_ref[g], NN)
            dyg = dy[:, sl]
            dz_ref[:, sl] = (dyg * mixed * ps[:, sl] * dsilu[:, sl]).astype(dz_ref.dtype)
            dps_ref[:, sl] += jnp.sum(dyg * mixed * silu[:, sl], axis=0, keepdims=True)
            dmix = dyg * ps[:, sl] * silu[:, sl]
            dpw_ref[g] += _dot(pg, dmix, TN)
            u_ref[:, sl] = _dot(dmix, pw_ref[g], NT) / jnp.minimum(pos, float(w))

    row = pl.BlockSpec((T, KW), lambda i: (i, 0))
    return pl.pallas_call(
        body, name="pool_bwd", grid=(S // T,),
        in_specs=[row, pl.BlockSpec((T, KW), lambda i: (i, 5)), row,
                  pl.BlockSpec((G, GW, GW), lambda i: (0, 0, 0)), pl.BlockSpec((1, KW), lambda i: (0, 0))],
        out_specs=[row, row, pl.BlockSpec((1, KW), lambda i: (0, 0)), pl.BlockSpec((G, GW, GW), lambda i: (0, 0, 0))],
        out_shape=[jax.ShapeDtypeStruct((S, KW), F32), jax.ShapeDtypeStruct((S, KW), MXU_DT),
                   jax.ShapeDtypeStruct((1, KW), F32), jax.ShapeDtypeStruct((G, GW, GW), F32)],
        compiler_params=_params(("arbitrary",)))(dyb, proj, pooled, pool_w, pool_scale)


def _pool_bwd_window(u, KW):
    S = u.shape[0]
    GW = KW // len(WINDOWS)
    T = _tile(S, 256)
    nb = S // T
    n = T + HALO

    def body(u_ref, halo_ref, dv_ref):
        i = pl.program_id(0)
        uv = u_ref[...]
        ext = jnp.concatenate([uv, jnp.where(i < nb - 1, halo_ref[...], 0.0)], axis=0)
        pos = _pool_pos(i, T)
        for g, w in enumerate(WINDOWS):
            sl = slice(g * GW, (g + 1) * GW)
            sg = ext[:, sl]
            for jj in range(g + 1):
                sg = sg + pltpu.roll(sg, n - (1 << jj), axis=0)
            dv_ref[:, sl] = (sg[:T] - uv[:, sl] * jnp.minimum(pos, float(w))).astype(dv_ref.dtype)

    hb = T // HALO
    return pl.pallas_call(
        body, name="pool_bwd_window", grid=(nb,),
        in_specs=[pl.BlockSpec((T, KW), lambda i: (i, 0)),
                  pl.BlockSpec((HALO, KW), lambda i: (jnp.minimum((i + 1) * hb, S // HALO - 1), 0))],
        out_specs=pl.BlockSpec((T, KW), lambda i: (i, 0)), out_shape=jax.ShapeDtypeStruct((S, KW), MXU_DT),
        compiler_params=_params(("parallel",)))(u, u)


def _softmax_rows(x):
    m = jnp.max(x, axis=0, keepdims=True)
    e = jnp.exp(x - m)
    return e / jnp.sum(e, axis=0, keepdims=True)


def _lb_table(lower_bounds):
    L, KW = lower_bounds.shape

    def body(x_ref, o_ref):
        sm = _softmax_rows(x_ref[...])
        acc = jnp.zeros((1, KW), F32)
        o_ref[0:1, :] = acc
        for l in range(1, L):
            acc = acc + sm[l:l + 1]
            o_ref[l:l + 1, :] = acc

    return pl.pallas_call(body, name="lb_table", out_shape=jax.ShapeDtypeStruct((L, KW), F32))(lower_bounds)


def _mod_part(c_all, w_ada):
    L, D, C = w_ada.shape
    B = c_all.shape[0]
    tn = _tile(C, 512)

    def body(c_ref, w_ref, o_ref):
        cv = c_ref[...]
        o_ref[...] = _dot(cv * _sig(cv), w_ref[...], NN)

    return pl.pallas_call(
        body, name="mod_part", grid=(L, C // tn),
        in_specs=[pl.BlockSpec((B, D), lambda l, j: (0, 0)), pl.BlockSpec((None, D, tn), lambda l, j: (l, 0, j))],
        out_specs=pl.BlockSpec((None, B, tn), lambda l, j: (l, 0, j)), out_shape=jax.ShapeDtypeStruct((L, B, C), F32),
        compiler_params=_params(("parallel", "parallel")))(c_all, w_ada)


def _adamw_math(w, g, m, v):
    m = ADAM_B1 * m + (1.0 - ADAM_B1) * g
    v = ADAM_B2 * v + (1.0 - ADAM_B2) * (g * g)
    m_hat = m / (1.0 - ADAM_B1 ** ADAM_STEP)
    v_hat = v / (1.0 - ADAM_B2 ** ADAM_STEP)
    delta = -ADAM_LR * (m_hat / (jnp.sqrt(v_hat) + ADAM_EPS) + ADAM_WD * w)
    return delta, m, v


def _adamw(w, g, m, v, name):
    shape = w.shape
    C = shape[-1]
    R = w.size // C
    tr = _tile(R, 128)
    flat = lambda t: t.reshape(R, C)

    def body(w_ref, g_ref, m_ref, v_ref, d_ref, nm_ref, nv_ref):
        d, nm, nv = _adamw_math(w_ref[...], g_ref[...], m_ref[...], v_ref[...])
        d_ref[...] = d
        nm_ref[...] = nm
        nv_ref[...] = nv

    blk = pl.BlockSpec((tr, C), lambda i: (i, 0))
    os = jax.ShapeDtypeStruct((R, C), F32)
    outs = pl.pallas_call(body, name=name, grid=(R // tr,), in_specs=[blk] * 4, out_specs=[blk] * 3, out_shape=[os] * 3,
                          compiler_params=_params(("parallel",)))(flat(w), flat(g), flat(m), flat(v))
    return [t.reshape(shape) for t in outs]


def _w_ada_update(c_all, dmod, w, m, v):
    L, D, C = w.shape
    B = c_all.shape[0]
    tn = _tile(C, 256)

    def body(c_ref, dm_ref, w_ref, m_ref, v_ref, g_ref, d_ref, nm_ref, nv_ref):
        cv = c_ref[...]
        g = lax.dot_general(cv * _sig(cv), dm_ref[...], (TN, ((), ())), preferred_element_type=F32,
                            precision=lax.Precision.HIGHEST)
        d, nm, nv = _adamw_math(w_ref[...], g, m_ref[...], v_ref[...])
        g_ref[...] = g
        d_ref[...] = d
        nm_ref[...] = nm
        nv_ref[...] = nv

    blk = pl.BlockSpec((None, D, tn), lambda l, j: (l, 0, j))
    os = jax.ShapeDtypeStruct((L, D, C), F32)
    return pl.pallas_call(
        body, name="w_ada_update", grid=(L, C // tn),
        in_specs=[pl.BlockSpec((B, D), lambda l, j: (0, 0)), pl.BlockSpec((None, B, tn), lambda l, j: (l, 0, j)), blk, blk, blk],
        out_specs=[blk] * 4, out_shape=[os] * 4, compiler_params=_params(("parallel", "parallel")))(c_all, dmod, w, m, v)


def _small_update(parts, lower_bounds, wmv, D, KW):
    L = parts.shape[1]
    widths = [3 * D, D, D, KW, KW, KW]
    offs = [sum(widths[:i]) for i in range(len(widths))]

    def body(p_ref, *refs):
        ins, outs = refs[:18], refs[18:]
        tot = p_ref[0]
        for dev in range(1, N_DEV):
            tot = tot + p_ref[dev]
        grads = [tot[:, o:o + wd] for o, wd in zip(offs, widths)]
        sm = _softmax_rows(ins[9][...])
        dlb = grads[3]
        dsm = [jnp.zeros((1, KW), F32)]
        for j in range(1, L):
            acc = dlb[j:j + 1]
            for l in range(j + 1, L):
                acc = acc + dlb[l:l + 1]
            dsm.append(acc)
        dsm = jnp.concatenate(dsm, axis=0)
        grads[3] = sm * (dsm - jnp.sum(sm * dsm, axis=0, keepdims=True))
        for p in range(6):
            w_ref, m_ref, v_ref = ins[3 * p:3 * p + 3]
            d, nm, nv = _adamw_math(w_ref[...], grads[p], m_ref[...], v_ref[...])
            outs[4 * p][...] = grads[p]
            outs[4 * p + 1][...] = d
            outs[4 * p + 2][...] = nm
            outs[4 * p + 3][...] = nv

    flat = [t for trip in wmv for t in trip]
    out_shape = []
    for wd in widths:
        out_shape += [jax.ShapeDtypeStruct((L, wd), F32)] * 4
    res = pl.pallas_call(body, name="small_update", out_shape=out_shape,
                         compiler_params=pltpu.CompilerParams(vmem_limit_bytes=VMEM_LIMIT_BYTES))(parts, *flat)
    return [res[4 * p:4 * p + 4] for p in range(6)]


def _place():
    x, y, c = lax.axis_index("x"), lax.axis_index("y"), lax.axis_index("c")
    chips = [(1 - x, y), (x, 1 - y), (1 - x, 1 - y)]
    return x, y, c, chips


def _all_gather_rows(blk):
    m_per, n = blk.shape

    def body(x_ref, out_ref, send_sems, recv_sems, local_sem):
        x, y, c, chips = _place()
        me, sibling = (x, y, c), (x, y, 1 - c)

        def rows(px, py, pc):
            return out_ref.at[pl.ds((4 * px + 2 * py + pc) * m_per, m_per), :]

        def copy(k, block, to, src=None):
            return pltpu.make_async_remote_copy(src_ref=rows(*block) if src is None else src, dst_ref=rows(*block),
                                                send_sem=send_sems.at[k], recv_sem=recv_sems.at[k], device_id=to, device_id_type=MESH)

        mine = pltpu.make_async_copy(x_ref, rows(*me), local_sem)
        mine.start()
        first = [copy(0, me, sibling, src=x_ref)]
        first += [copy(1 + j, me, (*chip, c), src=x_ref) for j, chip in enumerate(chips)]
        for cp in first:
            cp.start()
        passed = [copy(4 + j, (*chip, c), sibling) for j, chip in enumerate(chips)]
        for j, chip in enumerate(chips):
            copy(1 + j, (*chip, c), me).wait_recv()
            passed[j].start()
        copy(0, sibling, me).wait_recv()
        for j, chip in enumerate(chips):
            copy(4 + j, (*chip, 1 - c), me).wait_recv()
        for cp in first + passed:
            cp.wait_send()
        mine.wait()

    return pl.pallas_call(
        body, name="all_gather_rows", out_shape=jax.ShapeDtypeStruct((N_DEV * m_per, n), blk.dtype),
        in_specs=[pl.BlockSpec(memory_space=pltpu.VMEM)], out_specs=pl.BlockSpec(memory_space=pltpu.VMEM),
        scratch_shapes=[pltpu.SemaphoreType.DMA((7,)), pltpu.SemaphoreType.DMA((7,)), pltpu.SemaphoreType.DMA],
        compiler_params=pltpu.CompilerParams(vmem_limit_bytes=VMEM_LIMIT_BYTES))(blk)


SHARD_AXIS = (1, 1, 1, 1, 0)
HALF_AXIS = (0, 0, 0, 0, 1)
N_STACKS = 5


def _cut(ref, axis, part, n):
    idx = [slice(None)] * len(ref.shape)
    idx[axis] = pl.ds(pl.multiple_of(part * n, n), n)
    return ref.at[tuple(idx)]


def _quarter(ref, t, chip):
    return _cut(ref, SHARD_AXIS[t], chip, ref.shape[SHARD_AXIS[t]] // N_CHIPS)


def _half(ref, t, core):
    return _cut(ref, HALF_AXIS[t], core, ref.shape[HALF_AXIS[t]] // 2)


def _with_axis(shape, axis, n):
    return tuple(n if a == axis else s for a, s in enumerate(shape))


def _view3(t):
    return t.reshape((1,) * (3 - t.ndim) + t.shape)


def _axis3(t, axis):
    return axis + 3 - t.ndim


def _blocked(shape3, blk, offsets=None):
    offsets = offsets or {}

    def index(b, r, c, *pre):
        idx = [b, r, c]
        for ax, fn in offsets.items():
            idx[ax] = idx[ax] + fn(*pre)
        return tuple(idx)

    return pl.BlockSpec((None,) + tuple(blk), index)


def _cast_into_full(stack, l, t, chip_idx):
    shard = stack.shape[1:]
    nd = len(shard)
    w4 = stack.reshape((stack.shape[0],) + (1,) * (3 - nd) + shard)
    ax = SHARD_AXIS[t] + 3 - nd
    _, B, R, C = w4.shape
    tr = _tile(R, 256)
    per = (R // tr, 1)[ax - 1]

    def body(j_ref, w_ref, o_ref):
        o_ref[...] = w_ref[...].astype(o_ref.dtype)

    full3 = _with_axis((B, R, C), ax, N_CHIPS * (B, R, C)[ax])
    grid_spec = pltpu.PrefetchScalarGridSpec(
        num_scalar_prefetch=1, grid=(B, R // tr, 1),
        in_specs=[pl.BlockSpec((None, None, tr, C), lambda b, r, c, j_ref: (l, b, r, c))],
        out_specs=_blocked(full3, (tr, C), {ax: lambda j_ref: j_ref[0] * per}))
    o = pl.pallas_call(body, name=f"cast_into_full{t}", grid_spec=grid_spec, out_shape=jax.ShapeDtypeStruct(full3, MXU_DT),
                       compiler_params=_params(("parallel",) * 3))(chip_idx, w4)
    return o.reshape(_with_axis(shard, SHARD_AXIS[t], N_CHIPS * shard[SHARD_AXIS[t]]))


def _remote(src, dst, send_sems, recv_sems, k, to):
    return pltpu.make_async_remote_copy(src_ref=src, dst_ref=dst, send_sem=send_sems.at[k], recv_sem=recv_sems.at[k],
                                        device_id=to, device_id_type=MESH)


def _side(inputs, out_shapes, aliases, n_sems, copies):
    def start(ins, outs, ss, rs):
        for snd, _ in copies(ins, outs, ss, rs):
            snd.start()

    def finish(ins, outs, ss, rs):
        for snd, rcv in copies(ins, outs, ss, rs):
            rcv.wait_recv()
            snd.wait_send()

    return dict(inputs=list(inputs), out_shapes=list(out_shapes), aliases=aliases, n_sems=n_sems, start=start, finish=finish)


def _run_side(side, name):
    n_in, n_out = len(side["inputs"]), len(side["out_shapes"])

    def body(*refs):
        ins, outs, sems = refs[:n_in], refs[n_in:n_in + n_out], refs[n_in + n_out:]
        side["start"](ins, outs, *sems)
        side["finish"](ins, outs, *sems)

    any_spec = pl.BlockSpec(memory_space=pl.ANY)
    return list(pl.pallas_call(
        body, name=name, out_shape=side["out_shapes"], in_specs=[any_spec] * n_in, out_specs=[any_spec] * n_out,
        input_output_aliases=side["aliases"],
        scratch_shapes=[pltpu.SemaphoreType.DMA((side["n_sems"],)), pltpu.SemaphoreType.DMA((side["n_sems"],))])(*side["inputs"]))


def _same(arrays):
    return [jax.ShapeDtypeStruct(a.shape, a.dtype) for a in arrays]


def _gather_ici_side(ws):
    def copies(ins, outs, ss, rs):
        x, y, c, chips = _place()
        out = []
        for k, (cx, cy) in enumerate(chips):
            for t in range(N_STACKS):
                mine = _half(_quarter(outs[t], t, 2 * x + y), t, c)
                theirs = _half(_quarter(outs[t], t, 2 * cx + cy), t, c)
                n = k * N_STACKS + t
                out.append((_remote(mine, mine, ss, rs, n, (cx, cy, c)), _remote(theirs, theirs, ss, rs, n, (cx, cy, c))))
        return out

    return _side(ws, _same(ws), {t: t for t in range(N_STACKS)}, 3 * N_STACKS, copies)


def _gather_d2d_side(ws):
    def copies(ins, outs, ss, rs):
        x, y, c, chips = _place()
        out = []
        for k, (cx, cy) in enumerate(chips):
            for t in range(N_STACKS):
                q = _quarter(outs[t], t, 2 * cx + cy)
                mine, theirs = _half(q, t, c), _half(q, t, 1 - c)
                n = k * N_STACKS + t
                out.append((_remote(mine, mine, ss, rs, n, (x, y, 1 - c)), _remote(theirs, theirs, ss, rs, n, (x, y, 1 - c))))
        return out

    return _side(ws, _same(ws), {t: t for t in range(N_STACKS)}, 3 * N_STACKS, copies)


def _swap_side(gs):
    shapes = [jax.ShapeDtypeStruct(_with_axis(g.shape, HALF_AXIS[t], g.shape[HALF_AXIS[t]] // 2), g.dtype) for t, g in enumerate(gs)]

    def copies(ins, outs, ss, rs):
        x, y, c, _ = _place()
        cps = [_remote(_half(ins[t], t, 1 - c), outs[t], ss, rs, t, (x, y, 1 - c)) for t in range(N_STACKS)]
        return [(cp, cp) for cp in cps]

    return _side(gs, shapes, {}, N_STACKS, copies)


def _pair_sum(g, recv, t, c_idx):
    g3, r3 = _view3(g), _view3(recv)
    hf_ax = _axis3(g, HALF_AXIS[t])
    B, R, C = r3.shape
    tr, tc = _tile(R, 256), _tile(C, 2048)
    grid = (B, R // tr, C // tc)

    def body(c_ref, g_ref, r_ref, o_ref):
        o_ref[...] = (g_ref[...] + r_ref[...]).astype(o_ref.dtype)

    grid_spec = pltpu.PrefetchScalarGridSpec(
        num_scalar_prefetch=1, grid=grid,
        in_specs=[_blocked(g3.shape, (tr, tc), {hf_ax: lambda c_ref: c_ref[0] * grid[hf_ax]}), _blocked(r3.shape, (tr, tc))],
        out_specs=_blocked(r3.shape, (tr, tc)))
    out = pl.pallas_call(body, name=f"pair_sum{t}", grid_spec=grid_spec, out_shape=jax.ShapeDtypeStruct(r3.shape, BF16),
                         compiler_params=_params(("parallel",) * 3))(c_idx, g3, r3)
    return out.reshape(recv.shape)


def _scatter_side(ps):
    shapes = [jax.ShapeDtypeStruct((N_CHIPS - 1,) + _with_axis(p.shape, SHARD_AXIS[t], p.shape[SHARD_AXIS[t]] // N_CHIPS), p.dtype)
              for t, p in enumerate(ps)]

    def copies(ins, outs, ss, rs):
        x, y, c, chips = _place()
        cps = [_remote(_quarter(ins[t], t, 2 * cx + cy), outs[t].at[k], ss, rs, k * N_STACKS + t, (cx, cy, c))
               for k, (cx, cy) in enumerate(chips) for t in range(N_STACKS)]
        return [(cp, cp) for cp in cps]

    return _side(ps, shapes, {}, 3 * N_STACKS, copies)


def _sum_chips(p, slots, t, chip_idx, c_idx):
    p3 = _view3(p)
    sh_ax, hf_ax = _axis3(p, SHARD_AXIS[t]), _axis3(p, HALF_AXIS[t])
    piece3 = _with_axis(p3.shape, sh_ax, p3.shape[sh_ax] // N_CHIPS)
    s4 = slots.reshape((N_CHIPS - 1,) + piece3)
    B, R, C = piece3
    tr = _tile(R, 256)
    grid = (B, R // tr, 1)
    full3 = _with_axis(piece3, hf_ax, 2 * piece3[hf_ax])

    def body(j_ref, c_ref, own_ref, s_ref, o_ref):
        acc = own_ref[...].astype(F32)
        for k in range(N_CHIPS - 1):
            acc = acc + s_ref[k].astype(F32)
        o_ref[...] = acc

    grid_spec = pltpu.PrefetchScalarGridSpec(
        num_scalar_prefetch=2, grid=grid,
        in_specs=[_blocked(p3.shape, (tr, C), {sh_ax: lambda j_ref, c_ref: j_ref[0] * grid[sh_ax]}),
                  pl.BlockSpec((N_CHIPS - 1, None, tr, C), lambda b, r, cc, j_ref, c_ref: (0, b, r, cc))],
        out_specs=_blocked(full3, (tr, C), {hf_ax: lambda j_ref, c_ref: c_ref[0] * grid[hf_ax]}))
    out = pl.pallas_call(body, name=f"sum_chips{t}", grid_spec=grid_spec, out_shape=jax.ShapeDtypeStruct(full3, F32),
                         compiler_params=_params(("parallel",) * 3))(chip_idx, c_idx, p3, s4)
    return out.reshape(_with_axis(slots.shape[1:], HALF_AXIS[t], 2 * slots.shape[1 + HALF_AXIS[t]]))


def _join_side(rs_):
    def copies(ins, outs, ss, rs):
        x, y, c, _ = _place()
        return [(_remote(_half(outs[t], t, c), _half(outs[t], t, c), ss, rs, t, (x, y, 1 - c)),
                 _remote(_half(outs[t], t, 1 - c), _half(outs[t], t, 1 - c), ss, rs, t, (x, y, 1 - c))) for t in range(N_STACKS)]

    return _side(rs_, _same(rs_), {t: t for t in range(N_STACKS)}, N_STACKS, copies)


def _layer_weights(shards, l, chip_idx):
    return [_cast_into_full(s, l, t, chip_idx) for t, s in enumerate(shards)]


def _gather_alone(w):
    return _run_side(_gather_d2d_side(_run_side(_gather_ici_side(w), "gather_ici")), "gather_d2d")


def _forward_layer(x, l, w, small, mods, lb_all, KW, nxt):
    norm_pre, norm_post, hgrn_norm, pool_scale = small
    shift, scale, gate = mods
    row = lambda t: t[l:l + 1]
    h = _prenorm(x, row(norm_pre), row(scale), row(shift))
    if nxt is None:
        proj = _matmul(h, w[0], "nn", F32, "proj")
    else:
        proj, nxt = _matmul(h, w[0], "nn", F32, "proj_gather", side=_gather_ici_side(nxt))
    o_a, y_a, s0 = _hgrn_fwd(proj, row(lb_all), row(hgrn_norm), KW)
    pooled, y_b = _pool_fwd(proj, w[1], row(pool_scale), KW)
    pa, pb, merged = _merge(y_a, y_b, w[2], w[3], proj)
    if nxt is None:
        out = _matmul(merged, w[4], "nn", F32, "out_proj")
    else:
        out, nxt = _matmul(merged, w[4], "nn", F32, "out_proj_gather", side=_gather_d2d_side(nxt))
    x_new = _postnorm(x, out, row(gate), row(norm_post))
    saved = (x, h, proj, o_a, y_a, s0, pooled, y_b, pa, pb, merged, out)
    return x_new, saved, nxt


def _backward_layer(dx, l, saved, w, small, mods, lb_all, KW, above, idx):
    norm_pre, norm_post, hgrn_norm, pool_scale = small
    shift, scale, gate = mods
    chip_idx, c_idx = idx
    x, h, proj, o_a, y_a, s0, pooled, y_b, pa, pb, merged, out = saved
    row = lambda t: t[l:l + 1]
    dout, d_gate, d_npost = _postnorm_bwd(dx, out, row(gate), row(norm_post))
    if above is None:
        g_out = _matmul(merged, dout, "tn", F32, "grad_w_out")
    else:
        g_out, recv = _matmul(merged, dout, "tn", F32, "grad_w_out_swap", side=_swap_side(above))
        pair = [_pair_sum(g, r, t, c_idx) for t, (g, r) in enumerate(zip(above, recv))]
    dpa, dpb, dga, dgb = _merge_bwd(dout, w[4], proj, pa, pb)
    g_pa = _matmul(y_a, dpa, "tn", F32, "grad_w_proj_a")
    g_pb = _matmul(y_b, dpb, "tn", F32, "grad_w_proj_b")
    dya = _matmul(dpa, w[2], "nt", F32, "d_y_a")
    dyb = _matmul(dpb, w[3], "nt", F32, "d_y_b")
    do, dza, d_hn = _gate_a_bwd(dya, o_a, proj, row(hgrn_norm), KW)
    dq, df, dva, d_lb = _hgrn_bwd(proj, do, s0, row(lb_all), KW)
    u, dzb, d_ps, g_pool = _pool_bwd(dyb, proj, pooled, w[1], row(pool_scale), KW)
    dvb = _pool_bwd_window(u, KW)
    dproj = [dq, df, dva, dza, dvb, dzb, dga, dgb]
    g_in = lax.empty(w[0].shape, F32)
    col = 0
    dproj_k = [[dq, df], [dva, dza], [dvb, dzb], dga, dgb]
    if above is None:
        dh = _matmul(dproj_k, w[0], "nt", F32, "d_h", tm=512)
        reduced = None
        for piece in dproj:
            g_in = _matmul(h, piece, "tn", F32, "grad_w_in_piece", into=(g_in, col))
            col += piece.shape[1]
    else:
        dh, slots = _matmul(dproj_k, w[0], "nt", F32, "d_h_scatter", tm=512, side=_scatter_side(pair))
        part = [_sum_chips(p, s, t, chip_idx, c_idx) for t, (p, s) in enumerate(zip(pair, slots))]
        for n, piece in enumerate(dproj):
            if n == 0:
                g_in, reduced = _matmul(h, piece, "tn", F32, "grad_w_in_piece_join", into=(g_in, col), side=_join_side(part))
            else:
                g_in = _matmul(h, piece, "tn", F32, "grad_w_in_piece", into=(g_in, col))
            col += piece.shape[1]
    dx_new, d_shift, d_scale, d_npre = _prenorm_bwd(dh, x, dx, row(norm_pre), row(scale))
    small_g = jnp.concatenate([d_shift, d_scale, d_gate, d_npre, d_npost, d_lb, d_hn, d_ps], axis=1)
    return dx_new, small_g, [g_in, g_pool, g_pa, g_pb, g_out], reduced


def kernel(x, c, w_ada, b_ada, norm_pre, norm_post, w_in, lower_bounds, hgrn_norm, pool_w, pool_scale, w_proj_a, w_proj_b, w_out, loss_target, m_w_ada, m_b_ada, m_norm_pre, m_norm_post, m_w_in, m_lower_bounds, m_hgrn_norm, m_pool_w, m_pool_scale, m_w_proj_a, m_w_proj_b, m_w_out, v_w_ada, v_b_ada, v_norm_pre, v_norm_post, v_w_in, v_lower_bounds, v_hgrn_norm, v_pool_w, v_pool_scale, v_w_proj_a, v_w_proj_b, v_w_out):
    _, S, D = x.shape
    L = w_in.shape[0]
    KW = D // 2
    xi, yi, ci = lax.axis_index("x"), lax.axis_index("y"), lax.axis_index("c")
    chip = 2 * xi + yi
    dev = 4 * xi + 2 * yi + ci

    c_all = _all_gather_rows(c.reshape(8, D // 8)).reshape(N_DEV, D)
    modp = _mod_part(c_all, w_ada)
    cols = modp.shape[-1]
    modg = _all_gather_rows(modp.reshape(L * N_DEV, cols)).reshape(N_DEV, L, N_DEV, cols)
    mod_all = jnp.transpose(modg[0::2], (1, 2, 0, 3)).reshape(L, N_DEV, 3 * D)
    mod = lax.dynamic_index_in_dim(mod_all, dev, axis=1, keepdims=False) + b_ada
    mods = (mod[:, :D], mod[:, D:2 * D], mod[:, 2 * D:])
    lb_all = _lb_table(lower_bounds)

    shards = [w_in, pool_w, w_proj_a, w_proj_b, w_out]
    chip_idx = chip.astype(jnp.int32).reshape(1)
    c_idx = ci.astype(jnp.int32).reshape(1)
    small = (norm_pre, norm_post, hgrn_norm, pool_scale)

    xs = x[0]
    saved, ws = [], []
    w = _gather_alone(_layer_weights(shards, 0, chip_idx))
    for l in range(L):
        nxt = _layer_weights(shards, l + 1, chip_idx) if l + 1 < L else None
        xs, sv, nxt = _forward_layer(xs, l, w, small, mods, lb_all, KW, nxt)
        saved.append(sv)
        ws.append(w)
        w = nxt
    dx, sq = _loss_head(xs, loss_target[0])
    loss = lax.psum(0.5 * jnp.sum(sq) / D, ("x", "y", "c"))

    small_g, reduced = [None] * L, [None] * L
    above = None
    for l in reversed(range(L)):
        dx, small_g[l], above, red = _backward_layer(dx, l, saved[l], ws[l], small, mods, lb_all, KW, above, (chip_idx, c_idx))
        if red is not None:
            reduced[l + 1] = red
    recv = _run_side(_swap_side(above), "swap")
    pair = [_pair_sum(g, r, t, c_idx) for t, (g, r) in enumerate(zip(above, recv))]
    slots = _run_side(_scatter_side(pair), "scatter")
    reduced[0] = _run_side(_join_side([_sum_chips(p, s, t, chip_idx, c_idx) for t, (p, s) in enumerate(zip(pair, slots))]), "join")
    grad_x = dx[None]

    sg = jnp.concatenate(small_g, axis=0)
    sg = jnp.concatenate([sg, jnp.zeros((8 - L, sg.shape[1]), F32)], axis=0)
    parts = _all_gather_rows(sg).reshape(N_DEV, 8, sg.shape[1])[:, :L]
    wmv = [(b_ada, m_b_ada, v_b_ada), (norm_pre, m_norm_pre, v_norm_pre), (norm_post, m_norm_post, v_norm_post),
           (lower_bounds, m_lower_bounds, v_lower_bounds), (hgrn_norm, m_hgrn_norm, v_hgrn_norm), (pool_scale, m_pool_scale, v_pool_scale)]
    r_b_ada, r_npre, r_npost, r_lb, r_hn, r_ps = _small_update(parts, lower_bounds, wmv, D, KW)
    dmod = lax.dynamic_slice_in_dim(parts[:, :, :3 * D], chip * cols, cols, axis=2)
    r_w_ada = _w_ada_update(c_all, jnp.transpose(dmod, (1, 0, 2)), w_ada, m_w_ada, v_w_ada)

    grads = [jnp.stack([reduced[l][t] for l in range(L)], axis=0) for t in range(N_STACKS)]
    ms = [m_w_in, m_pool_w, m_w_proj_a, m_w_proj_b, m_w_out]
    vs = [v_w_in, v_pool_w, v_w_proj_a, v_w_proj_b, v_w_out]
    r_big = [[g] + _adamw(w, g, m, v, f"adamw{t}") for t, (w, g, m, v) in enumerate(zip(shards, grads, ms, vs))]
    r_w_in, r_pool_w, r_pa, r_pb, r_w_out = r_big

    order = [r_w_ada, r_b_ada, r_npre, r_npost, r_w_in, r_lb, r_hn, r_pool_w, r_ps, r_pa, r_pb, r_w_out]
    outs = [loss, grad_x]
    for k in range(4):
        outs += [r[k] for r in order]
    return tuple(outs)
```

```python
import functools

import jax
import jax.numpy as jnp
from jax import lax
from jax.experimental import pallas as pl
from jax.experimental.pallas import tpu as pltpu

F32 = jnp.float32
BF16 = jnp.bfloat16
MXU_DT = jnp.bfloat16

CHUNK = 64
SUBLANES = 8
SUB = 8
LEVELS = (32, 16, 8)
HEAD = 128
EPS = 1e-6
MIN_FORGET = 1e-30
WINDOWS = (2, 4, 8, 16)
HALO = 16
N_CHIPS = 4
N_DEV = 8
VMEM_LIMIT_BYTES = 56 * 1024 * 1024

ADAM_LR = 0.001
ADAM_B1 = 0.9
ADAM_B2 = 0.999
ADAM_EPS = 1e-08
ADAM_WD = 0.01
ADAM_STEP = 10

NN = ((1,), (0,))
NT = ((1,), (1,))
TN = ((0,), (0,))
MESH = pl.DeviceIdType.MESH


def _params(sem):
    return pltpu.CompilerParams(dimension_semantics=sem, vmem_limit_bytes=VMEM_LIMIT_BYTES)


def _tile(n, pref):
    return pref if n % pref == 0 else n


def _dot(a, b, dims):
    return lax.dot_general(a.astype(MXU_DT), b.astype(MXU_DT), (dims, ((), ())), preferred_element_type=F32)


def _sig_pair(a):
    e = jnp.exp(-jnp.abs(a))
    inv = 1.0 / (1.0 + e)
    pos = a >= 0
    return jnp.where(pos, inv, e * inv), jnp.where(pos, e * inv, inv)


def _sig(a):
    return 0.5 * jnp.tanh(0.5 * a) + 0.5


def _split_dot(tri, x):
    def top(y):
        return lax.bitcast_convert_type(lax.bitcast_convert_type(y, jnp.uint32) & jnp.uint32(0xFFFF0000), F32)

    hi = top(x)
    r1 = x - hi
    mid = top(r1)
    lo = r1 - mid
    d = lambda y: jnp.dot(tri, y.astype(BF16), preferred_element_type=F32)
    return d(hi) + d(mid) + d(lo)


def _rms(xv):
    return lax.rsqrt(jnp.mean(xv * xv, axis=-1, keepdims=True) + EPS)


def _prenorm(x, g, scale, shift):
    S, D = x.shape
    tm = _tile(S, 256)

    def body(x_ref, g_ref, sc_ref, sh_ref, h_ref):
        xv = x_ref[...]
        h = (xv * _rms(xv)) * g_ref[...] * (1.0 + sc_ref[...]) + sh_ref[...]
        h_ref[...] = h.astype(h_ref.dtype)

    row = pl.BlockSpec((tm, D), lambda i: (i, 0))
    vec = pl.BlockSpec((1, D), lambda i: (0, 0))
    return pl.pallas_call(body, name="prenorm", grid=(S // tm,), in_specs=[row, vec, vec, vec], out_specs=row,
                          out_shape=jax.ShapeDtypeStruct((S, D), MXU_DT), compiler_params=_params(("parallel",)))(x, g, scale, shift)


def _prenorm_bwd(dh, x, dx, g, scale):
    S, D = x.shape
    tm = _tile(S, 256)

    def body(dh_ref, x_ref, dx_ref, g_ref, sc_ref, o_ref, dsh_ref, dsc_ref, dg_ref):
        @pl.when(pl.program_id(0) == 0)
        def _():
            dsh_ref[...] = jnp.zeros_like(dsh_ref)
            dsc_ref[...] = jnp.zeros_like(dsc_ref)
            dg_ref[...] = jnp.zeros_like(dg_ref)

        xv = x_ref[...]
        r = _rms(xv)
        xh = xv * r
        dh = dh_ref[...]
        gv = g_ref[...]
        one_sc = 1.0 + sc_ref[...]
        dsh_ref[...] += jnp.sum(dh, axis=0, keepdims=True)
        dsc_ref[...] += jnp.sum(dh * xh * gv, axis=0, keepdims=True)
        dg_ref[...] += jnp.sum(dh * xh * one_sc, axis=0, keepdims=True)
        dxh = dh * gv * one_sc
        o_ref[...] = dx_ref[...] + r * (dxh - xh * jnp.mean(dxh * xh, axis=-1, keepdims=True))

    row = pl.BlockSpec((tm, D), lambda i: (i, 0))
    vec = pl.BlockSpec((1, D), lambda i: (0, 0))
    vs = jax.ShapeDtypeStruct((1, D), F32)
    return pl.pallas_call(body, name="prenorm_bwd", grid=(S // tm,), in_specs=[row, row, row, vec, vec],
                          out_specs=[row, vec, vec, vec], out_shape=[jax.ShapeDtypeStruct((S, D), F32), vs, vs, vs],
                          compiler_params=_params(("arbitrary",)))(dh, x, dx, g, scale)


def _postnorm(x, out, gate, g):
    S, D = x.shape
    tm = _tile(S, 256)

    def body(x_ref, o_ref, gt_ref, g_ref, y_ref):
        ov = o_ref[...]
        y_ref[...] = x_ref[...] + gt_ref[...] * ((ov * _rms(ov)) * g_ref[...])

    row = pl.BlockSpec((tm, D), lambda i: (i, 0))
    vec = pl.BlockSpec((1, D), lambda i: (0, 0))
    return pl.pallas_call(body, name="postnorm", grid=(S // tm,), in_specs=[row, row, vec, vec], out_specs=row,
                          out_shape=jax.ShapeDtypeStruct((S, D), F32), compiler_params=_params(("parallel",)))(x, out, gate, g)


def _postnorm_bwd(dx, out, gate, g):
    S, D = dx.shape
    tm = _tile(S, 256)

    def body(dx_ref, o_ref, gt_ref, g_ref, do_ref, dgt_ref, dg_ref):
        @pl.when(pl.program_id(0) == 0)
        def _():
            dgt_ref[...] = jnp.zeros_like(dgt_ref)
            dg_ref[...] = jnp.zeros_like(dg_ref)

        ov = o_ref[...]
        r = _rms(ov)
        rn = ov * r
        dxv = dx_ref[...]
        gv = g_ref[...]
        dgt_ref[...] += jnp.sum(dxv * rn * gv, axis=0, keepdims=True)
        du = dxv * gt_ref[...]
        dg_ref[...] += jnp.sum(du * rn, axis=0, keepdims=True)
        drn = du * gv
        do_ref[...] = (r * (drn - rn * jnp.mean(drn * rn, axis=-1, keepdims=True))).astype(do_ref.dtype)

    row = pl.BlockSpec((tm, D), lambda i: (i, 0))
    vec = pl.BlockSpec((1, D), lambda i: (0, 0))
    vs = jax.ShapeDtypeStruct((1, D), F32)
    return pl.pallas_call(body, name="postnorm_bwd", grid=(S // tm,), in_specs=[row, row, vec, vec],
                          out_specs=[row, vec, vec], out_shape=[jax.ShapeDtypeStruct((S, D), MXU_DT), vs, vs],
                          compiler_params=_params(("arbitrary",)))(dx, out, gate, g)


def _loss_head(y, target):
    S, D = y.shape
    tm = _tile(S, 256)

    def body(y_ref, t_ref, dy_ref, sq_ref):
        @pl.when(pl.program_id(0) == 0)
        def _():
            sq_ref[...] = jnp.zeros_like(sq_ref)

        e = y_ref[...] - t_ref[...]
        dy_ref[...] = e * (1.0 / D)
        sq_ref[...] += jnp.sum(e * e, axis=0, keepdims=True)

    row = pl.BlockSpec((tm, D), lambda i: (i, 0))
    vec = pl.BlockSpec((1, D), lambda i: (0, 0))
    return pl.pallas_call(body, name="loss_head", grid=(S // tm,), in_specs=[row, row], out_specs=[row, vec],
                          out_shape=[jax.ShapeDtypeStruct((S, D), F32), jax.ShapeDtypeStruct((1, D), F32)],
                          compiler_params=_params(("arbitrary",)))(y, target)


def _matmul(a, b, mode, out_dtype, name, *, side=None, into=None, tm=1024, tn=1024, tk=2048):
    groups = [list(g) if isinstance(g, (list, tuple)) else [g] for g in (a if isinstance(a, (list, tuple)) else [a])]
    pieces = [p for g in groups for p in g]
    na = len(pieces)
    if mode == "tn":
        K, M = pieces[0].shape
        gwidths = [K]
    else:
        M = pieces[0].shape[0]
        gwidths = [sum(p.shape[1] for p in g) for g in groups]
        K = sum(gwidths)
    N = b.shape[0] if mode == "nt" else b.shape[1]
    tm, tn, tk = _tile(M, tm), _tile(N, tn), _tile(min(gwidths), tk)
    ni, nj, nk = M // tm, N // tn, K // tk
    dims = {"nn": NN, "nt": NT, "tn": TN}[mode]

    a_specs, ranges, k0 = [], [], 0
    for g, wd in zip(groups, gwidths):
        n = wd // tk
        if mode == "tn":
            a_specs.append(pl.BlockSpec((tk, tm), lambda i, j, k: (k, i)))
        elif len(g) == 1:
            a_specs.append(pl.BlockSpec((tm, tk), lambda i, j, k, k0=k0, n=n: (i, jnp.clip(k - k0, 0, n - 1))))
        else:
            a_specs += [pl.BlockSpec((tm, p.shape[1]), lambda i, j, k: (i, 0)) for p in g]
        ranges.append((k0, k0 + n))
        k0 += n
    if mode == "nt":
        b_spec = pl.BlockSpec((tn, tk), lambda i, j, k: (j, k))
    else:
        b_spec = pl.BlockSpec((tk, tn), lambda i, j, k: (k, j))
    n_in = len(side["inputs"]) if side else 0
    n_out = len(side["out_shapes"]) if side else 0
    n_buf = 1 if into else 0

    def body(*refs):
        a_refs, b_ref, rest = refs[:na], refs[na], refs[na + 1:]
        side_in, o_ref = rest[:n_in], rest[n_in + n_buf]
        side_out = rest[n_in + n_buf + 1:n_in + n_buf + 1 + n_out]
        scratch = rest[n_in + n_buf + 1 + n_out:]
        i, j, k = pl.program_id(0), pl.program_id(1), pl.program_id(2)
        if side:
            sems = scratch[-2:]

            @pl.when((i == 0) & (j == 0) & (k == 0))
            def _():
                side["start"](side_in, side_out, *sems)

        def accumulate(g_refs):
            p, off = None, 0
            for a_ref in g_refs:
                wd = a_ref.shape[0] if mode == "tn" else a_ref.shape[1]
                bv = b_ref[...] if len(g_refs) == 1 else (b_ref[:, off:off + wd] if mode == "nt" else b_ref[off:off + wd, :])
                q = lax.dot_general(a_ref[...], bv, (dims, ((), ())), preferred_element_type=F32)
                p = q if p is None else p + q
                off += wd
            if nk == 1:
                o_ref[...] = p.astype(o_ref.dtype)
            else:
                acc_ref = scratch[0]

                @pl.when(k == 0)
                def _():
                    acc_ref[...] = p

                @pl.when(k > 0)
                def _():
                    acc_ref[...] += p

                @pl.when(k == nk - 1)
                def _():
                    o_ref[...] = acc_ref[...].astype(o_ref.dtype)

        if len(groups) == 1:
            accumulate(a_refs)
        else:
            first = 0
            for (lo, hi), g in zip(ranges, groups):
                pl.when((k >= lo) & (k < hi))(functools.partial(accumulate, a_refs[first:first + len(g)]))
                first += len(g)

        if side:
            @pl.when((i == ni - 1) & (j == nj - 1) & (k == nk - 1))
            def _():
                side["finish"](side_in, side_out, *sems)

    any_spec = pl.BlockSpec(memory_space=pl.ANY)
    scratch_shapes = [pltpu.VMEM((tm, tn), F32)] if nk > 1 else []
    if side:
        scratch_shapes += [pltpu.SemaphoreType.DMA((side["n_sems"],)), pltpu.SemaphoreType.DMA((side["n_sems"],))]
    aliases = {na + 1 + s: 1 + d for s, d in side["aliases"].items()} if side else {}
    args = pieces + [b] + (list(side["inputs"]) if side else [])
    if into:
        buf, col = into
        aliases[len(args)] = 0
        args.append(buf)
        j0 = col // tn
        out_spec, out_shape = pl.BlockSpec((tm, tn), lambda i, j, k: (i, j0 + j)), jax.ShapeDtypeStruct(buf.shape, buf.dtype)
    else:
        out_spec, out_shape = pl.BlockSpec((tm, tn), lambda i, j, k: (i, j)), jax.ShapeDtypeStruct((M, N), out_dtype)
    res = pl.pallas_call(
        body, name=name, grid=(ni, nj, nk), in_specs=a_specs + [b_spec] + [any_spec] * (n_in + n_buf),
        out_specs=[out_spec] + [any_spec] * n_out,
        out_shape=[out_shape] + (list(side["out_shapes"]) if side else []),
        scratch_shapes=scratch_shapes, input_output_aliases=aliases,
        compiler_params=_params(("arbitrary",) * 3 if side else ("parallel", "parallel", "arbitrary")))(*args)
    return (res[0], list(res[1:])) if side else res[0]


def _merge(ya, yb, wpa, wpb, proj):
    S, KW = ya.shape
    D = 2 * KW
    tm, tn = _tile(S, 1024), _tile(D, 512)
    ga0, gb0 = 3 * D // tn, 4 * D // tn

    def body(ya_ref, yb_ref, wa_ref, wb_ref, ga_ref, gb_ref, pa_ref, pb_ref, m_ref):
        pa = jnp.dot(ya_ref[...], wa_ref[...], preferred_element_type=F32)
        pb = jnp.dot(yb_ref[...], wb_ref[...], preferred_element_type=F32)
        pa_ref[...] = pa.astype(pa_ref.dtype)
        pb_ref[...] = pb.astype(pb_ref.dtype)
        m_ref[...] = (_sig(ga_ref[...]) * pa + _sig(gb_ref[...]) * pb).astype(m_ref.dtype)

    y_spec = pl.BlockSpec((tm, KW), lambda i, j: (i, 0))
    w_spec = pl.BlockSpec((KW, tn), lambda i, j: (0, j))
    o_spec = pl.BlockSpec((tm, tn), lambda i, j: (i, j))
    return pl.pallas_call(
        body, name="merge", grid=(S // tm, D // tn),
        in_specs=[y_spec, y_spec, w_spec, w_spec, pl.BlockSpec((tm, tn), lambda i, j: (i, ga0 + j)),
                  pl.BlockSpec((tm, tn), lambda i, j: (i, gb0 + j))],
        out_specs=[o_spec, o_spec, o_spec],
        out_shape=[jax.ShapeDtypeStruct((S, D), MXU_DT), jax.ShapeDtypeStruct((S, D), MXU_DT), jax.ShapeDtypeStruct((S, D), MXU_DT)],
        compiler_params=_params(("parallel", "parallel")))(ya, yb, wpa, wpb, proj, proj)


def _merge_bwd(dout, w_out, proj, pa, pb):
    S, D = dout.shape
    tm, tn = _tile(S, 1024), _tile(D, 512)
    ga0, gb0 = 3 * D // tn, 4 * D // tn

    def body(do_ref, w_ref, ga_ref, gb_ref, pa_ref, pb_ref, dpa_ref, dpb_ref, dga_ref, dgb_ref):
        dm = lax.dot_general(do_ref[...], w_ref[...], (NT, ((), ())), preferred_element_type=F32)
        sa, sb = _sig(ga_ref[...]), _sig(gb_ref[...])
        sna, snb = 1.0 - sa, 1.0 - sb
        dpa = dm * sa
        dpb = dm * sb
        dpa_ref[...] = dpa.astype(dpa_ref.dtype)
        dpb_ref[...] = dpb.astype(dpb_ref.dtype)
        dga_ref[...] = (dpa * pa_ref[...].astype(F32) * sna).astype(dga_ref.dtype)
        dgb_ref[...] = (dpb * pb_ref[...].astype(F32) * snb).astype(dgb_ref.dtype)

    blk = pl.BlockSpec((tm, tn), lambda i, j: (i, j))
    os = jax.ShapeDtypeStruct((S, D), MXU_DT)
    return pl.pallas_call(
        body, name="merge_bwd", grid=(S // tm, D // tn),
        in_specs=[pl.BlockSpec((tm, D), lambda i, j: (i, 0)), pl.BlockSpec((tn, D), lambda i, j: (j, 0)),
                  pl.BlockSpec((tm, tn), lambda i, j: (i, ga0 + j)), pl.BlockSpec((tm, tn), lambda i, j: (i, gb0 + j)), blk, blk],
        out_specs=[blk, blk, blk, blk], out_shape=[os, os, os, os],
        compiler_params=_params(("parallel", "parallel")))(dout, w_out, proj, proj, pa, pb)


def _gates(qr, a, lbv):
    sq = _sig(qr)
    q = qr * sq
    s, sn = _sig_pair(a)
    omlb = 1.0 - lbv
    f = lbv + omlb * s
    logf = jnp.log(jnp.maximum(f, MIN_FORGET))
    kk = omlb * sn
    return sq, q, s, sn, omlb, f, logf, kk


def _shift_in_block(x, d):
    return pltpu.roll(x.reshape(CHUNK // SUBLANES, SUBLANES, HEAD), d, axis=1).reshape(CHUNK, HEAD)


def _level_masks(h):
    ti = lax.broadcasted_iota(jnp.int32, (CHUNK, 1), 0)
    si = lax.broadcasted_iota(jnp.int32, (1, CHUNK), 1)
    t2, s2 = (ti & (2 * h - 1)) >= h, (si & (2 * h - 1)) >= h
    same = (ti & -(2 * h)) == (si & -(2 * h))
    return t2, jnp.where(same & t2 & jnp.logical_not(s2), 1.0, 0.0)


def _level_factor(b, h):
    parts = [jnp.broadcast_to(b[g + h - 1:g + h], (2 * h, HEAD)) for g in range(0, CHUNK, 2 * h)]
    ref = parts[0] if len(parts) == 1 else jnp.concatenate(parts, axis=0)
    return jnp.exp(-jnp.abs(b - ref))


def _level_operands(q, kk, b, h, t2):
    fac = _level_factor(b, h)
    return jnp.where(t2, q * fac, 0.0), jnp.where(t2, 0.0, kk * fac), fac


def _heads_per_step(H, pref):
    return pref if H % pref == 0 else 1


def _tri(lower):
    r = lax.broadcasted_iota(jnp.int32, (CHUNK, CHUNK), 0)
    c = lax.broadcasted_iota(jnp.int32, (CHUNK, CHUNK), 1)
    return jnp.where((r >= c) if lower else (c >= r), 1.0, 0.0).astype(BF16)


def _hgrn_fwd(proj, lb, hn, KW):
    S = proj.shape[0]
    H = KW // HEAD
    T = _tile(S, 512)
    nci, nb = T // CHUNK, S // T
    HP = _heads_per_step(H, 8)

    def body(q_ref, f_ref, v_ref, z_ref, lb_ref, hn_ref, o_ref, y_ref, s0_ref, st_ref):
        @pl.when(pl.program_id(1) == 0)
        def _():
            st_ref[...] = jnp.zeros_like(st_ref)

        tril = _tri(True)
        rmod = lax.broadcasted_iota(jnp.int32, (CHUNK, 1), 0) & (SUB - 1)
        masks = [_level_masks(h) for h in LEVELS]
        lb_all = lb_ref[...]

        def chunk(ci, carry):
            rows = pl.ds(pl.multiple_of(ci * CHUNK, CHUNK), CHUNK)
            _, q_all, _, _, _, _, logf_all, kk_all = _gates(q_ref[rows, :], f_ref[rows, :], lb_all)
            b_all = _split_dot(tril, logf_all)
            for hp in range(HP):
                cols = slice(hp * HEAD, (hp + 1) * HEAD)
                one_head(ci, rows, hp, cols, q_all[:, cols], kk_all[:, cols], b_all[:, cols])
            return carry

        def one_head(ci, rows, hp, cols, q, kk, b):
            hnv = hn_ref[:, cols]
            vv = v_ref[rows, cols]
            eb = jnp.exp(b)
            st = st_ref[hp]
            s0_ref[ci, hp] = st
            p = None
            for h, (t2, m) in zip(LEVELS, masks):
                qs, ks, _ = _level_operands(q, kk, b, h, t2)
                pm = m * _dot(qs, ks, NT)
                p = pm if p is None else p + pm
            o = _dot(q * eb, st, NT) + _dot(p, vv, NN) + jnp.sum(q * kk, axis=-1, keepdims=True) * vv
            for d in range(1, SUB):
                kd_, bd_, vd_ = _shift_in_block(kk, d), _shift_in_block(b, d), _shift_in_block(vv, d)
                e = jnp.exp(jnp.minimum(b - bd_, 0.0))
                p = jnp.where(rmod >= d, jnp.sum(q * (kd_ * e), axis=-1, keepdims=True), 0.0)
                o = o + p * vd_
            bl = b[CHUNK - 1:CHUNK]
            st_ref[hp] = st * eb[CHUNK - 1:CHUNK] + _dot(vv, kk * jnp.exp(bl - b), TN)
            o_ref[rows, cols] = o
            z = z_ref[rows, cols]
            y_ref[rows, cols] = ((o * _rms(o)) * hnv * (z * _sig(z))).astype(y_ref.dtype)

        lax.fori_loop(0, nci, chunk, 0)

    W = HP * HEAD
    G = H // HP

    def col(off):
        return pl.BlockSpec((T, W), lambda h, t: (t, off + h))

    vec = pl.BlockSpec((1, W), lambda h, t: (0, h))
    return pl.pallas_call(
        body, name="hgrn_fwd", grid=(G, nb),
        in_specs=[col(0), col(G), col(2 * G), col(3 * G), vec, vec],
        out_specs=[pl.BlockSpec((T, W), lambda h, t: (t, h)), pl.BlockSpec((T, W), lambda h, t: (t, h)),
                   pl.BlockSpec((nci, HP, HEAD, HEAD), lambda h, t: (t, h, 0, 0))],
        out_shape=[jax.ShapeDtypeStruct((S, KW), F32), jax.ShapeDtypeStruct((S, KW), MXU_DT),
                   jax.ShapeDtypeStruct((S // CHUNK, H, HEAD, HEAD), F32)],
        scratch_shapes=[pltpu.VMEM((HP, HEAD, HEAD), F32)],
        compiler_params=_params(("parallel", "arbitrary")))(proj, proj, proj, proj, lb, hn)


def _gate_a_bwd(dya, o, proj, hn, KW):
    S = o.shape[0]
    H = KW // HEAD
    T = _tile(S, 512)

    def body(dy_ref, o_ref, z_ref, hn_ref, do_ref, dz_ref, dhn_ref):
        @pl.when(pl.program_id(0) == 0)
        def _():
            dhn_ref[...] = jnp.zeros_like(dhn_ref)

        for h in range(H):
            cols = slice(h * HEAD, (h + 1) * HEAD)
            ov = o_ref[:, cols]
            r = _rms(ov)
            rn = ov * r
            z = z_ref[:, cols]
            sz = _sig(z)
            silu = z * sz
            dy = dy_ref[:, cols]
            hnv = hn_ref[:, cols]
            dz_ref[:, cols] = (dy * rn * hnv * (sz * (1.0 + z * (1.0 - sz)))).astype(dz_ref.dtype)
            dhn_ref[:, cols] += jnp.sum(dy * rn * silu, axis=0, keepdims=True)
            drn = dy * hnv * silu
            do_ref[:, cols] = r * (drn - rn * jnp.mean(drn * rn, axis=-1, keepdims=True))

    blk = pl.BlockSpec((T, KW), lambda t: (t, 0))
    vec = pl.BlockSpec((1, KW), lambda t: (0, 0))
    return pl.pallas_call(
        body, name="gate_a_bwd", grid=(S // T,),
        in_specs=[blk, blk, pl.BlockSpec((T, KW), lambda t: (t, 3)), vec],
        out_specs=[blk, blk, vec],
        out_shape=[jax.ShapeDtypeStruct((S, KW), F32), jax.ShapeDtypeStruct((S, KW), MXU_DT), jax.ShapeDtypeStruct((1, KW), F32)],
        compiler_params=_params(("arbitrary",)))(dya, o, proj, hn)


def _hgrn_bwd(proj, do, s0, lb, KW):
    S = proj.shape[0]
    H = KW // HEAD
    T = _tile(S, 512)
    nci, nb = T // CHUNK, S // T
    HP = _heads_per_step(H, 8)

    def body(q_ref, f_ref, v_ref, do_ref, s0_ref, lb_ref, dq_ref, df_ref, dv_ref, dlb_ref, dst_ref):
        @pl.when(pl.program_id(1) == 0)
        def _():
            dst_ref[...] = jnp.zeros_like(dst_ref)
            dlb_ref[...] = jnp.zeros_like(dlb_ref)

        tril, triu = _tri(True), _tri(False)
        rmod = lax.broadcasted_iota(jnp.int32, (CHUNK, 1), 0) & (SUB - 1)
        masks = [_level_masks(h) for h in LEVELS]
        lb_all = lb_ref[...]

        def chunk(it, carry):
            ci = nci - 1 - it
            rows = pl.ds(pl.multiple_of(ci * CHUNK, CHUNK), CHUNK)
            qr_all = q_ref[rows, :]
            sq_all, q_all, s_all, sn_all, omlb, f_all, logf_all, kk_all = _gates(qr_all, f_ref[rows, :], lb_all)
            b_all = _split_dot(tril, logf_all)
            res = []
            for hp in range(HP):
                cols = slice(hp * HEAD, (hp + 1) * HEAD)
                res.append(one_head(ci, rows, hp, cols, q_all[:, cols], kk_all[:, cols], b_all[:, cols]))
            dq_t, dk_t, dv_t = (jnp.concatenate([r[i] for r in res], axis=1) for i in range(3))
            w = jnp.concatenate([r[3] for r in res], axis=1)
            dlogf = _split_dot(triu, q_all * dq_t - kk_all * dk_t) + w
            dlf = jnp.where(f_all > MIN_FORGET, dlogf / jnp.maximum(f_all, MIN_FORGET), 0.0)
            t1 = dlf - dk_t
            dlb_ref[...] += jnp.sum(sn_all * t1, axis=0, keepdims=True)
            df_ref[rows, :] = (omlb * (s_all * sn_all) * t1).astype(df_ref.dtype)
            dq_ref[rows, :] = (dq_t * (sq_all * (1.0 + qr_all * (1.0 - sq_all)))).astype(dq_ref.dtype)
            dv_ref[rows, :] = dv_t.astype(dv_ref.dtype)
            return carry

        def one_head(ci, rows, hp, cols, q, kk, b):
            vv, dov = v_ref[rows, cols], do_ref[rows, cols]
            eb = jnp.exp(b)
            st0 = s0_ref[ci, hp]
            dst = dst_ref[hp]
            bl, ec = b[CHUNK - 1:CHUNK], eb[CHUNK - 1:CHUNK]
            decl = jnp.exp(bl - b)
            kdec = kk * decl
            dq_t = eb * _dot(dov, st0, NN)
            dst_ref[hp] = dst * ec + _dot(dov, q * eb, TN)
            dv_t = _dot(kdec, dst, NT)
            dk_t = decl * _dot(vv, dst, NN)
            stc = st0 * ec + _dot(vv, kdec, TN)
            w = jnp.sum(stc * dst, axis=0, keepdims=True)
            dp_all = _dot(dov, vv, NT)
            p = None
            for h, (t2, m) in zip(LEVELS, masks):
                qs, ks, fac = _level_operands(q, kk, b, h, t2)
                pm = m * _dot(qs, ks, NT)
                p = pm if p is None else p + pm
                dpm = m * dp_all
                dq_t = dq_t + fac * _dot(dpm, ks, NN)
                dk_t = dk_t + fac * _dot(dpm, qs, TN)
            p0 = jnp.sum(q * kk, axis=-1, keepdims=True)
            dp0 = jnp.sum(dov * vv, axis=-1, keepdims=True)
            dq_t = dq_t + dp0 * kk
            dk_t = dk_t + dp0 * q
            dv_t = dv_t + _dot(p, dov, TN) + p0 * dov
            for d in range(1, SUB):
                kd_, bd_, vd_ = _shift_in_block(kk, d), _shift_in_block(b, d), _shift_in_block(vv, d)
                e = jnp.exp(jnp.minimum(b - bd_, 0.0))
                ke = kd_ * e
                m = rmod >= d
                p = jnp.where(m, jnp.sum(q * ke, axis=-1, keepdims=True), 0.0)
                dp = jnp.where(m, jnp.sum(dov * vd_, axis=-1, keepdims=True), 0.0)
                dq_t = dq_t + dp * ke
                dk_t = dk_t + _shift_in_block(dp * (q * e), SUBLANES - d)
                dv_t = dv_t + _shift_in_block(p * dov, SUBLANES - d)
            return dq_t, dk_t, dv_t, w

        lax.fori_loop(0, nci, chunk, 0)

    W = HP * HEAD
    G = H // HP

    def col(off):
        return pl.BlockSpec((T, W), lambda h, t: (nb - 1 - t, off + h))

    blk = pl.BlockSpec((T, W), lambda h, t: (nb - 1 - t, h))
    vec = pl.BlockSpec((1, W), lambda h, t: (0, h))
    os = jax.ShapeDtypeStruct((S, KW), MXU_DT)
    return pl.pallas_call(
        body, name="hgrn_bwd", grid=(G, nb),
        in_specs=[col(0), col(G), col(2 * G), blk, pl.BlockSpec((nci, HP, HEAD, HEAD), lambda h, t: (nb - 1 - t, h, 0, 0)), vec],
        out_specs=[blk, blk, blk, vec], out_shape=[os, os, os, jax.ShapeDtypeStruct((1, KW), F32)],
        scratch_shapes=[pltpu.VMEM((HP, HEAD, HEAD), F32)],
        compiler_params=_params(("parallel", "arbitrary")))(proj, proj, proj, do, s0, lb)


def _pool_pos(i, T):
    return (i * T + lax.broadcasted_iota(jnp.int32, (T, 1), 0) + 1).astype(F32)


def _pool_fwd(proj, pool_w, pool_scale, KW):
    S = proj.shape[0]
    GW = KW // len(WINDOWS)
    T = _tile(S, 256)

    def body(v_ref, halo_ref, z_ref, pw_ref, ps_ref, p_ref, y_ref):
        i = pl.program_id(0)
        vb = v_ref[...]
        ext = jnp.concatenate([jnp.where(i > 0, halo_ref[...], 0.0), vb], axis=0)
        pos = _pool_pos(i, T)
        z = z_ref[...]
        gate = ps_ref[...] * (z * _sig(z))
        for g, w in enumerate(WINDOWS):
            sl = slice(g * GW, (g + 1) * GW)
            sg = ext[:, sl]
            for jj in range(g + 1):
                sg = sg + pltpu.roll(sg, 1 << jj, axis=0)
            pooled = (sg[HALO:] / jnp.minimum(pos, float(w)) - vb[:, sl]).astype(p_ref.dtype)
            p_ref[:, sl] = pooled
            y_ref[:, sl] = (_dot(pooled, pw_ref[g], NN) * gate[:, sl]).astype(y_ref.dtype)

    row = pl.BlockSpec((T, KW), lambda i: (i, 0))
    hb = T // HALO
    return pl.pallas_call(
        body, name="pool_fwd", grid=(S // T,),
        in_specs=[pl.BlockSpec((T, KW), lambda i: (i, 4)), pl.BlockSpec((HALO, KW), lambda i: (jnp.maximum(i * hb - 1, 0), 4)),
                  pl.BlockSpec((T, KW), lambda i: (i, 5)), pl.BlockSpec((len(WINDOWS), GW, GW), lambda i: (0, 0, 0)),
                  pl.BlockSpec((1, KW), lambda i: (0, 0))],
        out_specs=[row, row], out_shape=[jax.ShapeDtypeStruct((S, KW), MXU_DT), jax.ShapeDtypeStruct((S, KW), MXU_DT)],
        compiler_params=_params(("parallel",)))(proj, proj, proj, pool_w, pool_scale)


def _pool_bwd(dyb, proj, pooled, pool_w, pool_scale, KW):
    S = proj.shape[0]
    G = len(WINDOWS)
    GW = KW // G
    T = _tile(S, 256)

    def body(dy_ref, z_ref, p_ref, pw_ref, ps_ref, u_ref, dz_ref, dps_ref, dpw_ref):
        i = pl.program_id(0)

        @pl.when(i == 0)
        def _():
            dps_ref[...] = jnp.zeros_like(dps_ref)
            dpw_ref[...] = jnp.zeros_like(dpw_ref)

        pos = _pool_pos(i, T)
        z = z_ref[...]
        sz = _sig(z)
        silu = z * sz
        dsilu = sz * (1.0 + z * (1.0 - sz))
        dy = dy_ref[...]
        ps = ps_ref[...]
        for g, w in enumerate(WINDOWS):
            sl = slice(g * GW, (g + 1) * GW)
            pg = p_ref[:, sl]
            mixed = _dot(pg, pw_ref[g], NN)
            dyg = dy[:, sl]
            dz_ref[:, sl] = (dyg * mixed * ps[:, sl] * dsilu[:, sl]).astype(dz_ref.dtype)
            dps_ref[:, sl] += jnp.sum(dyg * mixed * silu[:, sl], axis=0, keepdims=True)
            dmix = dyg * ps[:, sl] * silu[:, sl]
            dpw_ref[g] += _dot(pg, dmix, TN)
            u_ref[:, sl] = _dot(dmix, pw_ref[g], NT) / jnp.minimum(pos, float(w))

    row = pl.BlockSpec((T, KW), lambda i: (i, 0))
    return pl.pallas_call(
        body, name="pool_bwd", grid=(S // T,),
        in_specs=[row, pl.BlockSpec((T, KW), lambda i: (i, 5)), row,
                  pl.BlockSpec((G, GW, GW), lambda i: (0, 0, 0)), pl.BlockSpec((1, KW), lambda i: (0, 0))],
        out_specs=[row, row, pl.BlockSpec((1, KW), lambda i: (0, 0)), pl.BlockSpec((G, GW, GW), lambda i: (0, 0, 0))],
        out_shape=[jax.ShapeDtypeStruct((S, KW), F32), jax.ShapeDtypeStruct((S, KW), MXU_DT),
                   jax.ShapeDtypeStruct((1, KW), F32), jax.ShapeDtypeStruct((G, GW, GW), F32)],
        compiler_params=_params(("arbitrary",)))(dyb, proj, pooled, pool_w, pool_scale)


def _pool_bwd_window(u, KW):
    S = u.shape[0]
    GW = KW // len(WINDOWS)
    T = _tile(S, 256)
    nb = S // T
    n = T + HALO

    def body(u_ref, halo_ref, dv_ref):
        i = pl.program_id(0)
        uv = u_ref[...]
        ext = jnp.concatenate([uv, jnp.where(i < nb - 1, halo_ref[...], 0.0)], axis=0)
        pos = _pool_pos(i, T)
        for g, w in enumerate(WINDOWS):
            sl = slice(g * GW, (g + 1) * GW)
            sg = ext[:, sl]
            for jj in range(g + 1):
                sg = sg + pltpu.roll(sg, n - (1 << jj), axis=0)
            dv_ref[:, sl] = (sg[:T] - uv[:, sl] * jnp.minimum(pos, float(w))).astype(dv_ref.dtype)

    hb = T // HALO
    return pl.pallas_call(
        body, name="pool_bwd_window", grid=(nb,),
        in_specs=[pl.BlockSpec((T, KW), lambda i: (i, 0)),
                  pl.BlockSpec((HALO, KW), lambda i: (jnp.minimum((i + 1) * hb, S // HALO - 1), 0))],
        out_specs=pl.BlockSpec((T, KW), lambda i: (i, 0)), out_shape=jax.ShapeDtypeStruct((S, KW), MXU_DT),
        compiler_params=_params(("parallel",)))(u, u)


def _softmax_rows(x):
    m = jnp.max(x, axis=0, keepdims=True)
    e = jnp.exp(x - m)
    return e / jnp.sum(e, axis=0, keepdims=True)


def _lb_table(lower_bounds):
    L, KW = lower_bounds.shape

    def body(x_ref, o_ref):
        sm = _softmax_rows(x_ref[...])
        acc = jnp.zeros((1, KW), F32)
        o_ref[0:1, :] = acc
        for l in range(1, L):
            acc = acc + sm[l:l + 1]
            o_ref[l:l + 1, :] = acc

    return pl.pallas_call(body, name="lb_table", out_shape=jax.ShapeDtypeStruct((L, KW), F32))(lower_bounds)


def _mod_part(c_all, w_ada):
    L, D, C = w_ada.shape
    B = c_all.shape[0]
    tn = _tile(C, 512)

    def body(c_ref, w_ref, o_ref):
        cv = c_ref[...]
        o_ref[...] = _dot(cv * _sig(cv), w_ref[...], NN)

    return pl.pallas_call(
        body, name="mod_part", grid=(L, C // tn),
        in_specs=[pl.BlockSpec((B, D), lambda l, j: (0, 0)), pl.BlockSpec((None, D, tn), lambda l, j: (l, 0, j))],
        out_specs=pl.BlockSpec((None, B, tn), lambda l, j: (l, 0, j)), out_shape=jax.ShapeDtypeStruct((L, B, C), F32),
        compiler_params=_params(("parallel", "parallel")))(c_all, w_ada)


def _adamw_math(w, g, m, v):
    m = ADAM_B1 * m + (1.0 - ADAM_B1) * g
    v = ADAM_B2 * v + (1.0 - ADAM_B2) * (g * g)
    m_hat = m / (1.0 - ADAM_B1 ** ADAM_STEP)
    v_hat = v / (1.0 - ADAM_B2 ** ADAM_STEP)
    delta = -ADAM_LR * (m_hat / (jnp.sqrt(v_hat) + ADAM_EPS) + ADAM_WD * w)
    return delta, m, v


def _adamw(w, g, m, v, name):
    shape = w.shape
    C = shape[-1]
    R = w.size // C
    tr = _tile(R, 128)
    flat = lambda t: t.reshape(R, C)

    def body(w_ref, g_ref, m_ref, v_ref, d_ref, nm_ref, nv_ref):
        d, nm, nv = _adamw_math(w_ref[...], g_ref[...], m_ref[...], v_ref[...])
        d_ref[...] = d
        nm_ref[...] = nm
        nv_ref[...] = nv

    blk = pl.BlockSpec((tr, C), lambda i: (i, 0))
    os = jax.ShapeDtypeStruct((R, C), F32)
    outs = pl.pallas_call(body, name=name, grid=(R // tr,), in_specs=[blk] * 4, out_specs=[blk] * 3, out_shape=[os] * 3,
                          compiler_params=_params(("parallel",)))(flat(w), flat(g), flat(m), flat(v))
    return [t.reshape(shape) for t in outs]


def _w_ada_update(c_all, dmod, w, m, v):
    L, D, C = w.shape
    B = c_all.shape[0]
    tn = _tile(C, 256)

    def body(c_ref, dm_ref, w_ref, m_ref, v_ref, g_ref, d_ref, nm_ref, nv_ref):
        cv = c_ref[...]
        g = lax.dot_general(cv * _sig(cv), dm_ref[...], (TN, ((), ())), preferred_element_type=F32,
                            precision=lax.Precision.HIGHEST)
        d, nm, nv = _adamw_math(w_ref[...], g, m_ref[...], v_ref[...])
        g_ref[...] = g
        d_ref[...] = d
        nm_ref[...] = nm
        nv_ref[...] = nv

    blk = pl.BlockSpec((None, D, tn), lambda l, j: (l, 0, j))
    os = jax.ShapeDtypeStruct((L, D, C), F32)
    return pl.pallas_call(
        body, name="w_ada_update", grid=(L, C // tn),
        in_specs=[pl.BlockSpec((B, D), lambda l, j: (0, 0)), pl.BlockSpec((None, B, tn), lambda l, j: (l, 0, j)), blk, blk, blk],
        out_specs=[blk] * 4, out_shape=[os] * 4, compiler_params=_params(("parallel", "parallel")))(c_all, dmod, w, m, v)


def _small_update(parts, lower_bounds, wmv, D, KW):
    L = parts.shape[1]
    widths = [3 * D, D, D, KW, KW, KW]
    offs = [sum(widths[:i]) for i in range(len(widths))]

    def body(p_ref, *refs):
        ins, outs = refs[:18], refs[18:]
        tot = p_ref[0]
        for dev in range(1, N_DEV):
            tot = tot + p_ref[dev]
        grads = [tot[:, o:o + wd] for o, wd in zip(offs, widths)]
        sm = _softmax_rows(ins[9][...])
        dlb = grads[3]
        dsm = [jnp.zeros((1, KW), F32)]
        for j in range(1, L):
            acc = dlb[j:j + 1]
            for l in range(j + 1, L):
                acc = acc + dlb[l:l + 1]
            dsm.append(acc)
        dsm = jnp.concatenate(dsm, axis=0)
        grads[3] = sm * (dsm - jnp.sum(sm * dsm, axis=0, keepdims=True))
        for p in range(6):
            w_ref, m_ref, v_ref = ins[3 * p:3 * p + 3]
            d, nm, nv = _adamw_math(w_ref[...], grads[p], m_ref[...], v_ref[...])
            outs[4 * p][...] = grads[p]
            outs[4 * p + 1][...] = d
            outs[4 * p + 2][...] = nm
            outs[4 * p + 3][...] = nv

    flat = [t for trip in wmv for t in trip]
    out_shape = []
    for wd in widths:
        out_shape += [jax.ShapeDtypeStruct((L, wd), F32)] * 4
    res = pl.pallas_call(body, name="small_update", out_shape=out_shape,
                         compiler_params=pltpu.CompilerParams(vmem_limit_bytes=VMEM_LIMIT_BYTES))(parts, *flat)
    return [res[4 * p:4 * p + 4] for p in range(6)]


def _place():
    x, y, c = lax.axis_index("x"), lax.axis_index("y"), lax.axis_index("c")
    chips = [(1 - x, y), (x, 1 - y), (1 - x, 1 - y)]
    return x, y, c, chips


def _all_gather_rows(blk):
    m_per, n = blk.shape

    def body(x_ref, out_ref, send_sems, recv_sems, local_sem):
        x, y, c, chips = _place()
        me, sibling = (x, y, c), (x, y, 1 - c)

        def rows(px, py, pc):
            return out_ref.at[pl.ds((4 * px + 2 * py + pc) * m_per, m_per), :]

        def copy(k, block, to, src=None):
            return pltpu.make_async_remote_copy(src_ref=rows(*block) if src is None else src, dst_ref=rows(*block),
                                                send_sem=send_sems.at[k], recv_sem=recv_sems.at[k], device_id=to, device_id_type=MESH)

        mine = pltpu.make_async_copy(x_ref, rows(*me), local_sem)
        mine.start()
        first = [copy(0, me, sibling, src=x_ref)]
        first += [copy(1 + j, me, (*chip, c), src=x_ref) for j, chip in enumerate(chips)]
        for cp in first:
            cp.start()
        passed = [copy(4 + j, (*chip, c), sibling) for j, chip in enumerate(chips)]
        for j, chip in enumerate(chips):
            copy(1 + j, (*chip, c), me).wait_recv()
            passed[j].start()
        copy(0, sibling, me).wait_recv()
        for j, chip in enumerate(chips):
            copy(4 + j, (*chip, 1 - c), me).wait_recv()
        for cp in first + passed:
            cp.wait_send()
        mine.wait()

    return pl.pallas_call(
        body, name="all_gather_rows", out_shape=jax.ShapeDtypeStruct((N_DEV * m_per, n), blk.dtype),
        in_specs=[pl.BlockSpec(memory_space=pltpu.VMEM)], out_specs=pl.BlockSpec(memory_space=pltpu.VMEM),
        scratch_shapes=[pltpu.SemaphoreType.DMA((7,)), pltpu.SemaphoreType.DMA((7,)), pltpu.SemaphoreType.DMA],
        compiler_params=pltpu.CompilerParams(vmem_limit_bytes=VMEM_LIMIT_BYTES))(blk)


SHARD_AXIS = (1, 1, 1, 1, 0)
HALF_AXIS = (0, 0, 0, 0, 1)
N_STACKS = 5


def _cut(ref, axis, part, n):
    idx = [slice(None)] * len(ref.shape)
    idx[axis] = pl.ds(pl.multiple_of(part * n, n), n)
    return ref.at[tuple(idx)]


def _quarter(ref, t, chip):
    return _cut(ref, SHARD_AXIS[t], chip, ref.shape[SHARD_AXIS[t]] // N_CHIPS)


def _half(ref, t, core):
    return _cut(ref, HALF_AXIS[t], core, ref.shape[HALF_AXIS[t]] // 2)


def _with_axis(shape, axis, n):
    return tuple(n if a == axis else s for a, s in enumerate(shape))


def _view3(t):
    return t.reshape((1,) * (3 - t.ndim) + t.shape)


def _axis3(t, axis):
    return axis + 3 - t.ndim


def _blocked(shape3, blk, offsets=None):
    offsets = offsets or {}

    def index(b, r, c, *pre):
        idx = [b, r, c]
        for ax, fn in offsets.items():
            idx[ax] = idx[ax] + fn(*pre)
        return tuple(idx)

    return pl.BlockSpec((None,) + tuple(blk), index)


def _cast_into_full(stack, l, t, chip_idx):
    shard = stack.shape[1:]
    nd = len(shard)
    w4 = stack.reshape((stack.shape[0],) + (1,) * (3 - nd) + shard)
    ax = SHARD_AXIS[t] + 3 - nd
    _, B, R, C = w4.shape
    tr = _tile(R, 256)
    per = (R // tr, 1)[ax - 1]

    def body(j_ref, w_ref, o_ref):
        o_ref[...] = w_ref[...].astype(o_ref.dtype)

    full3 = _with_axis((B, R, C), ax, N_CHIPS * (B, R, C)[ax])
    grid_spec = pltpu.PrefetchScalarGridSpec(
        num_scalar_prefetch=1, grid=(B, R // tr, 1),
        in_specs=[pl.BlockSpec((None, None, tr, C), lambda b, r, c, j_ref: (l, b, r, c))],
        out_specs=_blocked(full3, (tr, C), {ax: lambda j_ref: j_ref[0] * per}))
    o = pl.pallas_call(body, name=f"cast_into_full{t}", grid_spec=grid_spec, out_shape=jax.ShapeDtypeStruct(full3, MXU_DT),
                       compiler_params=_params(("parallel",) * 3))(chip_idx, w4)
    return o.reshape(_with_axis(shard, SHARD_AXIS[t], N_CHIPS * shard[SHARD_AXIS[t]]))


def _remote(src, dst, send_sems, recv_sems, k, to):
    return pltpu.make_async_remote_copy(src_ref=src, dst_ref=dst, send_sem=send_sems.at[k], recv_sem=recv_sems.at[k],
                                        device_id=to, device_id_type=MESH)


def _side(inputs, out_shapes, aliases, n_sems, copies):
    def start(ins, outs, ss, rs):
        for snd, _ in copies(ins, outs, ss, rs):
            snd.start()

    def finish(ins, outs, ss, rs):
        for snd, rcv in copies(ins, outs, ss, rs):
            rcv.wait_recv()
            snd.wait_send()

    return dict(inputs=list(inputs), out_shapes=list(out_shapes), aliases=aliases, n_sems=n_sems, start=start, finish=finish)


def _run_side(side, name):
    n_in, n_out = len(side["inputs"]), len(side["out_shapes"])

    def body(*refs):
        ins, outs, sems = refs[:n_in], refs[n_in:n_in + n_out], refs[n_in + n_out:]
        side["start"](ins, outs, *sems)
        side["finish"](ins, outs, *sems)

    any_spec = pl.BlockSpec(memory_space=pl.ANY)
    return list(pl.pallas_call(
        body, name=name, out_shape=side["out_shapes"], in_specs=[any_spec] * n_in, out_specs=[any_spec] * n_out,
        input_output_aliases=side["aliases"],
        scratch_shapes=[pltpu.SemaphoreType.DMA((side["n_sems"],)), pltpu.SemaphoreType.DMA((side["n_sems"],))])(*side["inputs"]))


def _same(arrays):
    return [jax.ShapeDtypeStruct(a.shape, a.dtype) for a in arrays]


def _gather_ici_side(ws):
    def copies(ins, outs, ss, rs):
        x, y, c, chips = _place()
        out = []
        for k, (cx, cy) in enumerate(chips):
            for t in range(N_STACKS):
                mine = _half(_quarter(outs[t], t, 2 * x + y), t, c)
                theirs = _half(_quarter(outs[t], t, 2 * cx + cy), t, c)
                n = k * N_STACKS + t
                out.append((_remote(mine, mine, ss, rs, n, (cx, cy, c)), _remote(theirs, theirs, ss, rs, n, (cx, cy, c))))
        return out

    return _side(ws, _same(ws), {t: t for t in range(N_STACKS)}, 3 * N_STACKS, copies)


def _gather_d2d_side(ws):
    def copies(ins, outs, ss, rs):
        x, y, c, chips = _place()
        out = []
        for k, (cx, cy) in enumerate(chips):
            for t in range(N_STACKS):
                q = _quarter(outs[t], t, 2 * cx + cy)
                mine, theirs = _half(q, t, c), _half(q, t, 1 - c)
                n = k * N_STACKS + t
                out.append((_remote(mine, mine, ss, rs, n, (x, y, 1 - c)), _remote(theirs, theirs, ss, rs, n, (x, y, 1 - c))))
        return out

    return _side(ws, _same(ws), {t: t for t in range(N_STACKS)}, 3 * N_STACKS, copies)


def _swap_side(gs):
    shapes = [jax.ShapeDtypeStruct(_with_axis(g.shape, HALF_AXIS[t], g.shape[HALF_AXIS[t]] // 2), g.dtype) for t, g in enumerate(gs)]

    def copies(ins, outs, ss, rs):
        x, y, c, _ = _place()
        cps = [_remote(_half(ins[t], t, 1 - c), outs[t], ss, rs, t, (x, y, 1 - c)) for t in range(N_STACKS)]
        return [(cp, cp) for cp in cps]

    return _side(gs, shapes, {}, N_STACKS, copies)


def _pair_sum(g, recv, t, c_idx):
    g3, r3 = _view3(g), _view3(recv)
    hf_ax = _axis3(g, HALF_AXIS[t])
    B, R, C = r3.shape
    tr, tc = _tile(R, 256), _tile(C, 2048)
    grid = (B, R // tr, C // tc)

    def body(c_ref, g_ref, r_ref, o_ref):
        o_ref[...] = (g_ref[...] + r_ref[...]).astype(o_ref.dtype)

    grid_spec = pltpu.PrefetchScalarGridSpec(
        num_scalar_prefetch=1, grid=grid,
        in_specs=[_blocked(g3.shape, (tr, tc), {hf_ax: lambda c_ref: c_ref[0] * grid[hf_ax]}), _blocked(r3.shape, (tr, tc))],
        out_specs=_blocked(r3.shape, (tr, tc)))
    out = pl.pallas_call(body, name=f"pair_sum{t}", grid_spec=grid_spec, out_shape=jax.ShapeDtypeStruct(r3.shape, BF16),
                         compiler_params=_params(("parallel",) * 3))(c_idx, g3, r3)
    return out.reshape(recv.shape)


def _scatter_side(ps):
    shapes = [jax.ShapeDtypeStruct((N_CHIPS - 1,) + _with_axis(p.shape, SHARD_AXIS[t], p.shape[SHARD_AXIS[t]] // N_CHIPS), p.dtype)
              for t, p in enumerate(ps)]

    def copies(ins, outs, ss, rs):
        x, y, c, chips = _place()
        cps = [_remote(_quarter(ins[t], t, 2 * cx + cy), outs[t].at[k], ss, rs, k * N_STACKS + t, (cx, cy, c))
               for k, (cx, cy) in enumerate(chips) for t in range(N_STACKS)]
        return [(cp, cp) for cp in cps]

    return _side(ps, shapes, {}, 3 * N_STACKS, copies)


def _sum_chips(p, slots, t, chip_idx, c_idx):
    p3 = _view3(p)
    sh_ax, hf_ax = _axis3(p, SHARD_AXIS[t]), _axis3(p, HALF_AXIS[t])
    piece3 = _with_axis(p3.shape, sh_ax, p3.shape[sh_ax] // N_CHIPS)
    s4 = slots.reshape((N_CHIPS - 1,) + piece3)
    B, R, C = piece3
    tr = _tile(R, 256)
    grid = (B, R // tr, 1)
    full3 = _with_axis(piece3, hf_ax, 2 * piece3[hf_ax])

    def body(j_ref, c_ref, own_ref, s_ref, o_ref):
        acc = own_ref[...].astype(F32)
        for k in range(N_CHIPS - 1):
            acc = acc + s_ref[k].astype(F32)
        o_ref[...] = acc

    grid_spec = pltpu.PrefetchScalarGridSpec(
        num_scalar_prefetch=2, grid=grid,
        in_specs=[_blocked(p3.shape, (tr, C), {sh_ax: lambda j_ref, c_ref: j_ref[0] * grid[sh_ax]}),
                  pl.BlockSpec((N_CHIPS - 1, None, tr, C), lambda b, r, cc, j_ref, c_ref: (0, b, r, cc))],
        out_specs=_blocked(full3, (tr, C), {hf_ax: lambda j_ref, c_ref: c_ref[0] * grid[hf_ax]}))
    out = pl.pallas_call(body, name=f"sum_chips{t}", grid_spec=grid_spec, out_shape=jax.ShapeDtypeStruct(full3, F32),
                         compiler_params=_params(("parallel",) * 3))(chip_idx, c_idx, p3, s4)
    return out.reshape(_with_axis(slots.shape[1:], HALF_AXIS[t], 2 * slots.shape[1 + HALF_AXIS[t]]))


def _join_side(rs_):
    def copies(ins, outs, ss, rs):
        x, y, c, _ = _place()
        return [(_remote(_half(outs[t], t, c), _half(outs[t], t, c), ss, rs, t, (x, y, 1 - c)),
                 _remote(_half(outs[t], t, 1 - c), _half(outs[t], t, 1 - c), ss, rs, t, (x, y, 1 - c))) for t in range(N_STACKS)]

    return _side(rs_, _same(rs_), {t: t for t in range(N_STACKS)}, N_STACKS, copies)


def _layer_weights(shards, l, chip_idx):
    return [_cast_into_full(s, l, t, chip_idx) for t, s in enumerate(shards)]


def _gather_alone(w):
    return _run_side(_gather_d2d_side(_run_side(_gather_ici_side(w), "gather_ici")), "gather_d2d")


def _forward_layer(x, l, w, small, mods, lb_all, KW, nxt):
    norm_pre, norm_post, hgrn_norm, pool_scale = small
    shift, scale, gate = mods
    row = lambda t: t[l:l + 1]
    h = _prenorm(x, row(norm_pre), row(scale), row(shift))
    if nxt is None:
        proj = _matmul(h, w[0], "nn", F32, "proj")
    else:
        proj, nxt = _matmul(h, w[0], "nn", F32, "proj_gather", side=_gather_ici_side(nxt))
    o_a, y_a, s0 = _hgrn_fwd(proj, row(lb_all), row(hgrn_norm), KW)
    pooled, y_b = _pool_fwd(proj, w[1], row(pool_scale), KW)
    pa, pb, merged = _merge(y_a, y_b, w[2], w[3], proj)
    if nxt is None:
        out = _matmul(merged, w[4], "nn", F32, "out_proj")
    else:
        out, nxt = _matmul(merged, w[4], "nn", F32, "out_proj_gather", side=_gather_d2d_side(nxt))
    x_new = _postnorm(x, out, row(gate), row(norm_post))
    saved = (x, h, proj, o_a, y_a, s0, pooled, y_b, pa, pb, merged, out)
    return x_new, saved, nxt


def _backward_layer(dx, l, saved, w, small, mods, lb_all, KW, above, idx):
    norm_pre, norm_post, hgrn_norm, pool_scale = small
    shift, scale, gate = mods
    chip_idx, c_idx = idx
    x, h, proj, o_a, y_a, s0, pooled, y_b, pa, pb, merged, out = saved
    row = lambda t: t[l:l + 1]
    dout, d_gate, d_npost = _postnorm_bwd(dx, out, row(gate), row(norm_post))
    if above is None:
        g_out = _matmul(merged, dout, "tn", F32, "grad_w_out")
    else:
        g_out, recv = _matmul(merged, dout, "tn", F32, "grad_w_out_swap", side=_swap_side(above))
        pair = [_pair_sum(g, r, t, c_idx) for t, (g, r) in enumerate(zip(above, recv))]
    dpa, dpb, dga, dgb = _merge_bwd(dout, w[4], proj, pa, pb)
    g_pa = _matmul(y_a, dpa, "tn", F32, "grad_w_proj_a")
    g_pb = _matmul(y_b, dpb, "tn", F32, "grad_w_proj_b")
    dya = _matmul(dpa, w[2], "nt", F32, "d_y_a")
    dyb = _matmul(dpb, w[3], "nt", F32, "d_y_b")
    do, dza, d_hn = _gate_a_bwd(dya, o_a, proj, row(hgrn_norm), KW)
    dq, df, dva, d_lb = _hgrn_bwd(proj, do, s0, row(lb_all), KW)
    u, dzb, d_ps, g_pool = _pool_bwd(dyb, proj, pooled, w[1], row(pool_scale), KW)
    dvb = _pool_bwd_window(u, KW)
    dproj = [dq, df, dva, dza, dvb, dzb, dga, dgb]
    g_in = lax.empty(w[0].shape, F32)
    col = 0
    dproj_k = [[dq, df], [dva, dza], [dvb, dzb], dga, dgb]
    if above is None:
        dh = _matmul(dproj_k, w[0], "nt", F32, "d_h", tm=512)
        reduced = None
        for piece in dproj:
            g_in = _matmul(h, piece, "tn", F32, "grad_w_in_piece", into=(g_in, col))
            col += piece.shape[1]
    else:
        dh, slots = _matmul(dproj_k, w[0], "nt", F32, "d_h_scatter", tm=512, side=_scatter_side(pair))
        part = [_sum_chips(p, s, t, chip_idx, c_idx) for t, (p, s) in enumerate(zip(pair, slots))]
        for n, piece in enumerate(dproj):
            if n == 0:
                g_in, reduced = _matmul(h, piece, "tn", F32, "grad_w_in_piece_join", into=(g_in, col), side=_join_side(part))
            else:
                g_in = _matmul(h, piece, "tn", F32, "grad_w_in_piece", into=(g_in, col))
            col += piece.shape[1]
    dx_new, d_shift, d_scale, d_npre = _prenorm_bwd(dh, x, dx, row(norm_pre), row(scale))
    small_g = jnp.concatenate([d_shift, d_scale, d_gate, d_npre, d_npost, d_lb, d_hn, d_ps], axis=1)
    return dx_new, small_g, [g_in, g_pool, g_pa, g_pb, g_out], reduced


def kernel(x, c, w_ada, b_ada, norm_pre, norm_post, w_in, lower_bounds, hgrn_norm, pool_w, pool_scale, w_proj_a, w_proj_b, w_out, loss_target, m_w_ada, m_b_ada, m_norm_pre, m_norm_post, m_w_in, m_lower_bounds, m_hgrn_norm, m_pool_w, m_pool_scale, m_w_proj_a, m_w_proj_b, m_w_out, v_w_ada, v_b_ada, v_norm_pre, v_norm_post, v_w_in, v_lower_bounds, v_hgrn_norm, v_pool_w, v_pool_scale, v_w_proj_a, v_w_proj_b, v_w_out):
    _, S, D = x.shape
    L = w_in.shape[0]
    KW = D // 2
    xi, yi, ci = lax.axis_index("x"), lax.axis_index("y"), lax.axis_index("c")
    chip = 2 * xi + yi
    dev = 4 * xi + 2 * yi + ci

    c_all = _all_gather_rows(c.reshape(8, D // 8)).reshape(N_DEV, D)
    modp = _mod_part(c_all, w_ada)
    cols = modp.shape[-1]
    modg = _all_gather_rows(modp.reshape(L * N_DEV, cols)).reshape(N_DEV, L, N_DEV, cols)
    mod_all = jnp.transpose(modg[0::2], (1, 2, 0, 3)).reshape(L, N_DEV, 3 * D)
    mod = lax.dynamic_index_in_dim(mod_all, dev, axis=1, keepdims=False) + b_ada
    mods = (mod[:, :D], mod[:, D:2 * D], mod[:, 2 * D:])
    lb_all = _lb_table(lower_bounds)

    shards = [w_in, pool_w, w_proj_a, w_proj_b, w_out]
    chip_idx = chip.astype(jnp.int32).reshape(1)
    c_idx = ci.astype(jnp.int32).reshape(1)
    small = (norm_pre, norm_post, hgrn_norm, pool_scale)

    xs = x[0]
    saved, ws = [], []
    w = _gather_alone(_layer_weights(shards, 0, chip_idx))
    for l in range(L):
        nxt = _layer_weights(shards, l + 1, chip_idx) if l + 1 < L else None
        xs, sv, nxt = _forward_layer(xs, l, w, small, mods, lb_all, KW, nxt)
        saved.append(sv)
        ws.append(w)
        w = nxt
    dx, sq = _loss_head(xs, loss_target[0])
    loss = lax.psum(0.5 * jnp.sum(sq) / D, ("x", "y", "c"))

    small_g, reduced = [None] * L, [None] * L
    above = None
    for l in reversed(range(L)):
        dx, small_g[l], above, red = _backward_layer(dx, l, saved[l], ws[l], small, mods, lb_all, KW, above, (chip_idx, c_idx))
        if red is not None:
            reduced[l + 1] = red
    recv = _run_side(_swap_side(above), "swap")
    pair = [_pair_sum(g, r, t, c_idx) for t, (g, r) in enumerate(zip(above, recv))]
    slots = _run_side(_scatter_side(pair), "scatter")
    reduced[0] = _run_side(_join_side([_sum_chips(p, s, t, chip_idx, c_idx) for t, (p, s) in enumerate(zip(pair, slots))]), "join")
    grad_x = dx[None]

    sg = jnp.concatenate(small_g, axis=0)
    sg = jnp.concatenate([sg, jnp.zeros((8 - L, sg.shape[1]), F32)], axis=0)
    parts = _all_gather_rows(sg).reshape(N_DEV, 8, sg.shape[1])[:, :L]
    wmv = [(b_ada, m_b_ada, v_b_ada), (norm_pre, m_norm_pre, v_norm_pre), (norm_post, m_norm_post, v_norm_post),
           (lower_bounds, m_lower_bounds, v_lower_bounds), (hgrn_norm, m_hgrn_norm, v_hgrn_norm), (pool_scale, m_pool_scale, v_pool_scale)]
    r_b_ada, r_npre, r_npost, r_lb, r_hn, r_ps = _small_update(parts, lower_bounds, wmv, D, KW)
    dmod = lax.dynamic_slice_in_dim(parts[:, :, :3 * D], chip * cols, cols, axis=2)
    r_w_ada = _w_ada_update(c_all, jnp.transpose(dmod, (1, 0, 2)), w_ada, m_w_ada, v_w_ada)

    grads = [jnp.stack([reduced[l][t] for l in range(L)], axis=0) for t in range(N_STACKS)]
    ms = [m_w_in, m_pool_w, m_w_proj_a, m_w_proj_b, m_w_out]
    vs = [v_w_in, v_pool_w, v_w_proj_a, v_w_proj_b, v_w_out]
    r_big = [[g] + _adamw(w, g, m, v, f"adamw{t}") for t, (w, g, m, v) in enumerate(zip(shards, grads, ms, vs))]
    r_w_in, r_pool_w, r_pa, r_pb, r_w_out = r_big

    order = [r_w_ada, r_b_ada, r_npre, r_npost, r_w_in, r_lb, r_hn, r_pool_w, r_ps, r_pa, r_pb, r_w_out]
    outs = [loss, grad_x]
    for k in range(4):
        outs += [r[k] for r in order]
    return tuple(outs)
```

```python
import functools

import jax
import jax.numpy as jnp
from jax import lax
from jax.experimental import pallas as pl
from jax.experimental.pallas import tpu as pltpu

F32 = jnp.float32
BF16 = jnp.bfloat16
MXU_DT = jnp.bfloat16

CHUNK = 64
SUBLANES = 8
SUB = 8
LEVELS = (32, 16, 8)
HEAD = 128
EPS = 1e-6
MIN_FORGET = 1e-30
WINDOWS = (2, 4, 8, 16)
HALO = 16
N_CHIPS = 4
N_DEV = 8
VMEM_LIMIT_BYTES = 56 * 1024 * 1024

ADAM_LR = 0.001
ADAM_B1 = 0.9
ADAM_B2 = 0.999
ADAM_EPS = 1e-08
ADAM_WD = 0.01
ADAM_STEP = 10

NN = ((1,), (0,))
NT = ((1,), (1,))
TN = ((0,), (0,))
MESH = pl.DeviceIdType.MESH


def _params(sem):
    return pltpu.CompilerParams(dimension_semantics=sem, vmem_limit_bytes=VMEM_LIMIT_BYTES)


def _tile(n, pref):
    return pref if n % pref == 0 else n


def _dot(a, b, dims):
    return lax.dot_general(a.astype(MXU_DT), b.astype(MXU_DT), (dims, ((), ())), preferred_element_type=F32)


def _sig_pair(a):
    e = jnp.exp(-jnp.abs(a))
    inv = 1.0 / (1.0 + e)
    pos = a >= 0
    return jnp.where(pos, inv, e * inv), jnp.where(pos, e * inv, inv)


def _sig(a):
    return 0.5 * jnp.tanh(0.5 * a) + 0.5


def _split_dot(tri, x):
    def top(y):
        return lax.bitcast_convert_type(lax.bitcast_convert_type(y, jnp.uint32) & jnp.uint32(0xFFFF0000), F32)

    hi = top(x)
    r1 = x - hi
    mid = top(r1)
    lo = r1 - mid
    d = lambda y: jnp.dot(tri, y.astype(BF16), preferred_element_type=F32)
    return d(hi) + d(mid) + d(lo)


def _rms(xv):
    return lax.rsqrt(jnp.mean(xv * xv, axis=-1, keepdims=True) + EPS)


def _prenorm(x, g, scale, shift):
    S, D = x.shape
    tm = _tile(S, 256)

    def body(x_ref, g_ref, sc_ref, sh_ref, h_ref):
        xv = x_ref[...]
        h = (xv * _rms(xv)) * g_ref[...] * (1.0 + sc_ref[...]) + sh_ref[...]
        h_ref[...] = h.astype(h_ref.dtype)

    row = pl.BlockSpec((tm, D), lambda i: (i, 0))
    vec = pl.BlockSpec((1, D), lambda i: (0, 0))
    return pl.pallas_call(body, name="prenorm", grid=(S // tm,), in_specs=[row, vec, vec, vec], out_specs=row,
                          out_shape=jax.ShapeDtypeStruct((S, D), MXU_DT), compiler_params=_params(("parallel",)))(x, g, scale, shift)


def _prenorm_bwd(dh, x, dx, g, scale):
    S, D = x.shape
    tm = _tile(S, 256)

    def body(dh_ref, x_ref, dx_ref, g_ref, sc_ref, o_ref, dsh_ref, dsc_ref, dg_ref):
        @pl.when(pl.program_id(0) == 0)
        def _():
            dsh_ref[...] = jnp.zeros_like(dsh_ref)
            dsc_ref[...] = jnp.zeros_like(dsc_ref)
            dg_ref[...] = jnp.zeros_like(dg_ref)

        xv = x_ref[...]
        r = _rms(xv)
        xh = xv * r
        dh = dh_ref[...]
        gv = g_ref[...]
        one_sc = 1.0 + sc_ref[...]
        dsh_ref[...] += jnp.sum(dh, axis=0, keepdims=True)
        dsc_ref[...] += jnp.sum(dh * xh * gv, axis=0, keepdims=True)
        dg_ref[...] += jnp.sum(dh * xh * one_sc, axis=0, keepdims=True)
        dxh = dh * gv * one_sc
        o_ref[...] = dx_ref[...] + r * (dxh - xh * jnp.mean(dxh * xh, axis=-1, keepdims=True))

    row = pl.BlockSpec((tm, D), lambda i: (i, 0))
    vec = pl.BlockSpec((1, D), lambda i: (0, 0))
    vs = jax.ShapeDtypeStruct((1, D), F32)
    return pl.pallas_call(body, name="prenorm_bwd", grid=(S // tm,), in_specs=[row, row, row, vec, vec],
                          out_specs=[row, vec, vec, vec], out_shape=[jax.ShapeDtypeStruct((S, D), F32), vs, vs, vs],
                          compiler_params=_params(("arbitrary",)))(dh, x, dx, g, scale)


def _postnorm(x, out, gate, g):
    S, D = x.shape
    tm = _tile(S, 256)

    def body(x_ref, o_ref, gt_ref, g_ref, y_ref):
        ov = o_ref[...]
        y_ref[...] = x_ref[...] + gt_ref[...] * ((ov * _rms(ov)) * g_ref[...])

    row = pl.BlockSpec((tm, D), lambda i: (i, 0))
    vec = pl.BlockSpec((1, D), lambda i: (0, 0))
    return pl.pallas_call(body, name="postnorm", grid=(S // tm,), in_specs=[row, row, vec, vec], out_specs=row,
                          out_shape=jax.ShapeDtypeStruct((S, D), F32), compiler_params=_params(("parallel",)))(x, out, gate, g)


def _postnorm_bwd(dx, out, gate, g):
    S, D = dx.shape
    tm = _tile(S, 256)

    def body(dx_ref, o_ref, gt_ref, g_ref, do_ref, dgt_ref, dg_ref):
        @pl.when(pl.program_id(0) == 0)
        def _():
            dgt_ref[...] = jnp.zeros_like(dgt_ref)
            dg_ref[...] = jnp.zeros_like(dg_ref)

        ov = o_ref[...]
        r = _rms(ov)
        rn = ov * r
        dxv = dx_ref[...]
        gv = g_ref[...]
        dgt_ref[...] += jnp.sum(dxv * rn * gv, axis=0, keepdims=True)
        du = dxv * gt_ref[...]
        dg_ref[...] += jnp.sum(du * rn, axis=0, keepdims=True)
        drn = du * gv
        do_ref[...] = (r * (drn - rn * jnp.mean(drn * rn, axis=-1, keepdims=True))).astype(do_ref.dtype)

    row = pl.BlockSpec((tm, D), lambda i: (i, 0))
    vec = pl.BlockSpec((1, D), lambda i: (0, 0))
    vs = jax.ShapeDtypeStruct((1, D), F32)
    return pl.pallas_call(body, name="postnorm_bwd", grid=(S // tm,), in_specs=[row, row, vec, vec],
                          out_specs=[row, vec, vec], out_shape=[jax.ShapeDtypeStruct((S, D), MXU_DT), vs, vs],
                          compiler_params=_params(("arbitrary",)))(dx, out, gate, g)


def _loss_head(y, target):
    S, D = y.shape
    tm = _tile(S, 256)

    def body(y_ref, t_ref, dy_ref, sq_ref):
        @pl.when(pl.program_id(0) == 0)
        def _():
            sq_ref[...] = jnp.zeros_like(sq_ref)

        e = y_ref[...] - t_ref[...]
        dy_ref[...] = e * (1.0 / D)
        sq_ref[...] += jnp.sum(e * e, axis=0, keepdims=True)

    row = pl.BlockSpec((tm, D), lambda i: (i, 0))
    vec = pl.BlockSpec((1, D), lambda i: (0, 0))
    return pl.pallas_call(body, name="loss_head", grid=(S // tm,), in_specs=[row, row], out_specs=[row, vec],
                          out_shape=[jax.ShapeDtypeStruct((S, D), F32), jax.ShapeDtypeStruct((1, D), F32)],
                          compiler_params=_params(("arbitrary",)))(y, target)


def _matmul(a, b, mode, out_dtype, name, *, side=None, into=None, tm=1024, tn=1024, tk=2048):
    groups = [list(g) if isinstance(g, (list, tuple)) else [g] for g in (a if isinstance(a, (list, tuple)) else [a])]
    pieces = [p for g in groups for p in g]
    na = len(pieces)
    if mode == "tn":
        K, M = pieces[0].shape
        gwidths = [K]
    else:
        M = pieces[0].shape[0]
        gwidths = [sum(p.shape[1] for p in g) for g in groups]
        K = sum(gwidths)
    N = b.shape[0] if mode == "nt" else b.shape[1]
    tm, tn, tk = _tile(M, tm), _tile(N, tn), _tile(min(gwidths), tk)
    ni, nj, nk = M // tm, N // tn, K // tk
    dims = {"nn": NN, "nt": NT, "tn": TN}[mode]

    a_specs, ranges, k0 = [], [], 0
    for g, wd in zip(groups, gwidths):
        n = wd // tk
        if mode == "tn":
            a_specs.append(pl.BlockSpec((tk, tm), lambda i, j, k: (k, i)))
        elif len(g) == 1:
            a_specs.append(pl.BlockSpec((tm, tk), lambda i, j, k, k0=k0, n=n: (i, jnp.clip(k - k0, 0, n - 1))))
        else:
            a_specs += [pl.BlockSpec((tm, p.shape[1]), lambda i, j, k: (i, 0)) for p in g]
        ranges.append((k0, k0 + n))
        k0 += n
    if mode == "nt":
        b_spec = pl.BlockSpec((tn, tk), lambda i, j, k: (j, k))
    else:
        b_spec = pl.BlockSpec((tk, tn), lambda i, j, k: (k, j))
    n_in = len(side["inputs"]) if side else 0
    n_out = len(side["out_shapes"]) if side else 0
    n_buf = 1 if into else 0

    def body(*refs):
        a_refs, b_ref, rest = refs[:na], refs[na], refs[na + 1:]
        side_in, o_ref = rest[:n_in], rest[n_in + n_buf]
        side_out = rest[n_in + n_buf + 1:n_in + n_buf + 1 + n_out]
        scratch = rest[n_in + n_buf + 1 + n_out:]
        i, j, k = pl.program_id(0), pl.program_id(1), pl.program_id(2)
        if side:
            sems = scratch[-2:]

            @pl.when((i == 0) & (j == 0) & (k == 0))
            def _():
                side["start"](side_in, side_out, *sems)

        def accumulate(g_refs):
            p, off = None, 0
            for a_ref in g_refs:
                wd = a_ref.shape[0] if mode == "tn" else a_ref.shape[1]
                bv = b_ref[...] if len(g_refs) == 1 else (b_ref[:, off:off + wd] if mode == "nt" else b_ref[off:off + wd, :])
                q = lax.dot_general(a_ref[...], bv, (dims, ((), ())), preferred_element_type=F32)
                p = q if p is None else p + q
                off += wd
            if nk == 1:
                o_ref[...] = p.astype(o_ref.dtype)
            else:
                @pl.when(k == 0)
                def _():
                    o_ref[...] = p

                @pl.when(k > 0)
                def _():
                    o_ref[...] += p

        if len(groups) == 1:
            accumulate(a_refs)
        else:
            first = 0
            for (lo, hi), g in zip(ranges, groups):
                pl.when((k >= lo) & (k < hi))(functools.partial(accumulate, a_refs[first:first + len(g)]))
                first += len(g)

        if side:
            @pl.when((i == ni - 1) & (j == nj - 1) & (k == nk - 1))
            def _():
                side["finish"](side_in, side_out, *sems)

    any_spec = pl.BlockSpec(memory_space=pl.ANY)
    assert nk == 1 or out_dtype == F32, "a K loop accumulates in the output block"
    scratch_shapes = []
    if side:
        scratch_shapes += [pltpu.SemaphoreType.DMA((side["n_sems"],)), pltpu.SemaphoreType.DMA((side["n_sems"],))]
    aliases = {na + 1 + s: 1 + d for s, d in side["aliases"].items()} if side else {}
    args = pieces + [b] + (list(side["inputs"]) if side else [])
    if into:
        buf, col = into
        aliases[len(args)] = 0
        args.append(buf)
        j0 = col // tn
        out_spec, out_shape = pl.BlockSpec((tm, tn), lambda i, j, k: (i, j0 + j)), jax.ShapeDtypeStruct(buf.shape, buf.dtype)
    else:
        out_spec, out_shape = pl.BlockSpec((tm, tn), lambda i, j, k: (i, j)), jax.ShapeDtypeStruct((M, N), out_dtype)
    res = pl.pallas_call(
        body, name=name, grid=(ni, nj, nk), in_specs=a_specs + [b_spec] + [any_spec] * (n_in + n_buf),
        out_specs=[out_spec] + [any_spec] * n_out,
        out_shape=[out_shape] + (list(side["out_shapes"]) if side else []),
        scratch_shapes=scratch_shapes, input_output_aliases=aliases,
        compiler_params=_params(("arbitrary",) * 3 if side else ("parallel", "parallel", "arbitrary")))(*args)
    return (res[0], list(res[1:])) if side else res[0]


def _merge(ya, yb, wpa, wpb, proj):
    S, KW = ya.shape
    D = 2 * KW
    tm, tn = _tile(S, 1024), _tile(D, 512)
    ga0, gb0 = 3 * D // tn, 4 * D // tn

    def body(ya_ref, yb_ref, wa_ref, wb_ref, ga_ref, gb_ref, pa_ref, pb_ref, m_ref):
        pa = jnp.dot(ya_ref[...], wa_ref[...], preferred_element_type=F32)
        pb = jnp.dot(yb_ref[...], wb_ref[...], preferred_element_type=F32)
        pa_ref[...] = pa.astype(pa_ref.dtype)
        pb_ref[...] = pb.astype(pb_ref.dtype)
        m_ref[...] = (_sig(ga_ref[...]) * pa + _sig(gb_ref[...]) * pb).astype(m_ref.dtype)

    y_spec = pl.BlockSpec((tm, KW), lambda i, j: (i, 0))
    w_spec = pl.BlockSpec((KW, tn), lambda i, j: (0, j))
    o_spec = pl.BlockSpec((tm, tn), lambda i, j: (i, j))
    return pl.pallas_call(
        body, name="merge", grid=(S // tm, D // tn),
        in_specs=[y_spec, y_spec, w_spec, w_spec, pl.BlockSpec((tm, tn), lambda i, j: (i, ga0 + j)),
                  pl.BlockSpec((tm, tn), lambda i, j: (i, gb0 + j))],
        out_specs=[o_spec, o_spec, o_spec],
        out_shape=[jax.ShapeDtypeStruct((S, D), MXU_DT), jax.ShapeDtypeStruct((S, D), MXU_DT), jax.ShapeDtypeStruct((S, D), MXU_DT)],
        compiler_params=_params(("parallel", "parallel")))(ya, yb, wpa, wpb, proj, proj)


def _merge_bwd(dout, w_out, proj, pa, pb):
    S, D = dout.shape
    tm, tn = _tile(S, 1024), _tile(D, 512)
    ga0, gb0 = 3 * D // tn, 4 * D // tn

    def body(do_ref, w_ref, ga_ref, gb_ref, pa_ref, pb_ref, dpa_ref, dpb_ref, dga_ref, dgb_ref):
        dm = lax.dot_general(do_ref[...], w_ref[...], (NT, ((), ())), preferred_element_type=F32)
        sa, sb = _sig(ga_ref[...]), _sig(gb_ref[...])
        sna, snb = 1.0 - sa, 1.0 - sb
        dpa = dm * sa
        dpb = dm * sb
        dpa_ref[...] = dpa.astype(dpa_ref.dtype)
        dpb_ref[...] = dpb.astype(dpb_ref.dtype)
        dga_ref[...] = (dpa * pa_ref[...].astype(F32) * sna).astype(dga_ref.dtype)
        dgb_ref[...] = (dpb * pb_ref[...].astype(F32) * snb).astype(dgb_ref.dtype)

    blk = pl.BlockSpec((tm, tn), lambda i, j: (i, j))
    os = jax.ShapeDtypeStruct((S, D), MXU_DT)
    return pl.pallas_call(
        body, name="merge_bwd", grid=(S // tm, D // tn),
        in_specs=[pl.BlockSpec((tm, D), lambda i, j: (i, 0)), pl.BlockSpec((tn, D), lambda i, j: (j, 0)),
                  pl.BlockSpec((tm, tn), lambda i, j: (i, ga0 + j)), pl.BlockSpec((tm, tn), lambda i, j: (i, gb0 + j)), blk, blk],
        out_specs=[blk, blk, blk, blk], out_shape=[os, os, os, os],
        compiler_params=_params(("parallel", "parallel")))(dout, w_out, proj, proj, pa, pb)


def _gates(qr, a, lbv):
    sq = _sig(qr)
    q = qr * sq
    s, sn = _sig_pair(a)
    omlb = 1.0 - lbv
    f = lbv + omlb * s
    logf = jnp.log(jnp.maximum(f, MIN_FORGET))
    kk = omlb * sn
    return sq, q, s, sn, omlb, f, logf, kk


def _shift_in_block(x, d):
    return pltpu.roll(x.reshape(CHUNK // SUBLANES, SUBLANES, HEAD), d, axis=1).reshape(CHUNK, HEAD)


def _level_masks(h):
    ti = lax.broadcasted_iota(jnp.int32, (CHUNK, 1), 0)
    si = lax.broadcasted_iota(jnp.int32, (1, CHUNK), 1)
    t2, s2 = (ti & (2 * h - 1)) >= h, (si & (2 * h - 1)) >= h
    same = (ti & -(2 * h)) == (si & -(2 * h))
    return t2, jnp.where(same & t2 & jnp.logical_not(s2), 1.0, 0.0)


def _level_factor(b, h):
    parts = [jnp.broadcast_to(b[g + h - 1:g + h], (2 * h, HEAD)) for g in range(0, CHUNK, 2 * h)]
    ref = parts[0] if len(parts) == 1 else jnp.concatenate(parts, axis=0)
    return jnp.exp(-jnp.abs(b - ref))


def _level_operands(q, kk, b, h, t2):
    fac = _level_factor(b, h)
    return jnp.where(t2, q * fac, 0.0), jnp.where(t2, 0.0, kk * fac), fac


def _heads_per_step(H, pref):
    return pref if H % pref == 0 else 1


def _tri(lower):
    r = lax.broadcasted_iota(jnp.int32, (CHUNK, CHUNK), 0)
    c = lax.broadcasted_iota(jnp.int32, (CHUNK, CHUNK), 1)
    return jnp.where((r >= c) if lower else (c >= r), 1.0, 0.0).astype(BF16)


def _hgrn_fwd(proj, lb, hn, KW):
    S = proj.shape[0]
    H = KW // HEAD
    T = _tile(S, 512)
    nci, nb = T // CHUNK, S // T
    HP = _heads_per_step(H, 8)

    def body(q_ref, f_ref, v_ref, z_ref, lb_ref, hn_ref, o_ref, y_ref, s0_ref, st_ref):
        @pl.when(pl.program_id(1) == 0)
        def _():
            st_ref[...] = jnp.zeros_like(st_ref)

        tril = _tri(True)
        rmod = lax.broadcasted_iota(jnp.int32, (CHUNK, 1), 0) & (SUB - 1)
        masks = [_level_masks(h) for h in LEVELS]
        lb_all = lb_ref[...]

        def chunk(ci, carry):
            rows = pl.ds(pl.multiple_of(ci * CHUNK, CHUNK), CHUNK)
            _, q_all, _, _, _, _, logf_all, kk_all = _gates(q_ref[rows, :], f_ref[rows, :], lb_all)
            b_all = _split_dot(tril, logf_all)
            for hp in range(HP):
                cols = slice(hp * HEAD, (hp + 1) * HEAD)
                one_head(ci, rows, hp, cols, q_all[:, cols], kk_all[:, cols], b_all[:, cols])
            return carry

        def one_head(ci, rows, hp, cols, q, kk, b):
            hnv = hn_ref[:, cols]
            vv = v_ref[rows, cols]
            eb = jnp.exp(b)
            st = st_ref[hp]
            s0_ref[ci, hp] = st
            p = None
            for h, (t2, m) in zip(LEVELS, masks):
                qs, ks, _ = _level_operands(q, kk, b, h, t2)
                pm = m * _dot(qs, ks, NT)
                p = pm if p is None else p + pm
            o = _dot(q * eb, st, NT) + _dot(p, vv, NN) + jnp.sum(q * kk, axis=-1, keepdims=True) * vv
            for d in range(1, SUB):
                kd_, bd_, vd_ = _shift_in_block(kk, d), _shift_in_block(b, d), _shift_in_block(vv, d)
                e = jnp.exp(jnp.minimum(b - bd_, 0.0))
                p = jnp.where(rmod >= d, jnp.sum(q * (kd_ * e), axis=-1, keepdims=True), 0.0)
                o = o + p * vd_
            bl = b[CHUNK - 1:CHUNK]
            st_ref[hp] = st * eb[CHUNK - 1:CHUNK] + _dot(vv, kk * jnp.exp(bl - b), TN)
            o_ref[rows, cols] = o
            z = z_ref[rows, cols]
            y_ref[rows, cols] = ((o * _rms(o)) * hnv * (z * _sig(z))).astype(y_ref.dtype)

        lax.fori_loop(0, nci, chunk, 0)

    W = HP * HEAD
    G = H // HP

    def col(off):
        return pl.BlockSpec((T, W), lambda h, t: (t, off + h))

    vec = pl.BlockSpec((1, W), lambda h, t: (0, h))
    return pl.pallas_call(
        body, name="hgrn_fwd", grid=(G, nb),
        in_specs=[col(0), col(G), col(2 * G), col(3 * G), vec, vec],
        out_specs=[pl.BlockSpec((T, W), lambda h, t: (t, h)), pl.BlockSpec((T, W), lambda h, t: (t, h)),
                   pl.BlockSpec((nci, HP, HEAD, HEAD), lambda h, t: (t, h, 0, 0))],
        out_shape=[jax.ShapeDtypeStruct((S, KW), F32), jax.ShapeDtypeStruct((S, KW), MXU_DT),
                   jax.ShapeDtypeStruct((S // CHUNK, H, HEAD, HEAD), F32)],
        scratch_shapes=[pltpu.VMEM((HP, HEAD, HEAD), F32)],
        compiler_params=_params(("parallel", "arbitrary")))(proj, proj, proj, proj, lb, hn)


def _gate_a_bwd(dya, o, proj, hn, KW):
    S = o.shape[0]
    H = KW // HEAD
    T = _tile(S, 512)

    def body(dy_ref, o_ref, z_ref, hn_ref, do_ref, dz_ref, dhn_ref):
        @pl.when(pl.program_id(0) == 0)
        def _():
            dhn_ref[...] = jnp.zeros_like(dhn_ref)

        for h in range(H):
            cols = slice(h * HEAD, (h + 1) * HEAD)
            ov = o_ref[:, cols]
            r = _rms(ov)
            rn = ov * r
            z = z_ref[:, cols]
            sz = _sig(z)
            silu = z * sz
            dy = dy_ref[:, cols]
            hnv = hn_ref[:, cols]
            dz_ref[:, cols] = (dy * rn * hnv * (sz * (1.0 + z * (1.0 - sz)))).astype(dz_ref.dtype)
            dhn_ref[:, cols] += jnp.sum(dy * rn * silu, axis=0, keepdims=True)
            drn = dy * hnv * silu
            do_ref[:, cols] = r * (drn - rn * jnp.mean(drn * rn, axis=-1, keepdims=True))

    blk = pl.BlockSpec((T, KW), lambda t: (t, 0))
    vec = pl.BlockSpec((1, KW), lambda t: (0, 0))
    return pl.pallas_call(
        body, name="gate_a_bwd", grid=(S // T,),
        in_specs=[blk, blk, pl.BlockSpec((T, KW), lambda t: (t, 3)), vec],
        out_specs=[blk, blk, vec],
        out_shape=[jax.ShapeDtypeStruct((S, KW), F32), jax.ShapeDtypeStruct((S, KW), MXU_DT), jax.ShapeDtypeStruct((1, KW), F32)],
        compiler_params=_params(("arbitrary",)))(dya, o, proj, hn)


def _hgrn_bwd(proj, do, s0, lb, KW):
    S = proj.shape[0]
    H = KW // HEAD
    T = _tile(S, 512)
    nci, nb = T // CHUNK, S // T
    HP = _heads_per_step(H, 8)

    def body(q_ref, f_ref, v_ref, do_ref, s0_ref, lb_ref, dq_ref, df_ref, dv_ref, dlb_ref, dst_ref):
        @pl.when(pl.program_id(1) == 0)
        def _():
            dst_ref[...] = jnp.zeros_like(dst_ref)
            dlb_ref[...] = jnp.zeros_like(dlb_ref)

        tril, triu = _tri(True), _tri(False)
        rmod = lax.broadcasted_iota(jnp.int32, (CHUNK, 1), 0) & (SUB - 1)
        masks = [_level_masks(h) for h in LEVELS]
        lb_all = lb_ref[...]

        def chunk(it, carry):
            ci = nci - 1 - it
            rows = pl.ds(pl.multiple_of(ci * CHUNK, CHUNK), CHUNK)
            qr_all = q_ref[rows, :]
            sq_all, q_all, s_all, sn_all, omlb, f_all, logf_all, kk_all = _gates(qr_all, f_ref[rows, :], lb_all)
            b_all = _split_dot(tril, logf_all)
            res = []
            for hp in range(HP):
                cols = slice(hp * HEAD, (hp + 1) * HEAD)
                res.append(one_head(ci, rows, hp, cols, q_all[:, cols], kk_all[:, cols], b_all[:, cols]))
            dq_t, dk_t, dv_t = (jnp.concatenate([r[i] for r in res], axis=1) for i in range(3))
            w = jnp.concatenate([r[3] for r in res], axis=1)
            dlogf = _split_dot(triu, q_all * dq_t - kk_all * dk_t) + w
            dlf = jnp.where(f_all > MIN_FORGET, dlogf / jnp.maximum(f_all, MIN_FORGET), 0.0)
            t1 = dlf - dk_t
            dlb_ref[...] += jnp.sum(sn_all * t1, axis=0, keepdims=True)
            df_ref[rows, :] = (omlb * (s_all * sn_all) * t1).astype(df_ref.dtype)
            dq_ref[rows, :] = (dq_t * (sq_all * (1.0 + qr_all * (1.0 - sq_all)))).astype(dq_ref.dtype)
            dv_ref[rows, :] = dv_t.astype(dv_ref.dtype)
            return carry

        def one_head(ci, rows, hp, cols, q, kk, b):
            vv, dov = v_ref[rows, cols], do_ref[rows, cols]
            eb = jnp.exp(b)
            st0 = s0_ref[ci, hp]
            dst = dst_ref[hp]
            bl, ec = b[CHUNK - 1:CHUNK], eb[CHUNK - 1:CHUNK]
            decl = jnp.exp(bl - b)
            kdec = kk * decl
            dq_t = eb * _dot(dov, st0, NN)
            dst_ref[hp] = dst * ec + _dot(dov, q * eb, TN)
            dv_t = _dot(kdec, dst, NT)
            dk_t = decl * _dot(vv, dst, NN)
            stc = st0 * ec + _dot(vv, kdec, TN)
            w = jnp.sum(stc * dst, axis=0, keepdims=True)
            dp_all = _dot(dov, vv, NT)
            p = None
            for h, (t2, m) in zip(LEVELS, masks):
                qs, ks, fac = _level_operands(q, kk, b, h, t2)
                pm = m * _dot(qs, ks, NT)
                p = pm if p is None else p + pm
                dpm = m * dp_all
                dq_t = dq_t + fac * _dot(dpm, ks, NN)
                dk_t = dk_t + fac * _dot(dpm, qs, TN)
            p0 = jnp.sum(q * kk, axis=-1, keepdims=True)
            dp0 = jnp.sum(dov * vv, axis=-1, keepdims=True)
            dq_t = dq_t + dp0 * kk
            dk_t = dk_t + dp0 * q
            dv_t = dv_t + _dot(p, dov, TN) + p0 * dov
            for d in range(1, SUB):
                kd_, bd_, vd_ = _shift_in_block(kk, d), _shift_in_block(b, d), _shift_in_block(vv, d)
                e = jnp.exp(jnp.minimum(b - bd_, 0.0))
                ke = kd_ * e
                m = rmod >= d
                p = jnp.where(m, jnp.sum(q * ke, axis=-1, keepdims=True), 0.0)
                dp = jnp.where(m, jnp.sum(dov * vd_, axis=-1, keepdims=True), 0.0)
                dq_t = dq_t + dp * ke
                dk_t = dk_t + _shift_in_block(dp * (q * e), SUBLANES - d)
                dv_t = dv_t + _shift_in_block(p * dov, SUBLANES - d)
            return dq_t, dk_t, dv_t, w

        lax.fori_loop(0, nci, chunk, 0)

    W = HP * HEAD
    G = H // HP

    def col(off):
        return pl.BlockSpec((T, W), lambda h, t: (nb - 1 - t, off + h))

    blk = pl.BlockSpec((T, W), lambda h, t: (nb - 1 - t, h))
    vec = pl.BlockSpec((1, W), lambda h, t: (0, h))
    os = jax.ShapeDtypeStruct((S, KW), MXU_DT)
    return pl.pallas_call(
        body, name="hgrn_bwd", grid=(G, nb),
        in_specs=[col(0), col(G), col(2 * G), blk, pl.BlockSpec((nci, HP, HEAD, HEAD), lambda h, t: (nb - 1 - t, h, 0, 0)), vec],
        out_specs=[blk, blk, blk, vec], out_shape=[os, os, os, jax.ShapeDtypeStruct((1, KW), F32)],
        scratch_shapes=[pltpu.VMEM((HP, HEAD, HEAD), F32)],
        compiler_params=_params(("parallel", "arbitrary")))(proj, proj, proj, do, s0, lb)


def _pool_pos(i, T):
    return (i * T + lax.broadcasted_iota(jnp.int32, (T, 1), 0) + 1).astype(F32)


def _pool_fwd(proj, pool_w, pool_scale, KW):
    S = proj.shape[0]
    GW = KW // len(WINDOWS)
    T = _tile(S, 256)

    def body(v_ref, halo_ref, z_ref, pw_ref, ps_ref, p_ref, y_ref):
        i = pl.program_id(0)
        vb = v_ref[...]
        ext = jnp.concatenate([jnp.where(i > 0, halo_ref[...], 0.0), vb], axis=0)
        pos = _pool_pos(i, T)
        z = z_ref[...]
        gate = ps_ref[...] * (z * _sig(z))
        for g, w in enumerate(WINDOWS):
            sl = slice(g * GW, (g + 1) * GW)
            sg = ext[:, sl]
            for jj in range(g + 1):
                sg = sg + pltpu.roll(sg, 1 << jj, axis=0)
            pooled = (sg[HALO:] / jnp.minimum(pos, float(w)) - vb[:, sl]).astype(p_ref.dtype)
            p_ref[:, sl] = pooled
            y_ref[:, sl] = (_dot(pooled, pw_ref[g], NN) * gate[:, sl]).astype(y_ref.dtype)

    row = pl.BlockSpec((T, KW), lambda i: (i, 0))
    hb = T // HALO
    return pl.pallas_call(
        body, name="pool_fwd", grid=(S // T,),
        in_specs=[pl.BlockSpec((T, KW), lambda i: (i, 4)), pl.BlockSpec((HALO, KW), lambda i: (jnp.maximum(i * hb - 1, 0), 4)),
                  pl.BlockSpec((T, KW), lambda i: (i, 5)), pl.BlockSpec((len(WINDOWS), GW, GW), lambda i: (0, 0, 0)),
                  pl.BlockSpec((1, KW), lambda i: (0, 0))],
        out_specs=[row, row], out_shape=[jax.ShapeDtypeStruct((S, KW), MXU_DT), jax.ShapeDtypeStruct((S, KW), MXU_DT)],
        compiler_params=_params(("parallel",)))(proj, proj, proj, pool_w, pool_scale)


def _pool_bwd(dyb, proj, pooled, pool_w, pool_scale, KW):
    S = proj.shape[0]
    G = len(WINDOWS)
    GW = KW // G
    T = _tile(S, 256)

    def body(dy_ref, z_ref, p_ref, pw_ref, ps_ref, u_ref, dz_ref, dps_ref, dpw_ref):
        i = pl.program_id(0)

        @pl.when(i == 0)
        def _():
            dps_ref[...] = jnp.zeros_like(dps_ref)
            dpw_ref[...] = jnp.zeros_like(dpw_ref)

        pos = _pool_pos(i, T)
        z = z_ref[...]
        sz = _sig(z)
        silu = z * sz
        dsilu = sz * (1.0 + z * (1.0 - sz))
        dy = dy_ref[...]
        ps = ps_ref[...]
        for g, w in enumerate(WINDOWS):
            sl = slice(g * GW, (g + 1) * GW)
            pg = p_ref[:, sl]
            mixed = _dot(pg, pw_ref[g], NN)
            dyg = dy[:, sl]
            dz_ref[:, sl] = (dyg * mixed * ps[:, sl] * dsilu[:, sl]).astype(dz_ref.dtype)
            dps_ref[:, sl] += jnp.sum(dyg * mixed * silu[:, sl], axis=0, keepdims=True)
            dmix = dyg * ps[:, sl] * silu[:, sl]
            dpw_ref[g] += _dot(pg, dmix, TN)
            u_ref[:, sl] = _dot(dmix, pw_ref[g], NT) / jnp.minimum(pos, float(w))

    row = pl.BlockSpec((T, KW), lambda i: (i, 0))
    return pl.pallas_call(
        body, name="pool_bwd", grid=(S // T,),
        in_specs=[row, pl.BlockSpec((T, KW), lambda i: (i, 5)), row,
                  pl.BlockSpec((G, GW, GW), lambda i: (0, 0, 0)), pl.BlockSpec((1, KW), lambda i: (0, 0))],
        out_specs=[row, row, pl.BlockSpec((1, KW), lambda i: (0, 0)), pl.BlockSpec((G, GW, GW), lambda i: (0, 0, 0))],
        out_shape=[jax.ShapeDtypeStruct((S, KW), F32), jax.ShapeDtypeStruct((S, KW), MXU_DT),
                   jax.ShapeDtypeStruct((1, KW), F32), jax.ShapeDtypeStruct((G, GW, GW), F32)],
        compiler_params=_params(("arbitrary",)))(dyb, proj, pooled, pool_w, pool_scale)


def _pool_bwd_window(u, KW):
    S = u.shape[0]
    GW = KW // len(WINDOWS)
    T = _tile(S, 256)
    nb = S // T
    n = T + HALO

    def body(u_ref, halo_ref, dv_ref):
        i = pl.program_id(0)
        uv = u_ref[...]
        ext = jnp.concatenate([uv, jnp.where(i < nb - 1, halo_ref[...], 0.0)], axis=0)
        pos = _pool_pos(i, T)
        for g, w in enumerate(WINDOWS):
            sl = slice(g * GW, (g + 1) * GW)
            sg = ext[:, sl]
            for jj in range(g + 1):
                sg = sg + pltpu.roll(sg, n - (1 << jj), axis=0)
            dv_ref[:, sl] = (sg[:T] - uv[:, sl] * jnp.minimum(pos, float(w))).astype(dv_ref.dtype)

    hb = T // HALO
    return pl.pallas_call(
        body, name="pool_bwd_window", grid=(nb,),
        in_specs=[pl.BlockSpec((T, KW), lambda i: (i, 0)),
                  pl.BlockSpec((HALO, KW), lambda i: (jnp.minimum((i + 1) * hb, S // HALO - 1), 0))],
        out_specs=pl.BlockSpec((T, KW), lambda i: (i, 0)), out_shape=jax.ShapeDtypeStruct((S, KW), MXU_DT),
        compiler_params=_params(("parallel",)))(u, u)


def _softmax_rows(x):
    m = jnp.max(x, axis=0, keepdims=True)
    e = jnp.exp(x - m)
    return e / jnp.sum(e, axis=0, keepdims=True)


def _lb_table(lower_bounds):
    L, KW = lower_bounds.shape

    def body(x_ref, o_ref):
        sm = _softmax_rows(x_ref[...])
        acc = jnp.zeros((1, KW), F32)
        o_ref[0:1, :] = acc
        for l in range(1, L):
            acc = acc + sm[l:l + 1]
            o_ref[l:l + 1, :] = acc

    return pl.pallas_call(body, name="lb_table", out_shape=jax.ShapeDtypeStruct((L, KW), F32))(lower_bounds)


def _mod_part(c_all, w_ada):
    L, D, C = w_ada.shape
    B = c_all.shape[0]
    tn = _tile(C, 512)

    def body(c_ref, w_ref, o_ref):
        cv = c_ref[...]
        o_ref[...] = _dot(cv * _sig(cv), w_ref[...], NN)

    return pl.pallas_call(
        body, name="mod_part", grid=(L, C // tn),
        in_specs=[pl.BlockSpec((B, D), lambda l, j: (0, 0)), pl.BlockSpec((None, D, tn), lambda l, j: (l, 0, j))],
        out_specs=pl.BlockSpec((None, B, tn), lambda l, j: (l, 0, j)), out_shape=jax.ShapeDtypeStruct((L, B, C), F32),
        compiler_params=_params(("parallel", "parallel")))(c_all, w_ada)


def _adamw_math(w, g, m, v):
    m = ADAM_B1 * m + (1.0 - ADAM_B1) * g
    v = ADAM_B2 * v + (1.0 - ADAM_B2) * (g * g)
    m_hat = m / (1.0 - ADAM_B1 ** ADAM_STEP)
    v_hat = v / (1.0 - ADAM_B2 ** ADAM_STEP)
    delta = -ADAM_LR * (m_hat / (jnp.sqrt(v_hat) + ADAM_EPS) + ADAM_WD * w)
    return delta, m, v


def _adamw(w, g, m, v, name):
    shape = w.shape
    C = shape[-1]
    R = w.size // C
    tr = _tile(R, 128)
    flat = lambda t: t.reshape(R, C)

    def body(w_ref, g_ref, m_ref, v_ref, d_ref, nm_ref, nv_ref):
        d, nm, nv = _adamw_math(w_ref[...], g_ref[...], m_ref[...], v_ref[...])
        d_ref[...] = d
        nm_ref[...] = nm
        nv_ref[...] = nv

    blk = pl.BlockSpec((tr, C), lambda i: (i, 0))
    os = jax.ShapeDtypeStruct((R, C), F32)
    outs = pl.pallas_call(body, name=name, grid=(R // tr,), in_specs=[blk] * 4, out_specs=[blk] * 3, out_shape=[os] * 3,
                          compiler_params=_params(("parallel",)))(flat(w), flat(g), flat(m), flat(v))
    return [t.reshape(shape) for t in outs]


def _w_ada_update(c_all, dmod, w, m, v):
    L, D, C = w.shape
    B = c_all.shape[0]
    tn = _tile(C, 256)

    def body(c_ref, dm_ref, w_ref, m_ref, v_ref, g_ref, d_ref, nm_ref, nv_ref):
        cv = c_ref[...]
        g = lax.dot_general(cv * _sig(cv), dm_ref[...], (TN, ((), ())), preferred_element_type=F32,
                            precision=lax.Precision.HIGHEST)
        d, nm, nv = _adamw_math(w_ref[...], g, m_ref[...], v_ref[...])
        g_ref[...] = g
        d_ref[...] = d
        nm_ref[...] = nm
        nv_ref[...] = nv

    blk = pl.BlockSpec((None, D, tn), lambda l, j: (l, 0, j))
    os = jax.ShapeDtypeStruct((L, D, C), F32)
    return pl.pallas_call(
        body, name="w_ada_update", grid=(L, C // tn),
        in_specs=[pl.BlockSpec((B, D), lambda l, j: (0, 0)), pl.BlockSpec((None, B, tn), lambda l, j: (l, 0, j)), blk, blk, blk],
        out_specs=[blk] * 4, out_shape=[os] * 4, compiler_params=_params(("parallel", "parallel")))(c_all, dmod, w, m, v)


def _small_update(parts, lower_bounds, wmv, D, KW):
    L = parts.shape[1]
    widths = [3 * D, D, D, KW, KW, KW]
    offs = [sum(widths[:i]) for i in range(len(widths))]

    def body(p_ref, *refs):
        ins, outs = refs[:18], refs[18:]
        tot = p_ref[0]
        for dev in range(1, N_DEV):
            tot = tot + p_ref[dev]
        grads = [tot[:, o:o + wd] for o, wd in zip(offs, widths)]
        sm = _softmax_rows(ins[9][...])
        dlb = grads[3]
        dsm = [jnp.zeros((1, KW), F32)]
        for j in range(1, L):
            acc = dlb[j:j + 1]
            for l in range(j + 1, L):
                acc = acc + dlb[l:l + 1]
            dsm.append(acc)
        dsm = jnp.concatenate(dsm, axis=0)
        grads[3] = sm * (dsm - jnp.sum(sm * dsm, axis=0, keepdims=True))
        for p in range(6):
            w_ref, m_ref, v_ref = ins[3 * p:3 * p + 3]
            d, nm, nv = _adamw_math(w_ref[...], grads[p], m_ref[...], v_ref[...])
            outs[4 * p][...] = grads[p]
            outs[4 * p + 1][...] = d
            outs[4 * p + 2][...] = nm
            outs[4 * p + 3][...] = nv

    flat = [t for trip in wmv for t in trip]
    out_shape = []
    for wd in widths:
        out_shape += [jax.ShapeDtypeStruct((L, wd), F32)] * 4
    res = pl.pallas_call(body, name="small_update", out_shape=out_shape,
                         compiler_params=pltpu.CompilerParams(vmem_limit_bytes=VMEM_LIMIT_BYTES))(parts, *flat)
    return [res[4 * p:4 * p + 4] for p in range(6)]


def _place():
    x, y, c = lax.axis_index("x"), lax.axis_index("y"), lax.axis_index("c")
    chips = [(1 - x, y), (x, 1 - y), (1 - x, 1 - y)]
    return x, y, c, chips


def _all_gather_rows(blk):
    m_per, n = blk.shape

    def body(x_ref, out_ref, send_sems, recv_sems, local_sem):
        x, y, c, chips = _place()
        me, sibling = (x, y, c), (x, y, 1 - c)

        def rows(px, py, pc):
            return out_ref.at[pl.ds((4 * px + 2 * py + pc) * m_per, m_per), :]

        def copy(k, block, to, src=None):
            return pltpu.make_async_remote_copy(src_ref=rows(*block) if src is None else src, dst_ref=rows(*block),
                                                send_sem=send_sems.at[k], recv_sem=recv_sems.at[k], device_id=to, device_id_type=MESH)

        mine = pltpu.make_async_copy(x_ref, rows(*me), local_sem)
        mine.start()
        first = [copy(0, me, sibling, src=x_ref)]
        first += [copy(1 + j, me, (*chip, c), src=x_ref) for j, chip in enumerate(chips)]
        for cp in first:
            cp.start()
        passed = [copy(4 + j, (*chip, c), sibling) for j, chip in enumerate(chips)]
        for j, chip in enumerate(chips):
            copy(1 + j, (*chip, c), me).wait_recv()
            passed[j].start()
        copy(0, sibling, me).wait_recv()
        for j, chip in enumerate(chips):
            copy(4 + j, (*chip, 1 - c), me).wait_recv()
        for cp in first + passed:
            cp.wait_send()
        mine.wait()

    return pl.pallas_call(
        body, name="all_gather_rows", out_shape=jax.ShapeDtypeStruct((N_DEV * m_per, n), blk.dtype),
        in_specs=[pl.BlockSpec(memory_space=pltpu.VMEM)], out_specs=pl.BlockSpec(memory_space=pltpu.VMEM),
        scratch_shapes=[pltpu.SemaphoreType.DMA((7,)), pltpu.SemaphoreType.DMA((7,)), pltpu.SemaphoreType.DMA],
        compiler_params=pltpu.CompilerParams(vmem_limit_bytes=VMEM_LIMIT_BYTES))(blk)


SHARD_AXIS = (1, 1, 1, 1, 0)
HALF_AXIS = (0, 0, 0, 0, 1)
N_STACKS = 5


def _cut(ref, axis, part, n):
    idx = [slice(None)] * len(ref.shape)
    idx[axis] = pl.ds(pl.multiple_of(part * n, n), n)
    return ref.at[tuple(idx)]


def _quarter(ref, t, chip):
    return _cut(ref, SHARD_AXIS[t], chip, ref.shape[SHARD_AXIS[t]] // N_CHIPS)


def _half(ref, t, core):
    return _cut(ref, HALF_AXIS[t], core, ref.shape[HALF_AXIS[t]] // 2)


def _with_axis(shape, axis, n):
    return tuple(n if a == axis else s for a, s in enumerate(shape))


def _view3(t):
    return t.reshape((1,) * (3 - t.ndim) + t.shape)


def _axis3(t, axis):
    return axis + 3 - t.ndim


def _blocked(shape3, blk, offsets=None):
    offsets = offsets or {}

    def index(b, r, c, *pre):
        idx = [b, r, c]
        for ax, fn in offsets.items():
            idx[ax] = idx[ax] + fn(*pre)
        return tuple(idx)

    return pl.BlockSpec((None,) + tuple(blk), index)


def _cast_into_full(stack, l, t, chip_idx):
    shard = stack.shape[1:]
    nd = len(shard)
    w4 = stack.reshape((stack.shape[0],) + (1,) * (3 - nd) + shard)
    ax = SHARD_AXIS[t] + 3 - nd
    _, B, R, C = w4.shape
    tr = _tile(R, 256)
    per = (R // tr, 1)[ax - 1]

    def body(j_ref, w_ref, o_ref):
        o_ref[...] = w_ref[...].astype(o_ref.dtype)

    full3 = _with_axis((B, R, C), ax, N_CHIPS * (B, R, C)[ax])
    grid_spec = pltpu.PrefetchScalarGridSpec(
        num_scalar_prefetch=1, grid=(B, R // tr, 1),
        in_specs=[pl.BlockSpec((None, None, tr, C), lambda b, r, c, j_ref: (l, b, r, c))],
        out_specs=_blocked(full3, (tr, C), {ax: lambda j_ref: j_ref[0] * per}))
    o = pl.pallas_call(body, name=f"cast_into_full{t}", grid_spec=grid_spec, out_shape=jax.ShapeDtypeStruct(full3, MXU_DT),
                       compiler_params=_params(("parallel",) * 3))(chip_idx, w4)
    return o.reshape(_with_axis(shard, SHARD_AXIS[t], N_CHIPS * shard[SHARD_AXIS[t]]))


def _remote(src, dst, send_sems, recv_sems, k, to):
    return pltpu.make_async_remote_copy(src_ref=src, dst_ref=dst, send_sem=send_sems.at[k], recv_sem=recv_sems.at[k],
                                        device_id=to, device_id_type=MESH)


def _side(inputs, out_shapes, aliases, n_sems, copies):
    def start(ins, outs, ss, rs):
        for snd, _ in copies(ins, outs, ss, rs):
            snd.start()

    def finish(ins, outs, ss, rs):
        for snd, rcv in copies(ins, outs, ss, rs):
            rcv.wait_recv()
            snd.wait_send()

    return dict(inputs=list(inputs), out_shapes=list(out_shapes), aliases=aliases, n_sems=n_sems, start=start, finish=finish)


def _run_side(side, name):
    n_in, n_out = len(side["inputs"]), len(side["out_shapes"])

    def body(*refs):
        ins, outs, sems = refs[:n_in], refs[n_in:n_in + n_out], refs[n_in + n_out:]
        side["start"](ins, outs, *sems)
        side["finish"](ins, outs, *sems)

    any_spec = pl.BlockSpec(memory_space=pl.ANY)
    return list(pl.pallas_call(
        body, name=name, out_shape=side["out_shapes"], in_specs=[any_spec] * n_in, out_specs=[any_spec] * n_out,
        input_output_aliases=side["aliases"],
        scratch_shapes=[pltpu.SemaphoreType.DMA((side["n_sems"],)), pltpu.SemaphoreType.DMA((side["n_sems"],))])(*side["inputs"]))


def _same(arrays):
    return [jax.ShapeDtypeStruct(a.shape, a.dtype) for a in arrays]


def _gather_ici_side(ws):
    def copies(ins, outs, ss, rs):
        x, y, c, chips = _place()
        out = []
        for k, (cx, cy) in enumerate(chips):
            for t in range(N_STACKS):
                mine = _half(_quarter(outs[t], t, 2 * x + y), t, c)
                theirs = _half(_quarter(outs[t], t, 2 * cx + cy), t, c)
                n = k * N_STACKS + t
                out.append((_remote(mine, mine, ss, rs, n, (cx, cy, c)), _remote(theirs, theirs, ss, rs, n, (cx, cy, c))))
        return out

    return _side(ws, _same(ws), {t: t for t in range(N_STACKS)}, 3 * N_STACKS, copies)


def _gather_d2d_side(ws):
    def copies(ins, outs, ss, rs):
        x, y, c, chips = _place()
        out = []
        for k, (cx, cy) in enumerate(chips):
            for t in range(N_STACKS):
                q = _quarter(outs[t], t, 2 * cx + cy)
                mine, theirs = _half(q, t, c), _half(q, t, 1 - c)
                n = k * N_STACKS + t
                out.append((_remote(mine, mine, ss, rs, n, (x, y, 1 - c)), _remote(theirs, theirs, ss, rs, n, (x, y, 1 - c))))
        return out

    return _side(ws, _same(ws), {t: t for t in range(N_STACKS)}, 3 * N_STACKS, copies)


def _swap_side(gs):
    shapes = [jax.ShapeDtypeStruct(_with_axis(g.shape, HALF_AXIS[t], g.shape[HALF_AXIS[t]] // 2), g.dtype) for t, g in enumerate(gs)]

    def copies(ins, outs, ss, rs):
        x, y, c, _ = _place()
        cps = [_remote(_half(ins[t], t, 1 - c), outs[t], ss, rs, t, (x, y, 1 - c)) for t in range(N_STACKS)]
        return [(cp, cp) for cp in cps]

    return _side(gs, shapes, {}, N_STACKS, copies)


def _pair_sum(g, recv, t, c_idx):
    g3, r3 = _view3(g), _view3(recv)
    hf_ax = _axis3(g, HALF_AXIS[t])
    B, R, C = r3.shape
    tr, tc = _tile(R, 256), _tile(C, 2048)
    grid = (B, R // tr, C // tc)

    def body(c_ref, g_ref, r_ref, o_ref):
        o_ref[...] = (g_ref[...] + r_ref[...]).astype(o_ref.dtype)

    grid_spec = pltpu.PrefetchScalarGridSpec(
        num_scalar_prefetch=1, grid=grid,
        in_specs=[_blocked(g3.shape, (tr, tc), {hf_ax: lambda c_ref: c_ref[0] * grid[hf_ax]}), _blocked(r3.shape, (tr, tc))],
        out_specs=_blocked(r3.shape, (tr, tc)))
    out = pl.pallas_call(body, name=f"pair_sum{t}", grid_spec=grid_spec, out_shape=jax.ShapeDtypeStruct(r3.shape, BF16),
                         compiler_params=_params(("parallel",) * 3))(c_idx, g3, r3)
    return out.reshape(recv.shape)


def _scatter_side(ps):
    shapes = [jax.ShapeDtypeStruct((N_CHIPS - 1,) + _with_axis(p.shape, SHARD_AXIS[t], p.shape[SHARD_AXIS[t]] // N_CHIPS), p.dtype)
              for t, p in enumerate(ps)]

    def copies(ins, outs, ss, rs):
        x, y, c, chips = _place()
        cps = [_remote(_quarter(ins[t], t, 2 * cx + cy), outs[t].at[k], ss, rs, k * N_STACKS + t, (cx, cy, c))
               for k, (cx, cy) in enumerate(chips) for t in range(N_STACKS)]
        return [(cp, cp) for cp in cps]

    return _side(ps, shapes, {}, 3 * N_STACKS, copies)


def _sum_chips(p, slots, t, chip_idx, c_idx):
    p3 = _view3(p)
    sh_ax, hf_ax = _axis3(p, SHARD_AXIS[t]), _axis3(p, HALF_AXIS[t])
    piece3 = _with_axis(p3.shape, sh_ax, p3.shape[sh_ax] // N_CHIPS)
    s4 = slots.reshape((N_CHIPS - 1,) + piece3)
    B, R, C = piece3
    tr = _tile(R, 256)
    grid = (B, R // tr, 1)
    full3 = _with_axis(piece3, hf_ax, 2 * piece3[hf_ax])

    def body(j_ref, c_ref, own_ref, s_ref, o_ref):
        acc = own_ref[...].astype(F32)
        for k in range(N_CHIPS - 1):
            acc = acc + s_ref[k].astype(F32)
        o_ref[...] = acc

    grid_spec = pltpu.PrefetchScalarGridSpec(
        num_scalar_prefetch=2, grid=grid,
        in_specs=[_blocked(p3.shape, (tr, C), {sh_ax: lambda j_ref, c_ref: j_ref[0] * grid[sh_ax]}),
                  pl.BlockSpec((N_CHIPS - 1, None, tr, C), lambda b, r, cc, j_ref, c_ref: (0, b, r, cc))],
        out_specs=_blocked(full3, (tr, C), {hf_ax: lambda j_ref, c_ref: c_ref[0] * grid[hf_ax]}))
    out = pl.pallas_call(body, name=f"sum_chips{t}", grid_spec=grid_spec, out_shape=jax.ShapeDtypeStruct(full3, F32),
                         compiler_params=_params(("parallel",) * 3))(chip_idx, c_idx, p3, s4)
    return out.reshape(_with_axis(slots.shape[1:], HALF_AXIS[t], 2 * slots.shape[1 + HALF_AXIS[t]]))


def _join_side(rs_):
    def copies(ins, outs, ss, rs):
        x, y, c, _ = _place()
        return [(_remote(_half(outs[t], t, c), _half(outs[t], t, c), ss, rs, t, (x, y, 1 - c)),
                 _remote(_half(outs[t], t, 1 - c), _half(outs[t], t, 1 - c), ss, rs, t, (x, y, 1 - c))) for t in range(N_STACKS)]

    return _side(rs_, _same(rs_), {t: t for t in range(N_STACKS)}, N_STACKS, copies)


def _layer_weights(shards, l, chip_idx):
    return [_cast_into_full(s, l, t, chip_idx) for t, s in enumerate(shards)]


def _gather_alone(w):
    return _run_side(_gather_d2d_side(_run_side(_gather_ici_side(w), "gather_ici")), "gather_d2d")


def _forward_layer(x, l, w, small, mods, lb_all, KW, nxt):
    norm_pre, norm_post, hgrn_norm, pool_scale = small
    shift, scale, gate = mods
    row = lambda t: t[l:l + 1]
    h = _prenorm(x, row(norm_pre), row(scale), row(shift))
    if nxt is None:
        proj = _matmul(h, w[0], "nn", F32, "proj")
    else:
        proj, nxt = _matmul(h, w[0], "nn", F32, "proj_gather", side=_gather_ici_side(nxt))
    o_a, y_a, s0 = _hgrn_fwd(proj, row(lb_all), row(hgrn_norm), KW)
    pooled, y_b = _pool_fwd(proj, w[1], row(pool_scale), KW)
    pa, pb, merged = _merge(y_a, y_b, w[2], w[3], proj)
    if nxt is None:
        out = _matmul(merged, w[4], "nn", F32, "out_proj")
    else:
        out, nxt = _matmul(merged, w[4], "nn", F32, "out_proj_gather", side=_gather_d2d_side(nxt))
    x_new = _postnorm(x, out, row(gate), row(norm_post))
    saved = (x, h, proj, o_a, y_a, s0, pooled, y_b, pa, pb, merged, out)
    return x_new, saved, nxt


def _backward_layer(dx, l, saved, w, small, mods, lb_all, KW, above, idx):
    norm_pre, norm_post, hgrn_norm, pool_scale = small
    shift, scale, gate = mods
    chip_idx, c_idx = idx
    x, h, proj, o_a, y_a, s0, pooled, y_b, pa, pb, merged, out = saved
    row = lambda t: t[l:l + 1]
    dout, d_gate, d_npost = _postnorm_bwd(dx, out, row(gate), row(norm_post))
    if above is None:
        g_out = _matmul(merged, dout, "tn", F32, "grad_w_out")
    else:
        g_out, recv = _matmul(merged, dout, "tn", F32, "grad_w_out_swap", side=_swap_side(above))
        pair = [_pair_sum(g, r, t, c_idx) for t, (g, r) in enumerate(zip(above, recv))]
    dpa, dpb, dga, dgb = _merge_bwd(dout, w[4], proj, pa, pb)
    g_pa = _matmul(y_a, dpa, "tn", F32, "grad_w_proj_a")
    g_pb = _matmul(y_b, dpb, "tn", F32, "grad_w_proj_b")
    dya = _matmul(dpa, w[2], "nt", F32, "d_y_a")
    dyb = _matmul(dpb, w[3], "nt", F32, "d_y_b")
    do, dza, d_hn = _gate_a_bwd(dya, o_a, proj, row(hgrn_norm), KW)
    dq, df, dva, d_lb = _hgrn_bwd(proj, do, s0, row(lb_all), KW)
    u, dzb, d_ps, g_pool = _pool_bwd(dyb, proj, pooled, w[1], row(pool_scale), KW)
    dvb = _pool_bwd_window(u, KW)
    dproj = [dq, df, dva, dza, dvb, dzb, dga, dgb]
    g_in = lax.empty(w[0].shape, F32)
    col = 0
    dproj_k = [[dq, df], [dva, dza], [dvb, dzb], dga, dgb]
    if above is None:
        dh = _matmul(dproj_k, w[0], "nt", F32, "d_h", tm=512)
        reduced = None
        for piece in dproj:
            g_in = _matmul(h, piece, "tn", F32, "grad_w_in_piece", into=(g_in, col))
            col += piece.shape[1]
    else:
        dh, slots = _matmul(dproj_k, w[0], "nt", F32, "d_h_scatter", tm=512, side=_scatter_side(pair))
        part = [_sum_chips(p, s, t, chip_idx, c_idx) for t, (p, s) in enumerate(zip(pair, slots))]
        for n, piece in enumerate(dproj):
            if n == 0:
                g_in, reduced = _matmul(h, piece, "tn", F32, "grad_w_in_piece_join", into=(g_in, col), side=_join_side(part))
            else:
                g_in = _matmul(h, piece, "tn", F32, "grad_w_in_piece", into=(g_in, col))
            col += piece.shape[1]
    dx_new, d_shift, d_scale, d_npre = _prenorm_bwd(dh, x, dx, row(norm_pre), row(scale))
    small_g = jnp.concatenate([d_shift, d_scale, d_gate, d_npre, d_npost, d_lb, d_hn, d_ps], axis=1)
    return dx_new, small_g, [g_in, g_pool, g_pa, g_pb, g_out], reduced


def kernel(x, c, w_ada, b_ada, norm_pre, norm_post, w_in, lower_bounds, hgrn_norm, pool_w, pool_scale, w_proj_a, w_proj_b, w_out, loss_target, m_w_ada, m_b_ada, m_norm_pre, m_norm_post, m_w_in, m_lower_bounds, m_hgrn_norm, m_pool_w, m_pool_scale, m_w_proj_a, m_w_proj_b, m_w_out, v_w_ada, v_b_ada, v_norm_pre, v_norm_post, v_w_in, v_lower_bounds, v_hgrn_norm, v_pool_w, v_pool_scale, v_w_proj_a, v_w_proj_b, v_w_out):
    _, S, D = x.shape
    L = w_in.shape[0]
    KW = D // 2
    xi, yi, ci = lax.axis_index("x"), lax.axis_index("y"), lax.axis_index("c")
    chip = 2 * xi + yi
    dev = 4 * xi + 2 * yi + ci

    c_all = _all_gather_rows(c.reshape(8, D // 8)).reshape(N_DEV, D)
    modp = _mod_part(c_all, w_ada)
    cols = modp.shape[-1]
    modg = _all_gather_rows(modp.reshape(L * N_DEV, cols)).reshape(N_DEV, L, N_DEV, cols)
    mod_all = jnp.transpose(modg[0::2], (1, 2, 0, 3)).reshape(L, N_DEV, 3 * D)
    mod = lax.dynamic_index_in_dim(mod_all, dev, axis=1, keepdims=False) + b_ada
    mods = (mod[:, :D], mod[:, D:2 * D], mod[:, 2 * D:])
    lb_all = _lb_table(lower_bounds)

    shards = [w_in, pool_w, w_proj_a, w_proj_b, w_out]
    chip_idx = chip.astype(jnp.int32).reshape(1)
    c_idx = ci.astype(jnp.int32).reshape(1)
    small = (norm_pre, norm_post, hgrn_norm, pool_scale)

    xs = x[0]
    saved, ws = [], []
    w = _gather_alone(_layer_weights(shards, 0, chip_idx))
    for l in range(L):
        nxt = _layer_weights(shards, l + 1, chip_idx) if l + 1 < L else None
        xs, sv, nxt = _forward_layer(xs, l, w, small, mods, lb_all, KW, nxt)
        saved.append(sv)
        ws.append(w)
        w = nxt
    dx, sq = _loss_head(xs, loss_target[0])
    loss = lax.psum(0.5 * jnp.sum(sq) / D, ("x", "y", "c"))

    small_g, reduced = [None] * L, [None] * L
    above = None
    for l in reversed(range(L)):
        dx, small_g[l], above, red = _backward_layer(dx, l, saved[l], ws[l], small, mods, lb_all, KW, above, (chip_idx, c_idx))
        if red is not None:
            reduced[l + 1] = red
    recv = _run_side(_swap_side(above), "swap")
    pair = [_pair_sum(g, r, t, c_idx) for t, (g, r) in enumerate(zip(above, recv))]
    slots = _run_side(_scatter_side(pair), "scatter")
    reduced[0] = _run_side(_join_side([_sum_chips(p, s, t, chip_idx, c_idx) for t, (p, s) in enumerate(zip(pair, slots))]), "join")
    grad_x = dx[None]

    sg = jnp.concatenate(small_g, axis=0)
    sg = jnp.concatenate([sg, jnp.zeros((8 - L, sg.shape[1]), F32)], axis=0)
    parts = _all_gather_rows(sg).reshape(N_DEV, 8, sg.shape[1])[:, :L]
    wmv = [(b_ada, m_b_ada, v_b_ada), (norm_pre, m_norm_pre, v_norm_pre), (norm_post, m_norm_post, v_norm_post),
           (lower_bounds, m_lower_bounds, v_lower_bounds), (hgrn_norm, m_hgrn_norm, v_hgrn_norm), (pool_scale, m_pool_scale, v_pool_scale)]
    r_b_ada, r_npre, r_npost, r_lb, r_hn, r_ps = _small_update(parts, lower_bounds, wmv, D, KW)
    dmod = lax.dynamic_slice_in_dim(parts[:, :, :3 * D], chip * cols, cols, axis=2)
    r_w_ada = _w_ada_update(c_all, jnp.transpose(dmod, (1, 0, 2)), w_ada, m_w_ada, v_w_ada)

    grads = [jnp.stack([reduced[l][t] for l in range(L)], axis=0) for t in range(N_STACKS)]
    ms = [m_w_in, m_pool_w, m_w_proj_a, m_w_proj_b, m_w_out]
    vs = [v_w_in, v_pool_w, v_w_proj_a, v_w_proj_b, v_w_out]
    r_big = [[g] + _adamw(w, g, m, v, f"adamw{t}") for t, (w, g, m, v) in enumerate(zip(shards, grads, ms, vs))]
    r_w_in, r_pool_w, r_pa, r_pb, r_w_out = r_big

    order = [r_w_ada, r_b_ada, r_npre, r_npost, r_w_in, r_lb, r_hn, r_pool_w, r_ps, r_pa, r_pb, r_w_out]
    outs = [loss, grad_x]
    for k in range(4):
        outs += [r[k] for r in order]
    return tuple(outs)
```
